```python
import jax, jax.numpy as jnp
from jax import lax
import numpy as np

D_MODEL = 2048
BATCH = 1
SEQ = 16384
DEPTH = 2

GRID_W = 64
CTX_LEN = 256
A_HEADS = D_MODEL // 256
A_HEAD_DIM = 128
A_WIDTH = A_HEADS * A_HEAD_DIM
HGRN_CHUNK = 64
B_Q_HEADS = D_MODEL // 256
B_KV_HEADS = 2
B_HEAD_DIM = 128
B_WIDTH = B_Q_HEADS * B_HEAD_DIM
B_KV_WIDTH = B_KV_HEADS * B_HEAD_DIM
EVEN_IN = 5 * A_WIDTH + B_WIDTH + 2 * B_KV_WIDTH
MLA_HEADS = D_MODEL // 128
MLA_Q_LORA = 512
MLA_KV_LORA = 512
MLA_NOPE = 128
MLA_ROPE = 64
MLA_V = 128
MLA_DOWN = MLA_Q_LORA + MLA_KV_LORA + MLA_ROPE
N_EXPERTS = 16
EXPERT_FF = D_MODEL // 2
EC_CAPACITY_FACTOR = 2
Q_BLOCK = 128
ROPE_THETA = 10000.0
NORM_EPS = 1e-6
N_EVEN = (DEPTH + 1) // 2
N_ODD = DEPTH // 2
DEEPNORM_ALPHA = (2.0 * DEPTH) ** 0.25
DEEPNORM_BETA = (8.0 * DEPTH) ** -0.25

kernel_name = 'hybrid_hgrn2_gqa_mla_ecmoe_dit_trunk'


def layer_norm(x):
    xf = x.astype(jnp.float32)
    mu = jnp.mean(xf, axis=-1, keepdims=True)
    var = jnp.mean(jnp.square(xf - mu), axis=-1, keepdims=True)
    return (xf - mu) * lax.rsqrt(var + NORM_EPS)


def rms_norm(x, g):
    xf = x.astype(jnp.float32)
    y = xf * lax.rsqrt(jnp.mean(jnp.square(xf), axis=-1, keepdims=True) + NORM_EPS) * g
    return y.astype(x.dtype)


def modulate(x, shift, scale):
    return (layer_norm(x) * (1.0 + scale) + shift).astype(x.dtype)


def post_norm(x, y, gate, g, b):
    return (layer_norm(DEEPNORM_ALPHA * x + gate * y) * g + b).astype(x.dtype)


def axial_rope_tables(n_tokens, rot_dim):
    rows = n_tokens // GRID_W
    row = jnp.repeat(jnp.arange(rows, dtype=jnp.float32), GRID_W)
    col = jnp.tile(jnp.arange(GRID_W, dtype=jnp.float32), rows)
    n_freq = rot_dim // 4
    inv = ROPE_THETA ** (-jnp.arange(n_freq, dtype=jnp.float32) / n_freq)
    ang = jnp.concatenate([row[:, None] * inv, col[:, None] * inv], axis=-1)
    return jnp.cos(ang), jnp.sin(ang)


def apply_rope(x, cos, sin):
    half = x.shape[-1] // 2
    xf = x.astype(jnp.float32)
    x1, x2 = xf[..., :half], xf[..., half:]
    c = cos[None, :, None, :]
    s = sin[None, :, None, :]
    return jnp.concatenate([x1 * c - x2 * s, x2 * c + x1 * s], axis=-1).astype(x.dtype)


def block_attention(q, k, v):
    b, n, hk, g, d = q.shape
    scale = d ** -0.5
    nblk = n // Q_BLOCK
    qb = q.reshape(b, nblk, Q_BLOCK, hk, g, d).swapaxes(0, 1)

    def one_block(qi):
        s = jnp.einsum('bqhgd,bkhd->bhgqk', qi, k, preferred_element_type=jnp.float32) * scale
        p = jax.nn.softmax(s, axis=-1)
        return jnp.einsum('bhgqk,bkhe->bqhge', p.astype(v.dtype), v)

    o = lax.map(one_block, qb)
    return o.swapaxes(0, 1).reshape(b, n, hk * g * v.shape[-1])


def hgrn2_chunked(q, k, logf, v, s0):
    b, L, h, dk = q.shape
    dv = v.shape[-1]
    n = L // HGRN_CHUNK

    def to_chunks(t):
        return t.reshape(b, n, HGRN_CHUNK, h, t.shape[-1]).transpose(1, 0, 3, 2, 4)

    mask = jnp.tril(jnp.ones((HGRN_CHUNK, HGRN_CHUNK), dtype=bool))[:, :, None]

    def step(S, inp):
        qc, kc, gc, vc = inp
        cum = jnp.cumsum(gc, axis=-2)
        rel = jnp.where(mask, cum[:, :, :, None, :] - cum[:, :, None, :, :], -jnp.inf)
        decay = jnp.exp(rel)
        scores = jnp.einsum('bhtc,bhsc,bhtsc->bhts', qc, kc, decay)
        o = (jnp.einsum('bhts,bhsv->bhtv', scores, vc)
             + jnp.einsum('bhtc,bhcv->bhtv', qc * jnp.exp(cum), S))
        last = cum[:, :, -1:, :]
        S_new = (jnp.exp(last)[:, :, 0, :, None] * S
                 + jnp.einsum('bhsc,bhsv->bhcv', kc * jnp.exp(last - cum), vc))
        return S_new, o

    s_fin, o = lax.scan(step, s0, (to_chunks(q), to_chunks(k), to_chunks(logf), to_chunks(v)))
    o = o.transpose(1, 0, 3, 2, 4).reshape(b, L, h, dv)
    return o, s_fin


def forget_gate(f_logit, lb):
    f = lb + (1.0 - lb) * jax.nn.sigmoid(f_logit.astype(jnp.float32))
    return jnp.log(f), (1.0 - f).astype(f_logit.dtype)


def heads(t, hd):
    return t.reshape(t.shape[0], t.shape[1], -1, hd)


def even_mixer(h_ctx, h_lat, w_in, w_out, lb_dir, a_norm_g, q_norm_g, k_norm_g, rope_b, need_ctx):
    splits = np.cumsum([A_WIDTH] * 5 + [B_WIDTH, B_KV_WIDTH]).tolist()

    def project(h):
        qa, fa_f, fa_b, ia, ga, qb, kb, vb = jnp.split(h @ w_in, splits, axis=-1)
        gf, kf = forget_gate(fa_f, lb_dir[0])
        gb, kbk = forget_gate(fa_b, lb_dir[1])
        hgrn = (heads(qa, A_HEAD_DIM), heads(ia, A_HEAD_DIM),
                heads(gf, A_HEAD_DIM), heads(kf, A_HEAD_DIM),
                heads(gb, A_HEAD_DIM), heads(kbk, A_HEAD_DIM), ga)
        gqa = (rms_norm(heads(qb, B_HEAD_DIM), q_norm_g),
               rms_norm(heads(kb, B_HEAD_DIM), k_norm_g),
               heads(vb, B_HEAD_DIM))
        return hgrn, gqa

    (qa_c, i_c, gf_c, kf_c, gb_c, kb_c, ga_c), (qb_c, kb_cx, vb_c) = project(h_ctx)
    (qa_l, i_l, gf_l, kf_l, gb_l, kb_l, ga_l), (qb_l, kb_lt, vb_l) = project(h_lat)

    flip = lambda t: t[:, ::-1]
    z0 = jnp.zeros((h_lat.shape[0], A_HEADS, A_HEAD_DIM, A_HEAD_DIM), jnp.float32)
    o_cf, s_f = hgrn2_chunked(qa_c, kf_c, gf_c, i_c, z0)
    o_lf, _ = hgrn2_chunked(qa_l, kf_l, gf_l, i_l, s_f)
    o_cb, s_b = hgrn2_chunked(flip(qa_c), flip(kb_c), flip(gb_c), flip(i_c), z0)
    o_lb, _ = hgrn2_chunked(flip(qa_l), flip(kb_l), flip(gb_l), flip(i_l), s_b)

    def hgrn_out(o, g):
        o = rms_norm(o, a_norm_g).reshape(g.shape[0], g.shape[1], A_WIDTH)
        return (o * jax.nn.silu(g)).astype(g.dtype)

    a_lat = hgrn_out(o_lf + flip(o_lb), ga_l)

    grp = B_Q_HEADS // B_KV_HEADS
    q_l = apply_rope(qb_l, *rope_b)
    k_l = apply_rope(kb_lt, *rope_b)
    k_all = jnp.concatenate([kb_cx, k_l], axis=1)
    v_all = jnp.concatenate([vb_c, vb_l], axis=1)
    b_lat = block_attention(q_l.reshape(q_l.shape[0], q_l.shape[1], B_KV_HEADS, grp, B_HEAD_DIM), k_all, v_all)
    y_lat = jnp.concatenate([a_lat, b_lat], axis=-1) @ w_out

    y_ctx = None
    if need_ctx:
        a_ctx = hgrn_out(o_cf + flip(o_cb), ga_c)
        b_ctx = block_attention(qb_c.reshape(qb_c.shape[0], qb_c.shape[1], B_KV_HEADS, grp, B_HEAD_DIM), kb_cx, vb_c)
        y_ctx = jnp.concatenate([a_ctx, b_ctx], axis=-1) @ w_out
    return y_ctx, y_lat


def mla_mixer(h_ctx, h_lat, w_down, q_norm_g, kv_norm_g, w_uq, w_ukv, w_o, rope_c, need_ctx):
    def project(h, rope, want_q):
        b, n, _ = h.shape
        cq, ckv, kr = jnp.split(h @ w_down, [MLA_Q_LORA, MLA_Q_LORA + MLA_KV_LORA], axis=-1)
        kv = (rms_norm(ckv, kv_norm_g) @ w_ukv).reshape(b, n, MLA_HEADS, MLA_NOPE + MLA_V)
        k_nope, v = kv[..., :MLA_NOPE], kv[..., MLA_NOPE:]
        kr = kr[:, :, None, :]
        if rope is not None:
            kr = apply_rope(kr, *rope)
        k = jnp.concatenate([k_nope, jnp.broadcast_to(kr, (b, n, MLA_HEADS, MLA_ROPE))], axis=-1)
        q = None
        if want_q:
            q = (rms_norm(cq, q_norm_g) @ w_uq).reshape(b, n, MLA_HEADS, MLA_NOPE + MLA_ROPE)
            if rope is not None:
                q = jnp.concatenate([q[..., :MLA_NOPE], apply_rope(q[..., MLA_NOPE:], *rope)], axis=-1)
        return q, k, v

    q_c, k_c, v_c = project(h_ctx, None, need_ctx)
    q_l, k_l, v_l = project(h_lat, rope_c, True)
    k_all = jnp.concatenate([k_c, k_l], axis=1)
    v_all = jnp.concatenate([v_c, v_l], axis=1)
    y_lat = block_attention(q_l[:, :, :, None, :], k_all, v_all) @ w_o
    y_ctx = None
    if need_ctx:
        y_ctx = block_attention(q_c[:, :, :, None, :], k_c, v_c) @ w_o
    return y_ctx, y_lat


def ec_moe(h, w_router, w_gate, w_up, w_down):
    b, n, d = h.shape
    cap = max(1, EC_CAPACITY_FACTOR * n // N_EXPERTS)
    aff = jax.nn.softmax(jnp.einsum('bnd,de->bne', h, w_router, preferred_element_type=jnp.float32), axis=-1)
    weight, idx = lax.top_k(aff.transpose(0, 2, 1), cap)
    xg = jax.vmap(lambda hb, ib: hb[ib])(h, idx)
    hid = jax.nn.silu(jnp.einsum('becd,edf->becf', xg, w_gate)) * jnp.einsum('becd,edf->becf', xg, w_up)
    y = jnp.einsum('becf,efd->becd', hid, w_down) * weight[..., None].astype(h.dtype)
    return jax.vmap(lambda yb, ib: jnp.zeros((n, d), y.dtype).at[ib.reshape(-1)].add(yb.reshape(-1, d)))(y, idx)


def setup_inputs(seed: int = 0) -> dict:
    key = jax.random.key(seed)
    ks = jax.random.split(key, 24)
    D = D_MODEL
    nrm = lambda k, shape, scale: jax.random.normal(k, shape, jnp.float32) * scale
    return {
        'x': nrm(ks[0], (BATCH, SEQ, D), 1.0),
        'c': nrm(ks[1], (BATCH, D), 1.0),
        'ctx': nrm(ks[2], (BATCH, CTX_LEN, D), 1.0),
        'c_ctx': nrm(ks[3], (D,), 1.0),
        'ada_w': nrm(ks[4], (DEPTH, D, 6 * D), D ** -0.5),
        'ada_b': nrm(ks[5], (DEPTH, 6 * D), 0.01),
        'ln_g': 1.0 + nrm(ks[6], (DEPTH, 2, D), 0.02),
        'ln_b': nrm(ks[7], (DEPTH, 2, D), 0.01),
        'ev_w_in': nrm(ks[8], (N_EVEN, D, EVEN_IN), D ** -0.5),
        'ev_w_out': nrm(ks[9], (N_EVEN, A_WIDTH + B_WIDTH, D), DEEPNORM_BETA * (A_WIDTH + B_WIDTH) ** -0.5),
        'hgrn_lb': nrm(ks[10], (2, DEPTH + 1, A_WIDTH), 0.1),
        'hgrn_norm_g': 1.0 + nrm(ks[11], (N_EVEN, A_HEAD_DIM), 0.02),
        'gqa_q_norm_g': 1.0 + nrm(ks[12], (N_EVEN, B_HEAD_DIM), 0.02),
        'gqa_k_norm_g': 1.0 + nrm(ks[13], (N_EVEN, B_HEAD_DIM), 0.02),
        'mla_w_down': nrm(ks[14], (N_ODD, D, MLA_DOWN), D ** -0.5),
        'mla_q_norm_g': 1.0 + nrm(ks[15], (N_ODD, MLA_Q_LORA), 0.02),
        'mla_kv_norm_g': 1.0 + nrm(ks[16], (N_ODD, MLA_KV_LORA), 0.02),
        'mla_w_uq': nrm(ks[17], (N_ODD, MLA_Q_LORA, MLA_HEADS * (MLA_NOPE + MLA_ROPE)), MLA_Q_LORA ** -0.5),
        'mla_w_ukv': nrm(ks[18], (N_ODD, MLA_KV_LORA, MLA_HEADS * (MLA_NOPE + MLA_V)), MLA_KV_LORA ** -0.5),
        'mla_w_o': nrm(ks[19], (N_ODD, MLA_HEADS * MLA_V, D), DEEPNORM_BETA * (MLA_HEADS * MLA_V) ** -0.5),
        'moe_router': nrm(ks[20], (DEPTH, D, N_EXPERTS), D ** -0.5),
        'moe_w_gate': nrm(ks[21], (DEPTH, N_EXPERTS, D, EXPERT_FF), D ** -0.5),
        'moe_w_up': nrm(ks[22], (DEPTH, N_EXPERTS, D, EXPERT_FF), D ** -0.5),
        'moe_w_down': nrm(ks[23], (DEPTH, N_EXPERTS, EXPERT_FF, D), DEEPNORM_BETA * EXPERT_FF ** -0.5),
    }


def reference(x, c, ctx, c_ctx, ada_w, ada_b, ln_g, ln_b, ev_w_in, ev_w_out, hgrn_lb, hgrn_norm_g,
              gqa_q_norm_g, gqa_k_norm_g, mla_w_down, mla_q_norm_g, mla_kv_norm_g, mla_w_uq, mla_w_ukv,
              mla_w_o, moe_router, moe_w_gate, moe_w_up, moe_w_down):
    n_tok = x.shape[1]
    rope_b = axial_rope_tables(n_tok, B_HEAD_DIM)
    rope_c = axial_rope_tables(n_tok, MLA_ROPE)
    lb_all = jnp.cumsum(jax.nn.softmax(hgrn_lb.astype(jnp.float32), axis=1), axis=1)
    for l in range(DEPTH):
        last = l == DEPTH - 1
        mod_l = (jax.nn.silu(c) @ ada_w[l] + ada_b[l])[:, None, :]
        mod_c = jax.nn.silu(c_ctx) @ ada_w[l] + ada_b[l]
        sh1, sc1, g1, sh2, sc2, g2 = jnp.split(mod_l, 6, axis=-1)
        csh1, csc1, cg1, csh2, csc2, cg2 = jnp.split(mod_c, 6, axis=-1)

        h_l = modulate(x, sh1, sc1)
        h_c = modulate(ctx, csh1, csc1)
        i = l // 2
        if l % 2 == 0:
            y_c, y_l = even_mixer(h_c, h_l, ev_w_in[i], ev_w_out[i], lb_all[:, l], hgrn_norm_g[i],
                                  gqa_q_norm_g[i], gqa_k_norm_g[i], rope_b, not last)
        else:
            y_c, y_l = mla_mixer(h_c, h_l, mla_w_down[i], mla_q_norm_g[i], mla_kv_norm_g[i],
                                 mla_w_uq[i], mla_w_ukv[i], mla_w_o[i], rope_c, not last)
        x = post_norm(x, y_l, g1, ln_g[l, 0], ln_b[l, 0])
        if not last:
            ctx = post_norm(ctx, y_c, cg1, ln_g[l, 0], ln_b[l, 0])

        y_l = ec_moe(modulate(x, sh2, sc2), moe_router[l], moe_w_gate[l], moe_w_up[l], moe_w_down[l])
        x = post_norm(x, y_l, g2, ln_g[l, 1], ln_b[l, 1])
        if not last:
            y_c = ec_moe(modulate(ctx, csh2, csc2), moe_router[l], moe_w_gate[l], moe_w_up[l], moe_w_down[l])
            ctx = post_norm(ctx, y_c, cg2, ln_g[l, 1], ln_b[l, 1])
    return x
```

```python
import functools
import math

import numpy as np
import jax
import jax.numpy as jnp
from jax import lax
from jax.experimental import pallas as pl
from jax.experimental.pallas import tpu as pltpu

F32 = jnp.float32
BF16 = jnp.bfloat16

D_MODEL = 2048
DEPTH = 2
GRID_W = 64
HEAD_DIM = 128
A_HEADS = D_MODEL // 256
A_WIDTH = A_HEADS * HEAD_DIM
B_Q_HEADS = D_MODEL // 256
B_KV_HEADS = 2
B_WIDTH = B_Q_HEADS * HEAD_DIM
B_KV_WIDTH = B_KV_HEADS * HEAD_DIM
MLA_HEADS = D_MODEL // 128
MLA_Q_LORA = 512
MLA_KV_LORA = 512
MLA_ROPE = 64
MLA_QK = 2 * HEAD_DIM
N_EXPERTS = 16
EXPERT_FF = D_MODEL // 2
EC_CAPACITY_FACTOR = 2
ROPE_THETA = 10000.0
NORM_EPS = 1e-6
DEEPNORM_ALPHA = (2.0 * DEPTH) ** 0.25

HGRN_CHUNK = 128
V7X_VMEM_BYTES = 64 * 1024 * 1024
VMEM_CAP_BYTES = V7X_VMEM_BYTES - 8 * 1024 * 1024


def _params(semantics, vmem_estimate_bytes):
    limit = int(min(max(2 * vmem_estimate_bytes, 32 * 1024 * 1024), VMEM_CAP_BYTES))
    return pltpu.CompilerParams(dimension_semantics=semantics, vmem_limit_bytes=limit)


def _layer_norm(x):
    mu = jnp.mean(x, axis=-1, keepdims=True)
    xc = x - mu
    var = jnp.mean(xc * xc, axis=-1, keepdims=True)
    return xc * lax.rsqrt(var + NORM_EPS)


def _rms(x):
    return x * lax.rsqrt(jnp.mean(x * x, axis=-1, keepdims=True) + NORM_EPS)


def _dot(a, b):
    return jnp.dot(a, b, preferred_element_type=F32)


def _dot_nt(a, b):
    return lax.dot_general(a, b, (((1,), (1,)), ((), ())), preferred_element_type=F32)


def _dot_tn(a, b):
    return lax.dot_general(a, b, (((0,), (0,)), ((), ())), preferred_element_type=F32)


def _split3(x):
    x1 = x.astype(BF16)
    r1 = x - x1.astype(F32)
    x2 = r1.astype(BF16)
    x3 = (r1 - x2.astype(F32)).astype(BF16)
    return x1, x2, x3


def _adaln_kernel(c_ref, w_ref, b_ref, o_ref):
    c = c_ref[...]
    s = c * jax.nn.sigmoid(c)
    w = w_ref[...]
    s1, s2, s3 = _split3(s)
    w1, w2, w3 = _split3(w)
    acc = _dot(s1, w3) + _dot(s3, w1) + _dot(s2, w2)
    acc = acc + _dot(s1, w2) + _dot(s2, w1)
    acc = acc + _dot(s1, w1)
    o_ref[...] = acc + b_ref[...]


def adaln(cc, w, b):
    d, n = w.shape
    tn = 1536 if n % 1536 == 0 else n
    est = 2 * d * tn * 4 * 2
    return pl.pallas_call(
        _adaln_kernel,
        out_shape=jax.ShapeDtypeStruct((8, n), F32),
        grid=(n // tn,),
        in_specs=[pl.BlockSpec((8, d), lambda j: (0, 0)),
                  pl.BlockSpec((d, tn), lambda j: (0, j)),
                  pl.BlockSpec((1, tn), lambda j: (0, j))],
        out_specs=pl.BlockSpec((8, tn), lambda j: (0, j)),
        compiler_params=_params(("parallel",), est),
        name="adaln",
    )(cc, w, b.reshape(1, n))


def _lnmod_mm_kernel(x_ref, sh_ref, sc_ref, w_ref, o_ref, h_ref, *, row):
    @pl.when(pl.program_id(1) == 0)
    def _():
        hn = _layer_norm(x_ref[...])
        h = hn * (1.0 + sc_ref[row:row + 1, :]) + sh_ref[row:row + 1, :]
        h_ref[...] = h.astype(BF16)

    o_ref[...] = _dot(h_ref[...], w_ref[...]).astype(o_ref.dtype)


def lnmod_matmul(x, mod, row, k_shift, k_scale, w, tn):
    m, d = x.shape
    n = w.shape[1]
    tm = min(m, 1024)
    est = 2 * tm * d * 4 + tm * d * 2 + 2 * d * tn * 2 + 2 * tm * tn * 4
    return pl.pallas_call(
        functools.partial(_lnmod_mm_kernel, row=row),
        out_shape=jax.ShapeDtypeStruct((m, n), F32),
        grid=(m // tm, n // tn),
        in_specs=[pl.BlockSpec((tm, d), lambda i, j: (i, 0)),
                  pl.BlockSpec((8, d), lambda i, j: (0, k_shift)),
                  pl.BlockSpec((8, d), lambda i, j: (0, k_scale)),
                  pl.BlockSpec((d, tn), lambda i, j: (0, j))],
        out_specs=pl.BlockSpec((tm, tn), lambda i, j: (i, j)),
        scratch_shapes=[pltpu.VMEM((tm, d), BF16)],
        compiler_params=_params(("parallel", "arbitrary"), est),
        name="lnmod_matmul",
    )(x, mod, mod, w)


def _hgrn_tables(c):
    n_lvl = int(math.log2(c))
    r = np.arange(c)
    u = np.arange(c)[None, :]
    blocks, masks = [], []
    for l in range(n_lvl):
        half = 1 << l
        base = (r // (2 * half)) * (2 * half)
        anchor = (base + half - 1)[:, None]
        upper = (r >= base + half)[:, None]
        rr = r[:, None]
        blocks.append(np.where(upper, (u > anchor) & (u <= rr), (u > rr) & (u <= anchor)))
        same = (r[:, None] // (2 * half)) == (r[None, :] // (2 * half))
        masks.append(same & upper & ~(upper.T))
    blocks.append(u <= r[:, None])
    blocks.append(u > r[:, None])
    blocks.append(np.ones((16, c), bool))
    masks.append(np.eye(c, dtype=bool))
    fwd_s = np.concatenate(blocks, axis=0).astype(np.float32)
    fwd_m = np.stack(masks).astype(np.float32)
    bwd_s = np.concatenate([b[::-1, ::-1] for b in blocks], axis=0).astype(np.float32)
    bwd_m = fwd_m[:, ::-1, ::-1]
    return (jnp.asarray(np.stack([fwd_s, bwd_s]), BF16), jnp.asarray(np.stack([fwd_m, bwd_m]), F32))


def _hgrn_kernel(q_ref, v_ref, f_ref, lb_ref, sums_ref, mask_ref, s0_ref, o_ref, sfin_ref, st_ref):
    c = q_ref.shape[0]
    n_lvl = mask_ref.shape[0] - 1
    j = pl.program_id(2)

    @pl.when(j == 0)
    def _():
        st_ref[...] = s0_ref[...]

    q = q_ref[...]
    vb = v_ref[...].astype(BF16)
    lb = lb_ref[0]
    f = lb + (1.0 - lb) * jax.nn.sigmoid(f_ref[...])
    g = jnp.log(f)
    k = 1.0 - f
    g1, g2, g3 = _split3(g)
    e3 = _dot(sums_ref[...], jnp.concatenate([g1, g2, g3], axis=1))
    e = (e3[:, 2 * HEAD_DIM:] + e3[:, HEAD_DIM:2 * HEAD_DIM]) + e3[:, :HEAD_DIM]

    scores = _dot_nt(q.astype(BF16), k.astype(BF16)) * mask_ref[n_lvl]
    for l in range(n_lvl):
        z = jnp.exp(e[l * c:(l + 1) * c])
        scores = scores + _dot_nt((q * z).astype(BF16), (k * z).astype(BF16)) * mask_ref[l]

    cum = e[n_lvl * c:(n_lvl + 1) * c]
    rem = e[(n_lvl + 1) * c:(n_lvl + 2) * c]
    tot = e[(n_lvl + 2) * c:(n_lvl + 2) * c + 1]
    st = st_ref[...]
    o = _dot(scores.astype(BF16), vb) + _dot_nt((q * jnp.exp(cum)).astype(BF16), st.astype(BF16))
    o_ref[...] = o
    st_new = st * jnp.exp(tot) + _dot_tn(vb, (k * jnp.exp(rem)).astype(BF16))
    st_ref[...] = st_new

    @pl.when(j == pl.num_programs(2) - 1)
    def _():
        sfin_ref[...] = st_new


def hgrn_scan(proj, lb, s0):
    seq = proj.shape[0]
    c = HGRN_CHUNK
    nc = seq // c
    sums, masks = _hgrn_tables(c)
    hd = HEAD_DIM

    def blk(d, j):
        return jnp.where(d == 0, j, nc - 1 - j)

    est = 2 * (3 * c * hd * 4 + sums.shape[1] * c * 2 + masks.shape[1] * c * c * 4 + 3 * hd * hd * 4 + c * hd * 4)
    return pl.pallas_call(
        _hgrn_kernel,
        out_shape=(jax.ShapeDtypeStruct((2, seq, A_WIDTH), F32),
                   jax.ShapeDtypeStruct((2, A_HEADS, hd, hd), F32)),
        grid=(2, A_HEADS, nc),
        in_specs=[pl.BlockSpec((c, hd), lambda d, h, j: (blk(d, j), h)),
                  pl.BlockSpec((c, hd), lambda d, h, j: (blk(d, j), 3 * A_HEADS + h)),
                  pl.BlockSpec((c, hd), lambda d, h, j: (blk(d, j), (1 + d) * A_HEADS + h)),
                  pl.BlockSpec((1, 1, hd), lambda d, h, j: (d * A_HEADS + h, 0, 0)),
                  pl.BlockSpec((None, sums.shape[1], c), lambda d, h, j: (d, 0, 0)),
                  pl.BlockSpec((None, masks.shape[1], c, c), lambda d, h, j: (d, 0, 0, 0)),
                  pl.BlockSpec((None, None, hd, hd), lambda d, h, j: (d, h, 0, 0))],
        out_specs=(pl.BlockSpec((None, c, hd), lambda d, h, j: (d, blk(d, j), h)),
                   pl.BlockSpec((None, None, hd, hd), lambda d, h, j: (d, h, 0, 0))),
        scratch_shapes=[pltpu.VMEM((hd, hd), F32)],
        compiler_params=_params(("parallel", "parallel", "arbitrary"), est),
        name="hgrn_scan",
    )(proj, proj, proj, lb, sums, masks, s0)


def _hgrn_out_kernel(o_ref, gate_ref, g_ref, a_ref):
    o = o_ref[0] + o_ref[1]
    gate = gate_ref[...]
    a_ref[...] = (_rms(o) * g_ref[...] * (gate * jax.nn.sigmoid(gate))).astype(a_ref.dtype)


def hgrn_out(o, proj, norm_g):
    seq = o.shape[1]
    tm = min(seq, 512)
    hd = HEAD_DIM
    return pl.pallas_call(
        _hgrn_out_kernel,
        out_shape=jax.ShapeDtypeStruct((seq, A_WIDTH), BF16),
        grid=(seq // tm, A_HEADS),
        in_specs=[pl.BlockSpec((2, tm, hd), lambda i, h: (0, i, h)),
                  pl.BlockSpec((tm, hd), lambda i, h: (i, 4 * A_HEADS + h)),
                  pl.BlockSpec((1, hd), lambda i, h: (0, 0))],
        out_specs=pl.BlockSpec((tm, hd), lambda i, h: (i, h)),
        compiler_params=_params(("parallel", "parallel"), 8 * tm * hd * 4),
        name="hgrn_out",
    )(o, proj, norm_g.reshape(1, hd))


def _norm_rope_kernel(x_ref, g_ref, cos_ref, sin_ref, o_ref, *, scale):
    y = _rms(x_ref[...]) * g_ref[...]
    y = y * cos_ref[...] + pltpu.roll(y, HEAD_DIM // 2, 1) * sin_ref[...]
    o_ref[...] = (y * scale).astype(o_ref.dtype)


def norm_rope(proj, col0, n_heads, g, cos, sin, scale):
    seq = proj.shape[0]
    tm = min(seq, 512)
    hd = HEAD_DIM
    return pl.pallas_call(
        functools.partial(_norm_rope_kernel, scale=scale),
        out_shape=jax.ShapeDtypeStruct((seq, n_heads * hd), BF16),
        grid=(seq // tm, n_heads),
        in_specs=[pl.BlockSpec((tm, hd), lambda i, h: (i, col0 + h)),
                  pl.BlockSpec((1, hd), lambda i, h: (0, 0)),
                  pl.BlockSpec((tm, hd), lambda i, h: (i, 0)),
                  pl.BlockSpec((tm, hd), lambda i, h: (i, 0))],
        out_specs=pl.BlockSpec((tm, hd), lambda i, h: (i, h)),
        compiler_params=_params(("parallel", "parallel"), 10 * tm * hd * 4),
        name="norm_rope",
    )(proj, g.reshape(1, hd), cos, sin)


def _flash_update(q, k, v, m_ref, l_ref, acc_ref):
    s = _dot_nt(q, k)
    m_prev = m_ref[...]
    m_new = jnp.maximum(m_prev, jnp.max(s, axis=-1, keepdims=True))
    alpha = jnp.exp(m_prev - m_new)
    p = jnp.exp(s - m_new)
    l_ref[...] = alpha * l_ref[...] + jnp.sum(p, axis=-1, keepdims=True)
    acc_ref[...] = alpha * acc_ref[...] + _dot(p.astype(BF16), v)
    m_ref[...] = m_new


def _flash_kernel(*refs, has_ctx):
    if has_ctx:
        q_ref, k_ref, v_ref, kc_ref, vc_ref, o_ref, m_ref, l_ref, acc_ref = refs
    else:
        q_ref, k_ref, v_ref, o_ref, m_ref, l_ref, acc_ref = refs
    j = pl.program_id(2)

    @pl.when(j == 0)
    def _():
        m_ref[...] = jnp.full(m_ref.shape, -jnp.inf, F32)
        l_ref[...] = jnp.zeros(l_ref.shape, F32)
        acc_ref[...] = jnp.zeros(acc_ref.shape, F32)
        if has_ctx:
            _flash_update(q_ref[...], kc_ref[...], vc_ref[...], m_ref, l_ref, acc_ref)

    _flash_update(q_ref[...], k_ref[...], v_ref[...], m_ref, l_ref, acc_ref)

    @pl.when(j == pl.num_programs(2) - 1)
    def _():
        o_ref[...] = (acc_ref[...] / l_ref[...]).astype(o_ref.dtype)


def flash_attention(q, k, v, k_ctx=None, v_ctx=None, *, n_heads, n_kv_heads, dq, dv):
    n, m = q.shape[0], k.shape[0]
    grp = n_heads // n_kv_heads
    tq = min(n, 512)
    tk = min(m, 512)
    has_ctx = k_ctx is not None
    in_specs = [pl.BlockSpec((tq, dq), lambda h, i, j: (i, h)),
                pl.BlockSpec((tk, dq), lambda h, i, j: (j, h // grp)),
                pl.BlockSpec((tk, dv), lambda h, i, j: (j, h // grp))]
    args = [q, k, v]
    if has_ctx:
        mc = k_ctx.shape[0]
        in_specs += [pl.BlockSpec((mc, dq), lambda h, i, j: (0, h // grp)),
                     pl.BlockSpec((mc, dv), lambda h, i, j: (0, h // grp))]
        args += [k_ctx, v_ctx]
    est = 2 * (tq * dq + tk * dq + tk * dv + tq * dv) * 2 + tq * (dv + 256) * 4 + 6 * tq * tk * 4
    return pl.pallas_call(
        functools.partial(_flash_kernel, has_ctx=has_ctx),
        out_shape=jax.ShapeDtypeStruct((n, n_heads * dv), BF16),
        grid=(n_heads, n // tq, m // tk),
        in_specs=in_specs,
        out_specs=pl.BlockSpec((tq, dv), lambda h, i, j: (i, h)),
        scratch_shapes=[pltpu.VMEM((tq, 1), F32), pltpu.VMEM((tq, 1), F32), pltpu.VMEM((tq, dv), F32)],
        compiler_params=_params(("parallel", "parallel", "arbitrary"), est),
        name="flash_attention",
    )(*args)


def _proj_postnorm_kernel(*refs, n_in, row):
    a_refs = refs[:n_in]
    w_refs = refs[n_in:2 * n_in]
    x_ref, gate_ref, g_ref, b_ref, o_ref = refs[2 * n_in:]
    y = _dot(a_refs[0][...], w_refs[0][...])
    for a_ref, w_ref in zip(a_refs[1:], w_refs[1:]):
        y = y + _dot(a_ref[...], w_ref[...])
    z = DEEPNORM_ALPHA * x_ref[...] + gate_ref[row:row + 1, :] * y
    o_ref[...] = _layer_norm(z) * g_ref[...] + b_ref[...]


def proj_postnorm(acts, ws, x, mod, row, k_gate, g, b):
    m, d = x.shape
    tm = min(m, 512)
    n_in = len(acts)
    once = pl.Buffered(1)
    in_specs = [pl.BlockSpec((tm, a.shape[1]), lambda i: (i, 0)) for a in acts]
    in_specs += [pl.BlockSpec(w.shape, lambda i: (0, 0), pipeline_mode=once) for w in ws]
    in_specs += [pl.BlockSpec((tm, d), lambda i: (i, 0)),
                 pl.BlockSpec((8, d), lambda i: (0, k_gate)),
                 pl.BlockSpec((1, d), lambda i: (0, 0)),
                 pl.BlockSpec((1, d), lambda i: (0, 0))]
    est = sum(w.size * 2 for w in ws) + sum(2 * tm * a.shape[1] * 2 for a in acts) + 6 * tm * d * 4
    return pl.pallas_call(
        functools.partial(_proj_postnorm_kernel, n_in=n_in, row=row),
        out_shape=jax.ShapeDtypeStruct((m, d), F32),
        grid=(m // tm,),
        in_specs=in_specs,
        out_specs=pl.BlockSpec((tm, d), lambda i: (i, 0)),
        compiler_params=_params(("parallel",), est),
        name="proj_postnorm",
    )(*acts, *ws, x, mod, g.reshape(1, d), b.reshape(1, d))


def _add_postnorm_kernel(y_ref, x_ref, gate_ref, g_ref, b_ref, o_ref, *, row):
    z = DEEPNORM_ALPHA * x_ref[...] + gate_ref[row:row + 1, :] * y_ref[...]
    o_ref[...] = _layer_norm(z) * g_ref[...] + b_ref[...]


def add_postnorm(y, x, mod, row, k_gate, g, b):
    m, d = x.shape
    tm = min(m, 512)
    return pl.pallas_call(
        functools.partial(_add_postnorm_kernel, row=row),
        out_shape=jax.ShapeDtypeStruct((m, d), F32),
        grid=(m // tm,),
        in_specs=[pl.BlockSpec((tm, d), lambda i: (i, 0)),
                  pl.BlockSpec((tm, d), lambda i: (i, 0)),
                  pl.BlockSpec((8, d), lambda i: (0, k_gate)),
                  pl.BlockSpec((1, d), lambda i: (0, 0)),
                  pl.BlockSpec((1, d), lambda i: (0, 0))],
        out_specs=pl.BlockSpec((tm, d), lambda i: (i, 0)),
        compiler_params=_params(("parallel",), 8 * tm * d * 4),
        name="add_postnorm",
    )(y, x, mod, g.reshape(1, d), b.reshape(1, d))


def _rms_mm_kernel(x_ref, g_ref, w_ref, o_ref, a_ref):
    @pl.when(pl.program_id(1) == 0)
    def _():
        a_ref[...] = (_rms(x_ref[...]) * g_ref[...]).astype(BF16)

    o_ref[...] = _dot(a_ref[...], w_ref[...]).astype(o_ref.dtype)


def _mla_q_kernel(x_ref, g_ref, w_ref, cos_ref, sa_ref, sb_ref, o_ref, a_ref, *, scale):
    @pl.when(pl.program_id(1) == 0)
    def _():
        a_ref[...] = (_rms(x_ref[...]) * g_ref[...]).astype(BF16)

    y = _dot(a_ref[...], w_ref[...])
    hd = HEAD_DIM
    for h in range(y.shape[1] // MLA_QK):
        c0 = h * MLA_QK
        o_ref[:, c0:c0 + hd] = (y[:, c0:c0 + hd] * scale).astype(o_ref.dtype)
        r = y[:, c0 + hd:c0 + 2 * hd]
        r = r * cos_ref[...] + pltpu.roll(r, hd - MLA_ROPE // 2, 1) * sa_ref[...] + pltpu.roll(r, MLA_ROPE // 2, 1) * sb_ref[...]
        o_ref[:, c0 + hd:c0 + 2 * hd] = (r * scale).astype(o_ref.dtype)


def _mla_k_kernel(x_ref, g_ref, w_ref, kr_ref, cos_ref, sa_ref, sb_ref, o_ref, a_ref, r_ref):
    hd = HEAD_DIM

    @pl.when(pl.program_id(1) == 0)
    def _():
        a_ref[...] = (_rms(x_ref[...]) * g_ref[...]).astype(BF16)
        r = kr_ref[...]
        r = r * cos_ref[...] + pltpu.roll(r, hd - MLA_ROPE // 2, 1) * sa_ref[...] + pltpu.roll(r, MLA_ROPE // 2, 1) * sb_ref[...]
        r_ref[...] = r.astype(BF16)

    y = _dot(a_ref[...], w_ref[...])
    for h in range(y.shape[1] // hd):
        o_ref[:, h * MLA_QK:h * MLA_QK + hd] = y[:, h * hd:(h + 1) * hd].astype(o_ref.dtype)
        o_ref[:, h * MLA_QK + hd:(h + 1) * MLA_QK] = r_ref[...]


def _mla_specs(m, tm, lora, col_blk):
    return [pl.BlockSpec((tm, lora), lambda i, j: (i, col_blk)),
            pl.BlockSpec((1, lora), lambda i, j: (0, 0))]


def rms_matmul(dn, col_blk, g, w, tn):
    m = dn.shape[0]
    lora, n = w.shape
    tm = min(m, 1024)
    est = 2 * tm * lora * 4 + tm * lora * 2 + 2 * lora * tn * 2 + 2 * tm * tn * 2 + tm * tn * 4
    return pl.pallas_call(
        _rms_mm_kernel,
        out_shape=jax.ShapeDtypeStruct((m, n), BF16),
        grid=(m // tm, n // tn),
        in_specs=_mla_specs(m, tm, lora, col_blk) + [pl.BlockSpec((lora, tn), lambda i, j: (0, j))],
        out_specs=pl.BlockSpec((tm, tn), lambda i, j: (i, j)),
        scratch_shapes=[pltpu.VMEM((tm, lora), BF16)],
        compiler_params=_params(("parallel", "arbitrary"), est),
        name="rms_matmul",
    )(dn, g.reshape(1, lora), w)


def mla_q(dn, g, w, cos, sa, sb, scale):
    m = dn.shape[0]
    lora, n = w.shape
    tm = min(m, 1024)
    tn = 4 * MLA_QK
    hd = HEAD_DIM
    est = 2 * tm * lora * 4 + tm * lora * 2 + 2 * lora * tn * 2 + 2 * tm * tn * 2 + 2 * tm * tn * 4 + 6 * tm * hd * 4
    rope_spec = pl.BlockSpec((tm, hd), lambda i, j: (i, 0))
    return pl.pallas_call(
        functools.partial(_mla_q_kernel, scale=scale),
        out_shape=jax.ShapeDtypeStruct((m, n), BF16),
        grid=(m // tm, n // tn),
        in_specs=_mla_specs(m, tm, lora, 0) + [pl.BlockSpec((lora, tn), lambda i, j: (0, j)),
                                               rope_spec, rope_spec, rope_spec],
        out_specs=pl.BlockSpec((tm, tn), lambda i, j: (i, j)),
        scratch_shapes=[pltpu.VMEM((tm, lora), BF16)],
        compiler_params=_params(("parallel", "arbitrary"), est),
        name="mla_q",
    )(dn, g.reshape(1, lora), w, cos, sa, sb)


def mla_k(dn, g, w, cos, sa, sb):
    m = dn.shape[0]
    lora, n = w.shape
    tm = min(m, 1024)
    hd = HEAD_DIM
    tn = 4 * hd
    kr_blk = (MLA_Q_LORA + MLA_KV_LORA) // hd
    est = 2 * tm * lora * 4 + tm * lora * 2 + 2 * lora * tn * 2 + 4 * tm * tn * 2 + tm * tn * 4 + 8 * tm * hd * 4
    rope_spec = pl.BlockSpec((tm, hd), lambda i, j: (i, 0))
    return pl.pallas_call(
        _mla_k_kernel,
        out_shape=jax.ShapeDtypeStruct((m, 2 * n), BF16),
        grid=(m // tm, n // tn),
        in_specs=_mla_specs(m, tm, lora, 1) + [pl.BlockSpec((lora, tn), lambda i, j: (0, j)),
                                               pl.BlockSpec((tm, hd), lambda i, j: (i, kr_blk)),
                                               rope_spec, rope_spec, rope_spec],
        out_specs=pl.BlockSpec((tm, 2 * tn), lambda i, j: (i, j)),
        scratch_shapes=[pltpu.VMEM((tm, lora), BF16), pltpu.VMEM((tm, hd), BF16)],
        compiler_params=_params(("parallel", "arbitrary"), est),
        name="mla_k",
    )(dn, g.reshape(1, lora), w, dn, cos, sa, sb)


def _router_kernel(x_ref, sh_ref, sc_ref, wr_ref, h_ref, aff_ref, *, row):
    hn = _layer_norm(x_ref[...])
    h = hn * (1.0 + sc_ref[row:row + 1, :]) + sh_ref[row:row + 1, :]
    hb = h.astype(BF16)
    h_ref[...] = hb
    w = wr_ref[...]
    w1 = w.astype(BF16)
    w2 = (w - w1.astype(F32)).astype(BF16)
    h2 = (h - hb.astype(F32)).astype(BF16)
    logits = _dot_nt(w1, hb) + (_dot_nt(w2, hb) + _dot_nt(w1, h2))
    mx = jnp.max(logits, axis=0, keepdims=True)
    p = jnp.exp(logits - mx)
    aff_ref[...] = p / jnp.sum(p, axis=0, keepdims=True)


def moe_router(x, mod, row, k_shift, k_scale, w_router_t):
    m, d = x.shape
    e = w_router_t.shape[0]
    tm = min(m, 512)
    return pl.pallas_call(
        functools.partial(_router_kernel, row=row),
        out_shape=(jax.ShapeDtypeStruct((m, d), BF16), jax.ShapeDtypeStruct((e, m), F32)),
        grid=(m // tm,),
        in_specs=[pl.BlockSpec((tm, d), lambda i: (i, 0)),
                  pl.BlockSpec((8, d), lambda i: (0, k_shift)),
                  pl.BlockSpec((8, d), lambda i: (0, k_scale)),
                  pl.BlockSpec((e, d), lambda i: (0, 0))],
        out_specs=(pl.BlockSpec((tm, d), lambda i: (i, 0)), pl.BlockSpec((e, tm), lambda i: (0, i))),
        compiler_params=_params(("parallel",), 8 * tm * d * 4),
        name="moe_router",
    )(x, mod, mod, w_router_t)


def _ffn_up_kernel(x_ref, wg_ref, wu_ref, o_ref):
    x = x_ref[...]
    g = _dot(x, wg_ref[...])
    u = _dot(x, wu_ref[...])
    o_ref[...] = (g * jax.nn.sigmoid(g) * u).astype(o_ref.dtype)


def _ffn_down_kernel(h_ref, wd_ref, wt_ref, o_ref):
    o_ref[...] = _dot(h_ref[...], wd_ref[...]) * wt_ref[...]


def expert_ffn(xg, wt, w_gate, w_up, w_down):
    e, r, d = xg.shape
    f = w_gate.shape[2]
    tf = min(f, 512)
    est = 2 * (r * d * 2 + 2 * d * tf * 2 + r * tf * 2) + 3 * r * tf * 4
    hid = pl.pallas_call(
        _ffn_up_kernel,
        out_shape=jax.ShapeDtypeStruct((e, r, f), BF16),
        grid=(e, f // tf),
        in_specs=[pl.BlockSpec((None, r, d), lambda i, j: (i, 0, 0)),
                  pl.BlockSpec((None, d, tf), lambda i, j: (i, 0, j)),
                  pl.BlockSpec((None, d, tf), lambda i, j: (i, 0, j))],
        out_specs=pl.BlockSpec((None, r, tf), lambda i, j: (i, 0, j)),
        compiler_params=_params(("parallel", "arbitrary"), est),
        name="ffn_up",
    )(xg, w_gate, w_up)
    tn = min(d, 512)
    est = 2 * (r * f * 2 + f * tn * 2 + r * tn * 4 + r * 128 * 4) + r * tn * 4
    return pl.pallas_call(
        _ffn_down_kernel,
        out_shape=jax.ShapeDtypeStruct((e, r, d), F32),
        grid=(e, d // tn),
        in_specs=[pl.BlockSpec((None, r, f), lambda i, j: (i, 0, 0)),
                  pl.BlockSpec((None, f, tn), lambda i, j: (i, 0, j)),
                  pl.BlockSpec((None, r, 1), lambda i, j: (i, 0, 0))],
        out_specs=pl.BlockSpec((None, r, tn), lambda i, j: (i, 0, j)),
        compiler_params=_params(("parallel", "arbitrary"), est),
        name="ffn_down",
    )(hid, w_down, wt)


def ec_moe(x, mod, row, w_router_t, w_gate, w_up, w_down):
    m, d = x.shape
    cap = max(1, EC_CAPACITY_FACTOR * m // N_EXPERTS)
    h, aff_t = moe_router(x, mod, row, 3, 4, w_router_t)
    weight, idx = lax.top_k(aff_t, cap)
    xg = jnp.take(h, idx.reshape(-1), axis=0).reshape(N_EXPERTS, cap, d)
    y = expert_ffn(xg, weight[..., None], w_gate, w_up, w_down)
    return jnp.zeros((m, d), F32).at[idx.reshape(-1)].add(y.reshape(-1, d))


def _rope_angles(n_tokens, rot_dim):
    rows = n_tokens // GRID_W
    row = jnp.repeat(jnp.arange(rows, dtype=F32), GRID_W)
    col = jnp.tile(jnp.arange(GRID_W, dtype=F32), rows)
    n_freq = rot_dim // 4
    inv = ROPE_THETA ** (-jnp.arange(n_freq, dtype=F32) / n_freq)
    return jnp.concatenate([row[:, None] * inv, col[:, None] * inv], axis=-1)


def _gqa_rope_tables(n_tokens):
    ang = _rope_angles(n_tokens, HEAD_DIM)
    c, s = jnp.cos(ang), jnp.sin(ang)
    return jnp.concatenate([c, c], axis=-1), jnp.concatenate([-s, s], axis=-1)


def _mla_rope_tables(n_tokens):
    ang = _rope_angles(n_tokens, MLA_ROPE)
    c, s = jnp.cos(ang), jnp.sin(ang)
    z = jnp.zeros_like(c)
    cos = jnp.concatenate([c, c, z, z], axis=-1)
    sa = jnp.concatenate([-s, z, z, z], axis=-1)
    sb = jnp.concatenate([z, s, z, z], axis=-1)
    return cos, sa, sb


def kernel(x, c, ctx, c_ctx, ada_w, ada_b, ln_g, ln_b, ev_w_in, ev_w_out, hgrn_lb, hgrn_norm_g, gqa_q_norm_g, gqa_k_norm_g, mla_w_down, mla_q_norm_g, mla_kv_norm_g, mla_w_uq, mla_w_ukv, mla_w_o, moe_router, moe_w_gate, moe_w_up, moe_w_down):
    d = D_MODEL
    xl = x[0]
    xc = ctx[0]
    n_lat, n_ctx = xl.shape[0], xc.shape[0]
    cc = jnp.zeros((8, d), F32).at[0].set(c[0]).at[1].set(c_ctx)
    lb_all = jnp.cumsum(jax.nn.softmax(hgrn_lb.astype(F32), axis=1), axis=1)
    gqa_tabs = _gqa_rope_tables(n_lat)
    gqa_tabs_ctx = [jnp.ones((n_ctx, HEAD_DIM), F32), jnp.zeros((n_ctx, HEAD_DIM), F32)]
    mla_tabs = _mla_rope_tables(n_lat)
    mla_tabs_ctx = [jnp.ones((n_ctx, HEAD_DIM), F32), jnp.zeros((n_ctx, HEAD_DIM), F32), jnp.zeros((n_ctx, HEAD_DIM), F32)]
    LAT, CTX = 0, 1

    for l in range(DEPTH):
        last = l == DEPTH - 1
        i = l // 2
        mod = adaln(cc, ada_w[l], ada_b[l])
        if l % 2 == 0:
            w_in = ev_w_in[i].astype(BF16)
            w_out = ev_w_out[i].astype(BF16)
            lb = lb_all[:, l].reshape(2 * A_HEADS, 1, HEAD_DIM)
            scale = HEAD_DIM ** -0.5
            proj_c = lnmod_matmul(xc, mod, CTX, 0, 1, w_in, 512)
            proj_l = lnmod_matmul(xl, mod, LAT, 0, 1, w_in, 512)
            s0 = jnp.zeros((2, A_HEADS, HEAD_DIM, HEAD_DIM), F32)
            o_c, s_c = hgrn_scan(proj_c, lb, s0)
            o_l, _ = hgrn_scan(proj_l, lb, s_c)
            a_l = hgrn_out(o_l, proj_l, hgrn_norm_g[i])
            qcol, kcol, vcol = 5 * A_HEADS, 5 * A_HEADS + B_Q_HEADS, 5 * A_WIDTH + B_WIDTH + B_KV_WIDTH
            q_l = norm_rope(proj_l, qcol, B_Q_HEADS, gqa_q_norm_g[i], *gqa_tabs, scale)
            k_l = norm_rope(proj_l, kcol, B_KV_HEADS, gqa_k_norm_g[i], *gqa_tabs, 1.0)
            k_c = norm_rope(proj_c, kcol, B_KV_HEADS, gqa_k_norm_g[i], *gqa_tabs_ctx, 1.0)
            v_l = proj_l[:, vcol:].astype(BF16)
            v_c = proj_c[:, vcol:].astype(BF16)
            att = dict(n_heads=B_Q_HEADS, n_kv_heads=B_KV_HEADS, dq=HEAD_DIM, dv=HEAD_DIM)
            b_l = flash_attention(q_l, k_l, v_l, k_c, v_c, **att)
            w_parts = [w_out[:A_WIDTH], w_out[A_WIDTH:]]
            xl_new = proj_postnorm([a_l, b_l], w_parts, xl, mod, LAT, 2, ln_g[l, 0], ln_b[l, 0])
            if not last:
                a_c = hgrn_out(o_c, proj_c, hgrn_norm_g[i])
                q_c = norm_rope(proj_c, qcol, B_Q_HEADS, gqa_q_norm_g[i], *gqa_tabs_ctx, scale)
                b_c = flash_attention(q_c, k_c, v_c, **att)
                xc = proj_postnorm([a_c, b_c], w_parts, xc, mod, CTX, 2, ln_g[l, 0], ln_b[l, 0])
            xl = xl_new
        else:
            hd = HEAD_DIM
            pad = (-mla_w_down.shape[2]) % hd
            w_down = jnp.pad(mla_w_down[i], ((0, 0), (0, pad))).astype(BF16)
            w_uq = mla_w_uq[i].reshape(MLA_Q_LORA, MLA_HEADS, hd + MLA_ROPE)
            w_uq = jnp.pad(w_uq, ((0, 0), (0, 0), (0, MLA_QK - hd - MLA_ROPE))).reshape(MLA_Q_LORA, MLA_HEADS * MLA_QK).astype(BF16)
            w_ukv = mla_w_ukv[i].reshape(MLA_KV_LORA, MLA_HEADS, 2 * hd)
            w_uk = w_ukv[:, :, :hd].reshape(MLA_KV_LORA, MLA_HEADS * hd).astype(BF16)
            w_uv = w_ukv[:, :, hd:].reshape(MLA_KV_LORA, MLA_HEADS * hd).astype(BF16)
            w_o = mla_w_o[i].astype(BF16)
            scale = (hd + MLA_ROPE) ** -0.5
            dn_c = lnmod_matmul(xc, mod, CTX, 0, 1, w_down, w_down.shape[1])
            dn_l = lnmod_matmul(xl, mod, LAT, 0, 1, w_down, w_down.shape[1])
            q_l = mla_q(dn_l, mla_q_norm_g[i], w_uq, *mla_tabs, scale)
            k_l = mla_k(dn_l, mla_kv_norm_g[i], w_uk, *mla_tabs)
            k_c = mla_k(dn_c, mla_kv_norm_g[i], w_uk, *mla_tabs_ctx)
            v_l = rms_matmul(dn_l, 1, mla_kv_norm_g[i], w_uv, 1024)
            v_c = rms_matmul(dn_c, 1, mla_kv_norm_g[i], w_uv, 1024)
            att = dict(n_heads=MLA_HEADS, n_kv_heads=MLA_HEADS, dq=MLA_QK, dv=hd)
            o_l = flash_attention(q_l, k_l, v_l, k_c, v_c, **att)
            xl_new = proj_postnorm([o_l], [w_o], xl, mod, LAT, 2, ln_g[l, 0], ln_b[l, 0])
            if not last:
                q_c = mla_q(dn_c, mla_q_norm_g[i], w_uq, *mla_tabs_ctx, scale)
                o_c = flash_attention(q_c, k_c, v_c, **att)
                xc = proj_postnorm([o_c], [w_o], xc, mod, CTX, 2, ln_g[l, 0], ln_b[l, 0])
            xl = xl_new

        w_router_t = moe_router[l].T
        w_gate, w_up, w_dn = moe_w_gate[l].astype(BF16), moe_w_up[l].astype(BF16), moe_w_down[l].astype(BF16)
        y_l = ec_moe(xl, mod, LAT, w_router_t, w_gate, w_up, w_dn)
        xl = add_postnorm(y_l, xl, mod, LAT, 5, ln_g[l, 1], ln_b[l, 1])
        if not last:
            y_c = ec_moe(xc, mod, CTX, w_router_t, w_gate, w_up, w_dn)
            xc = add_postnorm(y_c, xc, mod, CTX, 5, ln_g[l, 1], ln_b[l, 1])
    return xl[None]
```

```python
import functools
import math

import numpy as np
import jax
import jax.numpy as jnp
from jax import lax
from jax.experimental import pallas as pl
from jax.experimental.pallas import tpu as pltpu

F32 = jnp.float32
BF16 = jnp.bfloat16

D_MODEL = 2048
DEPTH = 2
GRID_W = 64
HEAD_DIM = 128
A_HEADS = D_MODEL // 256
A_WIDTH = A_HEADS * HEAD_DIM
B_Q_HEADS = D_MODEL // 256
B_KV_HEADS = 2
B_WIDTH = B_Q_HEADS * HEAD_DIM
B_KV_WIDTH = B_KV_HEADS * HEAD_DIM
MLA_HEADS = D_MODEL // 128
MLA_Q_LORA = 512
MLA_KV_LORA = 512
MLA_ROPE = 64
MLA_QK = 2 * HEAD_DIM
N_EXPERTS = 16
EXPERT_FF = D_MODEL // 2
EC_CAPACITY_FACTOR = 2
ROPE_THETA = 10000.0
NORM_EPS = 1e-6
DEEPNORM_ALPHA = (2.0 * DEPTH) ** 0.25

HGRN_CHUNK = 128
LANES = 128
LOG2E = math.log2(math.e)
V7X_VMEM_BYTES = 64 * 1024 * 1024
VMEM_CAP_BYTES = V7X_VMEM_BYTES - 8 * 1024 * 1024


def _params(semantics, vmem_estimate_bytes):
    limit = int(min(max(2 * vmem_estimate_bytes, 32 * 1024 * 1024), VMEM_CAP_BYTES))
    return pltpu.CompilerParams(dimension_semantics=semantics, vmem_limit_bytes=limit)


def _layer_norm(x):
    mu = jnp.mean(x, axis=-1, keepdims=True)
    xc = x - mu
    var = jnp.mean(xc * xc, axis=-1, keepdims=True)
    return xc * lax.rsqrt(var + NORM_EPS)


def _rms(x):
    return x * lax.rsqrt(jnp.mean(x * x, axis=-1, keepdims=True) + NORM_EPS)


def _dot(a, b):
    return jnp.dot(a, b, preferred_element_type=F32)


def _dot_nt(a, b):
    return lax.dot_general(a, b, (((1,), (1,)), ((), ())), preferred_element_type=F32)


def _dot_tn(a, b):
    return lax.dot_general(a, b, (((0,), (0,)), ((), ())), preferred_element_type=F32)


def _split3(x):
    x1 = x.astype(BF16)
    r1 = x - x1.astype(F32)
    x2 = r1.astype(BF16)
    x3 = (r1 - x2.astype(F32)).astype(BF16)
    return x1, x2, x3


def _adaln_kernel(c_ref, w_ref, b_ref, o_ref):
    c = c_ref[...]
    s = c * jax.nn.sigmoid(c)
    w = w_ref[...]
    s1, s2, s3 = _split3(s)
    w1, w2, w3 = _split3(w)
    acc = _dot(s1, w3) + _dot(s3, w1) + _dot(s2, w2)
    acc = acc + _dot(s1, w2) + _dot(s2, w1)
    acc = acc + _dot(s1, w1)
    o_ref[...] = acc + b_ref[...]


def adaln(cc, w, b):
    d, n = w.shape
    tn = 1536 if n % 1536 == 0 else n
    est = 2 * d * tn * 4 * 2
    return pl.pallas_call(
        _adaln_kernel,
        out_shape=jax.ShapeDtypeStruct((8, n), F32),
        grid=(n // tn,),
        in_specs=[pl.BlockSpec((8, d), lambda j: (0, 0)),
                  pl.BlockSpec((d, tn), lambda j: (0, j)),
                  pl.BlockSpec((1, tn), lambda j: (0, j))],
        out_specs=pl.BlockSpec((8, tn), lambda j: (0, j)),
        compiler_params=_params(("parallel",), est),
        name="adaln",
    )(cc, w, b.reshape(1, n))


def _lnmod_mm_kernel(x_ref, sh_ref, sc_ref, w_ref, o_ref, h_ref, *, row):
    @pl.when(pl.program_id(1) == 0)
    def _():
        hn = _layer_norm(x_ref[...])
        h = hn * (1.0 + sc_ref[row:row + 1, :]) + sh_ref[row:row + 1, :]
        h_ref[...] = h.astype(BF16)

    o_ref[...] = _dot(h_ref[...], w_ref[...]).astype(o_ref.dtype)


def lnmod_matmul(x, mod, row, k_shift, k_scale, w, tn):
    m, d = x.shape
    n = w.shape[1]
    tm = min(m, 1024)
    est = 2 * tm * d * 4 + tm * d * 2 + 2 * d * tn * 2 + 2 * tm * tn * 4
    return pl.pallas_call(
        functools.partial(_lnmod_mm_kernel, row=row),
        out_shape=jax.ShapeDtypeStruct((m, n), F32),
        grid=(m // tm, n // tn),
        in_specs=[pl.BlockSpec((tm, d), lambda i, j: (i, 0)),
                  pl.BlockSpec((8, d), lambda i, j: (0, k_shift)),
                  pl.BlockSpec((8, d), lambda i, j: (0, k_scale)),
                  pl.BlockSpec((d, tn), lambda i, j: (0, j))],
        out_specs=pl.BlockSpec((tm, tn), lambda i, j: (i, j)),
        scratch_shapes=[pltpu.VMEM((tm, d), BF16)],
        compiler_params=_params(("parallel", "arbitrary"), est),
        name="lnmod_matmul",
    )(x, mod, mod, w)


def _hgrn_tables(c):
    n_lvl = int(math.log2(c))
    r = np.arange(c)
    u = np.arange(c)[None, :]
    blocks, masks = [], []
    for l in range(n_lvl):
        half = 1 << l
        base = (r // (2 * half)) * (2 * half)
        anchor = (base + half - 1)[:, None]
        upper = (r >= base + half)[:, None]
        rr = r[:, None]
        blocks.append(np.where(upper, (u > anchor) & (u <= rr), (u > rr) & (u <= anchor)))
        same = (r[:, None] // (2 * half)) == (r[None, :] // (2 * half))
        masks.append(same & upper & ~(upper.T))
    blocks.append(u <= r[:, None])
    blocks.append(u > r[:, None])
    blocks.append(np.ones((16, c), bool))
    masks.append(np.eye(c, dtype=bool))
    fwd_s = np.concatenate(blocks, axis=0).astype(np.float32)
    fwd_m = np.stack(masks).astype(np.float32)
    bwd_s = np.concatenate([b[::-1, ::-1] for b in blocks], axis=0).astype(np.float32)
    bwd_m = fwd_m[:, ::-1, ::-1]
    return (jnp.asarray(np.stack([fwd_s, bwd_s]), BF16), jnp.asarray(np.stack([fwd_m, bwd_m]), F32))


def _hgrn_kernel(q_ref, v_ref, f_ref, lb_ref, sums_ref, mask_ref, s0_ref, o_ref, sfin_ref, st_ref):
    c = q_ref.shape[0]
    hd = HEAD_DIM
    n_lvl = mask_ref.shape[0] - 1
    j = pl.program_id(1)

    @pl.when(j == 0)
    def _():
        st_ref[...] = s0_ref[...]

    for h in range(q_ref.shape[1] // hd):
        cols = slice(h * hd, (h + 1) * hd)
        q = q_ref[:, cols]
        vb = v_ref[:, cols].astype(BF16)
        lb = lb_ref[:, cols]
        f = lb + (1.0 - lb) * jax.nn.sigmoid(f_ref[:, cols])
        g = jnp.log(f)
        k = 1.0 - f
        g1 = g.astype(BF16)
        g2 = (g - g1.astype(F32)).astype(BF16)
        e2 = _dot(sums_ref[...], jnp.concatenate([g1, g2], axis=1))
        e = e2[:, hd:] + e2[:, :hd]

        scores = _dot_nt(q.astype(BF16), k.astype(BF16)) * mask_ref[n_lvl]
        for l in range(n_lvl):
            z = jnp.exp(e[l * c:(l + 1) * c])
            scores = scores + _dot_nt((q * z).astype(BF16), (k * z).astype(BF16)) * mask_ref[l]

        cum = e[n_lvl * c:(n_lvl + 1) * c]
        rem = e[(n_lvl + 1) * c:(n_lvl + 2) * c]
        tot = e[(n_lvl + 2) * c:(n_lvl + 2) * c + 1]
        st = st_ref[h]
        o = _dot(scores.astype(BF16), vb) + _dot_nt((q * jnp.exp(cum)).astype(BF16), st.astype(BF16))
        o_ref[:, cols] = o
        st_new = st * jnp.exp(tot) + _dot_tn(vb, (k * jnp.exp(rem)).astype(BF16))
        st_ref[h] = st_new

    @pl.when(j == pl.num_programs(1) - 1)
    def _():
        sfin_ref[...] = st_ref[...]


def hgrn_scan(proj, lb, s0):
    seq = proj.shape[0]
    c = HGRN_CHUNK
    nc = seq // c
    sums, masks = _hgrn_tables(c)
    hd, w = HEAD_DIM, A_WIDTH

    def blk(d, j):
        return jnp.where(d == 0, j, nc - 1 - j)

    est = (2 * (4 * c * w * 4 + sums.shape[1] * c * 2 + masks.shape[1] * c * c * 4 + 2 * A_HEADS * hd * hd * 4)
           + A_HEADS * hd * hd * 4)
    return pl.pallas_call(
        _hgrn_kernel,
        out_shape=(jax.ShapeDtypeStruct((2, seq, w), F32),
                   jax.ShapeDtypeStruct((2, A_HEADS, hd, hd), F32)),
        grid=(2, nc),
        in_specs=[pl.BlockSpec((c, w), lambda d, j: (blk(d, j), 0)),
                  pl.BlockSpec((c, w), lambda d, j: (blk(d, j), 3)),
                  pl.BlockSpec((c, w), lambda d, j: (blk(d, j), 1 + d)),
                  pl.BlockSpec((None, 1, w), lambda d, j: (d, 0, 0)),
                  pl.BlockSpec((None, sums.shape[1], c), lambda d, j: (d, 0, 0)),
                  pl.BlockSpec((None, masks.shape[1], c, c), lambda d, j: (d, 0, 0, 0)),
                  pl.BlockSpec((None, A_HEADS, hd, hd), lambda d, j: (d, 0, 0, 0))],
        out_specs=(pl.BlockSpec((None, c, w), lambda d, j: (d, blk(d, j), 0)),
                   pl.BlockSpec((None, A_HEADS, hd, hd), lambda d, j: (d, 0, 0, 0))),
        scratch_shapes=[pltpu.VMEM((A_HEADS, hd, hd), F32)],
        compiler_params=_params(("parallel", "arbitrary"), est),
        name="hgrn_scan",
    )(proj, proj, proj, lb, sums, masks, s0)


def _hgrn_out_kernel(o_ref, gate_ref, g_ref, a_ref):
    o = o_ref[0] + o_ref[1]
    gate = gate_ref[...]
    a_ref[...] = (_rms(o) * g_ref[...] * (gate * jax.nn.sigmoid(gate))).astype(a_ref.dtype)


def hgrn_out(o, proj, norm_g):
    seq = o.shape[1]
    tm = min(seq, 512)
    hd = HEAD_DIM
    return pl.pallas_call(
        _hgrn_out_kernel,
        out_shape=jax.ShapeDtypeStruct((seq, A_WIDTH), BF16),
        grid=(seq // tm, A_HEADS),
        in_specs=[pl.BlockSpec((2, tm, hd), lambda i, h: (0, i, h)),
                  pl.BlockSpec((tm, hd), lambda i, h: (i, 4 * A_HEADS + h)),
                  pl.BlockSpec((1, hd), lambda i, h: (0, 0))],
        out_specs=pl.BlockSpec((tm, hd), lambda i, h: (i, h)),
        compiler_params=_params(("parallel", "parallel"), 8 * tm * hd * 4),
        name="hgrn_out",
    )(o, proj, norm_g.reshape(1, hd))


def _norm_rope_kernel(x_ref, g_ref, cos_ref, sin_ref, o_ref, *, scale):
    y = _rms(x_ref[...]) * g_ref[...]
    y = y * cos_ref[...] + pltpu.roll(y, HEAD_DIM // 2, 1) * sin_ref[...]
    o_ref[...] = (y * scale).astype(o_ref.dtype)


def norm_rope(proj, col0, n_heads, g, cos, sin, scale):
    seq = proj.shape[0]
    tm = min(seq, 512)
    hd = HEAD_DIM
    return pl.pallas_call(
        functools.partial(_norm_rope_kernel, scale=scale),
        out_shape=jax.ShapeDtypeStruct((seq, n_heads * hd), BF16),
        grid=(seq // tm, n_heads),
        in_specs=[pl.BlockSpec((tm, hd), lambda i, h: (i, col0 + h)),
                  pl.BlockSpec((1, hd), lambda i, h: (0, 0)),
                  pl.BlockSpec((tm, hd), lambda i, h: (i, 0)),
                  pl.BlockSpec((tm, hd), lambda i, h: (i, 0))],
        out_specs=pl.BlockSpec((tm, hd), lambda i, h: (i, h)),
        compiler_params=_params(("parallel", "parallel"), 10 * tm * hd * 4),
        name="norm_rope",
    )(proj, g.reshape(1, hd), cos, sin)


def _flash_update(q, k, v, m_ref, l_ref, acc_ref):
    s = _dot_nt(q, k)
    m_prev = m_ref[...]
    m_new = jnp.maximum(m_prev, jnp.max(s, axis=-1, keepdims=True))
    alpha = jnp.exp2(m_prev - m_new)
    ps = [jnp.exp2(s[:, c * LANES:(c + 1) * LANES] - m_new) for c in range(s.shape[1] // LANES)]
    psum = ps[0]
    for pc in ps[1:]:
        psum = psum + pc
    p = jnp.concatenate([pc.astype(BF16) for pc in ps], axis=1)
    l_ref[...] = alpha * l_ref[...] + psum
    acc_ref[...] = alpha * acc_ref[...] + _dot(p, v)
    m_ref[...] = m_new


def _flash_kernel(*refs, has_ctx):
    if has_ctx:
        q_ref, k_ref, v_ref, kc_ref, vc_ref, o_ref, m_ref, l_ref, acc_ref = refs
    else:
        q_ref, k_ref, v_ref, o_ref, m_ref, l_ref, acc_ref = refs
    j = pl.program_id(2)

    @pl.when(j == 0)
    def _():
        m_ref[...] = jnp.full(m_ref.shape, -jnp.inf, F32)
        l_ref[...] = jnp.zeros(l_ref.shape, F32)
        acc_ref[...] = jnp.zeros(acc_ref.shape, F32)
        if has_ctx:
            _flash_update(q_ref[...], kc_ref[...], vc_ref[...], m_ref, l_ref, acc_ref)

    _flash_update(q_ref[...], k_ref[...], v_ref[...], m_ref, l_ref, acc_ref)

    @pl.when(j == pl.num_programs(2) - 1)
    def _():
        l = jnp.sum(l_ref[...], axis=-1, keepdims=True)
        o_ref[...] = (acc_ref[...] / l).astype(o_ref.dtype)


def flash_attention(q, k, v, k_ctx=None, v_ctx=None, *, n_heads, n_kv_heads, dq, dv):
    n, m = q.shape[0], k.shape[0]
    grp = n_heads // n_kv_heads
    tq = min(n, 1024)
    tk = min(m, 1024)
    has_ctx = k_ctx is not None
    in_specs = [pl.BlockSpec((tq, dq), lambda h, i, j: (i, h)),
                pl.BlockSpec((tk, dq), lambda h, i, j: (j, h // grp)),
                pl.BlockSpec((tk, dv), lambda h, i, j: (j, h // grp))]
    args = [q, k, v]
    if has_ctx:
        mc = k_ctx.shape[0]
        in_specs += [pl.BlockSpec((mc, dq), lambda h, i, j: (0, h // grp)),
                     pl.BlockSpec((mc, dv), lambda h, i, j: (0, h // grp))]
        args += [k_ctx, v_ctx]
    est = 2 * (tq * dq + tk * dq + tk * dv + tq * dv) * 2 + tq * (dv + 256) * 4 + 6 * tq * tk * 4
    return pl.pallas_call(
        functools.partial(_flash_kernel, has_ctx=has_ctx),
        out_shape=jax.ShapeDtypeStruct((n, n_heads * dv), BF16),
        grid=(n_heads, n // tq, m // tk),
        in_specs=in_specs,
        out_specs=pl.BlockSpec((tq, dv), lambda h, i, j: (i, h)),
        scratch_shapes=[pltpu.VMEM((tq, LANES), F32), pltpu.VMEM((tq, LANES), F32), pltpu.VMEM((tq, dv), F32)],
        compiler_params=_params(("parallel", "parallel", "arbitrary"), est),
        name="flash_attention",
    )(*args)


def _proj_postnorm_kernel(*refs, n_in, row):
    a_refs = refs[:n_in]
    w_refs = refs[n_in:2 * n_in]
    x_ref, gate_ref, g_ref, b_ref, o_ref = refs[2 * n_in:]
    y = _dot(a_refs[0][...], w_refs[0][...])
    for a_ref, w_ref in zip(a_refs[1:], w_refs[1:]):
        y = y + _dot(a_ref[...], w_ref[...])
    z = DEEPNORM_ALPHA * x_ref[...] + gate_ref[row:row + 1, :] * y
    o_ref[...] = _layer_norm(z) * g_ref[...] + b_ref[...]


def proj_postnorm(acts, ws, x, mod, row, k_gate, g, b):
    m, d = x.shape
    tm = min(m, 512)
    n_in = len(acts)
    once = pl.Buffered(1)
    in_specs = [pl.BlockSpec((tm, a.shape[1]), lambda i: (i, 0)) for a in acts]
    in_specs += [pl.BlockSpec(w.shape, lambda i: (0, 0), pipeline_mode=once) for w in ws]
    in_specs += [pl.BlockSpec((tm, d), lambda i: (i, 0)),
                 pl.BlockSpec((8, d), lambda i: (0, k_gate)),
                 pl.BlockSpec((1, d), lambda i: (0, 0)),
                 pl.BlockSpec((1, d), lambda i: (0, 0))]
    est = sum(w.size * 2 for w in ws) + sum(2 * tm * a.shape[1] * 2 for a in acts) + 6 * tm * d * 4
    return pl.pallas_call(
        functools.partial(_proj_postnorm_kernel, n_in=n_in, row=row),
        out_shape=jax.ShapeDtypeStruct((m, d), F32),
        grid=(m // tm,),
        in_specs=in_specs,
        out_specs=pl.BlockSpec((tm, d), lambda i: (i, 0)),
        compiler_params=_params(("parallel",), est),
        name="proj_postnorm",
    )(*acts, *ws, x, mod, g.reshape(1, d), b.reshape(1, d))


def _add_postnorm_kernel(y_ref, x_ref, gate_ref, g_ref, b_ref, o_ref, *, row):
    z = DEEPNORM_ALPHA * x_ref[...] + gate_ref[row:row + 1, :] * y_ref[...]
    o_ref[...] = _layer_norm(z) * g_ref[...] + b_ref[...]


def add_postnorm(y, x, mod, row, k_gate, g, b):
    m, d = x.shape
    tm = min(m, 512)
    return pl.pallas_call(
        functools.partial(_add_postnorm_kernel, row=row),
        out_shape=jax.ShapeDtypeStruct((m, d), F32),
        grid=(m // tm,),
        in_specs=[pl.BlockSpec((tm, d), lambda i: (i, 0)),
                  pl.BlockSpec((tm, d), lambda i: (i, 0)),
                  pl.BlockSpec((8, d), lambda i: (0, k_gate)),
                  pl.BlockSpec((1, d), lambda i: (0, 0)),
                  pl.BlockSpec((1, d), lambda i: (0, 0))],
        out_specs=pl.BlockSpec((tm, d), lambda i: (i, 0)),
        compiler_params=_params(("parallel",), 8 * tm * d * 4),
        name="add_postnorm",
    )(y, x, mod, g.reshape(1, d), b.reshape(1, d))


def _rms_mm_kernel(x_ref, g_ref, w_ref, o_ref, a_ref):
    @pl.when(pl.program_id(1) == 0)
    def _():
        a_ref[...] = (_rms(x_ref[...]) * g_ref[...]).astype(BF16)

    o_ref[...] = _dot(a_ref[...], w_ref[...]).astype(o_ref.dtype)


def _mla_q_kernel(x_ref, g_ref, w_ref, cos_ref, sa_ref, sb_ref, o_ref, a_ref, *, scale):
    @pl.when(pl.program_id(1) == 0)
    def _():
        a_ref[...] = (_rms(x_ref[...]) * g_ref[...]).astype(BF16)

    y = _dot(a_ref[...], w_ref[...])
    hd = HEAD_DIM
    for h in range(y.shape[1] // MLA_QK):
        c0 = h * MLA_QK
        o_ref[:, c0:c0 + hd] = (y[:, c0:c0 + hd] * scale).astype(o_ref.dtype)
        r = y[:, c0 + hd:c0 + 2 * hd]
        r = r * cos_ref[...] + pltpu.roll(r, hd - MLA_ROPE // 2, 1) * sa_ref[...] + pltpu.roll(r, MLA_ROPE // 2, 1) * sb_ref[...]
        o_ref[:, c0 + hd:c0 + 2 * hd] = (r * scale).astype(o_ref.dtype)


def _mla_k_kernel(x_ref, g_ref, w_ref, kr_ref, cos_ref, sa_ref, sb_ref, o_ref, a_ref, r_ref):
    hd = HEAD_DIM

    @pl.when(pl.program_id(1) == 0)
    def _():
        a_ref[...] = (_rms(x_ref[...]) * g_ref[...]).astype(BF16)
        r = kr_ref[...]
        r = r * cos_ref[...] + pltpu.roll(r, hd - MLA_ROPE // 2, 1) * sa_ref[...] + pltpu.roll(r, MLA_ROPE // 2, 1) * sb_ref[...]
        r_ref[...] = r.astype(BF16)

    y = _dot(a_ref[...], w_ref[...])
    for h in range(y.shape[1] // hd):
        o_ref[:, h * MLA_QK:h * MLA_QK + hd] = y[:, h * hd:(h + 1) * hd].astype(o_ref.dtype)
        o_ref[:, h * MLA_QK + hd:(h + 1) * MLA_QK] = r_ref[...]


def _mla_specs(m, tm, lora, col_blk):
    return [pl.BlockSpec((tm, lora), lambda i, j: (i, col_blk)),
            pl.BlockSpec((1, lora), lambda i, j: (0, 0))]


def rms_matmul(dn, col_blk, g, w, tn):
    m = dn.shape[0]
    lora, n = w.shape
    tm = min(m, 1024)
    est = 2 * tm * lora * 4 + tm * lora * 2 + 2 * lora * tn * 2 + 2 * tm * tn * 2 + tm * tn * 4
    return pl.pallas_call(
        _rms_mm_kernel,
        out_shape=jax.ShapeDtypeStruct((m, n), BF16),
        grid=(m // tm, n // tn),
        in_specs=_mla_specs(m, tm, lora, col_blk) + [pl.BlockSpec((lora, tn), lambda i, j: (0, j))],
        out_specs=pl.BlockSpec((tm, tn), lambda i, j: (i, j)),
        scratch_shapes=[pltpu.VMEM((tm, lora), BF16)],
        compiler_params=_params(("parallel", "arbitrary"), est),
        name="rms_matmul",
    )(dn, g.reshape(1, lora), w)


def mla_q(dn, g, w, cos, sa, sb, scale):
    m = dn.shape[0]
    lora, n = w.shape
    tm = min(m, 1024)
    tn = 4 * MLA_QK
    hd = HEAD_DIM
    est = 2 * tm * lora * 4 + tm * lora * 2 + 2 * lora * tn * 2 + 2 * tm * tn * 2 + 2 * tm * tn * 4 + 6 * tm * hd * 4
    rope_spec = pl.BlockSpec((tm, hd), lambda i, j: (i, 0))
    return pl.pallas_call(
        functools.partial(_mla_q_kernel, scale=scale),
        out_shape=jax.ShapeDtypeStruct((m, n), BF16),
        grid=(m // tm, n // tn),
        in_specs=_mla_specs(m, tm, lora, 0) + [pl.BlockSpec((lora, tn), lambda i, j: (0, j)),
                                               rope_spec, rope_spec, rope_spec],
        out_specs=pl.BlockSpec((tm, tn), lambda i, j: (i, j)),
        scratch_shapes=[pltpu.VMEM((tm, lora), BF16)],
        compiler_params=_params(("parallel", "arbitrary"), est),
        name="mla_q",
    )(dn, g.reshape(1, lora), w, cos, sa, sb)


def mla_k(dn, g, w, cos, sa, sb):
    m = dn.shape[0]
    lora, n = w.shape
    tm = min(m, 1024)
    hd = HEAD_DIM
    tn = 4 * hd
    kr_blk = (MLA_Q_LORA + MLA_KV_LORA) // hd
    est = 2 * tm * lora * 4 + tm * lora * 2 + 2 * lora * tn * 2 + 4 * tm * tn * 2 + tm * tn * 4 + 8 * tm * hd * 4
    rope_spec = pl.BlockSpec((tm, hd), lambda i, j: (i, 0))
    return pl.pallas_call(
        _mla_k_kernel,
        out_shape=jax.ShapeDtypeStruct((m, 2 * n), BF16),
        grid=(m // tm, n // tn),
        in_specs=_mla_specs(m, tm, lora, 1) + [pl.BlockSpec((lora, tn), lambda i, j: (0, j)),
                                               pl.BlockSpec((tm, hd), lambda i, j: (i, kr_blk)),
                                               rope_spec, rope_spec, rope_spec],
        out_specs=pl.BlockSpec((tm, 2 * tn), lambda i, j: (i, j)),
        scratch_shapes=[pltpu.VMEM((tm, lora), BF16), pltpu.VMEM((tm, hd), BF16)],
        compiler_params=_params(("parallel", "arbitrary"), est),
        name="mla_k",
    )(dn, g.reshape(1, lora), w, dn, cos, sa, sb)


def _router_kernel(x_ref, sh_ref, sc_ref, wr_ref, h_ref, aff_ref, *, row):
    hn = _layer_norm(x_ref[...])
    h = hn * (1.0 + sc_ref[row:row + 1, :]) + sh_ref[row:row + 1, :]
    hb = h.astype(BF16)
    h_ref[...] = hb
    w = wr_ref[...]
    w1 = w.astype(BF16)
    w2 = (w - w1.astype(F32)).astype(BF16)
    h2 = (h - hb.astype(F32)).astype(BF16)
    logits = _dot_nt(w1, hb) + (_dot_nt(w2, hb) + _dot_nt(w1, h2))
    mx = jnp.max(logits, axis=0, keepdims=True)
    p = jnp.exp(logits - mx)
    aff_ref[...] = p / jnp.sum(p, axis=0, keepdims=True)


def moe_router(x, mod, row, k_shift, k_scale, w_router_t):
    m, d = x.shape
    e = w_router_t.shape[0]
    tm = min(m, 512)
    return pl.pallas_call(
        functools.partial(_router_kernel, row=row),
        out_shape=(jax.ShapeDtypeStruct((m, d), BF16), jax.ShapeDtypeStruct((e, m), F32)),
        grid=(m // tm,),
        in_specs=[pl.BlockSpec((tm, d), lambda i: (i, 0)),
                  pl.BlockSpec((8, d), lambda i: (0, k_shift)),
                  pl.BlockSpec((8, d), lambda i: (0, k_scale)),
                  pl.BlockSpec((e, d), lambda i: (0, 0))],
        out_specs=(pl.BlockSpec((tm, d), lambda i: (i, 0)), pl.BlockSpec((e, tm), lambda i: (0, i))),
        compiler_params=_params(("parallel",), 8 * tm * d * 4),
        name="moe_router",
    )(x, mod, mod, w_router_t)


def _ffn_up_kernel(x_ref, wg_ref, wu_ref, o_ref):
    x = x_ref[...]
    g = _dot(x, wg_ref[...])
    u = _dot(x, wu_ref[...])
    o_ref[...] = (g * jax.nn.sigmoid(g) * u).astype(o_ref.dtype)


def _ffn_down_kernel(h_ref, wd_ref, wt_ref, o_ref):
    o_ref[...] = _dot(h_ref[...], wd_ref[...]) * wt_ref[...]


def expert_ffn(xg, wt, w_gate, w_up, w_down):
    e, r, d = xg.shape
    f = w_gate.shape[2]
    tf = min(f, 512)
    est = 2 * (r * d * 2 + 2 * d * tf * 2 + r * tf * 2) + 3 * r * tf * 4
    hid = pl.pallas_call(
        _ffn_up_kernel,
        out_shape=jax.ShapeDtypeStruct((e, r, f), BF16),
        grid=(e, f // tf),
        in_specs=[pl.BlockSpec((None, r, d), lambda i, j: (i, 0, 0)),
                  pl.BlockSpec((None, d, tf), lambda i, j: (i, 0, j)),
                  pl.BlockSpec((None, d, tf), lambda i, j: (i, 0, j))],
        out_specs=pl.BlockSpec((None, r, tf), lambda i, j: (i, 0, j)),
        compiler_params=_params(("parallel", "arbitrary"), est),
        name="ffn_up",
    )(xg, w_gate, w_up)
    tn = min(d, 512)
    est = 2 * (r * f * 2 + f * tn * 2 + r * tn * 4 + r * 128 * 4) + r * tn * 4
    return pl.pallas_call(
        _ffn_down_kernel,
        out_shape=jax.ShapeDtypeStruct((e, r, d), F32),
        grid=(e, d // tn),
        in_specs=[pl.BlockSpec((None, r, f), lambda i, j: (i, 0, 0)),
                  pl.BlockSpec((None, f, tn), lambda i, j: (i, 0, j)),
                  pl.BlockSpec((None, r, 1), lambda i, j: (i, 0, 0))],
        out_specs=pl.BlockSpec((None, r, tn), lambda i, j: (i, 0, j)),
        compiler_params=_params(("parallel", "arbitrary"), est),
        name="ffn_down",
    )(hid, w_down, wt)


def ec_moe(x, mod, row, w_router_t, w_gate, w_up, w_down):
    m, d = x.shape
    cap = max(1, EC_CAPACITY_FACTOR * m // N_EXPERTS)
    h, aff_t = moe_router(x, mod, row, 3, 4, w_router_t)
    weight, idx = lax.top_k(aff_t, cap)
    xg = jnp.take(h, idx.reshape(-1), axis=0).reshape(N_EXPERTS, cap, d)
    y = expert_ffn(xg, weight[..., None], w_gate, w_up, w_down)
    return jnp.zeros((m, d), F32).at[idx.reshape(-1)].add(y.reshape(-1, d))


def _rope_angles(n_tokens, rot_dim):
    rows = n_tokens // GRID_W
    row = jnp.repeat(jnp.arange(rows, dtype=F32), GRID_W)
    col = jnp.tile(jnp.arange(GRID_W, dtype=F32), rows)
    n_freq = rot_dim // 4
    inv = ROPE_THETA ** (-jnp.arange(n_freq, dtype=F32) / n_freq)
    return jnp.concatenate([row[:, None] * inv, col[:, None] * inv], axis=-1)


def _gqa_rope_tables(n_tokens):
    ang = _rope_angles(n_tokens, HEAD_DIM)
    c, s = jnp.cos(ang), jnp.sin(ang)
    return jnp.concatenate([c, c], axis=-1), jnp.concatenate([-s, s], axis=-1)


def _mla_rope_tables(n_tokens):
    ang = _rope_angles(n_tokens, MLA_ROPE)
    c, s = jnp.cos(ang), jnp.sin(ang)
    z = jnp.zeros_like(c)
    cos = jnp.concatenate([c, c, z, z], axis=-1)
    sa = jnp.concatenate([-s, z, z, z], axis=-1)
    sb = jnp.concatenate([z, s, z, z], axis=-1)
    return cos, sa, sb


def kernel(x, c, ctx, c_ctx, ada_w, ada_b, ln_g, ln_b, ev_w_in, ev_w_out, hgrn_lb, hgrn_norm_g, gqa_q_norm_g, gqa_k_norm_g, mla_w_down, mla_q_norm_g, mla_kv_norm_g, mla_w_uq, mla_w_ukv, mla_w_o, moe_router, moe_w_gate, moe_w_up, moe_w_down):
    d = D_MODEL
    xl = x[0]
    xc = ctx[0]
    n_lat, n_ctx = xl.shape[0], xc.shape[0]
    cc = jnp.zeros((8, d), F32).at[0].set(c[0]).at[1].set(c_ctx)
    lb_all = jnp.cumsum(jax.nn.softmax(hgrn_lb.astype(F32), axis=1), axis=1)
    gqa_tabs = _gqa_rope_tables(n_lat)
    gqa_tabs_ctx = [jnp.ones((n_ctx, HEAD_DIM), F32), jnp.zeros((n_ctx, HEAD_DIM), F32)]
    mla_tabs = _mla_rope_tables(n_lat)
    mla_tabs_ctx = [jnp.ones((n_ctx, HEAD_DIM), F32), jnp.zeros((n_ctx, HEAD_DIM), F32), jnp.zeros((n_ctx, HEAD_DIM), F32)]
    LAT, CTX = 0, 1

    for l in range(DEPTH):
        last = l == DEPTH - 1
        i = l // 2
        mod = adaln(cc, ada_w[l], ada_b[l])
        if l % 2 == 0:
            w_in = ev_w_in[i].astype(BF16)
            w_out = ev_w_out[i].astype(BF16)
            lb = lb_all[:, l].reshape(2, 1, A_WIDTH)
            scale = HEAD_DIM ** -0.5 * LOG2E
            proj_c = lnmod_matmul(xc, mod, CTX, 0, 1, w_in, 512)
            proj_l = lnmod_matmul(xl, mod, LAT, 0, 1, w_in, 512)
            s0 = jnp.zeros((2, A_HEADS, HEAD_DIM, HEAD_DIM), F32)
            o_c, s_c = hgrn_scan(proj_c, lb, s0)
            o_l, _ = hgrn_scan(proj_l, lb, s_c)
            a_l = hgrn_out(o_l, proj_l, hgrn_norm_g[i])
            qcol, kcol, vcol = 5 * A_HEADS, 5 * A_HEADS + B_Q_HEADS, 5 * A_WIDTH + B_WIDTH + B_KV_WIDTH
            q_l = norm_rope(proj_l, qcol, B_Q_HEADS, gqa_q_norm_g[i], *gqa_tabs, scale)
            k_l = norm_rope(proj_l, kcol, B_KV_HEADS, gqa_k_norm_g[i], *gqa_tabs, 1.0)
            k_c = norm_rope(proj_c, kcol, B_KV_HEADS, gqa_k_norm_g[i], *gqa_tabs_ctx, 1.0)
            v_l = proj_l[:, vcol:].astype(BF16)
            v_c = proj_c[:, vcol:].astype(BF16)
            att = dict(n_heads=B_Q_HEADS, n_kv_heads=B_KV_HEADS, dq=HEAD_DIM, dv=HEAD_DIM)
            b_l = flash_attention(q_l, k_l, v_l, k_c, v_c, **att)
            w_parts = [w_out[:A_WIDTH], w_out[A_WIDTH:]]
            xl_new = proj_postnorm([a_l, b_l], w_parts, xl, mod, LAT, 2, ln_g[l, 0], ln_b[l, 0])
            if not last:
                a_c = hgrn_out(o_c, proj_c, hgrn_norm_g[i])
                q_c = norm_rope(proj_c, qcol, B_Q_HEADS, gqa_q_norm_g[i], *gqa_tabs_ctx, scale)
                b_c = flash_attention(q_c, k_c, v_c, **att)
                xc = proj_postnorm([a_c, b_c], w_parts, xc, mod, CTX, 2, ln_g[l, 0], ln_b[l, 0])
            xl = xl_new
        else:
            hd = HEAD_DIM
            pad = (-mla_w_down.shape[2]) % hd
            w_down = jnp.pad(mla_w_down[i], ((0, 0), (0, pad))).astype(BF16)
            w_uq = mla_w_uq[i].reshape(MLA_Q_LORA, MLA_HEADS, hd + MLA_ROPE)
            w_uq = jnp.pad(w_uq, ((0, 0), (0, 0), (0, MLA_QK - hd - MLA_ROPE))).reshape(MLA_Q_LORA, MLA_HEADS * MLA_QK).astype(BF16)
            w_ukv = mla_w_ukv[i].reshape(MLA_KV_LORA, MLA_HEADS, 2 * hd)
            w_uk = w_ukv[:, :, :hd].reshape(MLA_KV_LORA, MLA_HEADS * hd).astype(BF16)
            w_uv = w_ukv[:, :, hd:].reshape(MLA_KV_LORA, MLA_HEADS * hd).astype(BF16)
            w_o = mla_w_o[i].astype(BF16)
            scale = (hd + MLA_ROPE) ** -0.5 * LOG2E
            dn_c = lnmod_matmul(xc, mod, CTX, 0, 1, w_down, w_down.shape[1])
            dn_l = lnmod_matmul(xl, mod, LAT, 0, 1, w_down, w_down.shape[1])
            q_l = mla_q(dn_l, mla_q_norm_g[i], w_uq, *mla_tabs, scale)
            k_l = mla_k(dn_l, mla_kv_norm_g[i], w_uk, *mla_tabs)
            k_c = mla_k(dn_c, mla_kv_norm_g[i], w_uk, *mla_tabs_ctx)
            v_l = rms_matmul(dn_l, 1, mla_kv_norm_g[i], w_uv, 1024)
            v_c = rms_matmul(dn_c, 1, mla_kv_norm_g[i], w_uv, 1024)
            att = dict(n_heads=MLA_HEADS, n_kv_heads=MLA_HEADS, dq=MLA_QK, dv=hd)
            o_l = flash_attention(q_l, k_l, v_l, k_c, v_c, **att)
            xl_new = proj_postnorm([o_l], [w_o], xl, mod, LAT, 2, ln_g[l, 0], ln_b[l, 0])
            if not last:
                q_c = mla_q(dn_c, mla_q_norm_g[i], w_uq, *mla_tabs_ctx, scale)
                o_c = flash_attention(q_c, k_c, v_c, **att)
                xc = proj_postnorm([o_c], [w_o], xc, mod, CTX, 2, ln_g[l, 0], ln_b[l, 0])
            xl = xl_new

        w_router_t = moe_router[l].T
        w_gate, w_up, w_dn = moe_w_gate[l].astype(BF16), moe_w_up[l].astype(BF16), moe_w_down[l].astype(BF16)
        y_l = ec_moe(xl, mod, LAT, w_router_t, w_gate, w_up, w_dn)
        xl = add_postnorm(y_l, xl, mod, LAT, 5, ln_g[l, 1], ln_b[l, 1])
        if not last:
            y_c = ec_moe(xc, mod, CTX, w_router_t, w_gate, w_up, w_dn)
            xc = add_postnorm(y_c, xc, mod, CTX, 5, ln_g[l, 1], ln_b[l, 1])
    return xl[None]
```

```python
import functools
import math

import numpy as np
import jax
import jax.numpy as jnp
from jax import lax
from jax.experimental import pallas as pl
from jax.experimental.pallas import tpu as pltpu

F32 = jnp.float32
BF16 = jnp.bfloat16

D_MODEL = 2048
DEPTH = 2
GRID_W = 64
HEAD_DIM = 128
A_HEADS = D_MODEL // 256
A_WIDTH = A_HEADS * HEAD_DIM
B_Q_HEADS = D_MODEL // 256
B_KV_HEADS = 2
B_WIDTH = B_Q_HEADS * HEAD_DIM
B_KV_WIDTH = B_KV_HEADS * HEAD_DIM
MLA_HEADS = D_MODEL // 128
MLA_Q_LORA = 512
MLA_KV_LORA = 512
MLA_ROPE = 64
MLA_QK = 2 * HEAD_DIM
N_EXPERTS = 16
EXPERT_FF = D_MODEL // 2
EC_CAPACITY_FACTOR = 2
ROPE_THETA = 10000.0
NORM_EPS = 1e-6
DEEPNORM_ALPHA = (2.0 * DEPTH) ** 0.25

HGRN_CHUNK = 128
LANES = 128
LOG2E = math.log2(math.e)
V7X_VMEM_BYTES = 64 * 1024 * 1024
VMEM_CAP_BYTES = V7X_VMEM_BYTES - 8 * 1024 * 1024


def _params(semantics, vmem_estimate_bytes):
    limit = int(min(max(2 * vmem_estimate_bytes, 32 * 1024 * 1024), VMEM_CAP_BYTES))
    return pltpu.CompilerParams(dimension_semantics=semantics, vmem_limit_bytes=limit)


def _layer_norm(x):
    mu = jnp.mean(x, axis=-1, keepdims=True)
    xc = x - mu
    var = jnp.mean(xc * xc, axis=-1, keepdims=True)
    return xc * lax.rsqrt(var + NORM_EPS)


def _rms(x):
    return x * lax.rsqrt(jnp.mean(x * x, axis=-1, keepdims=True) + NORM_EPS)


def _dot(a, b):
    return jnp.dot(a, b, preferred_element_type=F32)


def _dot_nt(a, b):
    return lax.dot_general(a, b, (((1,), (1,)), ((), ())), preferred_element_type=F32)


def _dot_tn(a, b):
    return lax.dot_general(a, b, (((0,), (0,)), ((), ())), preferred_element_type=F32)


def _split3(x):
    x1 = x.astype(BF16)
    r1 = x - x1.astype(F32)
    x2 = r1.astype(BF16)
    x3 = (r1 - x2.astype(F32)).astype(BF16)
    return x1, x2, x3


def _adaln_kernel(c_ref, w_ref, b_ref, o_ref):
    c = c_ref[...]
    s = c * jax.nn.sigmoid(c)
    w = w_ref[...]
    s1, s2, s3 = _split3(s)
    w1, w2, w3 = _split3(w)
    acc = _dot(s1, w3) + _dot(s3, w1) + _dot(s2, w2)
    acc = acc + _dot(s1, w2) + _dot(s2, w1)
    acc = acc + _dot(s1, w1)
    o_ref[...] = acc + b_ref[...]


def adaln(cc, w, b):
    d, n = w.shape
    tn = 1536 if n % 1536 == 0 else n
    est = 2 * d * tn * 4 * 2
    return pl.pallas_call(
        _adaln_kernel,
        out_shape=jax.ShapeDtypeStruct((8, n), F32),
        grid=(n // tn,),
        in_specs=[pl.BlockSpec((8, d), lambda j: (0, 0)),
                  pl.BlockSpec((d, tn), lambda j: (0, j)),
                  pl.BlockSpec((1, tn), lambda j: (0, j))],
        out_specs=pl.BlockSpec((8, tn), lambda j: (0, j)),
        compiler_params=_params(("parallel",), est),
        name="adaln",
    )(cc, w, b.reshape(1, n))


def _lnmod_mm_kernel(x_ref, sh_ref, sc_ref, w_ref, o_ref, h_ref, *, row):
    @pl.when(pl.program_id(1) == 0)
    def _():
        hn = _layer_norm(x_ref[...])
        h = hn * (1.0 + sc_ref[row:row + 1, :]) + sh_ref[row:row + 1, :]
        h_ref[...] = h.astype(BF16)

    o_ref[...] = _dot(h_ref[...], w_ref[...]).astype(o_ref.dtype)


def lnmod_matmul(x, mod, row, k_shift, k_scale, w, tn):
    m, d = x.shape
    n = w.shape[1]
    tm = min(m, 1024)
    est = 2 * tm * d * 4 + tm * d * 2 + 2 * d * tn * 2 + 2 * tm * tn * 4
    return pl.pallas_call(
        functools.partial(_lnmod_mm_kernel, row=row),
        out_shape=jax.ShapeDtypeStruct((m, n), F32),
        grid=(m // tm, n // tn),
        in_specs=[pl.BlockSpec((tm, d), lambda i, j: (i, 0)),
                  pl.BlockSpec((8, d), lambda i, j: (0, k_shift)),
                  pl.BlockSpec((8, d), lambda i, j: (0, k_scale)),
                  pl.BlockSpec((d, tn), lambda i, j: (0, j))],
        out_specs=pl.BlockSpec((tm, tn), lambda i, j: (i, j)),
        scratch_shapes=[pltpu.VMEM((tm, d), BF16)],
        compiler_params=_params(("parallel", "arbitrary"), est),
        name="lnmod_matmul",
    )(x, mod, mod, w)


def _hgrn_tables(c):
    n_lvl = int(math.log2(c))
    r = np.arange(c)
    u = np.arange(c)[None, :]
    blocks, masks = [], []
    for l in range(n_lvl):
        half = 1 << l
        base = (r // (2 * half)) * (2 * half)
        anchor = (base + half - 1)[:, None]
        upper = (r >= base + half)[:, None]
        rr = r[:, None]
        blocks.append(np.where(upper, (u > anchor) & (u <= rr), (u > rr) & (u <= anchor)))
        same = (r[:, None] // (2 * half)) == (r[None, :] // (2 * half))
        masks.append(same & upper & ~(upper.T))
    blocks.append(u <= r[:, None])
    blocks.append(u > r[:, None])
    blocks.append(np.ones((16, c), bool))
    masks.append(np.eye(c, dtype=bool))
    fwd_s = np.concatenate(blocks, axis=0).astype(np.float32)
    fwd_m = np.stack(masks).astype(np.float32)
    bwd_s = np.concatenate([b[::-1, ::-1] for b in blocks], axis=0).astype(np.float32)
    bwd_m = fwd_m[:, ::-1, ::-1]
    return (jnp.asarray(np.stack([fwd_s, bwd_s]), BF16), jnp.asarray(np.stack([fwd_m, bwd_m]), F32))


def _hgrn_kernel(q_ref, v_ref, f_ref, lb_ref, sums_ref, mask_ref, s0_ref, o_ref, sfin_ref, st_ref):
    c = q_ref.shape[0]
    hd = HEAD_DIM
    n_lvl = mask_ref.shape[0] - 1
    j = pl.program_id(1)

    @pl.when(j == 0)
    def _():
        st_ref[...] = s0_ref[...]

    for h in range(q_ref.shape[1] // hd):
        cols = slice(h * hd, (h + 1) * hd)
        q = q_ref[:, cols]
        vb = v_ref[:, cols].astype(BF16)
        lb = lb_ref[:, cols]
        f = lb + (1.0 - lb) * jax.nn.sigmoid(f_ref[:, cols])
        g = jnp.log(f)
        k = 1.0 - f
        g1 = g.astype(BF16)
        g2 = (g - g1.astype(F32)).astype(BF16)
        e2 = _dot(sums_ref[...], jnp.concatenate([g1, g2], axis=1))
        e = e2[:, hd:] + e2[:, :hd]

        scores = _dot_nt(q.astype(BF16), k.astype(BF16)) * mask_ref[n_lvl]
        for l in range(n_lvl):
            z = jnp.exp(e[l * c:(l + 1) * c])
            scores = scores + _dot_nt((q * z).astype(BF16), (k * z).astype(BF16)) * mask_ref[l]

        cum = e[n_lvl * c:(n_lvl + 1) * c]
        rem = e[(n_lvl + 1) * c:(n_lvl + 2) * c]
        tot = e[(n_lvl + 2) * c:(n_lvl + 2) * c + 1]
        st = st_ref[h]
        o = _dot(scores.astype(BF16), vb) + _dot_nt((q * jnp.exp(cum)).astype(BF16), st.astype(BF16))
        o_ref[:, cols] = o
        st_new = st * jnp.exp(tot) + _dot_tn(vb, (k * jnp.exp(rem)).astype(BF16))
        st_ref[h] = st_new

    @pl.when(j == pl.num_programs(1) - 1)
    def _():
        sfin_ref[...] = st_ref[...]


def hgrn_scan(proj, lb, s0):
    seq = proj.shape[0]
    c = HGRN_CHUNK
    nc = seq // c
    sums, masks = _hgrn_tables(c)
    hd, w = HEAD_DIM, A_WIDTH

    def blk(d, j):
        return jnp.where(d == 0, j, nc - 1 - j)

    est = (2 * (4 * c * w * 4 + sums.shape[1] * c * 2 + masks.shape[1] * c * c * 4 + 2 * A_HEADS * hd * hd * 4)
           + A_HEADS * hd * hd * 4)
    return pl.pallas_call(
        _hgrn_kernel,
        out_shape=(jax.ShapeDtypeStruct((2, seq, w), F32),
                   jax.ShapeDtypeStruct((2, A_HEADS, hd, hd), F32)),
        grid=(2, nc),
        in_specs=[pl.BlockSpec((c, w), lambda d, j: (blk(d, j), 0)),
                  pl.BlockSpec((c, w), lambda d, j: (blk(d, j), 3)),
                  pl.BlockSpec((c, w), lambda d, j: (blk(d, j), 1 + d)),
                  pl.BlockSpec((None, 1, w), lambda d, j: (d, 0, 0)),
                  pl.BlockSpec((None, sums.shape[1], c), lambda d, j: (d, 0, 0)),
                  pl.BlockSpec((None, masks.shape[1], c, c), lambda d, j: (d, 0, 0, 0)),
                  pl.BlockSpec((None, A_HEADS, hd, hd), lambda d, j: (d, 0, 0, 0))],
        out_specs=(pl.BlockSpec((None, c, w), lambda d, j: (d, blk(d, j), 0)),
                   pl.BlockSpec((None, A_HEADS, hd, hd), lambda d, j: (d, 0, 0, 0))),
        scratch_shapes=[pltpu.VMEM((A_HEADS, hd, hd), F32)],
        compiler_params=_params(("parallel", "arbitrary"), est),
        name="hgrn_scan",
    )(proj, proj, proj, lb, sums, masks, s0)


def _hgrn_out_kernel(o_ref, gate_ref, g_ref, a_ref):
    o = o_ref[0] + o_ref[1]
    gate = gate_ref[...]
    a_ref[...] = (_rms(o) * g_ref[...] * (gate * jax.nn.sigmoid(gate))).astype(a_ref.dtype)


def hgrn_out(o, proj, norm_g):
    seq = o.shape[1]
    tm = min(seq, 512)
    hd = HEAD_DIM
    return pl.pallas_call(
        _hgrn_out_kernel,
        out_shape=jax.ShapeDtypeStruct((seq, A_WIDTH), BF16),
        grid=(seq // tm, A_HEADS),
        in_specs=[pl.BlockSpec((2, tm, hd), lambda i, h: (0, i, h)),
                  pl.BlockSpec((tm, hd), lambda i, h: (i, 4 * A_HEADS + h)),
                  pl.BlockSpec((1, hd), lambda i, h: (0, 0))],
        out_specs=pl.BlockSpec((tm, hd), lambda i, h: (i, h)),
        compiler_params=_params(("parallel", "parallel"), 8 * tm * hd * 4),
        name="hgrn_out",
    )(o, proj, norm_g.reshape(1, hd))


def _norm_rope_kernel(x_ref, g_ref, cos_ref, sin_ref, o_ref, *, scale):
    y = _rms(x_ref[...]) * g_ref[...]
    y = y * cos_ref[...] + pltpu.roll(y, HEAD_DIM // 2, 1) * sin_ref[...]
    o_ref[...] = (y * scale).astype(o_ref.dtype)


def norm_rope(proj, col0, n_heads, g, cos, sin, scale):
    seq = proj.shape[0]
    tm = min(seq, 512)
    hd = HEAD_DIM
    return pl.pallas_call(
        functools.partial(_norm_rope_kernel, scale=scale),
        out_shape=jax.ShapeDtypeStruct((seq, n_heads * hd), BF16),
        grid=(seq // tm, n_heads),
        in_specs=[pl.BlockSpec((tm, hd), lambda i, h: (i, col0 + h)),
                  pl.BlockSpec((1, hd), lambda i, h: (0, 0)),
                  pl.BlockSpec((tm, hd), lambda i, h: (i, 0)),
                  pl.BlockSpec((tm, hd), lambda i, h: (i, 0))],
        out_specs=pl.BlockSpec((tm, hd), lambda i, h: (i, h)),
        compiler_params=_params(("parallel", "parallel"), 10 * tm * hd * 4),
        name="norm_rope",
    )(proj, g.reshape(1, hd), cos, sin)


def _flash_update(q, k, v, m_ref, l_ref, acc_ref):
    s = _dot_nt(q, k)
    m_prev = m_ref[...]
    m_new = jnp.maximum(m_prev, jnp.max(s, axis=-1, keepdims=True))
    alpha = jnp.exp2(m_prev - m_new)
    ps = [jnp.exp2(s[:, c * LANES:(c + 1) * LANES] - m_new) for c in range(s.shape[1] // LANES)]
    psum = ps[0]
    for pc in ps[1:]:
        psum = psum + pc
    p = jnp.concatenate([pc.astype(BF16) for pc in ps], axis=1)
    l_ref[...] = alpha * l_ref[...] + psum
    acc_ref[...] = alpha * acc_ref[...] + _dot(p, v)
    m_ref[...] = m_new


def _flash_kernel(*refs, has_ctx):
    if has_ctx:
        q_ref, k_ref, v_ref, kc_ref, vc_ref, o_ref, m_ref, l_ref, acc_ref = refs
    else:
        q_ref, k_ref, v_ref, o_ref, m_ref, l_ref, acc_ref = refs
    j = pl.program_id(2)

    @pl.when(j == 0)
    def _():
        m_ref[...] = jnp.full(m_ref.shape, -jnp.inf, F32)
        l_ref[...] = jnp.zeros(l_ref.shape, F32)
        acc_ref[...] = jnp.zeros(acc_ref.shape, F32)
        if has_ctx:
            _flash_update(q_ref[...], kc_ref[...], vc_ref[...], m_ref, l_ref, acc_ref)

    _flash_update(q_ref[...], k_ref[...], v_ref[...], m_ref, l_ref, acc_ref)

    @pl.when(j == pl.num_programs(2) - 1)
    def _():
        l = jnp.sum(l_ref[...], axis=-1, keepdims=True)
        o_ref[...] = (acc_ref[...] / l).astype(o_ref.dtype)


def flash_attention(q, k, v, k_ctx=None, v_ctx=None, *, n_heads, n_kv_heads, dq, dv):
    n, m = q.shape[0], k.shape[0]
    grp = n_heads // n_kv_heads
    tq = min(n, 2048)
    tk = min(m, 2048)
    has_ctx = k_ctx is not None
    in_specs = [pl.BlockSpec((tq, dq), lambda h, i, j: (i, h)),
                pl.BlockSpec((tk, dq), lambda h, i, j: (j, h // grp)),
                pl.BlockSpec((tk, dv), lambda h, i, j: (j, h // grp))]
    args = [q, k, v]
    if has_ctx:
        mc = k_ctx.shape[0]
        in_specs += [pl.BlockSpec((mc, dq), lambda h, i, j: (0, h // grp)),
                     pl.BlockSpec((mc, dv), lambda h, i, j: (0, h // grp))]
        args += [k_ctx, v_ctx]
    est = 2 * (tq * dq + tk * dq + tk * dv + tq * dv) * 2 + tq * (dv + 256) * 4 + 6 * tq * tk * 4
    return pl.pallas_call(
        functools.partial(_flash_kernel, has_ctx=has_ctx),
        out_shape=jax.ShapeDtypeStruct((n, n_heads * dv), BF16),
        grid=(n_heads, n // tq, m // tk),
        in_specs=in_specs,
        out_specs=pl.BlockSpec((tq, dv), lambda h, i, j: (i, h)),
        scratch_shapes=[pltpu.VMEM((tq, LANES), F32), pltpu.VMEM((tq, LANES), F32), pltpu.VMEM((tq, dv), F32)],
        compiler_params=_params(("parallel", "parallel", "arbitrary"), est),
        name="flash_attention",
    )(*args)


def _proj_postnorm_kernel(*refs, n_in, row):
    a_refs = refs[:n_in]
    w_refs = refs[n_in:2 * n_in]
    x_ref, gate_ref, g_ref, b_ref, o_ref = refs[2 * n_in:]
    y = _dot(a_refs[0][...], w_refs[0][...])
    for a_ref, w_ref in zip(a_refs[1:], w_refs[1:]):
        y = y + _dot(a_ref[...], w_ref[...])
    z = DEEPNORM_ALPHA * x_ref[...] + gate_ref[row:row + 1, :] * y
    o_ref[...] = _layer_norm(z) * g_ref[...] + b_ref[...]


def proj_postnorm(acts, ws, x, mod, row, k_gate, g, b):
    m, d = x.shape
    tm = min(m, 512)
    n_in = len(acts)
    once = pl.Buffered(1)
    in_specs = [pl.BlockSpec((tm, a.shape[1]), lambda i: (i, 0)) for a in acts]
    in_specs += [pl.BlockSpec(w.shape, lambda i: (0, 0), pipeline_mode=once) for w in ws]
    in_specs += [pl.BlockSpec((tm, d), lambda i: (i, 0)),
                 pl.BlockSpec((8, d), lambda i: (0, k_gate)),
                 pl.BlockSpec((1, d), lambda i: (0, 0)),
                 pl.BlockSpec((1, d), lambda i: (0, 0))]
    est = sum(w.size * 2 for w in ws) + sum(2 * tm * a.shape[1] * 2 for a in acts) + 6 * tm * d * 4
    return pl.pallas_call(
        functools.partial(_proj_postnorm_kernel, n_in=n_in, row=row),
        out_shape=jax.ShapeDtypeStruct((m, d), F32),
        grid=(m // tm,),
        in_specs=in_specs,
        out_specs=pl.BlockSpec((tm, d), lambda i: (i, 0)),
        compiler_params=_params(("parallel",), est),
        name="proj_postnorm",
    )(*acts, *ws, x, mod, g.reshape(1, d), b.reshape(1, d))


def _add_postnorm_kernel(y_ref, x_ref, gate_ref, g_ref, b_ref, o_ref, *, row):
    z = DEEPNORM_ALPHA * x_ref[...] + gate_ref[row:row + 1, :] * y_ref[...]
    o_ref[...] = _layer_norm(z) * g_ref[...] + b_ref[...]


def add_postnorm(y, x, mod, row, k_gate, g, b):
    m, d = x.shape
    tm = min(m, 512)
    return pl.pallas_call(
        functools.partial(_add_postnorm_kernel, row=row),
        out_shape=jax.ShapeDtypeStruct((m, d), F32),
        grid=(m // tm,),
        in_specs=[pl.BlockSpec((tm, d), lambda i: (i, 0)),
                  pl.BlockSpec((tm, d), lambda i: (i, 0)),
                  pl.BlockSpec((8, d), lambda i: (0, k_gate)),
                  pl.BlockSpec((1, d), lambda i: (0, 0)),
                  pl.BlockSpec((1, d), lambda i: (0, 0))],
        out_specs=pl.BlockSpec((tm, d), lambda i: (i, 0)),
        compiler_params=_params(("parallel",), 8 * tm * d * 4),
        name="add_postnorm",
    )(y, x, mod, g.reshape(1, d), b.reshape(1, d))


def _rms_mm_kernel(x_ref, g_ref, w_ref, o_ref, a_ref):
    @pl.when(pl.program_id(1) == 0)
    def _():
        a_ref[...] = (_rms(x_ref[...]) * g_ref[...]).astype(BF16)

    o_ref[...] = _dot(a_ref[...], w_ref[...]).astype(o_ref.dtype)


def _mla_q_kernel(x_ref, g_ref, w_ref, cos_ref, sa_ref, sb_ref, o_ref, a_ref, *, scale):
    @pl.when(pl.program_id(1) == 0)
    def _():
        a_ref[...] = (_rms(x_ref[...]) * g_ref[...]).astype(BF16)

    y = _dot(a_ref[...], w_ref[...])
    hd = HEAD_DIM
    for h in range(y.shape[1] // MLA_QK):
        c0 = h * MLA_QK
        o_ref[:, c0:c0 + hd] = (y[:, c0:c0 + hd] * scale).astype(o_ref.dtype)
        r = y[:, c0 + hd:c0 + 2 * hd]
        r = r * cos_ref[...] + pltpu.roll(r, hd - MLA_ROPE // 2, 1) * sa_ref[...] + pltpu.roll(r, MLA_ROPE // 2, 1) * sb_ref[...]
        o_ref[:, c0 + hd:c0 + 2 * hd] = (r * scale).astype(o_ref.dtype)


def _mla_k_kernel(x_ref, g_ref, w_ref, kr_ref, cos_ref, sa_ref, sb_ref, o_ref, a_ref, r_ref):
    hd = HEAD_DIM

    @pl.when(pl.program_id(1) == 0)
    def _():
        a_ref[...] = (_rms(x_ref[...]) * g_ref[...]).astype(BF16)
        r = kr_ref[...]
        r = r * cos_ref[...] + pltpu.roll(r, hd - MLA_ROPE // 2, 1) * sa_ref[...] + pltpu.roll(r, MLA_ROPE // 2, 1) * sb_ref[...]
        r_ref[...] = r.astype(BF16)

    y = _dot(a_ref[...], w_ref[...])
    for h in range(y.shape[1] // hd):
        o_ref[:, h * MLA_QK:h * MLA_QK + hd] = y[:, h * hd:(h + 1) * hd].astype(o_ref.dtype)
        o_ref[:, h * MLA_QK + hd:(h + 1) * MLA_QK] = r_ref[...]


def _mla_specs(m, tm, lora, col_blk):
    return [pl.BlockSpec((tm, lora), lambda i, j: (i, col_blk)),
            pl.BlockSpec((1, lora), lambda i, j: (0, 0))]


def rms_matmul(dn, col_blk, g, w, tn):
    m = dn.shape[0]
    lora, n = w.shape
    tm = min(m, 1024)
    est = 2 * tm * lora * 4 + tm * lora * 2 + 2 * lora * tn * 2 + 2 * tm * tn * 2 + tm * tn * 4
    return pl.pallas_call(
        _rms_mm_kernel,
        out_shape=jax.ShapeDtypeStruct((m, n), BF16),
        grid=(m // tm, n // tn),
        in_specs=_mla_specs(m, tm, lora, col_blk) + [pl.BlockSpec((lora, tn), lambda i, j: (0, j))],
        out_specs=pl.BlockSpec((tm, tn), lambda i, j: (i, j)),
        scratch_shapes=[pltpu.VMEM((tm, lora), BF16)],
        compiler_params=_params(("parallel", "arbitrary"), est),
        name="rms_matmul",
    )(dn, g.reshape(1, lora), w)


def mla_q(dn, g, w, cos, sa, sb, scale):
    m = dn.shape[0]
    lora, n = w.shape
    tm = min(m, 1024)
    tn = 4 * MLA_QK
    hd = HEAD_DIM
    est = 2 * tm * lora * 4 + tm * lora * 2 + 2 * lora * tn * 2 + 2 * tm * tn * 2 + 2 * tm * tn * 4 + 6 * tm * hd * 4
    rope_spec = pl.BlockSpec((tm, hd), lambda i, j: (i, 0))
    return pl.pallas_call(
        functools.partial(_mla_q_kernel, scale=scale),
        out_shape=jax.ShapeDtypeStruct((m, n), BF16),
        grid=(m // tm, n // tn),
        in_specs=_mla_specs(m, tm, lora, 0) + [pl.BlockSpec((lora, tn), lambda i, j: (0, j)),
                                               rope_spec, rope_spec, rope_spec],
        out_specs=pl.BlockSpec((tm, tn), lambda i, j: (i, j)),
        scratch_shapes=[pltpu.VMEM((tm, lora), BF16)],
        compiler_params=_params(("parallel", "arbitrary"), est),
        name="mla_q",
    )(dn, g.reshape(1, lora), w, cos, sa, sb)


def mla_k(dn, g, w, cos, sa, sb):
    m = dn.shape[0]
    lora, n = w.shape
    tm = min(m, 1024)
    hd = HEAD_DIM
    tn = 4 * hd
    kr_blk = (MLA_Q_LORA + MLA_KV_LORA) // hd
    est = 2 * tm * lora * 4 + tm * lora * 2 + 2 * lora * tn * 2 + 4 * tm * tn * 2 + tm * tn * 4 + 8 * tm * hd * 4
    rope_spec = pl.BlockSpec((tm, hd), lambda i, j: (i, 0))
    return pl.pallas_call(
        _mla_k_kernel,
        out_shape=jax.ShapeDtypeStruct((m, 2 * n), BF16),
        grid=(m // tm, n // tn),
        in_specs=_mla_specs(m, tm, lora, 1) + [pl.BlockSpec((lora, tn), lambda i, j: (0, j)),
                                               pl.BlockSpec((tm, hd), lambda i, j: (i, kr_blk)),
                                               rope_spec, rope_spec, rope_spec],
        out_specs=pl.BlockSpec((tm, 2 * tn), lambda i, j: (i, j)),
        scratch_shapes=[pltpu.VMEM((tm, lora), BF16), pltpu.VMEM((tm, hd), BF16)],
        compiler_params=_params(("parallel", "arbitrary"), est),
        name="mla_k",
    )(dn, g.reshape(1, lora), w, dn, cos, sa, sb)


def _router_kernel(x_ref, sh_ref, sc_ref, wr_ref, h_ref, aff_ref, *, row):
    hn = _layer_norm(x_ref[...])
    h = hn * (1.0 + sc_ref[row:row + 1, :]) + sh_ref[row:row + 1, :]
    hb = h.astype(BF16)
    h_ref[...] = hb
    w = wr_ref[...]
    w1 = w.astype(BF16)
    w2 = (w - w1.astype(F32)).astype(BF16)
    h2 = (h - hb.astype(F32)).astype(BF16)
    logits = _dot_nt(w1, hb) + (_dot_nt(w2, hb) + _dot_nt(w1, h2))
    mx = jnp.max(logits, axis=0, keepdims=True)
    p = jnp.exp(logits - mx)
    aff_ref[...] = p / jnp.sum(p, axis=0, keepdims=True)


def moe_router(x, mod, row, k_shift, k_scale, w_router_t):
    m, d = x.shape
    e = w_router_t.shape[0]
    tm = min(m, 512)
    return pl.pallas_call(
        functools.partial(_router_kernel, row=row),
        out_shape=(jax.ShapeDtypeStruct((m, d), BF16), jax.ShapeDtypeStruct((e, m), F32)),
        grid=(m // tm,),
        in_specs=[pl.BlockSpec((tm, d), lambda i: (i, 0)),
                  pl.BlockSpec((8, d), lambda i: (0, k_shift)),
                  pl.BlockSpec((8, d), lambda i: (0, k_scale)),
                  pl.BlockSpec((e, d), lambda i: (0, 0))],
        out_specs=(pl.BlockSpec((tm, d), lambda i: (i, 0)), pl.BlockSpec((e, tm), lambda i: (0, i))),
        compiler_params=_params(("parallel",), 8 * tm * d * 4),
        name="moe_router",
    )(x, mod, mod, w_router_t)


def _ffn_up_kernel(*refs, n_seg):
    x_refs, (wg_ref, wu_ref), o_refs = refs[:n_seg], refs[n_seg:n_seg + 2], refs[n_seg + 2:]
    wg = wg_ref[...].astype(BF16)
    wu = wu_ref[...].astype(BF16)
    for x_ref, o_ref in zip(x_refs, o_refs):
        x = x_ref[...]
        g = _dot(x, wg)
        u = _dot(x, wu)
        o_ref[...] = (g * jax.nn.sigmoid(g) * u).astype(o_ref.dtype)


def _ffn_down_kernel(*refs, n_seg):
    h_refs, wd_ref, wt_refs, o_refs = refs[:n_seg], refs[n_seg], refs[n_seg + 1:2 * n_seg + 1], refs[2 * n_seg + 1:]
    wd = wd_ref[...].astype(BF16)
    for h_ref, wt_ref, o_ref in zip(h_refs, wt_refs, o_refs):
        o_ref[...] = _dot(h_ref[...], wd) * wt_ref[...]


def expert_ffn(xgs, wts, w_gate, w_up, w_down, layer):
    n_seg = len(xgs)
    e, _, d = xgs[0].shape
    f = w_gate.shape[3]
    rs = [x.shape[1] for x in xgs]
    r = sum(rs)
    tf = min(f, 256)
    est = 2 * (r * d * 2 + 2 * d * tf * 4 + r * tf * 2) + 2 * d * tf * 2 + 3 * r * tf * 4
    hids = pl.pallas_call(
        functools.partial(_ffn_up_kernel, n_seg=n_seg),
        out_shape=[jax.ShapeDtypeStruct((e, ri, f), BF16) for ri in rs],
        grid=(e, f // tf),
        in_specs=[pl.BlockSpec((None, ri, d), lambda i, j: (i, 0, 0)) for ri in rs]
        + [pl.BlockSpec((None, None, d, tf), lambda i, j: (layer, i, 0, j))] * 2,
        out_specs=[pl.BlockSpec((None, ri, tf), lambda i, j: (i, 0, j)) for ri in rs],
        compiler_params=_params(("parallel", "arbitrary"), est),
        name="ffn_up",
    )(*xgs, w_gate, w_up)
    tn = min(d, 512)
    est = 2 * (r * f * 2 + f * tn * 4 + r * tn * 4 + r * LANES * 4) + f * tn * 2 + r * tn * 4
    return pl.pallas_call(
        functools.partial(_ffn_down_kernel, n_seg=n_seg),
        out_shape=[jax.ShapeDtypeStruct((e, ri, d), F32) for ri in rs],
        grid=(e, d // tn),
        in_specs=[pl.BlockSpec((None, ri, f), lambda i, j: (i, 0, 0)) for ri in rs]
        + [pl.BlockSpec((None, None, f, tn), lambda i, j: (layer, i, 0, j))]
        + [pl.BlockSpec((None, ri, 1), lambda i, j: (i, 0, 0)) for ri in rs],
        out_specs=[pl.BlockSpec((None, ri, tn), lambda i, j: (i, 0, j)) for ri in rs],
        compiler_params=_params(("parallel", "arbitrary"), est),
        name="ffn_down",
    )(*hids, w_down, *wts)


def moe_route(x, mod, row, w_router_t):
    m, d = x.shape
    cap = max(1, EC_CAPACITY_FACTOR * m // N_EXPERTS)
    h, aff_t = moe_router(x, mod, row, 3, 4, w_router_t)
    weight, idx = lax.top_k(aff_t, cap)
    xg = jnp.take(h, idx.reshape(-1), axis=0).reshape(N_EXPERTS, cap, d)
    return xg, weight[..., None], idx


def moe_combine(y, idx, m):
    d = y.shape[-1]
    return jnp.zeros((m, d), F32).at[idx.reshape(-1)].add(y.reshape(-1, d))


def _rope_angles(n_tokens, rot_dim):
    rows = n_tokens // GRID_W
    row = jnp.repeat(jnp.arange(rows, dtype=F32), GRID_W)
    col = jnp.tile(jnp.arange(GRID_W, dtype=F32), rows)
    n_freq = rot_dim // 4
    inv = ROPE_THETA ** (-jnp.arange(n_freq, dtype=F32) / n_freq)
    return jnp.concatenate([row[:, None] * inv, col[:, None] * inv], axis=-1)


def _gqa_rope_tables(n_tokens):
    ang = _rope_angles(n_tokens, HEAD_DIM)
    c, s = jnp.cos(ang), jnp.sin(ang)
    return jnp.concatenate([c, c], axis=-1), jnp.concatenate([-s, s], axis=-1)


def _mla_rope_tables(n_tokens):
    ang = _rope_angles(n_tokens, MLA_ROPE)
    c, s = jnp.cos(ang), jnp.sin(ang)
    z = jnp.zeros_like(c)
    cos = jnp.concatenate([c, c, z, z], axis=-1)
    sa = jnp.concatenate([-s, z, z, z], axis=-1)
    sb = jnp.concatenate([z, s, z, z], axis=-1)
    return cos, sa, sb


def kernel(x, c, ctx, c_ctx, ada_w, ada_b, ln_g, ln_b, ev_w_in, ev_w_out, hgrn_lb, hgrn_norm_g, gqa_q_norm_g, gqa_k_norm_g, mla_w_down, mla_q_norm_g, mla_kv_norm_g, mla_w_uq, mla_w_ukv, mla_w_o, moe_router, moe_w_gate, moe_w_up, moe_w_down):
    d = D_MODEL
    xl = x[0]
    xc = ctx[0]
    n_lat, n_ctx = xl.shape[0], xc.shape[0]
    cc = jnp.zeros((8, d), F32).at[0].set(c[0]).at[1].set(c_ctx)
    lb_all = jnp.cumsum(jax.nn.softmax(hgrn_lb.astype(F32), axis=1), axis=1)
    gqa_tabs = _gqa_rope_tables(n_lat)
    gqa_tabs_ctx = [jnp.ones((n_ctx, HEAD_DIM), F32), jnp.zeros((n_ctx, HEAD_DIM), F32)]
    mla_tabs = _mla_rope_tables(n_lat)
    mla_tabs_ctx = [jnp.ones((n_ctx, HEAD_DIM), F32), jnp.zeros((n_ctx, HEAD_DIM), F32), jnp.zeros((n_ctx, HEAD_DIM), F32)]
    LAT, CTX = 0, 1

    for l in range(DEPTH):
        last = l == DEPTH - 1
        i = l // 2
        mod = adaln(cc, ada_w[l], ada_b[l])
        if l % 2 == 0:
            w_in = ev_w_in[i].astype(BF16)
            w_out = ev_w_out[i].astype(BF16)
            lb = lb_all[:, l].reshape(2, 1, A_WIDTH)
            scale = HEAD_DIM ** -0.5 * LOG2E
            proj_c = lnmod_matmul(xc, mod, CTX, 0, 1, w_in, 512)
            proj_l = lnmod_matmul(xl, mod, LAT, 0, 1, w_in, 512)
            s0 = jnp.zeros((2, A_HEADS, HEAD_DIM, HEAD_DIM), F32)
            o_c, s_c = hgrn_scan(proj_c, lb, s0)
            o_l, _ = hgrn_scan(proj_l, lb, s_c)
            a_l = hgrn_out(o_l, proj_l, hgrn_norm_g[i])
            qcol, kcol, vcol = 5 * A_HEADS, 5 * A_HEADS + B_Q_HEADS, 5 * A_WIDTH + B_WIDTH + B_KV_WIDTH
            q_l = norm_rope(proj_l, qcol, B_Q_HEADS, gqa_q_norm_g[i], *gqa_tabs, scale)
            k_l = norm_rope(proj_l, kcol, B_KV_HEADS, gqa_k_norm_g[i], *gqa_tabs, 1.0)
            k_c = norm_rope(proj_c, kcol, B_KV_HEADS, gqa_k_norm_g[i], *gqa_tabs_ctx, 1.0)
            v_l = proj_l[:, vcol:].astype(BF16)
            v_c = proj_c[:, vcol:].astype(BF16)
            att = dict(n_heads=B_Q_HEADS, n_kv_heads=B_KV_HEADS, dq=HEAD_DIM, dv=HEAD_DIM)
            b_l = flash_attention(q_l, k_l, v_l, k_c, v_c, **att)
            w_parts = [w_out[:A_WIDTH], w_out[A_WIDTH:]]
            xl_new = proj_postnorm([a_l, b_l], w_parts, xl, mod, LAT, 2, ln_g[l, 0], ln_b[l, 0])
            if not last:
                a_c = hgrn_out(o_c, proj_c, hgrn_norm_g[i])
                q_c = norm_rope(proj_c, qcol, B_Q_HEADS, gqa_q_norm_g[i], *gqa_tabs_ctx, scale)
                b_c = flash_attention(q_c, k_c, v_c, **att)
                xc = proj_postnorm([a_c, b_c], w_parts, xc, mod, CTX, 2, ln_g[l, 0], ln_b[l, 0])
            xl = xl_new
        else:
            hd = HEAD_DIM
            pad = (-mla_w_down.shape[2]) % hd
            w_down = jnp.pad(mla_w_down[i], ((0, 0), (0, pad))).astype(BF16)
            w_uq = mla_w_uq[i].reshape(MLA_Q_LORA, MLA_HEADS, hd + MLA_ROPE)
            w_uq = jnp.pad(w_uq, ((0, 0), (0, 0), (0, MLA_QK - hd - MLA_ROPE))).reshape(MLA_Q_LORA, MLA_HEADS * MLA_QK).astype(BF16)
            w_ukv = mla_w_ukv[i].reshape(MLA_KV_LORA, MLA_HEADS, 2 * hd)
            w_uk = w_ukv[:, :, :hd].reshape(MLA_KV_LORA, MLA_HEADS * hd).astype(BF16)
            w_uv = w_ukv[:, :, hd:].reshape(MLA_KV_LORA, MLA_HEADS * hd).astype(BF16)
            w_o = mla_w_o[i].astype(BF16)
            scale = (hd + MLA_ROPE) ** -0.5 * LOG2E
            dn_c = lnmod_matmul(xc, mod, CTX, 0, 1, w_down, w_down.shape[1])
            dn_l = lnmod_matmul(xl, mod, LAT, 0, 1, w_down, w_down.shape[1])
            q_l = mla_q(dn_l, mla_q_norm_g[i], w_uq, *mla_tabs, scale)
            k_l = mla_k(dn_l, mla_kv_norm_g[i], w_uk, *mla_tabs)
            k_c = mla_k(dn_c, mla_kv_norm_g[i], w_uk, *mla_tabs_ctx)
            v_l = rms_matmul(dn_l, 1, mla_kv_norm_g[i], w_uv, 1024)
            v_c = rms_matmul(dn_c, 1, mla_kv_norm_g[i], w_uv, 1024)
            att = dict(n_heads=MLA_HEADS, n_kv_heads=MLA_HEADS, dq=MLA_QK, dv=hd)
            o_l = flash_attention(q_l, k_l, v_l, k_c, v_c, **att)
            xl_new = proj_postnorm([o_l], [w_o], xl, mod, LAT, 2, ln_g[l, 0], ln_b[l, 0])
            if not last:
                q_c = mla_q(dn_c, mla_q_norm_g[i], w_uq, *mla_tabs_ctx, scale)
                o_c = flash_attention(q_c, k_c, v_c, **att)
                xc = proj_postnorm([o_c], [w_o], xc, mod, CTX, 2, ln_g[l, 0], ln_b[l, 0])
            xl = xl_new

        w_router_t = moe_router[l].T
        segs = [(xl, LAT)] if last else [(xl, LAT), (xc, CTX)]
        routes = [moe_route(xs, mod, row, w_router_t) for xs, row in segs]
        ys = expert_ffn([r[0] for r in routes], [r[1] for r in routes], moe_w_gate, moe_w_up, moe_w_down, l)
        outs = [add_postnorm(moe_combine(y, r[2], xs.shape[0]), xs, mod, row, 5, ln_g[l, 1], ln_b[l, 1])
                for y, r, (xs, row) in zip(ys, routes, segs)]
        xl = outs[0]
        if not last:
            xc = outs[1]
    return xl[None]
```

```python
import functools
import math

import numpy as np
import jax
import jax.numpy as jnp
from jax import lax
from jax.experimental import pallas as pl
from jax.experimental.pallas import tpu as pltpu

F32 = jnp.float32
BF16 = jnp.bfloat16

D_MODEL = 2048
DEPTH = 2
GRID_W = 64
HEAD_DIM = 128
A_HEADS = D_MODEL // 256
A_WIDTH = A_HEADS * HEAD_DIM
B_Q_HEADS = D_MODEL // 256
B_KV_HEADS = 2
B_WIDTH = B_Q_HEADS * HEAD_DIM
B_KV_WIDTH = B_KV_HEADS * HEAD_DIM
MLA_HEADS = D_MODEL // 128
MLA_Q_LORA = 512
MLA_KV_LORA = 512
MLA_ROPE = 64
MLA_QK = 2 * HEAD_DIM
N_EXPERTS = 16
EXPERT_FF = D_MODEL // 2
EC_CAPACITY_FACTOR = 2
ROPE_THETA = 10000.0
NORM_EPS = 1e-6
DEEPNORM_ALPHA = (2.0 * DEPTH) ** 0.25

HGRN_CHUNK = 128
COMBINE_WINDOW = 64
LANES = 128
LOG2E = math.log2(math.e)
V7X_VMEM_BYTES = 64 * 1024 * 1024
VMEM_CAP_BYTES = V7X_VMEM_BYTES - 8 * 1024 * 1024


def _params(semantics, vmem_estimate_bytes):
    limit = int(min(max(2 * vmem_estimate_bytes, 32 * 1024 * 1024), VMEM_CAP_BYTES))
    return pltpu.CompilerParams(dimension_semantics=semantics, vmem_limit_bytes=limit)


def _layer_norm(x):
    mu = jnp.mean(x, axis=-1, keepdims=True)
    xc = x - mu
    var = jnp.mean(xc * xc, axis=-1, keepdims=True)
    return xc * lax.rsqrt(var + NORM_EPS)


def _rms(x):
    return x * lax.rsqrt(jnp.mean(x * x, axis=-1, keepdims=True) + NORM_EPS)


def _dot(a, b):
    return jnp.dot(a, b, preferred_element_type=F32)


def _dot_nt(a, b):
    return lax.dot_general(a, b, (((1,), (1,)), ((), ())), preferred_element_type=F32)


def _dot_tn(a, b):
    return lax.dot_general(a, b, (((0,), (0,)), ((), ())), preferred_element_type=F32)


def _split3(x):
    x1 = x.astype(BF16)
    r1 = x - x1.astype(F32)
    x2 = r1.astype(BF16)
    x3 = (r1 - x2.astype(F32)).astype(BF16)
    return x1, x2, x3


def _adaln_kernel(c_ref, w_ref, b_ref, o_ref):
    c = c_ref[...]
    s = c * jax.nn.sigmoid(c)
    w = w_ref[...]
    s1, s2, s3 = _split3(s)
    w1, w2, w3 = _split3(w)
    acc = _dot(s1, w3) + _dot(s3, w1) + _dot(s2, w2)
    acc = acc + _dot(s1, w2) + _dot(s2, w1)
    acc = acc + _dot(s1, w1)
    o_ref[...] = acc + b_ref[...]


def adaln(cc, w, b):
    d, n = w.shape
    tn = 1536 if n % 1536 == 0 else n
    est = 2 * d * tn * 4 * 2
    return pl.pallas_call(
        _adaln_kernel,
        out_shape=jax.ShapeDtypeStruct((8, n), F32),
        grid=(n // tn,),
        in_specs=[pl.BlockSpec((8, d), lambda j: (0, 0)),
                  pl.BlockSpec((d, tn), lambda j: (0, j)),
                  pl.BlockSpec((1, tn), lambda j: (0, j))],
        out_specs=pl.BlockSpec((8, tn), lambda j: (0, j)),
        compiler_params=_params(("parallel",), est),
        name="adaln",
    )(cc, w, b.reshape(1, n))


def _lnmod_mm_kernel(x_ref, sh_ref, sc_ref, w_ref, o_ref, h_ref, *, row):
    @pl.when(pl.program_id(1) == 0)
    def _():
        hn = _layer_norm(x_ref[...])
        h = hn * (1.0 + sc_ref[row:row + 1, :]) + sh_ref[row:row + 1, :]
        h_ref[...] = h.astype(BF16)

    o_ref[...] = _dot(h_ref[...], w_ref[...]).astype(o_ref.dtype)


def lnmod_matmul(x, mod, row, k_shift, k_scale, w, tn):
    m, d = x.shape
    n = w.shape[1]
    tm = min(m, 1024)
    est = 2 * tm * d * 4 + tm * d * 2 + 2 * d * tn * 2 + 2 * tm * tn * 4
    return pl.pallas_call(
        functools.partial(_lnmod_mm_kernel, row=row),
        out_shape=jax.ShapeDtypeStruct((m, n), F32),
        grid=(m // tm, n // tn),
        in_specs=[pl.BlockSpec((tm, d), lambda i, j: (i, 0)),
                  pl.BlockSpec((8, d), lambda i, j: (0, k_shift)),
                  pl.BlockSpec((8, d), lambda i, j: (0, k_scale)),
                  pl.BlockSpec((d, tn), lambda i, j: (0, j))],
        out_specs=pl.BlockSpec((tm, tn), lambda i, j: (i, j)),
        scratch_shapes=[pltpu.VMEM((tm, d), BF16)],
        compiler_params=_params(("parallel", "arbitrary"), est),
        name="lnmod_matmul",
    )(x, mod, mod, w)


def _hgrn_tables(c):
    n_lvl = int(math.log2(c))
    r = np.arange(c)
    u = np.arange(c)[None, :]
    blocks, masks = [], []
    for l in range(n_lvl):
        half = 1 << l
        base = (r // (2 * half)) * (2 * half)
        anchor = (base + half - 1)[:, None]
        upper = (r >= base + half)[:, None]
        rr = r[:, None]
        blocks.append(np.where(upper, (u > anchor) & (u <= rr), (u > rr) & (u <= anchor)))
        same = (r[:, None] // (2 * half)) == (r[None, :] // (2 * half))
        masks.append(same & upper & ~(upper.T))
    blocks.append(u <= r[:, None])
    blocks.append(u > r[:, None])
    blocks.append(np.ones((16, c), bool))
    masks.append(np.eye(c, dtype=bool))
    fwd_s = np.concatenate(blocks, axis=0).astype(np.float32)
    fwd_m = np.stack(masks).astype(np.float32)
    bwd_s = np.concatenate([b[::-1, ::-1] for b in blocks], axis=0).astype(np.float32)
    bwd_m = fwd_m[:, ::-1, ::-1]
    return (jnp.asarray(np.stack([fwd_s, bwd_s]), BF16), jnp.asarray(np.stack([fwd_m, bwd_m]), F32))


def _hgrn_kernel(q_ref, v_ref, f_ref, lb_ref, sums_ref, mask_ref, s0_ref, o_ref, sfin_ref, st_ref):
    c = q_ref.shape[0]
    hd = HEAD_DIM
    n_lvl = mask_ref.shape[0] - 1
    j = pl.program_id(1)

    @pl.when(j == 0)
    def _():
        st_ref[...] = s0_ref[...]

    for h in range(q_ref.shape[1] // hd):
        cols = slice(h * hd, (h + 1) * hd)
        q = q_ref[:, cols]
        vb = v_ref[:, cols].astype(BF16)
        lb = lb_ref[:, cols]
        f = lb + (1.0 - lb) * jax.nn.sigmoid(f_ref[:, cols])
        g = jnp.log(f)
        k = 1.0 - f
        g1 = g.astype(BF16)
        g2 = (g - g1.astype(F32)).astype(BF16)
        e2 = _dot(sums_ref[...], jnp.concatenate([g1, g2], axis=1))
        e = e2[:, hd:] + e2[:, :hd]

        scores = _dot_nt(q.astype(BF16), k.astype(BF16)) * mask_ref[n_lvl]
        for l in range(n_lvl):
            z = jnp.exp(e[l * c:(l + 1) * c])
            scores = scores + _dot_nt((q * z).astype(BF16), (k * z).astype(BF16)) * mask_ref[l]

        cum = e[n_lvl * c:(n_lvl + 1) * c]
        rem = e[(n_lvl + 1) * c:(n_lvl + 2) * c]
        tot = e[(n_lvl + 2) * c:(n_lvl + 2) * c + 1]
        st = st_ref[h]
        o = _dot(scores.astype(BF16), vb) + _dot_nt((q * jnp.exp(cum)).astype(BF16), st.astype(BF16))
        o_ref[:, cols] = o
        st_new = st * jnp.exp(tot) + _dot_tn(vb, (k * jnp.exp(rem)).astype(BF16))
        st_ref[h] = st_new

    @pl.when(j == pl.num_programs(1) - 1)
    def _():
        sfin_ref[...] = st_ref[...]


def hgrn_scan(proj, lb, s0):
    seq = proj.shape[0]
    c = HGRN_CHUNK
    nc = seq // c
    sums, masks = _hgrn_tables(c)
    hd, w = HEAD_DIM, A_WIDTH

    def blk(d, j):
        return jnp.where(d == 0, j, nc - 1 - j)

    est = (2 * (4 * c * w * 4 + sums.shape[1] * c * 2 + masks.shape[1] * c * c * 4 + 2 * A_HEADS * hd * hd * 4)
           + A_HEADS * hd * hd * 4)
    return pl.pallas_call(
        _hgrn_kernel,
        out_shape=(jax.ShapeDtypeStruct((2, seq, w), F32),
                   jax.ShapeDtypeStruct((2, A_HEADS, hd, hd), F32)),
        grid=(2, nc),
        in_specs=[pl.BlockSpec((c, w), lambda d, j: (blk(d, j), 0)),
                  pl.BlockSpec((c, w), lambda d, j: (blk(d, j), 3)),
                  pl.BlockSpec((c, w), lambda d, j: (blk(d, j), 1 + d)),
                  pl.BlockSpec((None, 1, w), lambda d, j: (d, 0, 0)),
                  pl.BlockSpec((None, sums.shape[1], c), lambda d, j: (d, 0, 0)),
                  pl.BlockSpec((None, masks.shape[1], c, c), lambda d, j: (d, 0, 0, 0)),
                  pl.BlockSpec((None, A_HEADS, hd, hd), lambda d, j: (d, 0, 0, 0))],
        out_specs=(pl.BlockSpec((None, c, w), lambda d, j: (d, blk(d, j), 0)),
                   pl.BlockSpec((None, A_HEADS, hd, hd), lambda d, j: (d, 0, 0, 0))),
        scratch_shapes=[pltpu.VMEM((A_HEADS, hd, hd), F32)],
        compiler_params=_params(("parallel", "arbitrary"), est),
        name="hgrn_scan",
    )(proj, proj, proj, lb, sums, masks, s0)


def _hgrn_out_kernel(o_ref, gate_ref, g_ref, a_ref):
    o = o_ref[0] + o_ref[1]
    gate = gate_ref[...]
    a_ref[...] = (_rms(o) * g_ref[...] * (gate * jax.nn.sigmoid(gate))).astype(a_ref.dtype)


def hgrn_out(o, proj, norm_g):
    seq = o.shape[1]
    tm = min(seq, 512)
    hd = HEAD_DIM
    return pl.pallas_call(
        _hgrn_out_kernel,
        out_shape=jax.ShapeDtypeStruct((seq, A_WIDTH), BF16),
        grid=(seq // tm, A_HEADS),
        in_specs=[pl.BlockSpec((2, tm, hd), lambda i, h: (0, i, h)),
                  pl.BlockSpec((tm, hd), lambda i, h: (i, 4 * A_HEADS + h)),
                  pl.BlockSpec((1, hd), lambda i, h: (0, 0))],
        out_specs=pl.BlockSpec((tm, hd), lambda i, h: (i, h)),
        compiler_params=_params(("parallel", "parallel"), 8 * tm * hd * 4),
        name="hgrn_out",
    )(o, proj, norm_g.reshape(1, hd))


def _norm_rope_kernel(x_ref, g_ref, cos_ref, sin_ref, o_ref, *, scale):
    y = _rms(x_ref[...]) * g_ref[...]
    y = y * cos_ref[...] + pltpu.roll(y, HEAD_DIM // 2, 1) * sin_ref[...]
    o_ref[...] = (y * scale).astype(o_ref.dtype)


def norm_rope(proj, col0, n_heads, g, cos, sin, scale):
    seq = proj.shape[0]
    tm = min(seq, 512)
    hd = HEAD_DIM
    return pl.pallas_call(
        functools.partial(_norm_rope_kernel, scale=scale),
        out_shape=jax.ShapeDtypeStruct((seq, n_heads * hd), BF16),
        grid=(seq // tm, n_heads),
        in_specs=[pl.BlockSpec((tm, hd), lambda i, h: (i, col0 + h)),
                  pl.BlockSpec((1, hd), lambda i, h: (0, 0)),
                  pl.BlockSpec((tm, hd), lambda i, h: (i, 0)),
                  pl.BlockSpec((tm, hd), lambda i, h: (i, 0))],
        out_specs=pl.BlockSpec((tm, hd), lambda i, h: (i, h)),
        compiler_params=_params(("parallel", "parallel"), 10 * tm * hd * 4),
        name="norm_rope",
    )(proj, g.reshape(1, hd), cos, sin)


def _flash_update(q, k, v, m_ref, l_ref, acc_ref):
    s = _dot_nt(q, k)
    m_prev = m_ref[...]
    m_new = jnp.maximum(m_prev, jnp.max(s, axis=-1, keepdims=True))
    alpha = jnp.exp2(m_prev - m_new)
    ps = [jnp.exp2(s[:, c * LANES:(c + 1) * LANES] - m_new) for c in range(s.shape[1] // LANES)]
    psum = ps[0]
    for pc in ps[1:]:
        psum = psum + pc
    p = jnp.concatenate([pc.astype(BF16) for pc in ps], axis=1)
    l_ref[...] = alpha * l_ref[...] + psum
    acc_ref[...] = alpha * acc_ref[...] + _dot(p, v)
    m_ref[...] = m_new


def _flash_kernel(*refs, has_ctx):
    if has_ctx:
        q_ref, k_ref, v_ref, kc_ref, vc_ref, o_ref, m_ref, l_ref, acc_ref = refs
    else:
        q_ref, k_ref, v_ref, o_ref, m_ref, l_ref, acc_ref = refs
    j = pl.program_id(2)

    @pl.when(j == 0)
    def _():
        m_ref[...] = jnp.full(m_ref.shape, -jnp.inf, F32)
        l_ref[...] = jnp.zeros(l_ref.shape, F32)
        acc_ref[...] = jnp.zeros(acc_ref.shape, F32)
        if has_ctx:
            _flash_update(q_ref[...], kc_ref[...], vc_ref[...], m_ref, l_ref, acc_ref)

    _flash_update(q_ref[...], k_ref[...], v_ref[...], m_ref, l_ref, acc_ref)

    @pl.when(j == pl.num_programs(2) - 1)
    def _():
        l = jnp.sum(l_ref[...], axis=-1, keepdims=True)
        o_ref[...] = (acc_ref[...] / l).astype(o_ref.dtype)


def flash_attention(q, k, v, k_ctx=None, v_ctx=None, *, n_heads, n_kv_heads, dq, dv):
    n, m = q.shape[0], k.shape[0]
    grp = n_heads // n_kv_heads
    tq = min(n, 2048)
    tk = min(m, 2048)
    has_ctx = k_ctx is not None
    in_specs = [pl.BlockSpec((tq, dq), lambda h, i, j: (i, h)),
                pl.BlockSpec((tk, dq), lambda h, i, j: (j, h // grp)),
                pl.BlockSpec((tk, dv), lambda h, i, j: (j, h // grp))]
    args = [q, k, v]
    if has_ctx:
        mc = k_ctx.shape[0]
        in_specs += [pl.BlockSpec((mc, dq), lambda h, i, j: (0, h // grp)),
                     pl.BlockSpec((mc, dv), lambda h, i, j: (0, h // grp))]
        args += [k_ctx, v_ctx]
    est = 2 * (tq * dq + tk * dq + tk * dv + tq * dv) * 2 + tq * (dv + 256) * 4 + 6 * tq * tk * 4
    return pl.pallas_call(
        functools.partial(_flash_kernel, has_ctx=has_ctx),
        out_shape=jax.ShapeDtypeStruct((n, n_heads * dv), BF16),
        grid=(n_heads, n // tq, m // tk),
        in_specs=in_specs,
        out_specs=pl.BlockSpec((tq, dv), lambda h, i, j: (i, h)),
        scratch_shapes=[pltpu.VMEM((tq, LANES), F32), pltpu.VMEM((tq, LANES), F32), pltpu.VMEM((tq, dv), F32)],
        compiler_params=_params(("parallel", "parallel", "arbitrary"), est),
        name="flash_attention",
    )(*args)


def _proj_postnorm_kernel(*refs, n_in, row):
    a_refs = refs[:n_in]
    w_refs = refs[n_in:2 * n_in]
    x_ref, gate_ref, g_ref, b_ref, o_ref = refs[2 * n_in:]
    y = _dot(a_refs[0][...], w_refs[0][...])
    for a_ref, w_ref in zip(a_refs[1:], w_refs[1:]):
        y = y + _dot(a_ref[...], w_ref[...])
    z = DEEPNORM_ALPHA * x_ref[...] + gate_ref[row:row + 1, :] * y
    o_ref[...] = _layer_norm(z) * g_ref[...] + b_ref[...]


def proj_postnorm(acts, ws, x, mod, row, k_gate, g, b):
    m, d = x.shape
    tm = min(m, 512)
    n_in = len(acts)
    once = pl.Buffered(1)
    in_specs = [pl.BlockSpec((tm, a.shape[1]), lambda i: (i, 0)) for a in acts]
    in_specs += [pl.BlockSpec(w.shape, lambda i: (0, 0), pipeline_mode=once) for w in ws]
    in_specs += [pl.BlockSpec((tm, d), lambda i: (i, 0)),
                 pl.BlockSpec((8, d), lambda i: (0, k_gate)),
                 pl.BlockSpec((1, d), lambda i: (0, 0)),
                 pl.BlockSpec((1, d), lambda i: (0, 0))]
    est = sum(w.size * 2 for w in ws) + sum(2 * tm * a.shape[1] * 2 for a in acts) + 6 * tm * d * 4
    return pl.pallas_call(
        functools.partial(_proj_postnorm_kernel, n_in=n_in, row=row),
        out_shape=jax.ShapeDtypeStruct((m, d), F32),
        grid=(m // tm,),
        in_specs=in_specs,
        out_specs=pl.BlockSpec((tm, d), lambda i: (i, 0)),
        compiler_params=_params(("parallel",), est),
        name="proj_postnorm",
    )(*acts, *ws, x, mod, g.reshape(1, d), b.reshape(1, d))


def _add_postnorm_kernel(y_ref, x_ref, gate_ref, g_ref, b_ref, o_ref, *, row):
    z = DEEPNORM_ALPHA * x_ref[...] + gate_ref[row:row + 1, :] * y_ref[...]
    o_ref[...] = _layer_norm(z) * g_ref[...] + b_ref[...]


def add_postnorm(y, x, mod, row, k_gate, g, b):
    m, d = x.shape
    tm = min(m, 512)
    return pl.pallas_call(
        functools.partial(_add_postnorm_kernel, row=row),
        out_shape=jax.ShapeDtypeStruct((m, d), F32),
        grid=(m // tm,),
        in_specs=[pl.BlockSpec((tm, d), lambda i: (i, 0)),
                  pl.BlockSpec((tm, d), lambda i: (i, 0)),
                  pl.BlockSpec((8, d), lambda i: (0, k_gate)),
                  pl.BlockSpec((1, d), lambda i: (0, 0)),
                  pl.BlockSpec((1, d), lambda i: (0, 0))],
        out_specs=pl.BlockSpec((tm, d), lambda i: (i, 0)),
        compiler_params=_params(("parallel",), 8 * tm * d * 4),
        name="add_postnorm",
    )(y, x, mod, g.reshape(1, d), b.reshape(1, d))


def _rms_mm_kernel(x_ref, g_ref, w_ref, o_ref, a_ref):
    @pl.when(pl.program_id(1) == 0)
    def _():
        a_ref[...] = (_rms(x_ref[...]) * g_ref[...]).astype(BF16)

    o_ref[...] = _dot(a_ref[...], w_ref[...]).astype(o_ref.dtype)


def _mla_q_kernel(x_ref, g_ref, w_ref, cos_ref, sa_ref, sb_ref, o_ref, a_ref, *, scale):
    @pl.when(pl.program_id(1) == 0)
    def _():
        a_ref[...] = (_rms(x_ref[...]) * g_ref[...]).astype(BF16)

    y = _dot(a_ref[...], w_ref[...])
    hd = HEAD_DIM
    for h in range(y.shape[1] // MLA_QK):
        c0 = h * MLA_QK
        o_ref[:, c0:c0 + hd] = (y[:, c0:c0 + hd] * scale).astype(o_ref.dtype)
        r = y[:, c0 + hd:c0 + 2 * hd]
        r = r * cos_ref[...] + pltpu.roll(r, hd - MLA_ROPE // 2, 1) * sa_ref[...] + pltpu.roll(r, MLA_ROPE // 2, 1) * sb_ref[...]
        o_ref[:, c0 + hd:c0 + 2 * hd] = (r * scale).astype(o_ref.dtype)


def _mla_k_kernel(x_ref, g_ref, w_ref, kr_ref, cos_ref, sa_ref, sb_ref, o_ref, a_ref, r_ref):
    hd = HEAD_DIM

    @pl.when(pl.program_id(1) == 0)
    def _():
        a_ref[...] = (_rms(x_ref[...]) * g_ref[...]).astype(BF16)
        r = kr_ref[...]
        r = r * cos_ref[...] + pltpu.roll(r, hd - MLA_ROPE // 2, 1) * sa_ref[...] + pltpu.roll(r, MLA_ROPE // 2, 1) * sb_ref[...]
        r_ref[...] = r.astype(BF16)

    y = _dot(a_ref[...], w_ref[...])
    for h in range(y.shape[1] // hd):
        o_ref[:, h * MLA_QK:h * MLA_QK + hd] = y[:, h * hd:(h + 1) * hd].astype(o_ref.dtype)
        o_ref[:, h * MLA_QK + hd:(h + 1) * MLA_QK] = r_ref[...]


def _mla_specs(m, tm, lora, col_blk):
    return [pl.BlockSpec((tm, lora), lambda i, j: (i, col_blk)),
            pl.BlockSpec((1, lora), lambda i, j: (0, 0))]


def rms_matmul(dn, col_blk, g, w, tn):
    m = dn.shape[0]
    lora, n = w.shape
    tm = min(m, 1024)
    est = 2 * tm * lora * 4 + tm * lora * 2 + 2 * lora * tn * 2 + 2 * tm * tn * 2 + tm * tn * 4
    return pl.pallas_call(
        _rms_mm_kernel,
        out_shape=jax.ShapeDtypeStruct((m, n), BF16),
        grid=(m // tm, n // tn),
        in_specs=_mla_specs(m, tm, lora, col_blk) + [pl.BlockSpec((lora, tn), lambda i, j: (0, j))],
        out_specs=pl.BlockSpec((tm, tn), lambda i, j: (i, j)),
        scratch_shapes=[pltpu.VMEM((tm, lora), BF16)],
        compiler_params=_params(("parallel", "arbitrary"), est),
        name="rms_matmul",
    )(dn, g.reshape(1, lora), w)


def mla_q(dn, g, w, cos, sa, sb, scale):
    m = dn.shape[0]
    lora, n = w.shape
    tm = min(m, 1024)
    tn = 4 * MLA_QK
    hd = HEAD_DIM
    est = 2 * tm * lora * 4 + tm * lora * 2 + 2 * lora * tn * 2 + 2 * tm * tn * 2 + 2 * tm * tn * 4 + 6 * tm * hd * 4
    rope_spec = pl.BlockSpec((tm, hd), lambda i, j: (i, 0))
    return pl.pallas_call(
        functools.partial(_mla_q_kernel, scale=scale),
        out_shape=jax.ShapeDtypeStruct((m, n), BF16),
        grid=(m // tm, n // tn),
        in_specs=_mla_specs(m, tm, lora, 0) + [pl.BlockSpec((lora, tn), lambda i, j: (0, j)),
                                               rope_spec, rope_spec, rope_spec],
        out_specs=pl.BlockSpec((tm, tn), lambda i, j: (i, j)),
        scratch_shapes=[pltpu.VMEM((tm, lora), BF16)],
        compiler_params=_params(("parallel", "arbitrary"), est),
        name="mla_q",
    )(dn, g.reshape(1, lora), w, cos, sa, sb)


def mla_k(dn, g, w, cos, sa, sb):
    m = dn.shape[0]
    lora, n = w.shape
    tm = min(m, 1024)
    hd = HEAD_DIM
    tn = 4 * hd
    kr_blk = (MLA_Q_LORA + MLA_KV_LORA) // hd
    est = 2 * tm * lora * 4 + tm * lora * 2 + 2 * lora * tn * 2 + 4 * tm * tn * 2 + tm * tn * 4 + 8 * tm * hd * 4
    rope_spec = pl.BlockSpec((tm, hd), lambda i, j: (i, 0))
    return pl.pallas_call(
        _mla_k_kernel,
        out_shape=jax.ShapeDtypeStruct((m, 2 * n), BF16),
        grid=(m // tm, n // tn),
        in_specs=_mla_specs(m, tm, lora, 1) + [pl.BlockSpec((lora, tn), lambda i, j: (0, j)),
                                               pl.BlockSpec((tm, hd), lambda i, j: (i, kr_blk)),
                                               rope_spec, rope_spec, rope_spec],
        out_specs=pl.BlockSpec((tm, 2 * tn), lambda i, j: (i, j)),
        scratch_shapes=[pltpu.VMEM((tm, lora), BF16), pltpu.VMEM((tm, hd), BF16)],
        compiler_params=_params(("parallel", "arbitrary"), est),
        name="mla_k",
    )(dn, g.reshape(1, lora), w, dn, cos, sa, sb)


def _router_kernel(x_ref, sh_ref, sc_ref, wr_ref, h_ref, aff_ref, *, row):
    hn = _layer_norm(x_ref[...])
    h = hn * (1.0 + sc_ref[row:row + 1, :]) + sh_ref[row:row + 1, :]
    hb = h.astype(BF16)
    h_ref[...] = hb
    w = wr_ref[...]
    w1 = w.astype(BF16)
    w2 = (w - w1.astype(F32)).astype(BF16)
    h2 = (h - hb.astype(F32)).astype(BF16)
    logits = _dot_nt(w1, hb) + (_dot_nt(w2, hb) + _dot_nt(w1, h2))
    mx = jnp.max(logits, axis=0, keepdims=True)
    p = jnp.exp(logits - mx)
    aff_ref[...] = p / jnp.sum(p, axis=0, keepdims=True)


def moe_router(x, mod, row, k_shift, k_scale, w_router_t):
    m, d = x.shape
    e = w_router_t.shape[0]
    tm = min(m, 512)
    return pl.pallas_call(
        functools.partial(_router_kernel, row=row),
        out_shape=(jax.ShapeDtypeStruct((m, d), BF16), jax.ShapeDtypeStruct((e, m), F32)),
        grid=(m // tm,),
        in_specs=[pl.BlockSpec((tm, d), lambda i: (i, 0)),
                  pl.BlockSpec((8, d), lambda i: (0, k_shift)),
                  pl.BlockSpec((8, d), lambda i: (0, k_scale)),
                  pl.BlockSpec((e, d), lambda i: (0, 0))],
        out_specs=(pl.BlockSpec((tm, d), lambda i: (i, 0)), pl.BlockSpec((e, tm), lambda i: (0, i))),
        compiler_params=_params(("parallel",), 8 * tm * d * 4),
        name="moe_router",
    )(x, mod, mod, w_router_t)


def _ffn_up_kernel(*refs, n_seg):
    x_refs, (wg_ref, wu_ref), o_refs = refs[:n_seg], refs[n_seg:n_seg + 2], refs[n_seg + 2:]
    wg = wg_ref[...].astype(BF16)
    wu = wu_ref[...].astype(BF16)
    for x_ref, o_ref in zip(x_refs, o_refs):
        x = x_ref[...]
        g = _dot(x, wg)
        u = _dot(x, wu)
        o_ref[...] = (g * jax.nn.sigmoid(g) * u).astype(o_ref.dtype)


def _ffn_down_kernel(*refs, n_seg):
    h_refs, wd_ref, wt_refs, o_refs = refs[:n_seg], refs[n_seg], refs[n_seg + 1:2 * n_seg + 1], refs[2 * n_seg + 1:]
    wd = wd_ref[...].astype(BF16)
    for h_ref, wt_ref, o_ref in zip(h_refs, wt_refs, o_refs):
        y = _dot(h_ref[...], wd) * wt_ref[...]
        hi = y.astype(BF16)
        o_ref[0] = hi
        o_ref[1] = (y - hi.astype(F32)).astype(BF16)


def expert_ffn(xgs, wts, w_gate, w_up, w_down, layer):
    n_seg = len(xgs)
    e, _, d = xgs[0].shape
    f = w_gate.shape[3]
    rs = [x.shape[1] for x in xgs]
    r = sum(rs)
    tf = min(f, 256)
    est = 2 * (r * d * 2 + 2 * d * tf * 4 + r * tf * 2) + 2 * d * tf * 2 + 3 * r * tf * 4
    hids = pl.pallas_call(
        functools.partial(_ffn_up_kernel, n_seg=n_seg),
        out_shape=[jax.ShapeDtypeStruct((e, ri, f), BF16) for ri in rs],
        grid=(e, f // tf),
        in_specs=[pl.BlockSpec((None, ri, d), lambda i, j: (i, 0, 0)) for ri in rs]
        + [pl.BlockSpec((None, None, d, tf), lambda i, j: (layer, i, 0, j))] * 2,
        out_specs=[pl.BlockSpec((None, ri, tf), lambda i, j: (i, 0, j)) for ri in rs],
        compiler_params=_params(("parallel", "arbitrary"), est),
        name="ffn_up",
    )(*xgs, w_gate, w_up)
    tn = min(d, 512)
    est = 2 * (r * f * 2 + f * tn * 4 + r * tn * 4 + r * LANES * 4) + f * tn * 2 + 2 * r * tn * 4
    return pl.pallas_call(
        functools.partial(_ffn_down_kernel, n_seg=n_seg),
        out_shape=[jax.ShapeDtypeStruct((e, 2, ri, d), BF16) for ri in rs],
        grid=(e, d // tn),
        in_specs=[pl.BlockSpec((None, ri, f), lambda i, j: (i, 0, 0)) for ri in rs]
        + [pl.BlockSpec((None, None, f, tn), lambda i, j: (layer, i, 0, j))]
        + [pl.BlockSpec((None, ri, 1), lambda i, j: (i, 0, 0)) for ri in rs],
        out_specs=[pl.BlockSpec((None, 2, ri, tn), lambda i, j: (i, 0, 0, j)) for ri in rs],
        compiler_params=_params(("parallel", "arbitrary"), est),
        name="ffn_down",
    )(*hids, w_down, *wts)


def _select_kernel(aff_ref, pos_ref, lor_ref, cnt_ref, off_ref, ps_ref, *, cap):
    e, g, ln = aff_ref.shape
    bits = pltpu.bitcast(aff_ref[...], jnp.int32)

    def count(mask):
        per_lane = jnp.sum(jnp.where(mask, 1.0, 0.0), axis=1)
        return jnp.sum(per_lane, axis=1, keepdims=True)[:, :, None]

    def search(i, t):
        cand = t | jnp.left_shift(jnp.int32(1), 30 - i)
        return jnp.where(count(bits >= cand) >= cap, cand, t)

    thr = lax.fori_loop(0, 31, search, jnp.zeros((e, 1, 1), jnp.int32))
    gt = bits > thr
    eq = bits == thr
    need = cap - count(gt)

    r0 = lax.broadcasted_iota(jnp.int32, (ln, ln), 0)
    r1 = lax.broadcasted_iota(jnp.int32, (ln, ln), 1)
    upper = jnp.where(r0 <= r1, 1.0, 0.0).astype(BF16)
    ones = jnp.ones((ln, ln), BF16)
    g0 = lax.broadcasted_iota(jnp.int32, (g, g), 0)
    g1 = lax.broadcasted_iota(jnp.int32, (g, g), 1)
    earlier = jnp.where(g1 < g0, 1.0, 0.0).astype(BF16)

    def prefix(mask):
        x = jnp.where(mask, 1.0, 0.0).astype(BF16).reshape(e * g, ln)
        incl = _dot(x, upper).reshape(e, g, ln)
        tot = _dot(x, ones).reshape(e, g, ln)
        off = jnp.stack([_dot(earlier, tot[i].astype(BF16)) for i in range(e)])
        return incl, tot, off

    incl_eq, _, off_eq = prefix(eq)
    sel = gt | (eq & (off_eq + incl_eq - 1.0 < need))
    incl, tot, off = prefix(sel)
    pos_ref[...] = jnp.where(sel, off + incl - 1.0, -1.0).astype(jnp.int32)
    cnt_ref[...] = tot.astype(jnp.int32)
    off_ref[...] = off.astype(jnp.int32)
    ps_ref[...] = incl

    rank = lax.broadcasted_iota(jnp.int32, (1, ln, 1), 1).astype(F32)
    ones8 = jnp.ones((8, ln), BF16)

    def lane_of_rank(i, c):
        below = jnp.where(ps_ref[i][:, None, :] <= rank, 1.0, 0.0).astype(BF16).reshape(g * ln, ln)
        lor_ref[i] = _dot_nt(ones8, below)[0:1]
        return c

    lax.fori_loop(0, e, lane_of_rank, 0)


def moe_select(aff_t, cap):
    e, n = aff_t.shape
    g = n // LANES
    g_pad = -(-g // 16) * 16
    a = aff_t.reshape(e, g, LANES)
    if g_pad != g:
        a = jnp.concatenate([a, jnp.full((e, g_pad - g, LANES), -1.0, F32)], axis=1)
    shp = jax.ShapeDtypeStruct((e, g_pad, LANES), jnp.int32)
    full = pl.BlockSpec((e, g_pad, LANES), lambda i: (0, 0, 0))
    pos, lor, cnt, off = pl.pallas_call(
        functools.partial(_select_kernel, cap=cap),
        out_shape=(shp, jax.ShapeDtypeStruct((e, 1, g_pad * LANES), F32), shp, shp),
        grid=(1,),
        in_specs=[full],
        out_specs=(full, pl.BlockSpec((e, 1, g_pad * LANES), lambda i: (0, 0, 0)), full, full),
        scratch_shapes=[pltpu.VMEM((e, g_pad, LANES), F32)],
        compiler_params=_params(("arbitrary",), 24 * e * g_pad * LANES * 4 + 4 * g_pad * LANES * LANES * 4),
        name="moe_select",
    )(a)
    return (pos.reshape(e, g_pad * LANES)[:, :n], lor.astype(jnp.int32), cnt[:, :, 0].reshape(e, 1, g_pad),
            off[:, :g, 0])


def _gather_copy(h_hbm, x_hbm, sem, e, token, slot):
    return pltpu.make_async_copy(h_hbm.at[token], x_hbm.at[e, slot], sem)


def _gather_kernel(lor_ref, cnt_ref, aff_ref, h_hbm, x_hbm, wt_ref, sem, *, chunk):
    e = pl.program_id(0)

    def wait_chunk():
        pltpu.make_async_copy(h_hbm.at[pl.ds(0, chunk)], x_hbm.at[e, pl.ds(0, chunk)], sem).wait()

    def group(g, slot):
        def row(r, slot):
            token = g * LANES + lor_ref[0, g * LANES + r]
            wt_ref[0, slot] = aff_ref[0, token]
            _gather_copy(h_hbm, x_hbm, sem, e, token, slot).start()

            @pl.when(jnp.logical_and(slot >= chunk, (slot + 1) % chunk == 0))
            def _():
                wait_chunk()

            return slot + 1

        return lax.fori_loop(0, cnt_ref[0, g], row, slot)

    lax.fori_loop(0, cnt_ref.shape[1], group, 0)
    wait_chunk()


def moe_gather(h, aff_t, lor, cnt, cap):
    n, d = h.shape
    e = lor.shape[0]
    chunk = min(cap, 256)
    h3 = h.reshape(n, d // LANES, LANES)

    def smem(width):
        return pl.BlockSpec((None, 1, width), lambda i: (i, 0, 0), memory_space=pltpu.SMEM)

    x, wt = pl.pallas_call(
        functools.partial(_gather_kernel, chunk=chunk),
        out_shape=(jax.ShapeDtypeStruct((e, cap, d // LANES, LANES), BF16), jax.ShapeDtypeStruct((e, 1, cap), F32)),
        grid=(e,),
        in_specs=[smem(lor.shape[2]), smem(cnt.shape[2]), smem(n), pl.BlockSpec(memory_space=pl.ANY)],
        out_specs=(pl.BlockSpec(memory_space=pl.ANY), smem(cap)),
        scratch_shapes=[pltpu.SemaphoreType.DMA(())],
        compiler_params=pltpu.CompilerParams(dimension_semantics=("arbitrary",)),
        name="moe_gather",
    )(lor, cnt, aff_t.reshape(e, 1, n), h3)
    return x.reshape(e, cap, d), wt.reshape(e, cap, 1)


def _window_copy(y_hbm, dst, sem, e, src, win):
    return pltpu.make_async_copy(y_hbm.at[e, :, pl.ds(src, win)], dst, sem)


def _combine_kernel(offb_ref, y_hbm, pos_ref, x_ref, gate_ref, g_ref, b_ref, o_ref, ybuf, ybuf_x, sem, acc_ref,
                    *, row, cap, win):
    b = pl.program_id(0)
    n_exp, tb = pos_ref.shape
    half = n_exp // 2

    def window(e, k):
        first = (offb_ref[e, b] // 8) * 8 + k * win
        return first, pl.multiple_of(jnp.minimum(first, cap - win), 8)

    def onehot2(e, first, src):
        slots = src + lax.broadcasted_iota(jnp.int32, (win, 1), 0)
        hit = jnp.logical_and(pos_ref[e:e + 1, :] == slots, slots >= first)
        oh = jnp.where(hit, 1.0, 0.0).astype(BF16)
        return jnp.concatenate([oh, oh], axis=0)

    def copies(h):
        return [_window_copy(y_hbm, ybuf.at[h, j], sem.at[h], h * half + j, window(h * half + j, 0)[1], win)
                for j in range(half)]

    pending = [copies(0), copies(1)]
    for cps in pending:
        for cp in cps:
            cp.start()
    acc = None
    for h in range(2):
        for cp in pending[h]:
            cp.wait()
        lhs = jnp.concatenate([onehot2(h * half + j, *window(h * half + j, 0)) for j in range(half)], axis=0)
        part = _dot_tn(lhs, ybuf[h].reshape(half * 2 * win, ybuf.shape[-1]))
        acc = part if acc is None else acc + part
    acc_ref[...] = acc

    for e in range(n_exp):
        n_win = (offb_ref[e, b + 1] - (offb_ref[e, b] // 8) * 8 + win - 1) // win

        def extra(k, c, e=e):
            first_k, src_k = window(e, k)
            cp = _window_copy(y_hbm, ybuf_x, sem.at[2], e, src_k, win)
            cp.start()
            cp.wait()
            acc_ref[...] += _dot_tn(onehot2(e, first_k, src_k), ybuf_x[...].reshape(2 * win, ybuf_x.shape[-1]))
            return c

        lax.fori_loop(1, n_win, extra, 0)

    z = DEEPNORM_ALPHA * x_ref[...] + gate_ref[row:row + 1, :] * acc_ref[...]
    o_ref[...] = _layer_norm(z) * g_ref[...] + b_ref[...]


def moe_combine_postnorm(y, pos, off, x, mod, row, k_gate, g, b):
    n, d = x.shape
    e, _, cap, _ = y.shape
    tb = min(n, 256)
    nb = n // tb
    win = min(cap, COMBINE_WINDOW)
    offb = jnp.concatenate([off[:, ::tb // LANES], jnp.full((e, 1), cap, jnp.int32)], axis=1)
    grid_spec = pltpu.PrefetchScalarGridSpec(
        num_scalar_prefetch=1,
        grid=(nb,),
        in_specs=[pl.BlockSpec(memory_space=pl.ANY),
                  pl.BlockSpec((e, tb), lambda i, o: (0, i)),
                  pl.BlockSpec((tb, d), lambda i, o: (i, 0)),
                  pl.BlockSpec((8, d), lambda i, o: (0, k_gate)),
                  pl.BlockSpec((1, d), lambda i, o: (0, 0)),
                  pl.BlockSpec((1, d), lambda i, o: (0, 0))],
        out_specs=pl.BlockSpec((tb, d), lambda i, o: (i, 0)),
        scratch_shapes=[pltpu.VMEM((2, e // 2, 2, win, d), BF16), pltpu.VMEM((2, win, d), BF16),
                        pltpu.SemaphoreType.DMA((3,)), pltpu.VMEM((tb, d), F32)],
    )
    est = 8 * tb * d * 4 + (e + 1) * 2 * win * d * 2 + e * win * tb * 2
    return pl.pallas_call(
        functools.partial(_combine_kernel, row=row, cap=cap, win=win),
        out_shape=jax.ShapeDtypeStruct((n, d), F32),
        grid_spec=grid_spec,
        compiler_params=_params(("arbitrary",), est),
        name="moe_combine",
    )(offb, y, pos, x, mod, g.reshape(1, d), b.reshape(1, d))


def moe_route(x, mod, row, w_router_t):
    m = x.shape[0]
    cap = max(1, EC_CAPACITY_FACTOR * m // N_EXPERTS)
    h, aff_t = moe_router(x, mod, row, 3, 4, w_router_t)
    pos, lor, cnt, off = moe_select(aff_t, cap)
    xg, wt = moe_gather(h, aff_t, lor, cnt, cap)
    return xg, wt, (pos, off)


def _rope_angles(n_tokens, rot_dim):
    rows = n_tokens // GRID_W
    row = jnp.repeat(jnp.arange(rows, dtype=F32), GRID_W)
    col = jnp.tile(jnp.arange(GRID_W, dtype=F32), rows)
    n_freq = rot_dim // 4
    inv = ROPE_THETA ** (-jnp.arange(n_freq, dtype=F32) / n_freq)
    return jnp.concatenate([row[:, None] * inv, col[:, None] * inv], axis=-1)


def _gqa_rope_tables(n_tokens):
    ang = _rope_angles(n_tokens, HEAD_DIM)
    c, s = jnp.cos(ang), jnp.sin(ang)
    return jnp.concatenate([c, c], axis=-1), jnp.concatenate([-s, s], axis=-1)


def _mla_rope_tables(n_tokens):
    ang = _rope_angles(n_tokens, MLA_ROPE)
    c, s = jnp.cos(ang), jnp.sin(ang)
    z = jnp.zeros_like(c)
    cos = jnp.concatenate([c, c, z, z], axis=-1)
    sa = jnp.concatenate([-s, z, z, z], axis=-1)
    sb = jnp.concatenate([z, s, z, z], axis=-1)
    return cos, sa, sb


def kernel(x, c, ctx, c_ctx, ada_w, ada_b, ln_g, ln_b, ev_w_in, ev_w_out, hgrn_lb, hgrn_norm_g, gqa_q_norm_g, gqa_k_norm_g, mla_w_down, mla_q_norm_g, mla_kv_norm_g, mla_w_uq, mla_w_ukv, mla_w_o, moe_router, moe_w_gate, moe_w_up, moe_w_down):
    d = D_MODEL
    xl = x[0]
    xc = ctx[0]
    n_lat, n_ctx = xl.shape[0], xc.shape[0]
    cc = jnp.zeros((8, d), F32).at[0].set(c[0]).at[1].set(c_ctx)
    lb_all = jnp.cumsum(jax.nn.softmax(hgrn_lb.astype(F32), axis=1), axis=1)
    gqa_tabs = _gqa_rope_tables(n_lat)
    gqa_tabs_ctx = [jnp.ones((n_ctx, HEAD_DIM), F32), jnp.zeros((n_ctx, HEAD_DIM), F32)]
    mla_tabs = _mla_rope_tables(n_lat)
    mla_tabs_ctx = [jnp.ones((n_ctx, HEAD_DIM), F32), jnp.zeros((n_ctx, HEAD_DIM), F32), jnp.zeros((n_ctx, HEAD_DIM), F32)]
    LAT, CTX = 0, 1

    for l in range(DEPTH):
        last = l == DEPTH - 1
        i = l // 2
        mod = adaln(cc, ada_w[l], ada_b[l])
        if l % 2 == 0:
            w_in = ev_w_in[i].astype(BF16)
            w_out = ev_w_out[i].astype(BF16)
            lb = lb_all[:, l].reshape(2, 1, A_WIDTH)
            scale = HEAD_DIM ** -0.5 * LOG2E
            proj_c = lnmod_matmul(xc, mod, CTX, 0, 1, w_in, 512)
            proj_l = lnmod_matmul(xl, mod, LAT, 0, 1, w_in, 512)
            s0 = jnp.zeros((2, A_HEADS, HEAD_DIM, HEAD_DIM), F32)
            o_c, s_c = hgrn_scan(proj_c, lb, s0)
            o_l, _ = hgrn_scan(proj_l, lb, s_c)
            a_l = hgrn_out(o_l, proj_l, hgrn_norm_g[i])
            qcol, kcol, vcol = 5 * A_HEADS, 5 * A_HEADS + B_Q_HEADS, 5 * A_WIDTH + B_WIDTH + B_KV_WIDTH
            q_l = norm_rope(proj_l, qcol, B_Q_HEADS, gqa_q_norm_g[i], *gqa_tabs, scale)
            k_l = norm_rope(proj_l, kcol, B_KV_HEADS, gqa_k_norm_g[i], *gqa_tabs, 1.0)
            k_c = norm_rope(proj_c, kcol, B_KV_HEADS, gqa_k_norm_g[i], *gqa_tabs_ctx, 1.0)
            v_l = proj_l[:, vcol:].astype(BF16)
            v_c = proj_c[:, vcol:].astype(BF16)
            att = dict(n_heads=B_Q_HEADS, n_kv_heads=B_KV_HEADS, dq=HEAD_DIM, dv=HEAD_DIM)
            b_l = flash_attention(q_l, k_l, v_l, k_c, v_c, **att)
            w_parts = [w_out[:A_WIDTH], w_out[A_WIDTH:]]
            xl_new = proj_postnorm([a_l, b_l], w_parts, xl, mod, LAT, 2, ln_g[l, 0], ln_b[l, 0])
            if not last:
                a_c = hgrn_out(o_c, proj_c, hgrn_norm_g[i])
                q_c = norm_rope(proj_c, qcol, B_Q_HEADS, gqa_q_norm_g[i], *gqa_tabs_ctx, scale)
                b_c = flash_attention(q_c, k_c, v_c, **att)
                xc = proj_postnorm([a_c, b_c], w_parts, xc, mod, CTX, 2, ln_g[l, 0], ln_b[l, 0])
            xl = xl_new
        else:
            hd = HEAD_DIM
            pad = (-mla_w_down.shape[2]) % hd
            w_down = jnp.pad(mla_w_down[i], ((0, 0), (0, pad))).astype(BF16)
            w_uq = mla_w_uq[i].reshape(MLA_Q_LORA, MLA_HEADS, hd + MLA_ROPE)
            w_uq = jnp.pad(w_uq, ((0, 0), (0, 0), (0, MLA_QK - hd - MLA_ROPE))).reshape(MLA_Q_LORA, MLA_HEADS * MLA_QK).astype(BF16)
            w_ukv = mla_w_ukv[i].reshape(MLA_KV_LORA, MLA_HEADS, 2 * hd)
            w_uk = w_ukv[:, :, :hd].reshape(MLA_KV_LORA, MLA_HEADS * hd).astype(BF16)
            w_uv = w_ukv[:, :, hd:].reshape(MLA_KV_LORA, MLA_HEADS * hd).astype(BF16)
            w_o = mla_w_o[i].astype(BF16)
            scale = (hd + MLA_ROPE) ** -0.5 * LOG2E
            dn_c = lnmod_matmul(xc, mod, CTX, 0, 1, w_down, w_down.shape[1])
            dn_l = lnmod_matmul(xl, mod, LAT, 0, 1, w_down, w_down.shape[1])
            q_l = mla_q(dn_l, mla_q_norm_g[i], w_uq, *mla_tabs, scale)
            k_l = mla_k(dn_l, mla_kv_norm_g[i], w_uk, *mla_tabs)
            k_c = mla_k(dn_c, mla_kv_norm_g[i], w_uk, *mla_tabs_ctx)
            v_l = rms_matmul(dn_l, 1, mla_kv_norm_g[i], w_uv, 1024)
            v_c = rms_matmul(dn_c, 1, mla_kv_norm_g[i], w_uv, 1024)
            att = dict(n_heads=MLA_HEADS, n_kv_heads=MLA_HEADS, dq=MLA_QK, dv=hd)
            o_l = flash_attention(q_l, k_l, v_l, k_c, v_c, **att)
            xl_new = proj_postnorm([o_l], [w_o], xl, mod, LAT, 2, ln_g[l, 0], ln_b[l, 0])
            if not last:
                q_c = mla_q(dn_c, mla_q_norm_g[i], w_uq, *mla_tabs_ctx, scale)
                o_c = flash_attention(q_c, k_c, v_c, **att)
                xc = proj_postnorm([o_c], [w_o], xc, mod, CTX, 2, ln_g[l, 0], ln_b[l, 0])
            xl = xl_new

        w_router_t = moe_router[l].T
        segs = [(xl, LAT)] if last else [(xl, LAT), (xc, CTX)]
        routes = [moe_route(xs, mod, row, w_router_t) for xs, row in segs]
        ys = expert_ffn([r[0] for r in routes], [r[1] for r in routes], moe_w_gate, moe_w_up, moe_w_down, l)
        outs = [moe_combine_postnorm(y, *r[2], xs, mod, row, 5, ln_g[l, 1], ln_b[l, 1])
                for y, r, (xs, row) in zip(ys, routes, segs)]
        xl = outs[0]
        if not last:
            xc = outs[1]
    return xl[None]
```

```python
import functools
import math

import numpy as np
import jax
import jax.numpy as jnp
from jax import lax
from jax.experimental import pallas as pl
from jax.experimental.pallas import tpu as pltpu

F32 = jnp.float32
BF16 = jnp.bfloat16

D_MODEL = 2048
DEPTH = 2
GRID_W = 64
HEAD_DIM = 128
A_HEADS = D_MODEL // 256
A_WIDTH = A_HEADS * HEAD_DIM
B_Q_HEADS = D_MODEL // 256
B_KV_HEADS = 2
B_WIDTH = B_Q_HEADS * HEAD_DIM
B_KV_WIDTH = B_KV_HEADS * HEAD_DIM
MLA_HEADS = D_MODEL // 128
MLA_Q_LORA = 512
MLA_KV_LORA = 512
MLA_ROPE = 64
MLA_QK = 2 * HEAD_DIM
N_EXPERTS = 16
EXPERT_FF = D_MODEL // 2
EC_CAPACITY_FACTOR = 2
ROPE_THETA = 10000.0
NORM_EPS = 1e-6
DEEPNORM_ALPHA = (2.0 * DEPTH) ** 0.25

HGRN_CHUNK = 128
COMBINE_WINDOW = 64
LANES = 128
LOG2E = math.log2(math.e)
V7X_VMEM_BYTES = 64 * 1024 * 1024
VMEM_CAP_BYTES = V7X_VMEM_BYTES - 8 * 1024 * 1024


def _params(semantics, vmem_estimate_bytes):
    limit = int(min(max(2 * vmem_estimate_bytes, 32 * 1024 * 1024), VMEM_CAP_BYTES))
    return pltpu.CompilerParams(dimension_semantics=semantics, vmem_limit_bytes=limit)


def _layer_norm(x):
    mu = jnp.mean(x, axis=-1, keepdims=True)
    xc = x - mu
    var = jnp.mean(xc * xc, axis=-1, keepdims=True)
    return xc * lax.rsqrt(var + NORM_EPS)


def _rms(x):
    return x * lax.rsqrt(jnp.mean(x * x, axis=-1, keepdims=True) + NORM_EPS)


def _dot(a, b):
    return jnp.dot(a, b, preferred_element_type=F32)


def _dot_nt(a, b):
    return lax.dot_general(a, b, (((1,), (1,)), ((), ())), preferred_element_type=F32)


def _dot_tn(a, b):
    return lax.dot_general(a, b, (((0,), (0,)), ((), ())), preferred_element_type=F32)


def _split3(x):
    x1 = x.astype(BF16)
    r1 = x - x1.astype(F32)
    x2 = r1.astype(BF16)
    x3 = (r1 - x2.astype(F32)).astype(BF16)
    return x1, x2, x3


def _adaln_kernel(c_ref, w_ref, b_ref, o_ref):
    c = c_ref[...]
    s = c * jax.nn.sigmoid(c)
    w = w_ref[...]
    s1, s2, s3 = _split3(s)
    w1, w2, w3 = _split3(w)
    acc = _dot(s1, w3) + _dot(s3, w1) + _dot(s2, w2)
    acc = acc + _dot(s1, w2) + _dot(s2, w1)
    acc = acc + _dot(s1, w1)
    o_ref[...] = acc + b_ref[...]


def adaln(cc, w, b):
    d, n = w.shape
    tn = 1536 if n % 1536 == 0 else n
    est = 2 * d * tn * 4 * 2
    return pl.pallas_call(
        _adaln_kernel,
        out_shape=jax.ShapeDtypeStruct((8, n), F32),
        grid=(n // tn,),
        in_specs=[pl.BlockSpec((8, d), lambda j: (0, 0)),
                  pl.BlockSpec((d, tn), lambda j: (0, j)),
                  pl.BlockSpec((1, tn), lambda j: (0, j))],
        out_specs=pl.BlockSpec((8, tn), lambda j: (0, j)),
        compiler_params=_params(("parallel",), est),
        name="adaln",
    )(cc, w, b.reshape(1, n))


def _lnmod_mm_kernel(x_ref, sh_ref, sc_ref, w_ref, o_ref, h_ref, *, row):
    @pl.when(pl.program_id(1) == 0)
    def _():
        hn = _layer_norm(x_ref[...])
        h = hn * (1.0 + sc_ref[row:row + 1, :]) + sh_ref[row:row + 1, :]
        h_ref[...] = h.astype(BF16)

    o_ref[...] = _dot(h_ref[...], w_ref[...]).astype(o_ref.dtype)


def lnmod_matmul(x, mod, row, k_shift, k_scale, w, tn):
    m, d = x.shape
    n = w.shape[1]
    tm = min(m, 1024)
    est = 2 * tm * d * 4 + tm * d * 2 + 2 * d * tn * 2 + 2 * tm * tn * 4
    return pl.pallas_call(
        functools.partial(_lnmod_mm_kernel, row=row),
        out_shape=jax.ShapeDtypeStruct((m, n), F32),
        grid=(m // tm, n // tn),
        in_specs=[pl.BlockSpec((tm, d), lambda i, j: (i, 0)),
                  pl.BlockSpec((8, d), lambda i, j: (0, k_shift)),
                  pl.BlockSpec((8, d), lambda i, j: (0, k_scale)),
                  pl.BlockSpec((d, tn), lambda i, j: (0, j))],
        out_specs=pl.BlockSpec((tm, tn), lambda i, j: (i, j)),
        scratch_shapes=[pltpu.VMEM((tm, d), BF16)],
        compiler_params=_params(("parallel", "arbitrary"), est),
        name="lnmod_matmul",
    )(x, mod, mod, w)


def _hgrn_tables(c):
    n_lvl = int(math.log2(c))
    r = np.arange(c)
    u = np.arange(c)[None, :]
    blocks, masks = [], []
    for l in range(n_lvl):
        half = 1 << l
        base = (r // (2 * half)) * (2 * half)
        anchor = (base + half - 1)[:, None]
        upper = (r >= base + half)[:, None]
        rr = r[:, None]
        blocks.append(np.where(upper, (u > anchor) & (u <= rr), (u > rr) & (u <= anchor)))
        same = (r[:, None] // (2 * half)) == (r[None, :] // (2 * half))
        masks.append(same & upper & ~(upper.T))
    blocks.append(u <= r[:, None])
    blocks.append(u > r[:, None])
    blocks.append(np.ones((16, c), bool))
    masks.append(np.eye(c, dtype=bool))
    fwd_s = np.concatenate(blocks, axis=0).astype(np.float32)
    fwd_m = np.stack(masks).astype(np.float32)
    bwd_s = np.concatenate([b[::-1, ::-1] for b in blocks], axis=0).astype(np.float32)
    bwd_m = fwd_m[:, ::-1, ::-1]
    return (jnp.asarray(np.stack([fwd_s, bwd_s]), BF16), jnp.asarray(np.stack([fwd_m, bwd_m]), F32))


def _hgrn_kernel(q_ref, v_ref, f_ref, lb_ref, sums_ref, mask_ref, s0_ref, o_ref, sfin_ref, st_ref):
    c = q_ref.shape[0]
    hd = HEAD_DIM
    n_lvl = mask_ref.shape[0] - 1
    j = pl.program_id(1)

    @pl.when(j == 0)
    def _():
        st_ref[...] = s0_ref[...]

    for h in range(q_ref.shape[1] // hd):
        cols = slice(h * hd, (h + 1) * hd)
        q = q_ref[:, cols]
        vb = v_ref[:, cols].astype(BF16)
        lb = lb_ref[:, cols]
        f = lb + (1.0 - lb) * jax.nn.sigmoid(f_ref[:, cols])
        g = jnp.log(f)
        k = 1.0 - f
        g1 = g.astype(BF16)
        g2 = (g - g1.astype(F32)).astype(BF16)
        e2 = _dot(sums_ref[...], jnp.concatenate([g1, g2], axis=1))
        e = e2[:, hd:] + e2[:, :hd]

        scores = _dot_nt(q.astype(BF16), k.astype(BF16)) * mask_ref[n_lvl]
        for l in range(n_lvl):
            z = jnp.exp(e[l * c:(l + 1) * c])
            scores = scores + _dot_nt((q * z).astype(BF16), (k * z).astype(BF16)) * mask_ref[l]

        cum = e[n_lvl * c:(n_lvl + 1) * c]
        rem = e[(n_lvl + 1) * c:(n_lvl + 2) * c]
        tot = e[(n_lvl + 2) * c:(n_lvl + 2) * c + 1]
        st = st_ref[h]
        o = _dot(scores.astype(BF16), vb) + _dot_nt((q * jnp.exp(cum)).astype(BF16), st.astype(BF16))
        o_ref[:, cols] = o
        st_new = st * jnp.exp(tot) + _dot_tn(vb, (k * jnp.exp(rem)).astype(BF16))
        st_ref[h] = st_new

    @pl.when(j == pl.num_programs(1) - 1)
    def _():
        sfin_ref[...] = st_ref[...]


def hgrn_scan(proj, lb, s0):
    seq = proj.shape[0]
    c = HGRN_CHUNK
    nc = seq // c
    sums, masks = _hgrn_tables(c)
    hd, w = HEAD_DIM, A_WIDTH

    def blk(d, j):
        return jnp.where(d == 0, j, nc - 1 - j)

    est = (2 * (4 * c * w * 4 + sums.shape[1] * c * 2 + masks.shape[1] * c * c * 4 + 2 * A_HEADS * hd * hd * 4)
           + A_HEADS * hd * hd * 4)
    return pl.pallas_call(
        _hgrn_kernel,
        out_shape=(jax.ShapeDtypeStruct((2, seq, w), F32),
                   jax.ShapeDtypeStruct((2, A_HEADS, hd, hd), F32)),
        grid=(2, nc),
        in_specs=[pl.BlockSpec((c, w), lambda d, j: (blk(d, j), 0)),
                  pl.BlockSpec((c, w), lambda d, j: (blk(d, j), 3)),
                  pl.BlockSpec((c, w), lambda d, j: (blk(d, j), 1 + d)),
                  pl.BlockSpec((None, 1, w), lambda d, j: (d, 0, 0)),
                  pl.BlockSpec((None, sums.shape[1], c), lambda d, j: (d, 0, 0)),
                  pl.BlockSpec((None, masks.shape[1], c, c), lambda d, j: (d, 0, 0, 0)),
                  pl.BlockSpec((None, A_HEADS, hd, hd), lambda d, j: (d, 0, 0, 0))],
        out_specs=(pl.BlockSpec((None, c, w), lambda d, j: (d, blk(d, j), 0)),
                   pl.BlockSpec((None, A_HEADS, hd, hd), lambda d, j: (d, 0, 0, 0))),
        scratch_shapes=[pltpu.VMEM((A_HEADS, hd, hd), F32)],
        compiler_params=_params(("parallel", "arbitrary"), est),
        name="hgrn_scan",
    )(proj, proj, proj, lb, sums, masks, s0)


def _hgrn_out_kernel(o_ref, gate_ref, g_ref, a_ref):
    o = o_ref[0] + o_ref[1]
    gate = gate_ref[...]
    a_ref[...] = (_rms(o) * g_ref[...] * (gate * jax.nn.sigmoid(gate))).astype(a_ref.dtype)


def hgrn_out(o, proj, norm_g):
    seq = o.shape[1]
    tm = min(seq, 512)
    hd = HEAD_DIM
    return pl.pallas_call(
        _hgrn_out_kernel,
        out_shape=jax.ShapeDtypeStruct((seq, A_WIDTH), BF16),
        grid=(seq // tm, A_HEADS),
        in_specs=[pl.BlockSpec((2, tm, hd), lambda i, h: (0, i, h)),
                  pl.BlockSpec((tm, hd), lambda i, h: (i, 4 * A_HEADS + h)),
                  pl.BlockSpec((1, hd), lambda i, h: (0, 0))],
        out_specs=pl.BlockSpec((tm, hd), lambda i, h: (i, h)),
        compiler_params=_params(("parallel", "parallel"), 8 * tm * hd * 4),
        name="hgrn_out",
    )(o, proj, norm_g.reshape(1, hd))


def _norm_rope_kernel(x_ref, g_ref, cos_ref, sin_ref, o_ref, *, scale):
    y = _rms(x_ref[...]) * g_ref[...]
    y = y * cos_ref[...] + pltpu.roll(y, HEAD_DIM // 2, 1) * sin_ref[...]
    o_ref[...] = (y * scale).astype(o_ref.dtype)


def norm_rope(proj, col0, n_heads, g, cos, sin, scale):
    seq = proj.shape[0]
    tm = min(seq, 512)
    hd = HEAD_DIM
    return pl.pallas_call(
        functools.partial(_norm_rope_kernel, scale=scale),
        out_shape=jax.ShapeDtypeStruct((seq, n_heads * hd), BF16),
        grid=(seq // tm, n_heads),
        in_specs=[pl.BlockSpec((tm, hd), lambda i, h: (i, col0 + h)),
                  pl.BlockSpec((1, hd), lambda i, h: (0, 0)),
                  pl.BlockSpec((tm, hd), lambda i, h: (i, 0)),
                  pl.BlockSpec((tm, hd), lambda i, h: (i, 0))],
        out_specs=pl.BlockSpec((tm, hd), lambda i, h: (i, h)),
        compiler_params=_params(("parallel", "parallel"), 10 * tm * hd * 4),
        name="norm_rope",
    )(proj, g.reshape(1, hd), cos, sin)


def _flash_update(q, k, v, m_ref, l_ref, acc_ref):
    s = _dot_nt(q, k)
    m_prev = m_ref[...]
    m_new = jnp.maximum(m_prev, jnp.max(s, axis=-1, keepdims=True))
    alpha = jnp.exp2(m_prev - m_new)
    ps = [jnp.exp2(s[:, c * LANES:(c + 1) * LANES] - m_new) for c in range(s.shape[1] // LANES)]
    psum = ps[0]
    for pc in ps[1:]:
        psum = psum + pc
    p = jnp.concatenate([pc.astype(BF16) for pc in ps], axis=1)
    l_ref[...] = alpha * l_ref[...] + psum
    acc_ref[...] = alpha * acc_ref[...] + _dot(p, v)
    m_ref[...] = m_new


def _flash_kernel(*refs, has_ctx):
    if has_ctx:
        q_ref, k_ref, v_ref, kc_ref, vc_ref, o_ref, m_ref, l_ref, acc_ref = refs
    else:
        q_ref, k_ref, v_ref, o_ref, m_ref, l_ref, acc_ref = refs
    j = pl.program_id(2)

    @pl.when(j == 0)
    def _():
        m_ref[...] = jnp.full(m_ref.shape, -jnp.inf, F32)
        l_ref[...] = jnp.zeros(l_ref.shape, F32)
        acc_ref[...] = jnp.zeros(acc_ref.shape, F32)
        if has_ctx:
            _flash_update(q_ref[...], kc_ref[...], vc_ref[...], m_ref, l_ref, acc_ref)

    _flash_update(q_ref[...], k_ref[...], v_ref[...], m_ref, l_ref, acc_ref)

    @pl.when(j == pl.num_programs(2) - 1)
    def _():
        l = jnp.sum(l_ref[...], axis=-1, keepdims=True)
        o_ref[...] = (acc_ref[...] / l).astype(o_ref.dtype)


def flash_attention(q, k, v, k_ctx=None, v_ctx=None, *, n_heads, n_kv_heads, dq, dv):
    n, m = q.shape[0], k.shape[0]
    grp = n_heads // n_kv_heads
    tq = min(n, 2048)
    tk = min(m, 2048)
    has_ctx = k_ctx is not None
    in_specs = [pl.BlockSpec((tq, dq), lambda h, i, j: (i, h)),
                pl.BlockSpec((tk, dq), lambda h, i, j: (j, h // grp)),
                pl.BlockSpec((tk, dv), lambda h, i, j: (j, h // grp))]
    args = [q, k, v]
    if has_ctx:
        mc = k_ctx.shape[0]
        in_specs += [pl.BlockSpec((mc, dq), lambda h, i, j: (0, h // grp)),
                     pl.BlockSpec((mc, dv), lambda h, i, j: (0, h // grp))]
        args += [k_ctx, v_ctx]
    est = 2 * (tq * dq + tk * dq + tk * dv + tq * dv) * 2 + tq * (dv + 256) * 4 + 6 * tq * tk * 4
    return pl.pallas_call(
        functools.partial(_flash_kernel, has_ctx=has_ctx),
        out_shape=jax.ShapeDtypeStruct((n, n_heads * dv), BF16),
        grid=(n_heads, n // tq, m // tk),
        in_specs=in_specs,
        out_specs=pl.BlockSpec((tq, dv), lambda h, i, j: (i, h)),
        scratch_shapes=[pltpu.VMEM((tq, LANES), F32), pltpu.VMEM((tq, LANES), F32), pltpu.VMEM((tq, dv), F32)],
        compiler_params=_params(("parallel", "parallel", "arbitrary"), est),
        name="flash_attention",
    )(*args)


def _proj_postnorm_kernel(*refs, n_in, row):
    a_refs = refs[:n_in]
    w_refs = refs[n_in:2 * n_in]
    x_ref, gate_ref, g_ref, b_ref, o_ref = refs[2 * n_in:]
    y = _dot(a_refs[0][...], w_refs[0][...])
    for a_ref, w_ref in zip(a_refs[1:], w_refs[1:]):
        y = y + _dot(a_ref[...], w_ref[...])
    z = DEEPNORM_ALPHA * x_ref[...] + gate_ref[row:row + 1, :] * y
    o_ref[...] = _layer_norm(z) * g_ref[...] + b_ref[...]


def proj_postnorm(acts, ws, x, mod, row, k_gate, g, b):
    m, d = x.shape
    tm = min(m, 512)
    n_in = len(acts)
    once = pl.Buffered(1)
    in_specs = [pl.BlockSpec((tm, a.shape[1]), lambda i: (i, 0)) for a in acts]
    in_specs += [pl.BlockSpec(w.shape, lambda i: (0, 0), pipeline_mode=once) for w in ws]
    in_specs += [pl.BlockSpec((tm, d), lambda i: (i, 0)),
                 pl.BlockSpec((8, d), lambda i: (0, k_gate)),
                 pl.BlockSpec((1, d), lambda i: (0, 0)),
                 pl.BlockSpec((1, d), lambda i: (0, 0))]
    est = sum(w.size * 2 for w in ws) + sum(2 * tm * a.shape[1] * 2 for a in acts) + 6 * tm * d * 4
    return pl.pallas_call(
        functools.partial(_proj_postnorm_kernel, n_in=n_in, row=row),
        out_shape=jax.ShapeDtypeStruct((m, d), F32),
        grid=(m // tm,),
        in_specs=in_specs,
        out_specs=pl.BlockSpec((tm, d), lambda i: (i, 0)),
        compiler_params=_params(("parallel",), est),
        name="proj_postnorm",
    )(*acts, *ws, x, mod, g.reshape(1, d), b.reshape(1, d))


def _add_postnorm_kernel(y_ref, x_ref, gate_ref, g_ref, b_ref, o_ref, *, row):
    z = DEEPNORM_ALPHA * x_ref[...] + gate_ref[row:row + 1, :] * y_ref[...]
    o_ref[...] = _layer_norm(z) * g_ref[...] + b_ref[...]


def add_postnorm(y, x, mod, row, k_gate, g, b):
    m, d = x.shape
    tm = min(m, 512)
    return pl.pallas_call(
        functools.partial(_add_postnorm_kernel, row=row),
        out_shape=jax.ShapeDtypeStruct((m, d), F32),
        grid=(m // tm,),
        in_specs=[pl.BlockSpec((tm, d), lambda i: (i, 0)),
                  pl.BlockSpec((tm, d), lambda i: (i, 0)),
                  pl.BlockSpec((8, d), lambda i: (0, k_gate)),
                  pl.BlockSpec((1, d), lambda i: (0, 0)),
                  pl.BlockSpec((1, d), lambda i: (0, 0))],
        out_specs=pl.BlockSpec((tm, d), lambda i: (i, 0)),
        compiler_params=_params(("parallel",), 8 * tm * d * 4),
        name="add_postnorm",
    )(y, x, mod, g.reshape(1, d), b.reshape(1, d))


def _rms_mm_kernel(x_ref, g_ref, w_ref, o_ref, a_ref):
    @pl.when(pl.program_id(1) == 0)
    def _():
        a_ref[...] = (_rms(x_ref[...]) * g_ref[...]).astype(BF16)

    o_ref[...] = _dot(a_ref[...], w_ref[...]).astype(o_ref.dtype)


def _mla_q_kernel(x_ref, g_ref, w_ref, cos_ref, sa_ref, sb_ref, o_ref, a_ref, *, scale):
    @pl.when(pl.program_id(1) == 0)
    def _():
        a_ref[...] = (_rms(x_ref[...]) * g_ref[...]).astype(BF16)

    y = _dot(a_ref[...], w_ref[...])
    hd = HEAD_DIM
    for h in range(y.shape[1] // MLA_QK):
        c0 = h * MLA_QK
        o_ref[:, c0:c0 + hd] = (y[:, c0:c0 + hd] * scale).astype(o_ref.dtype)
        r = y[:, c0 + hd:c0 + 2 * hd]
        r = r * cos_ref[...] + pltpu.roll(r, hd - MLA_ROPE // 2, 1) * sa_ref[...] + pltpu.roll(r, MLA_ROPE // 2, 1) * sb_ref[...]
        o_ref[:, c0 + hd:c0 + 2 * hd] = (r * scale).astype(o_ref.dtype)


def _mla_k_kernel(x_ref, g_ref, w_ref, kr_ref, cos_ref, sa_ref, sb_ref, o_ref, a_ref, r_ref):
    hd = HEAD_DIM

    @pl.when(pl.program_id(1) == 0)
    def _():
        a_ref[...] = (_rms(x_ref[...]) * g_ref[...]).astype(BF16)
        r = kr_ref[...]
        r = r * cos_ref[...] + pltpu.roll(r, hd - MLA_ROPE // 2, 1) * sa_ref[...] + pltpu.roll(r, MLA_ROPE // 2, 1) * sb_ref[...]
        r_ref[...] = r.astype(BF16)

    y = _dot(a_ref[...], w_ref[...])
    for h in range(y.shape[1] // hd):
        o_ref[:, h * MLA_QK:h * MLA_QK + hd] = y[:, h * hd:(h + 1) * hd].astype(o_ref.dtype)
        o_ref[:, h * MLA_QK + hd:(h + 1) * MLA_QK] = r_ref[...]


def _mla_specs(m, tm, lora, col_blk):
    return [pl.BlockSpec((tm, lora), lambda i, j: (i, col_blk)),
            pl.BlockSpec((1, lora), lambda i, j: (0, 0))]


def rms_matmul(dn, col_blk, g, w, tn):
    m = dn.shape[0]
    lora, n = w.shape
    tm = min(m, 1024)
    est = 2 * tm * lora * 4 + tm * lora * 2 + 2 * lora * tn * 2 + 2 * tm * tn * 2 + tm * tn * 4
    return pl.pallas_call(
        _rms_mm_kernel,
        out_shape=jax.ShapeDtypeStruct((m, n), BF16),
        grid=(m // tm, n // tn),
        in_specs=_mla_specs(m, tm, lora, col_blk) + [pl.BlockSpec((lora, tn), lambda i, j: (0, j))],
        out_specs=pl.BlockSpec((tm, tn), lambda i, j: (i, j)),
        scratch_shapes=[pltpu.VMEM((tm, lora), BF16)],
        compiler_params=_params(("parallel", "arbitrary"), est),
        name="rms_matmul",
    )(dn, g.reshape(1, lora), w)


def mla_q(dn, g, w, cos, sa, sb, scale):
    m = dn.shape[0]
    lora, n = w.shape
    tm = min(m, 1024)
    tn = 4 * MLA_QK
    hd = HEAD_DIM
    est = 2 * tm * lora * 4 + tm * lora * 2 + 2 * lora * tn * 2 + 2 * tm * tn * 2 + 2 * tm * tn * 4 + 6 * tm * hd * 4
    rope_spec = pl.BlockSpec((tm, hd), lambda i, j: (i, 0))
    return pl.pallas_call(
        functools.partial(_mla_q_kernel, scale=scale),
        out_shape=jax.ShapeDtypeStruct((m, n), BF16),
        grid=(m // tm, n // tn),
        in_specs=_mla_specs(m, tm, lora, 0) + [pl.BlockSpec((lora, tn), lambda i, j: (0, j)),
                                               rope_spec, rope_spec, rope_spec],
        out_specs=pl.BlockSpec((tm, tn), lambda i, j: (i, j)),
        scratch_shapes=[pltpu.VMEM((tm, lora), BF16)],
        compiler_params=_params(("parallel", "arbitrary"), est),
        name="mla_q",
    )(dn, g.reshape(1, lora), w, cos, sa, sb)


def mla_k(dn, g, w, cos, sa, sb):
    m = dn.shape[0]
    lora, n = w.shape
    tm = min(m, 1024)
    hd = HEAD_DIM
    tn = 4 * hd
    kr_blk = (MLA_Q_LORA + MLA_KV_LORA) // hd
    est = 2 * tm * lora * 4 + tm * lora * 2 + 2 * lora * tn * 2 + 4 * tm * tn * 2 + tm * tn * 4 + 8 * tm * hd * 4
    rope_spec = pl.BlockSpec((tm, hd), lambda i, j: (i, 0))
    return pl.pallas_call(
        _mla_k_kernel,
        out_shape=jax.ShapeDtypeStruct((m, 2 * n), BF16),
        grid=(m // tm, n // tn),
        in_specs=_mla_specs(m, tm, lora, 1) + [pl.BlockSpec((lora, tn), lambda i, j: (0, j)),
                                               pl.BlockSpec((tm, hd), lambda i, j: (i, kr_blk)),
                                               rope_spec, rope_spec, rope_spec],
        out_specs=pl.BlockSpec((tm, 2 * tn), lambda i, j: (i, j)),
        scratch_shapes=[pltpu.VMEM((tm, lora), BF16), pltpu.VMEM((tm, hd), BF16)],
        compiler_params=_params(("parallel", "arbitrary"), est),
        name="mla_k",
    )(dn, g.reshape(1, lora), w, dn, cos, sa, sb)


def _router_kernel(x_ref, sh_ref, sc_ref, wr_ref, h_ref, aff_ref, *, row):
    hn = _layer_norm(x_ref[...])
    h = hn * (1.0 + sc_ref[row:row + 1, :]) + sh_ref[row:row + 1, :]
    hb = h.astype(BF16)
    half = h.shape[1] // 2
    bits = pltpu.bitcast(hb.astype(F32), jnp.uint32)
    h_ref[...] = (bits[:, half:] & jnp.uint32(0xFFFF0000)) | (bits[:, :half] >> 16)
    w = wr_ref[...]
    w1 = w.astype(BF16)
    w2 = (w - w1.astype(F32)).astype(BF16)
    h2 = (h - hb.astype(F32)).astype(BF16)
    logits = _dot_nt(w1, hb) + (_dot_nt(w2, hb) + _dot_nt(w1, h2))
    mx = jnp.max(logits, axis=0, keepdims=True)
    p = jnp.exp(logits - mx)
    aff_ref[...] = p / jnp.sum(p, axis=0, keepdims=True)


def moe_router(x, mod, row, k_shift, k_scale, w_router_t):
    m, d = x.shape
    e = w_router_t.shape[0]
    tm = min(m, 512)
    return pl.pallas_call(
        functools.partial(_router_kernel, row=row),
        out_shape=(jax.ShapeDtypeStruct((m, d // 2), jnp.uint32), jax.ShapeDtypeStruct((e, m), F32)),
        grid=(m // tm,),
        in_specs=[pl.BlockSpec((tm, d), lambda i: (i, 0)),
                  pl.BlockSpec((8, d), lambda i: (0, k_shift)),
                  pl.BlockSpec((8, d), lambda i: (0, k_scale)),
                  pl.BlockSpec((e, d), lambda i: (0, 0))],
        out_specs=(pl.BlockSpec((tm, d // 2), lambda i: (i, 0)), pl.BlockSpec((e, tm), lambda i: (0, i))),
        compiler_params=_params(("parallel",), 8 * tm * d * 4),
        name="moe_router",
    )(x, mod, mod, w_router_t)


def _ffn_up_kernel(*refs, n_seg):
    x_refs, (wg_ref, wu_ref) = refs[:n_seg], refs[n_seg:n_seg + 2]
    o_refs, xs_refs = refs[n_seg + 2:2 * n_seg + 2], refs[2 * n_seg + 2:]

    @pl.when(pl.program_id(1) == 0)
    def _():
        for x_ref, xs_ref in zip(x_refs, xs_refs):
            word = x_ref[...]
            first = pltpu.bitcast(word << 16, F32)
            second = pltpu.bitcast(word & jnp.uint32(0xFFFF0000), F32)
            xs_ref[...] = jnp.concatenate([first, second], axis=1).astype(BF16)

    wg = wg_ref[...].astype(BF16)
    wu = wu_ref[...].astype(BF16)
    for xs_ref, o_ref in zip(xs_refs, o_refs):
        x = xs_ref[...]
        g = _dot(x, wg)
        u = _dot(x, wu)
        o_ref[...] = (g * jax.nn.sigmoid(g) * u).astype(o_ref.dtype)


def _ffn_down_kernel(*refs, n_seg):
    h_refs, wd_ref, wt_refs, o_refs = refs[:n_seg], refs[n_seg], refs[n_seg + 1:2 * n_seg + 1], refs[2 * n_seg + 1:]
    wd = wd_ref[...].astype(BF16)
    for h_ref, wt_ref, o_ref in zip(h_refs, wt_refs, o_refs):
        y = _dot(h_ref[...], wd) * wt_ref[...]
        hi = y.astype(BF16)
        o_ref[0] = hi
        o_ref[1] = (y - hi.astype(F32)).astype(BF16)


def expert_ffn(xgs, wts, w_gate, w_up, w_down, layer):
    n_seg = len(xgs)
    e = xgs[0].shape[0]
    d = 2 * xgs[0].shape[2]
    f = w_gate.shape[3]
    rs = [x.shape[1] for x in xgs]
    r = sum(rs)
    tf = min(f, 256)
    est = 2 * (r * d * 2 + 2 * d * tf * 4 + r * tf * 2) + r * d * 2 + 2 * d * tf * 2 + 3 * r * tf * 4
    hids = pl.pallas_call(
        functools.partial(_ffn_up_kernel, n_seg=n_seg),
        out_shape=[jax.ShapeDtypeStruct((e, ri, f), BF16) for ri in rs],
        grid=(e, f // tf),
        in_specs=[pl.BlockSpec((None, ri, d // 2), lambda i, j: (i, 0, 0)) for ri in rs]
        + [pl.BlockSpec((None, None, d, tf), lambda i, j: (layer, i, 0, j))] * 2,
        out_specs=[pl.BlockSpec((None, ri, tf), lambda i, j: (i, 0, j)) for ri in rs],
        scratch_shapes=[pltpu.VMEM((ri, d), BF16) for ri in rs],
        compiler_params=_params(("parallel", "arbitrary"), est),
        name="ffn_up",
    )(*xgs, w_gate, w_up)
    tn = min(d, 512)
    est = 2 * (r * f * 2 + f * tn * 4 + r * tn * 4 + r * LANES * 4) + f * tn * 2 + 2 * r * tn * 4
    return pl.pallas_call(
        functools.partial(_ffn_down_kernel, n_seg=n_seg),
        out_shape=[jax.ShapeDtypeStruct((e, 2, ri, d), BF16) for ri in rs],
        grid=(e, d // tn),
        in_specs=[pl.BlockSpec((None, ri, f), lambda i, j: (i, 0, 0)) for ri in rs]
        + [pl.BlockSpec((None, None, f, tn), lambda i, j: (layer, i, 0, j))]
        + [pl.BlockSpec((None, ri, 1), lambda i, j: (i, 0, 0)) for ri in rs],
        out_specs=[pl.BlockSpec((None, 2, ri, tn), lambda i, j: (i, 0, 0, j)) for ri in rs],
        compiler_params=_params(("parallel", "arbitrary"), est),
        name="ffn_down",
    )(*hids, w_down, *wts)


def _select_kernel(aff_ref, pos_ref, lor_ref, cnt_ref, off_ref, ps_ref, *, cap):
    e, g, ln = aff_ref.shape
    bits = pltpu.bitcast(aff_ref[...], jnp.int32)

    def count(mask):
        per_lane = jnp.sum(jnp.where(mask, 1.0, 0.0), axis=1)
        return jnp.sum(per_lane, axis=1, keepdims=True)[:, :, None]

    def search(i, t):
        cand = t | jnp.left_shift(jnp.int32(1), 30 - i)
        return jnp.where(count(bits >= cand) >= cap, cand, t)

    thr = lax.fori_loop(0, 31, search, jnp.zeros((e, 1, 1), jnp.int32))
    gt = bits > thr
    eq = bits == thr
    need = cap - count(gt)

    r0 = lax.broadcasted_iota(jnp.int32, (ln, ln), 0)
    r1 = lax.broadcasted_iota(jnp.int32, (ln, ln), 1)
    upper = jnp.where(r0 <= r1, 1.0, 0.0).astype(BF16)
    ones = jnp.ones((ln, ln), BF16)
    g0 = lax.broadcasted_iota(jnp.int32, (g, g), 0)
    g1 = lax.broadcasted_iota(jnp.int32, (g, g), 1)
    earlier = jnp.where(g1 < g0, 1.0, 0.0).astype(BF16)

    def prefix(mask):
        x = jnp.where(mask, 1.0, 0.0).astype(BF16).reshape(e * g, ln)
        incl = _dot(x, upper).reshape(e, g, ln)
        tot = _dot(x, ones).reshape(e, g, ln)
        off = jnp.stack([_dot(earlier, tot[i].astype(BF16)) for i in range(e)])
        return incl, tot, off

    incl_eq, _, off_eq = prefix(eq)
    sel = gt | (eq & (off_eq + incl_eq - 1.0 < need))
    incl, tot, off = prefix(sel)
    pos_ref[...] = jnp.where(sel, off + incl - 1.0, -1.0).astype(jnp.int32)
    cnt_ref[...] = tot.astype(jnp.int32)
    off_ref[...] = off.astype(jnp.int32)
    ps_ref[...] = incl

    rank = lax.broadcasted_iota(jnp.int32, (1, ln, 1), 1).astype(F32)
    ones8 = jnp.ones((8, ln), BF16)

    def lane_of_rank(i, c):
        below = jnp.where(ps_ref[i][:, None, :] <= rank, 1.0, 0.0).astype(BF16).reshape(g * ln, ln)
        lor_ref[i] = _dot_nt(ones8, below)[0:1]
        return c

    lax.fori_loop(0, e, lane_of_rank, 0)


def moe_select(aff_t, cap):
    e, n = aff_t.shape
    g = n // LANES
    g_pad = -(-g // 16) * 16
    a = aff_t.reshape(e, g, LANES)
    if g_pad != g:
        a = jnp.concatenate([a, jnp.full((e, g_pad - g, LANES), -1.0, F32)], axis=1)
    shp = jax.ShapeDtypeStruct((e, g_pad, LANES), jnp.int32)
    full = pl.BlockSpec((e, g_pad, LANES), lambda i: (0, 0, 0))
    pos, lor, cnt, off = pl.pallas_call(
        functools.partial(_select_kernel, cap=cap),
        out_shape=(shp, jax.ShapeDtypeStruct((e, 1, g_pad * LANES), F32), shp, shp),
        grid=(1,),
        in_specs=[full],
        out_specs=(full, pl.BlockSpec((e, 1, g_pad * LANES), lambda i: (0, 0, 0)), full, full),
        scratch_shapes=[pltpu.VMEM((e, g_pad, LANES), F32)],
        compiler_params=_params(("arbitrary",), 24 * e * g_pad * LANES * 4 + 4 * g_pad * LANES * LANES * 4),
        name="moe_select",
    )(a)
    return (pos.reshape(e, g_pad * LANES)[:, :n], lor.astype(jnp.int32), cnt[:, :, 0].reshape(e, 1, g_pad),
            off[:, :g, 0])


def _row_copy(h_hbm, x_ref, sem, token, row):
    return pltpu.make_async_copy(h_hbm.at[pl.ds(token, 1), :], x_ref.at[0, pl.ds(row, 1), :], sem)


def _gather_kernel(lor_ref, cnt_ref, aff_ref, h_hbm, x_ref, wt_ref, idx_ref, sem):
    c = pl.program_id(1)
    chunk = x_ref.shape[1]

    @pl.when(c == 0)
    def _():
        def group(g, slot):
            def row(r, slot):
                token = g * LANES + lor_ref[0, g * LANES + r]
                idx_ref[slot] = token
                wt_ref[0, slot] = aff_ref[0, token]
                return slot + 1

            return lax.fori_loop(0, cnt_ref[0, g], row, slot)

        lax.fori_loop(0, cnt_ref.shape[1], group, 0)

    def issue(s, carry):
        _row_copy(h_hbm, x_ref, sem, idx_ref[c * chunk + s], s).start()
        return carry

    lax.fori_loop(0, chunk, issue, 0)
    pltpu.make_async_copy(h_hbm.at[pl.ds(0, chunk), :], x_ref.at[0], sem).wait()


def moe_gather(hp, aff_t, lor, cnt, cap):
    n, w = hp.shape
    e = lor.shape[0]
    chunk = min(cap, 1024)

    def smem(width):
        return pl.BlockSpec((None, 1, width), lambda i, c: (i, 0, 0), memory_space=pltpu.SMEM)

    x, wt = pl.pallas_call(
        _gather_kernel,
        out_shape=(jax.ShapeDtypeStruct((e, cap, w), jnp.uint32), jax.ShapeDtypeStruct((e, 1, cap), F32)),
        grid=(e, cap // chunk),
        in_specs=[smem(lor.shape[2]), smem(cnt.shape[2]), smem(n), pl.BlockSpec(memory_space=pl.ANY)],
        out_specs=(pl.BlockSpec((1, chunk, w), lambda i, c: (i, c, 0)), smem(cap)),
        scratch_shapes=[pltpu.SMEM((cap,), jnp.int32), pltpu.SemaphoreType.DMA(())],
        compiler_params=_params(("arbitrary", "arbitrary"), 2 * chunk * w * 4),
        name="moe_gather",
    )(lor, cnt, aff_t.reshape(e, 1, n), hp)
    return x, wt.reshape(e, cap, 1)


def _window_copy(y_hbm, dst, sem, e, src, win):
    return pltpu.make_async_copy(y_hbm.at[e, :, pl.ds(src, win)], dst, sem)


def _combine_kernel(offb_ref, y_hbm, pos_ref, x_ref, gate_ref, g_ref, b_ref, o_ref, ybuf, ybuf_x, sem, acc_ref,
                    *, row, cap, win):
    b = pl.program_id(0)
    n_exp, tb = pos_ref.shape
    half = n_exp // 2

    def window(e, k):
        first = (offb_ref[e, b] // 8) * 8 + k * win
        return first, pl.multiple_of(jnp.minimum(first, cap - win), 8)

    def onehot2(e, first, src):
        slots = src + lax.broadcasted_iota(jnp.int32, (win, 1), 0)
        hit = jnp.logical_and(pos_ref[e:e + 1, :] == slots, slots >= first)
        oh = jnp.where(hit, 1.0, 0.0).astype(BF16)
        return jnp.concatenate([oh, oh], axis=0)

    def copies(h):
        return [_window_copy(y_hbm, ybuf.at[h, j], sem.at[h], h * half + j, window(h * half + j, 0)[1], win)
                for j in range(half)]

    pending = [copies(0), copies(1)]
    for cps in pending:
        for cp in cps:
            cp.start()
    acc = None
    for h in range(2):
        for cp in pending[h]:
            cp.wait()
        lhs = jnp.concatenate([onehot2(h * half + j, *window(h * half + j, 0)) for j in range(half)], axis=0)
        part = _dot_tn(lhs, ybuf[h].reshape(half * 2 * win, ybuf.shape[-1]))
        acc = part if acc is None else acc + part
    acc_ref[...] = acc

    for e in range(n_exp):
        n_win = (offb_ref[e, b + 1] - (offb_ref[e, b] // 8) * 8 + win - 1) // win

        def extra(k, c, e=e):
            first_k, src_k = window(e, k)
            cp = _window_copy(y_hbm, ybuf_x, sem.at[2], e, src_k, win)
            cp.start()
            cp.wait()
            acc_ref[...] += _dot_tn(onehot2(e, first_k, src_k), ybuf_x[...].reshape(2 * win, ybuf_x.shape[-1]))
            return c

        lax.fori_loop(1, n_win, extra, 0)

    z = DEEPNORM_ALPHA * x_ref[...] + gate_ref[row:row + 1, :] * acc_ref[...]
    o_ref[...] = _layer_norm(z) * g_ref[...] + b_ref[...]


def moe_combine_postnorm(y, pos, off, x, mod, row, k_gate, g, b):
    n, d = x.shape
    e, _, cap, _ = y.shape
    tb = min(n, 256)
    nb = n // tb
    win = min(cap, COMBINE_WINDOW)
    offb = jnp.concatenate([off[:, ::tb // LANES], jnp.full((e, 1), cap, jnp.int32)], axis=1)
    grid_spec = pltpu.PrefetchScalarGridSpec(
        num_scalar_prefetch=1,
        grid=(nb,),
        in_specs=[pl.BlockSpec(memory_space=pl.ANY),
                  pl.BlockSpec((e, tb), lambda i, o: (0, i)),
                  pl.BlockSpec((tb, d), lambda i, o: (i, 0)),
                  pl.BlockSpec((8, d), lambda i, o: (0, k_gate)),
                  pl.BlockSpec((1, d), lambda i, o: (0, 0)),
                  pl.BlockSpec((1, d), lambda i, o: (0, 0))],
        out_specs=pl.BlockSpec((tb, d), lambda i, o: (i, 0)),
        scratch_shapes=[pltpu.VMEM((2, e // 2, 2, win, d), BF16), pltpu.VMEM((2, win, d), BF16),
                        pltpu.SemaphoreType.DMA((3,)), pltpu.VMEM((tb, d), F32)],
    )
    est = 8 * tb * d * 4 + (e + 1) * 2 * win * d * 2 + e * win * tb * 2
    return pl.pallas_call(
        functools.partial(_combine_kernel, row=row, cap=cap, win=win),
        out_shape=jax.ShapeDtypeStruct((n, d), F32),
        grid_spec=grid_spec,
        compiler_params=_params(("arbitrary",), est),
        name="moe_combine",
    )(offb, y, pos, x, mod, g.reshape(1, d), b.reshape(1, d))


def moe_route(x, mod, row, w_router_t):
    m = x.shape[0]
    cap = max(1, EC_CAPACITY_FACTOR * m // N_EXPERTS)
    h, aff_t = moe_router(x, mod, row, 3, 4, w_router_t)
    pos, lor, cnt, off = moe_select(aff_t, cap)
    xg, wt = moe_gather(h, aff_t, lor, cnt, cap)
    return xg, wt, (pos, off)


def _rope_angles(n_tokens, rot_dim):
    rows = n_tokens // GRID_W
    row = jnp.repeat(jnp.arange(rows, dtype=F32), GRID_W)
    col = jnp.tile(jnp.arange(GRID_W, dtype=F32), rows)
    n_freq = rot_dim // 4
    inv = ROPE_THETA ** (-jnp.arange(n_freq, dtype=F32) / n_freq)
    return jnp.concatenate([row[:, None] * inv, col[:, None] * inv], axis=-1)


def _gqa_rope_tables(n_tokens):
    ang = _rope_angles(n_tokens, HEAD_DIM)
    c, s = jnp.cos(ang), jnp.sin(ang)
    return jnp.concatenate([c, c], axis=-1), jnp.concatenate([-s, s], axis=-1)


def _mla_rope_tables(n_tokens):
    ang = _rope_angles(n_tokens, MLA_ROPE)
    c, s = jnp.cos(ang), jnp.sin(ang)
    z = jnp.zeros_like(c)
    cos = jnp.concatenate([c, c, z, z], axis=-1)
    sa = jnp.concatenate([-s, z, z, z], axis=-1)
    sb = jnp.concatenate([z, s, z, z], axis=-1)
    return cos, sa, sb


def kernel(x, c, ctx, c_ctx, ada_w, ada_b, ln_g, ln_b, ev_w_in, ev_w_out, hgrn_lb, hgrn_norm_g, gqa_q_norm_g, gqa_k_norm_g, mla_w_down, mla_q_norm_g, mla_kv_norm_g, mla_w_uq, mla_w_ukv, mla_w_o, moe_router, moe_w_gate, moe_w_up, moe_w_down):
    d = D_MODEL
    xl = x[0]
    xc = ctx[0]
    n_lat, n_ctx = xl.shape[0], xc.shape[0]
    cc = jnp.zeros((8, d), F32).at[0].set(c[0]).at[1].set(c_ctx)
    lb_all = jnp.cumsum(jax.nn.softmax(hgrn_lb.astype(F32), axis=1), axis=1)
    gqa_tabs = _gqa_rope_tables(n_lat)
    gqa_tabs_ctx = [jnp.ones((n_ctx, HEAD_DIM), F32), jnp.zeros((n_ctx, HEAD_DIM), F32)]
    mla_tabs = _mla_rope_tables(n_lat)
    mla_tabs_ctx = [jnp.ones((n_ctx, HEAD_DIM), F32), jnp.zeros((n_ctx, HEAD_DIM), F32), jnp.zeros((n_ctx, HEAD_DIM), F32)]
    LAT, CTX = 0, 1

    for l in range(DEPTH):
        last = l == DEPTH - 1
        i = l // 2
        mod = adaln(cc, ada_w[l], ada_b[l])
        if l % 2 == 0:
            w_in = ev_w_in[i].astype(BF16)
            w_out = ev_w_out[i].astype(BF16)
            lb = lb_all[:, l].reshape(2, 1, A_WIDTH)
            scale = HEAD_DIM ** -0.5 * LOG2E
            proj_c = lnmod_matmul(xc, mod, CTX, 0, 1, w_in, 512)
            proj_l = lnmod_matmul(xl, mod, LAT, 0, 1, w_in, 512)
            s0 = jnp.zeros((2, A_HEADS, HEAD_DIM, HEAD_DIM), F32)
            o_c, s_c = hgrn_scan(proj_c, lb, s0)
            o_l, _ = hgrn_scan(proj_l, lb, s_c)
            a_l = hgrn_out(o_l, proj_l, hgrn_norm_g[i])
            qcol, kcol, vcol = 5 * A_HEADS, 5 * A_HEADS + B_Q_HEADS, 5 * A_WIDTH + B_WIDTH + B_KV_WIDTH
            q_l = norm_rope(proj_l, qcol, B_Q_HEADS, gqa_q_norm_g[i], *gqa_tabs, scale)
            k_l = norm_rope(proj_l, kcol, B_KV_HEADS, gqa_k_norm_g[i], *gqa_tabs, 1.0)
            k_c = norm_rope(proj_c, kcol, B_KV_HEADS, gqa_k_norm_g[i], *gqa_tabs_ctx, 1.0)
            v_l = proj_l[:, vcol:].astype(BF16)
            v_c = proj_c[:, vcol:].astype(BF16)
            att = dict(n_heads=B_Q_HEADS, n_kv_heads=B_KV_HEADS, dq=HEAD_DIM, dv=HEAD_DIM)
            b_l = flash_attention(q_l, k_l, v_l, k_c, v_c, **att)
            w_parts = [w_out[:A_WIDTH], w_out[A_WIDTH:]]
            xl_new = proj_postnorm([a_l, b_l], w_parts, xl, mod, LAT, 2, ln_g[l, 0], ln_b[l, 0])
            if not last:
                a_c = hgrn_out(o_c, proj_c, hgrn_norm_g[i])
                q_c = norm_rope(proj_c, qcol, B_Q_HEADS, gqa_q_norm_g[i], *gqa_tabs_ctx, scale)
                b_c = flash_attention(q_c, k_c, v_c, **att)
                xc = proj_postnorm([a_c, b_c], w_parts, xc, mod, CTX, 2, ln_g[l, 0], ln_b[l, 0])
            xl = xl_new
        else:
            hd = HEAD_DIM
            pad = (-mla_w_down.shape[2]) % hd
            w_down = jnp.pad(mla_w_down[i], ((0, 0), (0, pad))).astype(BF16)
            w_uq = mla_w_uq[i].reshape(MLA_Q_LORA, MLA_HEADS, hd + MLA_ROPE)
            w_uq = jnp.pad(w_uq, ((0, 0), (0, 0), (0, MLA_QK - hd - MLA_ROPE))).reshape(MLA_Q_LORA, MLA_HEADS * MLA_QK).astype(BF16)
            w_ukv = mla_w_ukv[i].reshape(MLA_KV_LORA, MLA_HEADS, 2 * hd)
            w_uk = w_ukv[:, :, :hd].reshape(MLA_KV_LORA, MLA_HEADS * hd).astype(BF16)
            w_uv = w_ukv[:, :, hd:].reshape(MLA_KV_LORA, MLA_HEADS * hd).astype(BF16)
            w_o = mla_w_o[i].astype(BF16)
            scale = (hd + MLA_ROPE) ** -0.5 * LOG2E
            dn_c = lnmod_matmul(xc, mod, CTX, 0, 1, w_down, w_down.shape[1])
            dn_l = lnmod_matmul(xl, mod, LAT, 0, 1, w_down, w_down.shape[1])
            q_l = mla_q(dn_l, mla_q_norm_g[i], w_uq, *mla_tabs, scale)
            k_l = mla_k(dn_l, mla_kv_norm_g[i], w_uk, *mla_tabs)
            k_c = mla_k(dn_c, mla_kv_norm_g[i], w_uk, *mla_tabs_ctx)
            v_l = rms_matmul(dn_l, 1, mla_kv_norm_g[i], w_uv, 1024)
            v_c = rms_matmul(dn_c, 1, mla_kv_norm_g[i], w_uv, 1024)
            att = dict(n_heads=MLA_HEADS, n_kv_heads=MLA_HEADS, dq=MLA_QK, dv=hd)
            o_l = flash_attention(q_l, k_l, v_l, k_c, v_c, **att)
            xl_new = proj_postnorm([o_l], [w_o], xl, mod, LAT, 2, ln_g[l, 0], ln_b[l, 0])
            if not last:
                q_c = mla_q(dn_c, mla_q_norm_g[i], w_uq, *mla_tabs_ctx, scale)
                o_c = flash_attention(q_c, k_c, v_c, **att)
                xc = proj_postnorm([o_c], [w_o], xc, mod, CTX, 2, ln_g[l, 0], ln_b[l, 0])
            xl = xl_new

        w_router_t = moe_router[l].T
        segs = [(xl, LAT)] if last else [(xl, LAT), (xc, CTX)]
        routes = [moe_route(xs, mod, row, w_router_t) for xs, row in segs]
        ys = expert_ffn([r[0] for r in routes], [r[1] for r in routes], moe_w_gate, moe_w_up, moe_w_down, l)
        outs = [moe_combine_postnorm(y, *r[2], xs, mod, row, 5, ln_g[l, 1], ln_b[l, 1])
                for y, r, (xs, row) in zip(ys, routes, segs)]
        xl = outs[0]
        if not last:
            xc = outs[1]
    return xl[None]
```

```python
import functools
import math

import numpy as np
import jax
import jax.numpy as jnp
from jax import lax
from jax.experimental import pallas as pl
from jax.experimental.pallas import tpu as pltpu

F32 = jnp.float32
BF16 = jnp.bfloat16

D_MODEL = 2048
DEPTH = 2
GRID_W = 64
HEAD_DIM = 128
A_HEADS = D_MODEL // 256
A_WIDTH = A_HEADS * HEAD_DIM
B_Q_HEADS = D_MODEL // 256
B_KV_HEADS = 2
B_WIDTH = B_Q_HEADS * HEAD_DIM
B_KV_WIDTH = B_KV_HEADS * HEAD_DIM
MLA_HEADS = D_MODEL // 128
MLA_Q_LORA = 512
MLA_KV_LORA = 512
MLA_ROPE = 64
MLA_QK = 2 * HEAD_DIM
N_EXPERTS = 16
EXPERT_FF = D_MODEL // 2
EC_CAPACITY_FACTOR = 2
ROPE_THETA = 10000.0
NORM_EPS = 1e-6
DEEPNORM_ALPHA = (2.0 * DEPTH) ** 0.25

HGRN_CHUNK = 128
COMBINE_WINDOW = 64
LANES = 128
LOG2E = math.log2(math.e)
V7X_VMEM_BYTES = 64 * 1024 * 1024
VMEM_CAP_BYTES = V7X_VMEM_BYTES - 8 * 1024 * 1024


def _params(semantics, vmem_estimate_bytes):
    limit = int(min(max(2 * vmem_estimate_bytes, 32 * 1024 * 1024), VMEM_CAP_BYTES))
    return pltpu.CompilerParams(dimension_semantics=semantics, vmem_limit_bytes=limit)


def _layer_norm(x):
    mu = jnp.mean(x, axis=-1, keepdims=True)
    xc = x - mu
    var = jnp.mean(xc * xc, axis=-1, keepdims=True)
    return xc * lax.rsqrt(var + NORM_EPS)


def _rms(x):
    return x * lax.rsqrt(jnp.mean(x * x, axis=-1, keepdims=True) + NORM_EPS)


def _dot(a, b):
    return jnp.dot(a, b, preferred_element_type=F32)


def _dot_nt(a, b):
    return lax.dot_general(a, b, (((1,), (1,)), ((), ())), preferred_element_type=F32)


def _dot_tn(a, b):
    return lax.dot_general(a, b, (((0,), (0,)), ((), ())), preferred_element_type=F32)


def _split3(x):
    x1 = x.astype(BF16)
    r1 = x - x1.astype(F32)
    x2 = r1.astype(BF16)
    x3 = (r1 - x2.astype(F32)).astype(BF16)
    return x1, x2, x3


def _adaln_kernel(c_ref, w_ref, b_ref, o_ref):
    c = c_ref[...]
    s = c * jax.nn.sigmoid(c)
    w = w_ref[...]
    s1, s2, s3 = _split3(s)
    w1, w2, w3 = _split3(w)
    acc = _dot(s1, w3) + _dot(s3, w1) + _dot(s2, w2)
    acc = acc + _dot(s1, w2) + _dot(s2, w1)
    acc = acc + _dot(s1, w1)
    o_ref[...] = acc + b_ref[...]


def adaln(cc, w, b):
    d, n = w.shape
    tn = 1536 if n % 1536 == 0 else n
    est = 2 * d * tn * 4 * 2
    return pl.pallas_call(
        _adaln_kernel,
        out_shape=jax.ShapeDtypeStruct((8, n), F32),
        grid=(n // tn,),
        in_specs=[pl.BlockSpec((8, d), lambda j: (0, 0)),
                  pl.BlockSpec((d, tn), lambda j: (0, j)),
                  pl.BlockSpec((1, tn), lambda j: (0, j))],
        out_specs=pl.BlockSpec((8, tn), lambda j: (0, j)),
        compiler_params=_params(("parallel",), est),
        name="adaln",
    )(cc, w, b.reshape(1, n))


def _lnmod_mm_kernel(x_ref, sh_ref, sc_ref, w_ref, o_ref, h_ref, *, row):
    @pl.when(pl.program_id(1) == 0)
    def _():
        hn = _layer_norm(x_ref[...])
        h = hn * (1.0 + sc_ref[row:row + 1, :]) + sh_ref[row:row + 1, :]
        h_ref[...] = h.astype(BF16)

    o_ref[...] = _dot(h_ref[...], w_ref[...]).astype(o_ref.dtype)


def lnmod_matmul(x, mod, row, k_shift, k_scale, w, tn):
    m, d = x.shape
    n = w.shape[1]
    tm = min(m, 1024)
    est = 2 * tm * d * 4 + tm * d * 2 + 2 * d * tn * 2 + 2 * tm * tn * 4
    return pl.pallas_call(
        functools.partial(_lnmod_mm_kernel, row=row),
        out_shape=jax.ShapeDtypeStruct((m, n), F32),
        grid=(m // tm, n // tn),
        in_specs=[pl.BlockSpec((tm, d), lambda i, j: (i, 0)),
                  pl.BlockSpec((8, d), lambda i, j: (0, k_shift)),
                  pl.BlockSpec((8, d), lambda i, j: (0, k_scale)),
                  pl.BlockSpec((d, tn), lambda i, j: (0, j))],
        out_specs=pl.BlockSpec((tm, tn), lambda i, j: (i, j)),
        scratch_shapes=[pltpu.VMEM((tm, d), BF16)],
        compiler_params=_params(("parallel", "arbitrary"), est),
        name="lnmod_matmul",
    )(x, mod, mod, w)


def _hgrn_tables(c):
    n_lvl = int(math.log2(c))
    r = np.arange(c)
    u = np.arange(c)[None, :]
    blocks, masks = [], []
    for l in range(n_lvl):
        half = 1 << l
        base = (r // (2 * half)) * (2 * half)
        anchor = (base + half - 1)[:, None]
        upper = (r >= base + half)[:, None]
        rr = r[:, None]
        blocks.append(np.where(upper, (u > anchor) & (u <= rr), (u > rr) & (u <= anchor)))
        same = (r[:, None] // (2 * half)) == (r[None, :] // (2 * half))
        masks.append(same & upper & ~(upper.T))
    blocks.append(u <= r[:, None])
    blocks.append(u > r[:, None])
    blocks.append(np.ones((16, c), bool))
    masks.append(np.eye(c, dtype=bool))
    fwd_s = np.concatenate(blocks, axis=0).astype(np.float32)
    fwd_m = np.stack(masks).astype(np.float32)
    bwd_s = np.concatenate([b[::-1, ::-1] for b in blocks], axis=0).astype(np.float32)
    bwd_m = fwd_m[:, ::-1, ::-1]
    return (jnp.asarray(np.stack([fwd_s, bwd_s]), BF16), jnp.asarray(np.stack([fwd_m, bwd_m]), F32))


def _hgrn_kernel(q_ref, v_ref, f_ref, lb_ref, sums_ref, mask_ref, s0_ref, o_ref, sfin_ref, st_ref):
    c = q_ref.shape[0]
    hd = HEAD_DIM
    n_lvl = mask_ref.shape[0] - 1
    j = pl.program_id(1)

    @pl.when(j == 0)
    def _():
        st_ref[...] = s0_ref[...]

    for h in range(q_ref.shape[1] // hd):
        cols = slice(h * hd, (h + 1) * hd)
        q = q_ref[:, cols]
        vb = v_ref[:, cols].astype(BF16)
        lb = lb_ref[:, cols]
        f = lb + (1.0 - lb) * jax.nn.sigmoid(f_ref[:, cols])
        g = jnp.log(f)
        k = 1.0 - f
        g1 = g.astype(BF16)
        g2 = (g - g1.astype(F32)).astype(BF16)
        e2 = _dot(sums_ref[...], jnp.concatenate([g1, g2], axis=1))
        e = e2[:, hd:] + e2[:, :hd]

        scores = _dot_nt(q.astype(BF16), k.astype(BF16)) * mask_ref[n_lvl]
        for l in range(n_lvl):
            z = jnp.exp(e[l * c:(l + 1) * c])
            scores = scores + _dot_nt((q * z).astype(BF16), (k * z).astype(BF16)) * mask_ref[l]

        cum = e[n_lvl * c:(n_lvl + 1) * c]
        rem = e[(n_lvl + 1) * c:(n_lvl + 2) * c]
        tot = e[(n_lvl + 2) * c:(n_lvl + 2) * c + 1]
        st = st_ref[h]
        o = _dot(scores.astype(BF16), vb) + _dot_nt((q * jnp.exp(cum)).astype(BF16), st.astype(BF16))
        o_ref[:, cols] = o
        st_new = st * jnp.exp(tot) + _dot_tn(vb, (k * jnp.exp(rem)).astype(BF16))
        st_ref[h] = st_new

    @pl.when(j == pl.num_programs(1) - 1)
    def _():
        sfin_ref[...] = st_ref[...]


def hgrn_scan(proj, lb, s0):
    seq = proj.shape[0]
    c = HGRN_CHUNK
    nc = seq // c
    sums, masks = _hgrn_tables(c)
    hd, w = HEAD_DIM, A_WIDTH

    def blk(d, j):
        return jnp.where(d == 0, j, nc - 1 - j)

    est = (2 * (4 * c * w * 4 + sums.shape[1] * c * 2 + masks.shape[1] * c * c * 4 + 2 * A_HEADS * hd * hd * 4)
           + A_HEADS * hd * hd * 4)
    return pl.pallas_call(
        _hgrn_kernel,
        out_shape=(jax.ShapeDtypeStruct((2, seq, w), F32),
                   jax.ShapeDtypeStruct((2, A_HEADS, hd, hd), F32)),
        grid=(2, nc),
        in_specs=[pl.BlockSpec((c, w), lambda d, j: (blk(d, j), 0)),
                  pl.BlockSpec((c, w), lambda d, j: (blk(d, j), 3)),
                  pl.BlockSpec((c, w), lambda d, j: (blk(d, j), 1 + d)),
                  pl.BlockSpec((None, 1, w), lambda d, j: (d, 0, 0)),
                  pl.BlockSpec((None, sums.shape[1], c), lambda d, j: (d, 0, 0)),
                  pl.BlockSpec((None, masks.shape[1], c, c), lambda d, j: (d, 0, 0, 0)),
                  pl.BlockSpec((None, A_HEADS, hd, hd), lambda d, j: (d, 0, 0, 0))],
        out_specs=(pl.BlockSpec((None, c, w), lambda d, j: (d, blk(d, j), 0)),
                   pl.BlockSpec((None, A_HEADS, hd, hd), lambda d, j: (d, 0, 0, 0))),
        scratch_shapes=[pltpu.VMEM((A_HEADS, hd, hd), F32)],
        compiler_params=_params(("parallel", "arbitrary"), est),
        name="hgrn_scan",
    )(proj, proj, proj, lb, sums, masks, s0)


def _hgrn_out_kernel(o_ref, gate_ref, g_ref, a_ref):
    o = o_ref[0] + o_ref[1]
    gate = gate_ref[...]
    a_ref[...] = (_rms(o) * g_ref[...] * (gate * jax.nn.sigmoid(gate))).astype(a_ref.dtype)


def hgrn_out(o, proj, norm_g):
    seq = o.shape[1]
    tm = min(seq, 512)
    hd = HEAD_DIM
    return pl.pallas_call(
        _hgrn_out_kernel,
        out_shape=jax.ShapeDtypeStruct((seq, A_WIDTH), BF16),
        grid=(seq // tm, A_HEADS),
        in_specs=[pl.BlockSpec((2, tm, hd), lambda i, h: (0, i, h)),
                  pl.BlockSpec((tm, hd), lambda i, h: (i, 4 * A_HEADS + h)),
                  pl.BlockSpec((1, hd), lambda i, h: (0, 0))],
        out_specs=pl.BlockSpec((tm, hd), lambda i, h: (i, h)),
        compiler_params=_params(("parallel", "parallel"), 8 * tm * hd * 4),
        name="hgrn_out",
    )(o, proj, norm_g.reshape(1, hd))


def _norm_rope_kernel(x_ref, g_ref, cos_ref, sin_ref, o_ref, *, scale):
    y = _rms(x_ref[...]) * g_ref[...]
    y = y * cos_ref[...] + pltpu.roll(y, HEAD_DIM // 2, 1) * sin_ref[...]
    o_ref[...] = (y * scale).astype(o_ref.dtype)


def norm_rope(proj, col0, n_heads, g, cos, sin, scale):
    seq = proj.shape[0]
    tm = min(seq, 512)
    hd = HEAD_DIM
    return pl.pallas_call(
        functools.partial(_norm_rope_kernel, scale=scale),
        out_shape=jax.ShapeDtypeStruct((seq, n_heads * hd), BF16),
        grid=(seq // tm, n_heads),
        in_specs=[pl.BlockSpec((tm, hd), lambda i, h: (i, col0 + h)),
                  pl.BlockSpec((1, hd), lambda i, h: (0, 0)),
                  pl.BlockSpec((tm, hd), lambda i, h: (i, 0)),
                  pl.BlockSpec((tm, hd), lambda i, h: (i, 0))],
        out_specs=pl.BlockSpec((tm, hd), lambda i, h: (i, h)),
        compiler_params=_params(("parallel", "parallel"), 10 * tm * hd * 4),
        name="norm_rope",
    )(proj, g.reshape(1, hd), cos, sin)


def _flash_update(q, k, v, m_ref, l_ref, acc_ref):
    s = _dot_nt(q, k)
    m_prev = m_ref[...]
    m_new = jnp.maximum(m_prev, jnp.max(s, axis=-1, keepdims=True))
    alpha = jnp.exp2(m_prev - m_new)
    ps = [jnp.exp2(s[:, c * LANES:(c + 1) * LANES] - m_new) for c in range(s.shape[1] // LANES)]
    psum = ps[0]
    for pc in ps[1:]:
        psum = psum + pc
    p = jnp.concatenate([pc.astype(BF16) for pc in ps], axis=1)
    l_ref[...] = alpha * l_ref[...] + psum
    acc_ref[...] = alpha * acc_ref[...] + _dot(p, v)
    m_ref[...] = m_new


def _flash_kernel(*refs, has_ctx):
    if has_ctx:
        q_ref, k_ref, v_ref, kc_ref, vc_ref, o_ref, m_ref, l_ref, acc_ref = refs
    else:
        q_ref, k_ref, v_ref, o_ref, m_ref, l_ref, acc_ref = refs
    j = pl.program_id(2)

    @pl.when(j == 0)
    def _():
        m_ref[...] = jnp.full(m_ref.shape, -jnp.inf, F32)
        l_ref[...] = jnp.zeros(l_ref.shape, F32)
        acc_ref[...] = jnp.zeros(acc_ref.shape, F32)
        if has_ctx:
            _flash_update(q_ref[...], kc_ref[...], vc_ref[...], m_ref, l_ref, acc_ref)

    _flash_update(q_ref[...], k_ref[...], v_ref[...], m_ref, l_ref, acc_ref)

    @pl.when(j == pl.num_programs(2) - 1)
    def _():
        l = jnp.sum(l_ref[...], axis=-1, keepdims=True)
        o_ref[...] = (acc_ref[...] / l).astype(o_ref.dtype)


def flash_attention(q, k, v, k_ctx=None, v_ctx=None, *, n_heads, n_kv_heads, dq, dv):
    n, m = q.shape[0], k.shape[0]
    grp = n_heads // n_kv_heads
    tq = min(n, 2048)
    tk = min(m, 2048)
    has_ctx = k_ctx is not None
    in_specs = [pl.BlockSpec((tq, dq), lambda h, i, j: (i, h)),
                pl.BlockSpec((tk, dq), lambda h, i, j: (j, h // grp)),
                pl.BlockSpec((tk, dv), lambda h, i, j: (j, h // grp))]
    args = [q, k, v]
    if has_ctx:
        mc = k_ctx.shape[0]
        in_specs += [pl.BlockSpec((mc, dq), lambda h, i, j: (0, h // grp)),
                     pl.BlockSpec((mc, dv), lambda h, i, j: (0, h // grp))]
        args += [k_ctx, v_ctx]
    est = 2 * (tq * dq + tk * dq + tk * dv + tq * dv) * 2 + tq * (dv + 256) * 4 + 6 * tq * tk * 4
    return pl.pallas_call(
        functools.partial(_flash_kernel, has_ctx=has_ctx),
        out_shape=jax.ShapeDtypeStruct((n, n_heads * dv), BF16),
        grid=(n_heads, n // tq, m // tk),
        in_specs=in_specs,
        out_specs=pl.BlockSpec((tq, dv), lambda h, i, j: (i, h)),
        scratch_shapes=[pltpu.VMEM((tq, LANES), F32), pltpu.VMEM((tq, LANES), F32), pltpu.VMEM((tq, dv), F32)],
        compiler_params=_params(("parallel", "parallel", "arbitrary"), est),
        name="flash_attention",
    )(*args)


def _proj_postnorm_kernel(*refs, n_in, row):
    a_refs = refs[:n_in]
    w_refs = refs[n_in:2 * n_in]
    x_ref, gate_ref, g_ref, b_ref, o_ref = refs[2 * n_in:]
    y = _dot(a_refs[0][...], w_refs[0][...])
    for a_ref, w_ref in zip(a_refs[1:], w_refs[1:]):
        y = y + _dot(a_ref[...], w_ref[...])
    z = DEEPNORM_ALPHA * x_ref[...] + gate_ref[row:row + 1, :] * y
    o_ref[...] = _layer_norm(z) * g_ref[...] + b_ref[...]


def proj_postnorm(acts, ws, x, mod, row, k_gate, g, b):
    m, d = x.shape
    tm = min(m, 512)
    n_in = len(acts)
    once = pl.Buffered(1)
    in_specs = [pl.BlockSpec((tm, a.shape[1]), lambda i: (i, 0)) for a in acts]
    in_specs += [pl.BlockSpec(w.shape, lambda i: (0, 0), pipeline_mode=once) for w in ws]
    in_specs += [pl.BlockSpec((tm, d), lambda i: (i, 0)),
                 pl.BlockSpec((8, d), lambda i: (0, k_gate)),
                 pl.BlockSpec((1, d), lambda i: (0, 0)),
                 pl.BlockSpec((1, d), lambda i: (0, 0))]
    est = sum(w.size * 2 for w in ws) + sum(2 * tm * a.shape[1] * 2 for a in acts) + 6 * tm * d * 4
    return pl.pallas_call(
        functools.partial(_proj_postnorm_kernel, n_in=n_in, row=row),
        out_shape=jax.ShapeDtypeStruct((m, d), F32),
        grid=(m // tm,),
        in_specs=in_specs,
        out_specs=pl.BlockSpec((tm, d), lambda i: (i, 0)),
        compiler_params=_params(("parallel",), est),
        name="proj_postnorm",
    )(*acts, *ws, x, mod, g.reshape(1, d), b.reshape(1, d))


def _add_postnorm_kernel(y_ref, x_ref, gate_ref, g_ref, b_ref, o_ref, *, row):
    z = DEEPNORM_ALPHA * x_ref[...] + gate_ref[row:row + 1, :] * y_ref[...]
    o_ref[...] = _layer_norm(z) * g_ref[...] + b_ref[...]


def add_postnorm(y, x, mod, row, k_gate, g, b):
    m, d = x.shape
    tm = min(m, 512)
    return pl.pallas_call(
        functools.partial(_add_postnorm_kernel, row=row),
        out_shape=jax.ShapeDtypeStruct((m, d), F32),
        grid=(m // tm,),
        in_specs=[pl.BlockSpec((tm, d), lambda i: (i, 0)),
                  pl.BlockSpec((tm, d), lambda i: (i, 0)),
                  pl.BlockSpec((8, d), lambda i: (0, k_gate)),
                  pl.BlockSpec((1, d), lambda i: (0, 0)),
                  pl.BlockSpec((1, d), lambda i: (0, 0))],
        out_specs=pl.BlockSpec((tm, d), lambda i: (i, 0)),
        compiler_params=_params(("parallel",), 8 * tm * d * 4),
        name="add_postnorm",
    )(y, x, mod, g.reshape(1, d), b.reshape(1, d))


def _rms_mm_kernel(x_ref, g_ref, w_ref, o_ref, a_ref):
    @pl.when(pl.program_id(1) == 0)
    def _():
        a_ref[...] = (_rms(x_ref[...]) * g_ref[...]).astype(BF16)

    o_ref[...] = _dot(a_ref[...], w_ref[...]).astype(o_ref.dtype)


def _mla_q_kernel(x_ref, g_ref, w_ref, cos_ref, sa_ref, sb_ref, o_ref, a_ref, *, scale):
    @pl.when(pl.program_id(1) == 0)
    def _():
        a_ref[...] = (_rms(x_ref[...]) * g_ref[...]).astype(BF16)

    y = _dot(a_ref[...], w_ref[...])
    hd = HEAD_DIM
    for h in range(y.shape[1] // MLA_QK):
        c0 = h * MLA_QK
        o_ref[:, c0:c0 + hd] = (y[:, c0:c0 + hd] * scale).astype(o_ref.dtype)
        r = y[:, c0 + hd:c0 + 2 * hd]
        r = r * cos_ref[...] + pltpu.roll(r, hd - MLA_ROPE // 2, 1) * sa_ref[...] + pltpu.roll(r, MLA_ROPE // 2, 1) * sb_ref[...]
        o_ref[:, c0 + hd:c0 + 2 * hd] = (r * scale).astype(o_ref.dtype)


def _mla_k_kernel(x_ref, g_ref, w_ref, kr_ref, cos_ref, sa_ref, sb_ref, o_ref, a_ref, r_ref):
    hd = HEAD_DIM

    @pl.when(pl.program_id(1) == 0)
    def _():
        a_ref[...] = (_rms(x_ref[...]) * g_ref[...]).astype(BF16)
        r = kr_ref[...]
        r = r * cos_ref[...] + pltpu.roll(r, hd - MLA_ROPE // 2, 1) * sa_ref[...] + pltpu.roll(r, MLA_ROPE // 2, 1) * sb_ref[...]
        r_ref[...] = r.astype(BF16)

    y = _dot(a_ref[...], w_ref[...])
    for h in range(y.shape[1] // hd):
        o_ref[:, h * MLA_QK:h * MLA_QK + hd] = y[:, h * hd:(h + 1) * hd].astype(o_ref.dtype)
        o_ref[:, h * MLA_QK + hd:(h + 1) * MLA_QK] = r_ref[...]


def _mla_specs(m, tm, lora, col_blk):
    return [pl.BlockSpec((tm, lora), lambda i, j: (i, col_blk)),
            pl.BlockSpec((1, lora), lambda i, j: (0, 0))]


def rms_matmul(dn, col_blk, g, w, tn):
    m = dn.shape[0]
    lora, n = w.shape
    tm = min(m, 1024)
    est = 2 * tm * lora * 4 + tm * lora * 2 + 2 * lora * tn * 2 + 2 * tm * tn * 2 + tm * tn * 4
    return pl.pallas_call(
        _rms_mm_kernel,
        out_shape=jax.ShapeDtypeStruct((m, n), BF16),
        grid=(m // tm, n // tn),
        in_specs=_mla_specs(m, tm, lora, col_blk) + [pl.BlockSpec((lora, tn), lambda i, j: (0, j))],
        out_specs=pl.BlockSpec((tm, tn), lambda i, j: (i, j)),
        scratch_shapes=[pltpu.VMEM((tm, lora), BF16)],
        compiler_params=_params(("parallel", "arbitrary"), est),
        name="rms_matmul",
    )(dn, g.reshape(1, lora), w)


def mla_q(dn, g, w, cos, sa, sb, scale):
    m = dn.shape[0]
    lora, n = w.shape
    tm = min(m, 1024)
    tn = 4 * MLA_QK
    hd = HEAD_DIM
    est = 2 * tm * lora * 4 + tm * lora * 2 + 2 * lora * tn * 2 + 2 * tm * tn * 2 + 2 * tm * tn * 4 + 6 * tm * hd * 4
    rope_spec = pl.BlockSpec((tm, hd), lambda i, j: (i, 0))
    return pl.pallas_call(
        functools.partial(_mla_q_kernel, scale=scale),
        out_shape=jax.ShapeDtypeStruct((m, n), BF16),
        grid=(m // tm, n // tn),
        in_specs=_mla_specs(m, tm, lora, 0) + [pl.BlockSpec((lora, tn), lambda i, j: (0, j)),
                                               rope_spec, rope_spec, rope_spec],
        out_specs=pl.BlockSpec((tm, tn), lambda i, j: (i, j)),
        scratch_shapes=[pltpu.VMEM((tm, lora), BF16)],
        compiler_params=_params(("parallel", "arbitrary"), est),
        name="mla_q",
    )(dn, g.reshape(1, lora), w, cos, sa, sb)


def mla_k(dn, g, w, cos, sa, sb):
    m = dn.shape[0]
    lora, n = w.shape
    tm = min(m, 1024)
    hd = HEAD_DIM
    tn = 4 * hd
    kr_blk = (MLA_Q_LORA + MLA_KV_LORA) // hd
    est = 2 * tm * lora * 4 + tm * lora * 2 + 2 * lora * tn * 2 + 4 * tm * tn * 2 + tm * tn * 4 + 8 * tm * hd * 4
    rope_spec = pl.BlockSpec((tm, hd), lambda i, j: (i, 0))
    return pl.pallas_call(
        _mla_k_kernel,
        out_shape=jax.ShapeDtypeStruct((m, 2 * n), BF16),
        grid=(m // tm, n // tn),
        in_specs=_mla_specs(m, tm, lora, 1) + [pl.BlockSpec((lora, tn), lambda i, j: (0, j)),
                                               pl.BlockSpec((tm, hd), lambda i, j: (i, kr_blk)),
                                               rope_spec, rope_spec, rope_spec],
        out_specs=pl.BlockSpec((tm, 2 * tn), lambda i, j: (i, j)),
        scratch_shapes=[pltpu.VMEM((tm, lora), BF16), pltpu.VMEM((tm, hd), BF16)],
        compiler_params=_params(("parallel", "arbitrary"), est),
        name="mla_k",
    )(dn, g.reshape(1, lora), w, dn, cos, sa, sb)


def _router_kernel(x_ref, sh_ref, sc_ref, wr_ref, h_ref, aff_ref, *, row):
    hn = _layer_norm(x_ref[...])
    h = hn * (1.0 + sc_ref[row:row + 1, :]) + sh_ref[row:row + 1, :]
    hb = h.astype(BF16)
    half = h.shape[1] // 2
    bits = pltpu.bitcast(hb.astype(F32), jnp.uint32)
    h_ref[...] = (bits[:, half:] & jnp.uint32(0xFFFF0000)) | (bits[:, :half] >> 16)
    w = wr_ref[...]
    w1 = w.astype(BF16)
    w2 = (w - w1.astype(F32)).astype(BF16)
    h2 = (h - hb.astype(F32)).astype(BF16)
    logits = _dot_nt(w1, hb) + (_dot_nt(w2, hb) + _dot_nt(w1, h2))
    mx = jnp.max(logits, axis=0, keepdims=True)
    p = jnp.exp(logits - mx)
    aff_ref[...] = p / jnp.sum(p, axis=0, keepdims=True)


def moe_router(x, mod, row, k_shift, k_scale, w_router_t):
    m, d = x.shape
    e = w_router_t.shape[0]
    tm = min(m, 512)
    return pl.pallas_call(
        functools.partial(_router_kernel, row=row),
        out_shape=(jax.ShapeDtypeStruct((m, d // 2), jnp.uint32), jax.ShapeDtypeStruct((e, m), F32)),
        grid=(m // tm,),
        in_specs=[pl.BlockSpec((tm, d), lambda i: (i, 0)),
                  pl.BlockSpec((8, d), lambda i: (0, k_shift)),
                  pl.BlockSpec((8, d), lambda i: (0, k_scale)),
                  pl.BlockSpec((e, d), lambda i: (0, 0))],
        out_specs=(pl.BlockSpec((tm, d // 2), lambda i: (i, 0)), pl.BlockSpec((e, tm), lambda i: (0, i))),
        compiler_params=_params(("parallel",), 8 * tm * d * 4),
        name="moe_router",
    )(x, mod, mod, w_router_t)


def _ffn_up_kernel(*refs, n_seg):
    x_refs, (wg_ref, wu_ref) = refs[:n_seg], refs[n_seg:n_seg + 2]
    o_refs, xs_refs = refs[n_seg + 2:2 * n_seg + 2], refs[2 * n_seg + 2:]

    @pl.when(pl.program_id(1) == 0)
    def _():
        for x_ref, xs_ref in zip(x_refs, xs_refs):
            word = x_ref[...]
            first = pltpu.bitcast(word << 16, F32)
            second = pltpu.bitcast(word & jnp.uint32(0xFFFF0000), F32)
            xs_ref[...] = jnp.concatenate([first, second], axis=1).astype(BF16)

    wg = wg_ref[...].astype(BF16)
    wu = wu_ref[...].astype(BF16)
    for xs_ref, o_ref in zip(xs_refs, o_refs):
        x = xs_ref[...]
        g = _dot(x, wg)
        u = _dot(x, wu)
        o_ref[...] = (g * jax.nn.sigmoid(g) * u).astype(o_ref.dtype)


def _ffn_down_kernel(*refs, n_seg):
    h_refs, wd_ref, wt_refs, o_refs = refs[:n_seg], refs[n_seg], refs[n_seg + 1:2 * n_seg + 1], refs[2 * n_seg + 1:]
    wd = wd_ref[...].astype(BF16)
    for h_ref, wt_ref, o_ref in zip(h_refs, wt_refs, o_refs):
        y = _dot(h_ref[...], wd) * wt_ref[...]
        hi = y.astype(BF16)
        o_ref[0] = hi
        o_ref[1] = (y - hi.astype(F32)).astype(BF16)


def expert_ffn(xgs, wts, w_gate, w_up, w_down, layer):
    n_seg = len(xgs)
    e = xgs[0].shape[0]
    d = 2 * xgs[0].shape[2]
    f = w_gate.shape[3]
    rs = [x.shape[1] for x in xgs]
    r = sum(rs)
    tf = min(f, 256)
    est = 2 * (r * d * 2 + 2 * d * tf * 4 + r * tf * 2) + r * d * 2 + 2 * d * tf * 2 + 3 * r * tf * 4
    hids = pl.pallas_call(
        functools.partial(_ffn_up_kernel, n_seg=n_seg),
        out_shape=[jax.ShapeDtypeStruct((e, ri, f), BF16) for ri in rs],
        grid=(e, f // tf),
        in_specs=[pl.BlockSpec((None, ri, d // 2), lambda i, j: (i, 0, 0)) for ri in rs]
        + [pl.BlockSpec((None, None, d, tf), lambda i, j: (layer, i, 0, j))] * 2,
        out_specs=[pl.BlockSpec((None, ri, tf), lambda i, j: (i, 0, j)) for ri in rs],
        scratch_shapes=[pltpu.VMEM((ri, d), BF16) for ri in rs],
        compiler_params=_params(("parallel", "arbitrary"), est),
        name="ffn_up",
    )(*xgs, w_gate, w_up)
    tn = min(d, 512)
    est = 2 * (r * f * 2 + f * tn * 4 + r * tn * 4 + r * LANES * 4) + f * tn * 2 + 2 * r * tn * 4
    return pl.pallas_call(
        functools.partial(_ffn_down_kernel, n_seg=n_seg),
        out_shape=[jax.ShapeDtypeStruct((e, 2, ri, d), BF16) for ri in rs],
        grid=(e, d // tn),
        in_specs=[pl.BlockSpec((None, ri, f), lambda i, j: (i, 0, 0)) for ri in rs]
        + [pl.BlockSpec((None, None, f, tn), lambda i, j: (layer, i, 0, j))]
        + [pl.BlockSpec((None, ri, 1), lambda i, j: (i, 0, 0)) for ri in rs],
        out_specs=[pl.BlockSpec((None, 2, ri, tn), lambda i, j: (i, 0, 0, j)) for ri in rs],
        compiler_params=_params(("parallel", "arbitrary"), est),
        name="ffn_down",
    )(*hids, w_down, *wts)


def _select_kernel(aff_ref, pos_ref, lor_ref, cnt_ref, off_ref, ps_ref, *, cap):
    e, g, ln = aff_ref.shape
    bits = pltpu.bitcast(aff_ref[...], jnp.int32)

    def count(mask):
        per_lane = jnp.sum(jnp.where(mask, 1.0, 0.0), axis=1)
        return jnp.sum(per_lane, axis=1, keepdims=True)[:, :, None]

    def search(i, t):
        cand = t | jnp.left_shift(jnp.int32(1), 30 - i)
        return jnp.where(count(bits >= cand) >= cap, cand, t)

    thr = lax.fori_loop(0, 31, search, jnp.zeros((e, 1, 1), jnp.int32))
    gt = bits > thr
    eq = bits == thr
    need = cap - count(gt)

    r0 = lax.broadcasted_iota(jnp.int32, (ln, ln), 0)
    r1 = lax.broadcasted_iota(jnp.int32, (ln, ln), 1)
    upper = jnp.where(r0 <= r1, 1.0, 0.0).astype(BF16)
    ones = jnp.ones((ln, ln), BF16)
    g0 = lax.broadcasted_iota(jnp.int32, (g, g), 0)
    g1 = lax.broadcasted_iota(jnp.int32, (g, g), 1)
    earlier = jnp.where(g1 < g0, 1.0, 0.0).astype(BF16)

    def prefix(mask):
        x = jnp.where(mask, 1.0, 0.0).astype(BF16).reshape(e * g, ln)
        incl = _dot(x, upper).reshape(e, g, ln)
        tot = _dot(x, ones).reshape(e, g, ln)
        off = jnp.stack([_dot(earlier, tot[i].astype(BF16)) for i in range(e)])
        return incl, tot, off

    incl_eq, _, off_eq = prefix(eq)
    sel = gt | (eq & (off_eq + incl_eq - 1.0 < need))
    incl, tot, off = prefix(sel)
    pos_ref[...] = jnp.where(sel, off + incl - 1.0, -1.0).astype(jnp.int32)
    cnt_ref[...] = tot.astype(jnp.int32)
    off_ref[...] = off.astype(jnp.int32)
    ps_ref[...] = incl

    rank = lax.broadcasted_iota(jnp.int32, (1, ln, 1), 1).astype(F32)
    ones8 = jnp.ones((8, ln), BF16)

    def lane_of_rank(i, c):
        below = jnp.where(ps_ref[i][:, None, :] <= rank, 1.0, 0.0).astype(BF16).reshape(g * ln, ln)
        lor_ref[i] = _dot_nt(ones8, below)[0:1]
        return c

    lax.fori_loop(0, e, lane_of_rank, 0)


def moe_select(aff_t, cap):
    e, n = aff_t.shape
    g = n // LANES
    g_pad = -(-g // 16) * 16
    a = aff_t.reshape(e, g, LANES)
    if g_pad != g:
        a = jnp.concatenate([a, jnp.full((e, g_pad - g, LANES), -1.0, F32)], axis=1)
    shp = jax.ShapeDtypeStruct((e, g_pad, LANES), jnp.int32)
    full = pl.BlockSpec((e, g_pad, LANES), lambda i: (0, 0, 0))
    pos, lor, cnt, off = pl.pallas_call(
        functools.partial(_select_kernel, cap=cap),
        out_shape=(shp, jax.ShapeDtypeStruct((e, 1, g_pad * LANES), F32), shp, shp),
        grid=(1,),
        in_specs=[full],
        out_specs=(full, pl.BlockSpec((e, 1, g_pad * LANES), lambda i: (0, 0, 0)), full, full),
        scratch_shapes=[pltpu.VMEM((e, g_pad, LANES), F32)],
        compiler_params=_params(("arbitrary",), 24 * e * g_pad * LANES * 4 + 4 * g_pad * LANES * LANES * 4),
        name="moe_select",
    )(a)
    return (pos.reshape(e, g_pad * LANES)[:, :n], lor.astype(jnp.int32), cnt[:, :, 0].reshape(e, 1, g_pad),
            off[:, :g, 0])


def _row_copy(h_hbm, x_ref, sem, token, row):
    return pltpu.make_async_copy(h_hbm.at[pl.ds(token, 1), :], x_ref.at[0, pl.ds(row, 1), :], sem)


def _gather_kernel(lor_ref, cnt_ref, aff_ref, h_hbm, x_ref, wt_ref, sem):
    cap = x_ref.shape[1]

    def group(g, slot):
        def row(r, slot):
            token = g * LANES + lor_ref[0, g * LANES + r]
            wt_ref[0, slot] = aff_ref[0, token]
            _row_copy(h_hbm, x_ref, sem, token, slot).start()
            return slot + 1

        return lax.fori_loop(0, cnt_ref[0, g], row, slot)

    lax.fori_loop(0, cnt_ref.shape[1], group, 0)
    pltpu.make_async_copy(h_hbm.at[pl.ds(0, cap), :], x_ref.at[0], sem).wait()


def moe_gather(hp, aff_t, lor, cnt, cap):
    n, w = hp.shape
    e = lor.shape[0]

    def smem(width):
        return pl.BlockSpec((None, 1, width), lambda i: (i, 0, 0), memory_space=pltpu.SMEM)

    x, wt = pl.pallas_call(
        _gather_kernel,
        out_shape=(jax.ShapeDtypeStruct((e, cap, w), jnp.uint32), jax.ShapeDtypeStruct((e, 1, cap), F32)),
        grid=(e,),
        in_specs=[smem(lor.shape[2]), smem(cnt.shape[2]), smem(n), pl.BlockSpec(memory_space=pl.ANY)],
        out_specs=(pl.BlockSpec((1, cap, w), lambda i: (i, 0, 0)), smem(cap)),
        scratch_shapes=[pltpu.SemaphoreType.DMA(())],
        compiler_params=_params(("arbitrary",), 2 * cap * w * 4),
        name="moe_gather",
    )(lor, cnt, aff_t.reshape(e, 1, n), hp)
    return x, wt.reshape(e, cap, 1)


def _window_copy(y_hbm, dst, sem, e, src, win):
    return pltpu.make_async_copy(y_hbm.at[e, :, pl.ds(src, win)], dst, sem)


def _combine_kernel(offb_ref, y_hbm, pos_ref, x_ref, gate_ref, g_ref, b_ref, o_ref, ybuf, ybuf_x, sem, acc_ref,
                    *, row, cap, win):
    b = pl.program_id(0)
    n_exp, tb = pos_ref.shape
    half = n_exp // 2
    par = b % 2

    def window(e, k, blk=b):
        first = (offb_ref[e, blk] // 8) * 8 + k * win
        return first, pl.multiple_of(jnp.minimum(first, cap - win), 8)

    def onehot2(e, first, src):
        slots = src + lax.broadcasted_iota(jnp.int32, (win, 1), 0)
        hit = jnp.logical_and(pos_ref[e:e + 1, :] == slots, slots >= first)
        oh = jnp.where(hit, 1.0, 0.0).astype(BF16)
        return jnp.concatenate([oh, oh], axis=0)

    def copies(h, blk, p):
        return [_window_copy(y_hbm, ybuf.at[p, h, j], sem.at[2 * p + h], h * half + j,
                             window(h * half + j, 0, blk)[1], win) for j in range(half)]

    def start_block(blk, p):
        for h in range(2):
            for cp in copies(h, blk, p):
                cp.start()

    @pl.when(b == 0)
    def _():
        start_block(b, par)

    @pl.when(b + 1 < pl.num_programs(0))
    def _():
        start_block(b + 1, 1 - par)

    acc = None
    for h in range(2):
        for cp in copies(h, b, par):
            cp.wait()
        lhs = jnp.concatenate([onehot2(h * half + j, *window(h * half + j, 0)) for j in range(half)], axis=0)
        part = _dot_tn(lhs, ybuf[par, h].reshape(half * 2 * win, ybuf.shape[-1]))
        acc = part if acc is None else acc + part
    acc_ref[...] = acc

    for e in range(n_exp):
        n_win = (offb_ref[e, b + 1] - (offb_ref[e, b] // 8) * 8 + win - 1) // win

        def extra(k, c, e=e):
            first_k, src_k = window(e, k)
            cp = _window_copy(y_hbm, ybuf_x, sem.at[4], e, src_k, win)
            cp.start()
            cp.wait()
            acc_ref[...] += _dot_tn(onehot2(e, first_k, src_k), ybuf_x[...].reshape(2 * win, ybuf_x.shape[-1]))
            return c

        lax.fori_loop(1, n_win, extra, 0)

    z = DEEPNORM_ALPHA * x_ref[...] + gate_ref[row:row + 1, :] * acc_ref[...]
    o_ref[...] = _layer_norm(z) * g_ref[...] + b_ref[...]


def moe_combine_postnorm(y, pos, off, x, mod, row, k_gate, g, b):
    n, d = x.shape
    e, _, cap, _ = y.shape
    tb = min(n, 256)
    nb = n // tb
    win = min(cap, COMBINE_WINDOW)
    offb = jnp.concatenate([off[:, ::tb // LANES], jnp.full((e, 1), cap, jnp.int32)], axis=1)
    grid_spec = pltpu.PrefetchScalarGridSpec(
        num_scalar_prefetch=1,
        grid=(nb,),
        in_specs=[pl.BlockSpec(memory_space=pl.ANY),
                  pl.BlockSpec((e, tb), lambda i, o: (0, i)),
                  pl.BlockSpec((tb, d), lambda i, o: (i, 0)),
                  pl.BlockSpec((8, d), lambda i, o: (0, k_gate)),
                  pl.BlockSpec((1, d), lambda i, o: (0, 0)),
                  pl.BlockSpec((1, d), lambda i, o: (0, 0))],
        out_specs=pl.BlockSpec((tb, d), lambda i, o: (i, 0)),
        scratch_shapes=[pltpu.VMEM((2, 2, e // 2, 2, win, d), BF16), pltpu.VMEM((2, win, d), BF16),
                        pltpu.SemaphoreType.DMA((5,)), pltpu.VMEM((tb, d), F32)],
    )
    est = 8 * tb * d * 4 + (2 * e + 1) * 2 * win * d * 2 + e * win * tb * 2
    return pl.pallas_call(
        functools.partial(_combine_kernel, row=row, cap=cap, win=win),
        out_shape=jax.ShapeDtypeStruct((n, d), F32),
        grid_spec=grid_spec,
        compiler_params=_params(("arbitrary",), est),
        name="moe_combine",
    )(offb, y, pos, x, mod, g.reshape(1, d), b.reshape(1, d))


def moe_route(x, mod, row, w_router_t):
    m = x.shape[0]
    cap = max(1, EC_CAPACITY_FACTOR * m // N_EXPERTS)
    h, aff_t = moe_router(x, mod, row, 3, 4, w_router_t)
    pos, lor, cnt, off = moe_select(aff_t, cap)
    xg, wt = moe_gather(h, aff_t, lor, cnt, cap)
    return xg, wt, (pos, off)


def _rope_angles(n_tokens, rot_dim):
    rows = n_tokens // GRID_W
    row = jnp.repeat(jnp.arange(rows, dtype=F32), GRID_W)
    col = jnp.tile(jnp.arange(GRID_W, dtype=F32), rows)
    n_freq = rot_dim // 4
    inv = ROPE_THETA ** (-jnp.arange(n_freq, dtype=F32) / n_freq)
    return jnp.concatenate([row[:, None] * inv, col[:, None] * inv], axis=-1)


def _gqa_rope_tables(n_tokens):
    ang = _rope_angles(n_tokens, HEAD_DIM)
    c, s = jnp.cos(ang), jnp.sin(ang)
    return jnp.concatenate([c, c], axis=-1), jnp.concatenate([-s, s], axis=-1)


def _mla_rope_tables(n_tokens):
    ang = _rope_angles(n_tokens, MLA_ROPE)
    c, s = jnp.cos(ang), jnp.sin(ang)
    z = jnp.zeros_like(c)
    cos = jnp.concatenate([c, c, z, z], axis=-1)
    sa = jnp.concatenate([-s, z, z, z], axis=-1)
    sb = jnp.concatenate([z, s, z, z], axis=-1)
    return cos, sa, sb


def kernel(x, c, ctx, c_ctx, ada_w, ada_b, ln_g, ln_b, ev_w_in, ev_w_out, hgrn_lb, hgrn_norm_g, gqa_q_norm_g, gqa_k_norm_g, mla_w_down, mla_q_norm_g, mla_kv_norm_g, mla_w_uq, mla_w_ukv, mla_w_o, moe_router, moe_w_gate, moe_w_up, moe_w_down):
    d = D_MODEL
    xl = x[0]
    xc = ctx[0]
    n_lat, n_ctx = xl.shape[0], xc.shape[0]
    cc = jnp.zeros((8, d), F32).at[0].set(c[0]).at[1].set(c_ctx)
    lb_all = jnp.cumsum(jax.nn.softmax(hgrn_lb.astype(F32), axis=1), axis=1)
    gqa_tabs = _gqa_rope_tables(n_lat)
    gqa_tabs_ctx = [jnp.ones((n_ctx, HEAD_DIM), F32), jnp.zeros((n_ctx, HEAD_DIM), F32)]
    mla_tabs = _mla_rope_tables(n_lat)
    mla_tabs_ctx = [jnp.ones((n_ctx, HEAD_DIM), F32), jnp.zeros((n_ctx, HEAD_DIM), F32), jnp.zeros((n_ctx, HEAD_DIM), F32)]
    LAT, CTX = 0, 1

    for l in range(DEPTH):
        last = l == DEPTH - 1
        i = l // 2
        mod = adaln(cc, ada_w[l], ada_b[l])
        if l % 2 == 0:
            w_in = ev_w_in[i].astype(BF16)
            w_out = ev_w_out[i].astype(BF16)
            lb = lb_all[:, l].reshape(2, 1, A_WIDTH)
            scale = HEAD_DIM ** -0.5 * LOG2E
            proj_c = lnmod_matmul(xc, mod, CTX, 0, 1, w_in, 512)
            proj_l = lnmod_matmul(xl, mod, LAT, 0, 1, w_in, 512)
            s0 = jnp.zeros((2, A_HEADS, HEAD_DIM, HEAD_DIM), F32)
            o_c, s_c = hgrn_scan(proj_c, lb, s0)
            o_l, _ = hgrn_scan(proj_l, lb, s_c)
            a_l = hgrn_out(o_l, proj_l, hgrn_norm_g[i])
            qcol, kcol, vcol = 5 * A_HEADS, 5 * A_HEADS + B_Q_HEADS, 5 * A_WIDTH + B_WIDTH + B_KV_WIDTH
            q_l = norm_rope(proj_l, qcol, B_Q_HEADS, gqa_q_norm_g[i], *gqa_tabs, scale)
            k_l = norm_rope(proj_l, kcol, B_KV_HEADS, gqa_k_norm_g[i], *gqa_tabs, 1.0)
            k_c = norm_rope(proj_c, kcol, B_KV_HEADS, gqa_k_norm_g[i], *gqa_tabs_ctx, 1.0)
            v_l = proj_l[:, vcol:].astype(BF16)
            v_c = proj_c[:, vcol:].astype(BF16)
            att = dict(n_heads=B_Q_HEADS, n_kv_heads=B_KV_HEADS, dq=HEAD_DIM, dv=HEAD_DIM)
            b_l = flash_attention(q_l, k_l, v_l, k_c, v_c, **att)
            w_parts = [w_out[:A_WIDTH], w_out[A_WIDTH:]]
            xl_new = proj_postnorm([a_l, b_l], w_parts, xl, mod, LAT, 2, ln_g[l, 0], ln_b[l, 0])
            if not last:
                a_c = hgrn_out(o_c, proj_c, hgrn_norm_g[i])
                q_c = norm_rope(proj_c, qcol, B_Q_HEADS, gqa_q_norm_g[i], *gqa_tabs_ctx, scale)
                b_c = flash_attention(q_c, k_c, v_c, **att)
                xc = proj_postnorm([a_c, b_c], w_parts, xc, mod, CTX, 2, ln_g[l, 0], ln_b[l, 0])
            xl = xl_new
        else:
            hd = HEAD_DIM
            pad = (-mla_w_down.shape[2]) % hd
            w_down = jnp.pad(mla_w_down[i], ((0, 0), (0, pad))).astype(BF16)
            w_uq = mla_w_uq[i].reshape(MLA_Q_LORA, MLA_HEADS, hd + MLA_ROPE)
            w_uq = jnp.pad(w_uq, ((0, 0), (0, 0), (0, MLA_QK - hd - MLA_ROPE))).reshape(MLA_Q_LORA, MLA_HEADS * MLA_QK).astype(BF16)
            w_ukv = mla_w_ukv[i].reshape(MLA_KV_LORA, MLA_HEADS, 2 * hd)
            w_uk = w_ukv[:, :, :hd].reshape(MLA_KV_LORA, MLA_HEADS * hd).astype(BF16)
            w_uv = w_ukv[:, :, hd:].reshape(MLA_KV_LORA, MLA_HEADS * hd).astype(BF16)
            w_o = mla_w_o[i].astype(BF16)
            scale = (hd + MLA_ROPE) ** -0.5 * LOG2E
            dn_c = lnmod_matmul(xc, mod, CTX, 0, 1, w_down, w_down.shape[1])
            dn_l = lnmod_matmul(xl, mod, LAT, 0, 1, w_down, w_down.shape[1])
            q_l = mla_q(dn_l, mla_q_norm_g[i], w_uq, *mla_tabs, scale)
            k_l = mla_k(dn_l, mla_kv_norm_g[i], w_uk, *mla_tabs)
            k_c = mla_k(dn_c, mla_kv_norm_g[i], w_uk, *mla_tabs_ctx)
            v_l = rms_matmul(dn_l, 1, mla_kv_norm_g[i], w_uv, 1024)
            v_c = rms_matmul(dn_c, 1, mla_kv_norm_g[i], w_uv, 1024)
            att = dict(n_heads=MLA_HEADS, n_kv_heads=MLA_HEADS, dq=MLA_QK, dv=hd)
            o_l = flash_attention(q_l, k_l, v_l, k_c, v_c, **att)
            xl_new = proj_postnorm([o_l], [w_o], xl, mod, LAT, 2, ln_g[l, 0], ln_b[l, 0])
            if not last:
                q_c = mla_q(dn_c, mla_q_norm_g[i], w_uq, *mla_tabs_ctx, scale)
                o_c = flash_attention(q_c, k_c, v_c, **att)
                xc = proj_postnorm([o_c], [w_o], xc, mod, CTX, 2, ln_g[l, 0], ln_b[l, 0])
            xl = xl_new

        w_router_t = moe_router[l].T
        segs = [(xl, LAT)] if last else [(xl, LAT), (xc, CTX)]
        routes = [moe_route(xs, mod, row, w_router_t) for xs, row in segs]
        ys = expert_ffn([r[0] for r in routes], [r[1] for r in routes], moe_w_gate, moe_w_up, moe_w_down, l)
        outs = [moe_combine_postnorm(y, *r[2], xs, mod, row, 5, ln_g[l, 1], ln_b[l, 1])
                for y, r, (xs, row) in zip(ys, routes, segs)]
        xl = outs[0]
        if not last:
            xc = outs[1]
    return xl[None]
```

```python
import functools
import math

import numpy as np
import jax
import jax.numpy as jnp
from jax import lax
from jax.experimental import pallas as pl
from jax.experimental.pallas import tpu as pltpu

F32 = jnp.float32
BF16 = jnp.bfloat16

D_MODEL = 2048
DEPTH = 2
GRID_W = 64
HEAD_DIM = 128
A_HEADS = D_MODEL // 256
A_WIDTH = A_HEADS * HEAD_DIM
B_Q_HEADS = D_MODEL // 256
B_KV_HEADS = 2
B_WIDTH = B_Q_HEADS * HEAD_DIM
B_KV_WIDTH = B_KV_HEADS * HEAD_DIM
MLA_HEADS = D_MODEL // 128
MLA_Q_LORA = 512
MLA_KV_LORA = 512
MLA_ROPE = 64
MLA_QK = 2 * HEAD_DIM
N_EXPERTS = 16
EXPERT_FF = D_MODEL // 2
EC_CAPACITY_FACTOR = 2
ROPE_THETA = 10000.0
NORM_EPS = 1e-6
DEEPNORM_ALPHA = (2.0 * DEPTH) ** 0.25

HGRN_CHUNK = 128
COMBINE_WINDOW = 64
LANES = 128
LOG2E = math.log2(math.e)
V7X_VMEM_BYTES = 64 * 1024 * 1024
VMEM_CAP_BYTES = V7X_VMEM_BYTES - 8 * 1024 * 1024


def _params(semantics, vmem_estimate_bytes):
    limit = int(min(max(2 * vmem_estimate_bytes, 32 * 1024 * 1024), VMEM_CAP_BYTES))
    return pltpu.CompilerParams(dimension_semantics=semantics, vmem_limit_bytes=limit)


def _layer_norm(x):
    mu = jnp.mean(x, axis=-1, keepdims=True)
    xc = x - mu
    var = jnp.mean(xc * xc, axis=-1, keepdims=True)
    return xc * lax.rsqrt(var + NORM_EPS)


def _rms(x):
    return x * lax.rsqrt(jnp.mean(x * x, axis=-1, keepdims=True) + NORM_EPS)


def _dot(a, b):
    return jnp.dot(a, b, preferred_element_type=F32)


def _dot_nt(a, b):
    return lax.dot_general(a, b, (((1,), (1,)), ((), ())), preferred_element_type=F32)


def _dot_tn(a, b):
    return lax.dot_general(a, b, (((0,), (0,)), ((), ())), preferred_element_type=F32)


def _split3(x):
    x1 = x.astype(BF16)
    r1 = x - x1.astype(F32)
    x2 = r1.astype(BF16)
    x3 = (r1 - x2.astype(F32)).astype(BF16)
    return x1, x2, x3


def _adaln_kernel(c_ref, w_ref, b_ref, o_ref):
    c = c_ref[...]
    s = c * jax.nn.sigmoid(c)
    w = w_ref[...]
    s1, s2, s3 = _split3(s)
    w1, w2, w3 = _split3(w)
    acc = _dot(s1, w3) + _dot(s3, w1) + _dot(s2, w2)
    acc = acc + _dot(s1, w2) + _dot(s2, w1)
    acc = acc + _dot(s1, w1)
    o_ref[...] = acc + b_ref[...]


def adaln(cc, w, b):
    d, n = w.shape
    tn = 1536 if n % 1536 == 0 else n
    est = 2 * d * tn * 4 * 2
    return pl.pallas_call(
        _adaln_kernel,
        out_shape=jax.ShapeDtypeStruct((8, n), F32),
        grid=(n // tn,),
        in_specs=[pl.BlockSpec((8, d), lambda j: (0, 0)),
                  pl.BlockSpec((d, tn), lambda j: (0, j)),
                  pl.BlockSpec((1, tn), lambda j: (0, j))],
        out_specs=pl.BlockSpec((8, tn), lambda j: (0, j)),
        compiler_params=_params(("parallel",), est),
        name="adaln",
    )(cc, w, b.reshape(1, n))


def _lnmod_mm_kernel(x_ref, sh_ref, sc_ref, w_ref, o_ref, h_ref, *, row):
    @pl.when(pl.program_id(1) == 0)
    def _():
        hn = _layer_norm(x_ref[...])
        h = hn * (1.0 + sc_ref[row:row + 1, :]) + sh_ref[row:row + 1, :]
        h_ref[...] = h.astype(BF16)

    o_ref[...] = _dot(h_ref[...], w_ref[...]).astype(o_ref.dtype)


def lnmod_matmul(x, mod, row, k_shift, k_scale, w, tn):
    m, d = x.shape
    n = w.shape[1]
    tm = min(m, 1024)
    est = 2 * tm * d * 4 + tm * d * 2 + 2 * d * tn * 2 + 2 * tm * tn * 4
    return pl.pallas_call(
        functools.partial(_lnmod_mm_kernel, row=row),
        out_shape=jax.ShapeDtypeStruct((m, n), F32),
        grid=(m // tm, n // tn),
        in_specs=[pl.BlockSpec((tm, d), lambda i, j: (i, 0)),
                  pl.BlockSpec((8, d), lambda i, j: (0, k_shift)),
                  pl.BlockSpec((8, d), lambda i, j: (0, k_scale)),
                  pl.BlockSpec((d, tn), lambda i, j: (0, j))],
        out_specs=pl.BlockSpec((tm, tn), lambda i, j: (i, j)),
        scratch_shapes=[pltpu.VMEM((tm, d), BF16)],
        compiler_params=_params(("parallel", "arbitrary"), est),
        name="lnmod_matmul",
    )(x, mod, mod, w)


def _hgrn_tables(c):
    n_lvl = int(math.log2(c))
    r = np.arange(c)
    u = np.arange(c)[None, :]
    blocks, masks = [], []
    for l in range(n_lvl):
        half = 1 << l
        base = (r // (2 * half)) * (2 * half)
        anchor = (base + half - 1)[:, None]
        upper = (r >= base + half)[:, None]
        rr = r[:, None]
        blocks.append(np.where(upper, (u > anchor) & (u <= rr), (u > rr) & (u <= anchor)))
        same = (r[:, None] // (2 * half)) == (r[None, :] // (2 * half))
        masks.append(same & upper & ~(upper.T))
    blocks.append(u <= r[:, None])
    blocks.append(u > r[:, None])
    blocks.append(np.ones((16, c), bool))
    masks.append(np.eye(c, dtype=bool))
    fwd_s = np.concatenate(blocks, axis=0).astype(np.float32)
    fwd_m = np.stack(masks).astype(np.float32)
    bwd_s = np.concatenate([b[::-1, ::-1] for b in blocks], axis=0).astype(np.float32)
    bwd_m = fwd_m[:, ::-1, ::-1]
    return (jnp.asarray(np.stack([fwd_s, bwd_s]), BF16), jnp.asarray(np.stack([fwd_m, bwd_m]), F32))


def _hgrn_kernel(q_ref, v_ref, f_ref, lb_ref, sums_ref, mask_ref, s0_ref, o_ref, sfin_ref, st_ref):
    c = q_ref.shape[0]
    hd = HEAD_DIM
    n_lvl = mask_ref.shape[0] - 1
    j = pl.program_id(1)

    @pl.when(j == 0)
    def _():
        st_ref[...] = s0_ref[...]

    for h in range(q_ref.shape[1] // hd):
        cols = slice(h * hd, (h + 1) * hd)
        q = q_ref[:, cols]
        vb = v_ref[:, cols].astype(BF16)
        lb = lb_ref[:, cols]
        f = lb + (1.0 - lb) * jax.nn.sigmoid(f_ref[:, cols])
        g = jnp.log(f)
        k = 1.0 - f
        g1 = g.astype(BF16)
        g2 = (g - g1.astype(F32)).astype(BF16)
        e2 = _dot(sums_ref[...], jnp.concatenate([g1, g2], axis=1))
        e = e2[:, hd:] + e2[:, :hd]

        scores = _dot_nt(q.astype(BF16), k.astype(BF16)) * mask_ref[n_lvl]
        for l in range(n_lvl):
            z = jnp.exp(e[l * c:(l + 1) * c])
            scores = scores + _dot_nt((q * z).astype(BF16), (k * z).astype(BF16)) * mask_ref[l]

        cum = e[n_lvl * c:(n_lvl + 1) * c]
        rem = e[(n_lvl + 1) * c:(n_lvl + 2) * c]
        tot = e[(n_lvl + 2) * c:(n_lvl + 2) * c + 1]
        st = st_ref[h]
        o = _dot(scores.astype(BF16), vb) + _dot_nt((q * jnp.exp(cum)).astype(BF16), st.astype(BF16))
        o_ref[:, cols] = o
        st_new = st * jnp.exp(tot) + _dot_tn(vb, (k * jnp.exp(rem)).astype(BF16))
        st_ref[h] = st_new

    @pl.when(j == pl.num_programs(1) - 1)
    def _():
        sfin_ref[...] = st_ref[...]


def hgrn_scan(proj, lb, s0):
    seq = proj.shape[0]
    c = HGRN_CHUNK
    nc = seq // c
    sums, masks = _hgrn_tables(c)
    hd, w = HEAD_DIM, A_WIDTH

    def blk(d, j):
        return jnp.where(d == 0, j, nc - 1 - j)

    est = (2 * (4 * c * w * 4 + sums.shape[1] * c * 2 + masks.shape[1] * c * c * 4 + 2 * A_HEADS * hd * hd * 4)
           + A_HEADS * hd * hd * 4)
    return pl.pallas_call(
        _hgrn_kernel,
        out_shape=(jax.ShapeDtypeStruct((2, seq, w), F32),
                   jax.ShapeDtypeStruct((2, A_HEADS, hd, hd), F32)),
        grid=(2, nc),
        in_specs=[pl.BlockSpec((c, w), lambda d, j: (blk(d, j), 0)),
                  pl.BlockSpec((c, w), lambda d, j: (blk(d, j), 3)),
                  pl.BlockSpec((c, w), lambda d, j: (blk(d, j), 1 + d)),
                  pl.BlockSpec((None, 1, w), lambda d, j: (d, 0, 0)),
                  pl.BlockSpec((None, sums.shape[1], c), lambda d, j: (d, 0, 0)),
                  pl.BlockSpec((None, masks.shape[1], c, c), lambda d, j: (d, 0, 0, 0)),
                  pl.BlockSpec((None, A_HEADS, hd, hd), lambda d, j: (d, 0, 0, 0))],
        out_specs=(pl.BlockSpec((None, c, w), lambda d, j: (d, blk(d, j), 0)),
                   pl.BlockSpec((None, A_HEADS, hd, hd), lambda d, j: (d, 0, 0, 0))),
        scratch_shapes=[pltpu.VMEM((A_HEADS, hd, hd), F32)],
        compiler_params=_params(("parallel", "arbitrary"), est),
        name="hgrn_scan",
    )(proj, proj, proj, lb, sums, masks, s0)


def _hgrn_out_kernel(o_ref, gate_ref, g_ref, a_ref):
    hd = HEAD_DIM
    for h in range(a_ref.shape[1] // hd):
        cols = slice(h * hd, (h + 1) * hd)
        o = o_ref[0, :, cols] + o_ref[1, :, cols]
        gate = gate_ref[:, cols]
        a_ref[:, cols] = (_rms(o) * g_ref[...] * (gate * jax.nn.sigmoid(gate))).astype(a_ref.dtype)


def hgrn_out(o, proj, norm_g):
    seq = o.shape[1]
    tm = min(seq, 256)
    hd, w = HEAD_DIM, A_WIDTH
    return pl.pallas_call(
        _hgrn_out_kernel,
        out_shape=jax.ShapeDtypeStruct((seq, w), BF16),
        grid=(seq // tm,),
        in_specs=[pl.BlockSpec((2, tm, w), lambda i: (0, i, 0)),
                  pl.BlockSpec((tm, w), lambda i: (i, 4)),
                  pl.BlockSpec((1, hd), lambda i: (0, 0))],
        out_specs=pl.BlockSpec((tm, w), lambda i: (i, 0)),
        compiler_params=_params(("parallel",), 8 * tm * w * 4),
        name="hgrn_out",
    )(o, proj, norm_g.reshape(1, hd))


def _norm_rope_kernel(x_ref, g_ref, cos_ref, sin_ref, o_ref, *, scale):
    hd = HEAD_DIM
    for h in range(o_ref.shape[1] // hd):
        cols = slice(h * hd, (h + 1) * hd)
        y = _rms(x_ref[:, cols]) * g_ref[...]
        y = y * cos_ref[...] + pltpu.roll(y, hd // 2, 1) * sin_ref[...]
        o_ref[:, cols] = (y * scale).astype(o_ref.dtype)


def norm_rope(proj, col0, n_heads, g, cos, sin, scale):
    seq = proj.shape[0]
    tm = min(seq, 256)
    hd = HEAD_DIM
    w = n_heads * hd
    return pl.pallas_call(
        functools.partial(_norm_rope_kernel, scale=scale),
        out_shape=jax.ShapeDtypeStruct((seq, w), BF16),
        grid=(seq // tm,),
        in_specs=[pl.BlockSpec((tm, w), lambda i: (i, col0 // w)),
                  pl.BlockSpec((1, hd), lambda i: (0, 0)),
                  pl.BlockSpec((tm, hd), lambda i: (i, 0)),
                  pl.BlockSpec((tm, hd), lambda i: (i, 0))],
        out_specs=pl.BlockSpec((tm, w), lambda i: (i, 0)),
        compiler_params=_params(("parallel",), 6 * tm * w * 4 + 4 * tm * hd * 4),
        name="norm_rope",
    )(proj, g.reshape(1, hd), cos, sin)


def _flash_update(q, k, v, m_ref, l_ref, acc_ref):
    s = _dot_nt(q, k)
    m_prev = m_ref[...]
    m_new = jnp.maximum(m_prev, jnp.max(s, axis=-1, keepdims=True))
    alpha = jnp.exp2(m_prev - m_new)
    ps = [jnp.exp2(s[:, c * LANES:(c + 1) * LANES] - m_new) for c in range(s.shape[1] // LANES)]
    psum = ps[0]
    for pc in ps[1:]:
        psum = psum + pc
    p = jnp.concatenate([pc.astype(BF16) for pc in ps], axis=1)
    l_ref[...] = alpha * l_ref[...] + psum
    acc_ref[...] = alpha * acc_ref[...] + _dot(p, v)
    m_ref[...] = m_new


def _flash_kernel(*refs, has_ctx):
    if has_ctx:
        q_ref, k_ref, v_ref, kc_ref, vc_ref, o_ref, m_ref, l_ref, acc_ref = refs
    else:
        q_ref, k_ref, v_ref, o_ref, m_ref, l_ref, acc_ref = refs
    j = pl.program_id(2)

    @pl.when(j == 0)
    def _():
        m_ref[...] = jnp.full(m_ref.shape, -jnp.inf, F32)
        l_ref[...] = jnp.zeros(l_ref.shape, F32)
        acc_ref[...] = jnp.zeros(acc_ref.shape, F32)
        if has_ctx:
            _flash_update(q_ref[...], kc_ref[...], vc_ref[...], m_ref, l_ref, acc_ref)

    _flash_update(q_ref[...], k_ref[...], v_ref[...], m_ref, l_ref, acc_ref)

    @pl.when(j == pl.num_programs(2) - 1)
    def _():
        l = jnp.sum(l_ref[...], axis=-1, keepdims=True)
        o_ref[...] = (acc_ref[...] / l).astype(o_ref.dtype)


def flash_attention(q, k, v, k_ctx=None, v_ctx=None, *, n_heads, n_kv_heads, dq, dv):
    n, m = q.shape[0], k.shape[0]
    grp = n_heads // n_kv_heads
    tq = min(n, 2048)
    tk = min(m, 2048)
    has_ctx = k_ctx is not None
    in_specs = [pl.BlockSpec((tq, dq), lambda h, i, j: (i, h)),
                pl.BlockSpec((tk, dq), lambda h, i, j: (j, h // grp)),
                pl.BlockSpec((tk, dv), lambda h, i, j: (j, h // grp))]
    args = [q, k, v]
    if has_ctx:
        mc = k_ctx.shape[0]
        in_specs += [pl.BlockSpec((mc, dq), lambda h, i, j: (0, h // grp)),
                     pl.BlockSpec((mc, dv), lambda h, i, j: (0, h // grp))]
        args += [k_ctx, v_ctx]
    est = 2 * (tq * dq + tk * dq + tk * dv + tq * dv) * 2 + tq * (dv + 256) * 4 + 6 * tq * tk * 4
    return pl.pallas_call(
        functools.partial(_flash_kernel, has_ctx=has_ctx),
        out_shape=jax.ShapeDtypeStruct((n, n_heads * dv), BF16),
        grid=(n_heads, n // tq, m // tk),
        in_specs=in_specs,
        out_specs=pl.BlockSpec((tq, dv), lambda h, i, j: (i, h)),
        scratch_shapes=[pltpu.VMEM((tq, LANES), F32), pltpu.VMEM((tq, LANES), F32), pltpu.VMEM((tq, dv), F32)],
        compiler_params=_params(("parallel", "parallel", "arbitrary"), est),
        name="flash_attention",
    )(*args)


def _proj_postnorm_kernel(*refs, n_in, row):
    a_refs = refs[:n_in]
    w_refs = refs[n_in:2 * n_in]
    x_ref, gate_ref, g_ref, b_ref, o_ref = refs[2 * n_in:]
    y = _dot(a_refs[0][...], w_refs[0][...])
    for a_ref, w_ref in zip(a_refs[1:], w_refs[1:]):
        y = y + _dot(a_ref[...], w_ref[...])
    z = DEEPNORM_ALPHA * x_ref[...] + gate_ref[row:row + 1, :] * y
    o_ref[...] = _layer_norm(z) * g_ref[...] + b_ref[...]


def proj_postnorm(acts, ws, x, mod, row, k_gate, g, b):
    m, d = x.shape
    tm = min(m, 512)
    n_in = len(acts)
    once = pl.Buffered(1)
    in_specs = [pl.BlockSpec((tm, a.shape[1]), lambda i: (i, 0)) for a in acts]
    in_specs += [pl.BlockSpec(w.shape, lambda i: (0, 0), pipeline_mode=once) for w in ws]
    in_specs += [pl.BlockSpec((tm, d), lambda i: (i, 0)),
                 pl.BlockSpec((8, d), lambda i: (0, k_gate)),
                 pl.BlockSpec((1, d), lambda i: (0, 0)),
                 pl.BlockSpec((1, d), lambda i: (0, 0))]
    est = sum(w.size * 2 for w in ws) + sum(2 * tm * a.shape[1] * 2 for a in acts) + 6 * tm * d * 4
    return pl.pallas_call(
        functools.partial(_proj_postnorm_kernel, n_in=n_in, row=row),
        out_shape=jax.ShapeDtypeStruct((m, d), F32),
        grid=(m // tm,),
        in_specs=in_specs,
        out_specs=pl.BlockSpec((tm, d), lambda i: (i, 0)),
        compiler_params=_params(("parallel",), est),
        name="proj_postnorm",
    )(*acts, *ws, x, mod, g.reshape(1, d), b.reshape(1, d))


def _add_postnorm_kernel(y_ref, x_ref, gate_ref, g_ref, b_ref, o_ref, *, row):
    z = DEEPNORM_ALPHA * x_ref[...] + gate_ref[row:row + 1, :] * y_ref[...]
    o_ref[...] = _layer_norm(z) * g_ref[...] + b_ref[...]


def add_postnorm(y, x, mod, row, k_gate, g, b):
    m, d = x.shape
    tm = min(m, 512)
    return pl.pallas_call(
        functools.partial(_add_postnorm_kernel, row=row),
        out_shape=jax.ShapeDtypeStruct((m, d), F32),
        grid=(m // tm,),
        in_specs=[pl.BlockSpec((tm, d), lambda i: (i, 0)),
                  pl.BlockSpec((tm, d), lambda i: (i, 0)),
                  pl.BlockSpec((8, d), lambda i: (0, k_gate)),
                  pl.BlockSpec((1, d), lambda i: (0, 0)),
                  pl.BlockSpec((1, d), lambda i: (0, 0))],
        out_specs=pl.BlockSpec((tm, d), lambda i: (i, 0)),
        compiler_params=_params(("parallel",), 8 * tm * d * 4),
        name="add_postnorm",
    )(y, x, mod, g.reshape(1, d), b.reshape(1, d))


def _rms_mm_kernel(x_ref, g_ref, w_ref, o_ref, a_ref):
    @pl.when(pl.program_id(1) == 0)
    def _():
        a_ref[...] = (_rms(x_ref[...]) * g_ref[...]).astype(BF16)

    o_ref[...] = _dot(a_ref[...], w_ref[...]).astype(o_ref.dtype)


def _mla_q_kernel(x_ref, g_ref, w_ref, cos_ref, sa_ref, sb_ref, o_ref, a_ref, *, scale):
    @pl.when(pl.program_id(1) == 0)
    def _():
        a_ref[...] = (_rms(x_ref[...]) * g_ref[...]).astype(BF16)

    y = _dot(a_ref[...], w_ref[...])
    hd = HEAD_DIM
    for h in range(y.shape[1] // MLA_QK):
        c0 = h * MLA_QK
        o_ref[:, c0:c0 + hd] = (y[:, c0:c0 + hd] * scale).astype(o_ref.dtype)
        r = y[:, c0 + hd:c0 + 2 * hd]
        r = r * cos_ref[...] + pltpu.roll(r, hd - MLA_ROPE // 2, 1) * sa_ref[...] + pltpu.roll(r, MLA_ROPE // 2, 1) * sb_ref[...]
        o_ref[:, c0 + hd:c0 + 2 * hd] = (r * scale).astype(o_ref.dtype)


def _mla_k_kernel(x_ref, g_ref, w_ref, kr_ref, cos_ref, sa_ref, sb_ref, o_ref, a_ref, r_ref):
    hd = HEAD_DIM

    @pl.when(pl.program_id(1) == 0)
    def _():
        a_ref[...] = (_rms(x_ref[...]) * g_ref[...]).astype(BF16)
        r = kr_ref[...]
        r = r * cos_ref[...] + pltpu.roll(r, hd - MLA_ROPE // 2, 1) * sa_ref[...] + pltpu.roll(r, MLA_ROPE // 2, 1) * sb_ref[...]
        r_ref[...] = r.astype(BF16)

    y = _dot(a_ref[...], w_ref[...])
    for h in range(y.shape[1] // hd):
        o_ref[:, h * MLA_QK:h * MLA_QK + hd] = y[:, h * hd:(h + 1) * hd].astype(o_ref.dtype)
        o_ref[:, h * MLA_QK + hd:(h + 1) * MLA_QK] = r_ref[...]


def _mla_specs(m, tm, lora, col_blk):
    return [pl.BlockSpec((tm, lora), lambda i, j: (i, col_blk)),
            pl.BlockSpec((1, lora), lambda i, j: (0, 0))]


def rms_matmul(dn, col_blk, g, w, tn):
    m = dn.shape[0]
    lora, n = w.shape
    tm = min(m, 1024)
    est = 2 * tm * lora * 4 + tm * lora * 2 + 2 * lora * tn * 2 + 2 * tm * tn * 2 + tm * tn * 4
    return pl.pallas_call(
        _rms_mm_kernel,
        out_shape=jax.ShapeDtypeStruct((m, n), BF16),
        grid=(m // tm, n // tn),
        in_specs=_mla_specs(m, tm, lora, col_blk) + [pl.BlockSpec((lora, tn), lambda i, j: (0, j))],
        out_specs=pl.BlockSpec((tm, tn), lambda i, j: (i, j)),
        scratch_shapes=[pltpu.VMEM((tm, lora), BF16)],
        compiler_params=_params(("parallel", "arbitrary"), est),
        name="rms_matmul",
    )(dn, g.reshape(1, lora), w)


def mla_q(dn, g, w, cos, sa, sb, scale):
    m = dn.shape[0]
    lora, n = w.shape
    tm = min(m, 1024)
    tn = 4 * MLA_QK
    hd = HEAD_DIM
    est = 2 * tm * lora * 4 + tm * lora * 2 + 2 * lora * tn * 2 + 2 * tm * tn * 2 + 2 * tm * tn * 4 + 6 * tm * hd * 4
    rope_spec = pl.BlockSpec((tm, hd), lambda i, j: (i, 0))
    return pl.pallas_call(
        functools.partial(_mla_q_kernel, scale=scale),
        out_shape=jax.ShapeDtypeStruct((m, n), BF16),
        grid=(m // tm, n // tn),
        in_specs=_mla_specs(m, tm, lora, 0) + [pl.BlockSpec((lora, tn), lambda i, j: (0, j)),
                                               rope_spec, rope_spec, rope_spec],
        out_specs=pl.BlockSpec((tm, tn), lambda i, j: (i, j)),
        scratch_shapes=[pltpu.VMEM((tm, lora), BF16)],
        compiler_params=_params(("parallel", "arbitrary"), est),
        name="mla_q",
    )(dn, g.reshape(1, lora), w, cos, sa, sb)


def mla_k(dn, g, w, cos, sa, sb):
    m = dn.shape[0]
    lora, n = w.shape
    tm = min(m, 1024)
    hd = HEAD_DIM
    tn = 4 * hd
    kr_blk = (MLA_Q_LORA + MLA_KV_LORA) // hd
    est = 2 * tm * lora * 4 + tm * lora * 2 + 2 * lora * tn * 2 + 4 * tm * tn * 2 + tm * tn * 4 + 8 * tm * hd * 4
    rope_spec = pl.BlockSpec((tm, hd), lambda i, j: (i, 0))
    return pl.pallas_call(
        _mla_k_kernel,
        out_shape=jax.ShapeDtypeStruct((m, 2 * n), BF16),
        grid=(m // tm, n // tn),
        in_specs=_mla_specs(m, tm, lora, 1) + [pl.BlockSpec((lora, tn), lambda i, j: (0, j)),
                                               pl.BlockSpec((tm, hd), lambda i, j: (i, kr_blk)),
                                               rope_spec, rope_spec, rope_spec],
        out_specs=pl.BlockSpec((tm, 2 * tn), lambda i, j: (i, j)),
        scratch_shapes=[pltpu.VMEM((tm, lora), BF16), pltpu.VMEM((tm, hd), BF16)],
        compiler_params=_params(("parallel", "arbitrary"), est),
        name="mla_k",
    )(dn, g.reshape(1, lora), w, dn, cos, sa, sb)


def _router_kernel(x_ref, sh_ref, sc_ref, wr_ref, h_ref, aff_ref, *, row):
    hn = _layer_norm(x_ref[...])
    h = hn * (1.0 + sc_ref[row:row + 1, :]) + sh_ref[row:row + 1, :]
    hb = h.astype(BF16)
    half = h.shape[1] // 2
    bits = pltpu.bitcast(hb.astype(F32), jnp.uint32)
    h_ref[...] = (bits[:, half:] & jnp.uint32(0xFFFF0000)) | (bits[:, :half] >> 16)
    w = wr_ref[...]
    w1 = w.astype(BF16)
    w2 = (w - w1.astype(F32)).astype(BF16)
    h2 = (h - hb.astype(F32)).astype(BF16)
    logits = _dot_nt(w1, hb) + (_dot_nt(w2, hb) + _dot_nt(w1, h2))
    mx = jnp.max(logits, axis=0, keepdims=True)
    p = jnp.exp(logits - mx)
    aff_ref[...] = p / jnp.sum(p, axis=0, keepdims=True)


def moe_router(x, mod, row, k_shift, k_scale, w_router_t):
    m, d = x.shape
    e = w_router_t.shape[0]
    tm = min(m, 512)
    return pl.pallas_call(
        functools.partial(_router_kernel, row=row),
        out_shape=(jax.ShapeDtypeStruct((m, d // 2), jnp.uint32), jax.ShapeDtypeStruct((e, m), F32)),
        grid=(m // tm,),
        in_specs=[pl.BlockSpec((tm, d), lambda i: (i, 0)),
                  pl.BlockSpec((8, d), lambda i: (0, k_shift)),
                  pl.BlockSpec((8, d), lambda i: (0, k_scale)),
                  pl.BlockSpec((e, d), lambda i: (0, 0))],
        out_specs=(pl.BlockSpec((tm, d // 2), lambda i: (i, 0)), pl.BlockSpec((e, tm), lambda i: (0, i))),
        compiler_params=_params(("parallel",), 8 * tm * d * 4),
        name="moe_router",
    )(x, mod, mod, w_router_t)


def _ffn_up_kernel(*refs, n_seg):
    x_refs, (wg_ref, wu_ref) = refs[:n_seg], refs[n_seg:n_seg + 2]
    o_refs, xs_refs = refs[n_seg + 2:2 * n_seg + 2], refs[2 * n_seg + 2:]

    @pl.when(pl.program_id(1) == 0)
    def _():
        for x_ref, xs_ref in zip(x_refs, xs_refs):
            word = x_ref[...]
            first = pltpu.bitcast(word << 16, F32)
            second = pltpu.bitcast(word & jnp.uint32(0xFFFF0000), F32)
            xs_ref[...] = jnp.concatenate([first, second], axis=1).astype(BF16)

    wg = wg_ref[...].astype(BF16)
    wu = wu_ref[...].astype(BF16)
    for xs_ref, o_ref in zip(xs_refs, o_refs):
        x = xs_ref[...]
        g = _dot(x, wg)
        u = _dot(x, wu)
        o_ref[...] = (g * jax.nn.sigmoid(g) * u).astype(o_ref.dtype)


def _ffn_down_kernel(*refs, n_seg):
    h_refs, wd_ref, wt_refs, o_refs = refs[:n_seg], refs[n_seg], refs[n_seg + 1:2 * n_seg + 1], refs[2 * n_seg + 1:]
    wd = wd_ref[...].astype(BF16)
    for h_ref, wt_ref, o_ref in zip(h_refs, wt_refs, o_refs):
        y = _dot(h_ref[...], wd) * wt_ref[...]
        hi = y.astype(BF16)
        o_ref[0] = hi
        o_ref[1] = (y - hi.astype(F32)).astype(BF16)


def expert_ffn(xgs, wts, w_gate, w_up, w_down, layer):
    n_seg = len(xgs)
    e = xgs[0].shape[0]
    d = 2 * xgs[0].shape[2]
    f = w_gate.shape[3]
    rs = [x.shape[1] for x in xgs]
    r = sum(rs)
    tf = min(f, 256)
    est = 2 * (r * d * 2 + 2 * d * tf * 4 + r * tf * 2) + r * d * 2 + 2 * d * tf * 2 + 3 * r * tf * 4
    hids = pl.pallas_call(
        functools.partial(_ffn_up_kernel, n_seg=n_seg),
        out_shape=[jax.ShapeDtypeStruct((e, ri, f), BF16) for ri in rs],
        grid=(e, f // tf),
        in_specs=[pl.BlockSpec((None, ri, d // 2), lambda i, j: (i, 0, 0)) for ri in rs]
        + [pl.BlockSpec((None, None, d, tf), lambda i, j: (layer, i, 0, j))] * 2,
        out_specs=[pl.BlockSpec((None, ri, tf), lambda i, j: (i, 0, j)) for ri in rs],
        scratch_shapes=[pltpu.VMEM((ri, d), BF16) for ri in rs],
        compiler_params=_params(("parallel", "arbitrary"), est),
        name="ffn_up",
    )(*xgs, w_gate, w_up)
    tn = min(d, 512)
    est = 2 * (r * f * 2 + f * tn * 4 + r * tn * 4 + r * LANES * 4) + f * tn * 2 + 2 * r * tn * 4
    return pl.pallas_call(
        functools.partial(_ffn_down_kernel, n_seg=n_seg),
        out_shape=[jax.ShapeDtypeStruct((e, 2, ri, d), BF16) for ri in rs],
        grid=(e, d // tn),
        in_specs=[pl.BlockSpec((None, ri, f), lambda i, j: (i, 0, 0)) for ri in rs]
        + [pl.BlockSpec((None, None, f, tn), lambda i, j: (layer, i, 0, j))]
        + [pl.BlockSpec((None, ri, 1), lambda i, j: (i, 0, 0)) for ri in rs],
        out_specs=[pl.BlockSpec((None, 2, ri, tn), lambda i, j: (i, 0, 0, j)) for ri in rs],
        compiler_params=_params(("parallel", "arbitrary"), est),
        name="ffn_down",
    )(*hids, w_down, *wts)


def _select_kernel(aff_ref, pos_ref, lor_ref, cnt_ref, off_ref, ps_ref, *, cap):
    e, g, ln = aff_ref.shape
    bits = pltpu.bitcast(aff_ref[...], jnp.int32)

    def count(mask):
        per_lane = jnp.sum(jnp.where(mask, 1.0, 0.0), axis=1)
        return jnp.sum(per_lane, axis=1, keepdims=True)[:, :, None]

    def search(i, t):
        cand = t | jnp.left_shift(jnp.int32(1), 30 - i)
        return jnp.where(count(bits >= cand) >= cap, cand, t)

    thr = lax.fori_loop(0, 31, search, jnp.zeros((e, 1, 1), jnp.int32))
    gt = bits > thr
    eq = bits == thr
    need = cap - count(gt)

    r0 = lax.broadcasted_iota(jnp.int32, (ln, ln), 0)
    r1 = lax.broadcasted_iota(jnp.int32, (ln, ln), 1)
    upper = jnp.where(r0 <= r1, 1.0, 0.0).astype(BF16)
    ones = jnp.ones((ln, ln), BF16)
    g0 = lax.broadcasted_iota(jnp.int32, (g, g), 0)
    g1 = lax.broadcasted_iota(jnp.int32, (g, g), 1)
    earlier = jnp.where(g1 < g0, 1.0, 0.0).astype(BF16)

    def prefix(mask):
        x = jnp.where(mask, 1.0, 0.0).astype(BF16).reshape(e * g, ln)
        incl = _dot(x, upper).reshape(e, g, ln)
        tot = _dot(x, ones).reshape(e, g, ln)
        off = jnp.stack([_dot(earlier, tot[i].astype(BF16)) for i in range(e)])
        return incl, tot, off

    incl_eq, _, off_eq = prefix(eq)
    sel = gt | (eq & (off_eq + incl_eq - 1.0 < need))
    incl, tot, off = prefix(sel)
    pos_ref[...] = jnp.where(sel, off + incl - 1.0, -1.0).astype(jnp.int32)
    cnt_ref[...] = tot.astype(jnp.int32)
    off_ref[...] = off.astype(jnp.int32)
    ps_ref[...] = incl

    rank = lax.broadcasted_iota(jnp.int32, (1, ln, 1), 1).astype(F32)
    ones8 = jnp.ones((8, ln), BF16)

    def lane_of_rank(i, c):
        below = jnp.where(ps_ref[i][:, None, :] <= rank, 1.0, 0.0).astype(BF16).reshape(g * ln, ln)
        lor_ref[i] = _dot_nt(ones8, below)[0:1]
        return c

    lax.fori_loop(0, e, lane_of_rank, 0)


def moe_select(aff_t, cap):
    e, n = aff_t.shape
    g = n // LANES
    g_pad = -(-g // 16) * 16
    a = aff_t.reshape(e, g, LANES)
    if g_pad != g:
        a = jnp.concatenate([a, jnp.full((e, g_pad - g, LANES), -1.0, F32)], axis=1)
    shp = jax.ShapeDtypeStruct((e, g_pad, LANES), jnp.int32)
    full = pl.BlockSpec((e, g_pad, LANES), lambda i: (0, 0, 0))
    pos, lor, cnt, off = pl.pallas_call(
        functools.partial(_select_kernel, cap=cap),
        out_shape=(shp, jax.ShapeDtypeStruct((e, 1, g_pad * LANES), F32), shp, shp),
        grid=(1,),
        in_specs=[full],
        out_specs=(full, pl.BlockSpec((e, 1, g_pad * LANES), lambda i: (0, 0, 0)), full, full),
        scratch_shapes=[pltpu.VMEM((e, g_pad, LANES), F32)],
        compiler_params=_params(("arbitrary",), 24 * e * g_pad * LANES * 4 + 4 * g_pad * LANES * LANES * 4),
        name="moe_select",
    )(a)
    return (pos.reshape(e, g_pad * LANES)[:, :n], lor.astype(jnp.int32), cnt[:, :, 0].reshape(e, 1, g_pad),
            off[:, :g, 0])


def _row_copy(h_hbm, x_ref, sem, token, row):
    return pltpu.make_async_copy(h_hbm.at[pl.ds(token, 1), :], x_ref.at[0, pl.ds(row, 1), :], sem)


def _gather_kernel(lor_ref, cnt_ref, aff_ref, h_hbm, x_ref, wt_ref, sem):
    cap = x_ref.shape[1]

    def group(g, slot):
        def row(r, slot):
            token = g * LANES + lor_ref[0, g * LANES + r]
            wt_ref[0, slot] = aff_ref[0, token]
            _row_copy(h_hbm, x_ref, sem, token, slot).start()
            return slot + 1

        return lax.fori_loop(0, cnt_ref[0, g], row, slot)

    lax.fori_loop(0, cnt_ref.shape[1], group, 0)
    pltpu.make_async_copy(h_hbm.at[pl.ds(0, cap), :], x_ref.at[0], sem).wait()


def moe_gather(hp, aff_t, lor, cnt, cap):
    n, w = hp.shape
    e = lor.shape[0]

    def smem(width):
        return pl.BlockSpec((None, 1, width), lambda i: (i, 0, 0), memory_space=pltpu.SMEM)

    x, wt = pl.pallas_call(
        _gather_kernel,
        out_shape=(jax.ShapeDtypeStruct((e, cap, w), jnp.uint32), jax.ShapeDtypeStruct((e, 1, cap), F32)),
        grid=(e,),
        in_specs=[smem(lor.shape[2]), smem(cnt.shape[2]), smem(n), pl.BlockSpec(memory_space=pl.ANY)],
        out_specs=(pl.BlockSpec((1, cap, w), lambda i: (i, 0, 0)), smem(cap)),
        scratch_shapes=[pltpu.SemaphoreType.DMA(())],
        compiler_params=_params(("arbitrary",), 2 * cap * w * 4),
        name="moe_gather",
    )(lor, cnt, aff_t.reshape(e, 1, n), hp)
    return x, wt.reshape(e, cap, 1)


def _window_copy(y_hbm, dst, sem, e, src, win):
    return pltpu.make_async_copy(y_hbm.at[e, :, pl.ds(src, win)], dst, sem)


def _combine_kernel(offb_ref, y_hbm, pos_ref, x_ref, gate_ref, g_ref, b_ref, o_ref, ybuf, ybuf_x, sem, acc_ref,
                    *, row, cap, win):
    b = pl.program_id(0)
    n_exp, tb = pos_ref.shape
    half = n_exp // 2
    par = b % 2

    def window(e, k, blk=b):
        first = (offb_ref[e, blk] // 8) * 8 + k * win
        return first, pl.multiple_of(jnp.minimum(first, cap - win), 8)

    def onehot2(e, first, src):
        slots = src + lax.broadcasted_iota(jnp.int32, (win, 1), 0)
        hit = jnp.logical_and(pos_ref[e:e + 1, :] == slots, slots >= first)
        oh = jnp.where(hit, 1.0, 0.0).astype(BF16)
        return jnp.concatenate([oh, oh], axis=0)

    def copies(h, blk, p):
        return [_window_copy(y_hbm, ybuf.at[p, h, j], sem.at[2 * p + h], h * half + j,
                             window(h * half + j, 0, blk)[1], win) for j in range(half)]

    def start_block(blk, p):
        for h in range(2):
            for cp in copies(h, blk, p):
                cp.start()

    @pl.when(b == 0)
    def _():
        start_block(b, par)

    @pl.when(b + 1 < pl.num_programs(0))
    def _():
        start_block(b + 1, 1 - par)

    acc = None
    for h in range(2):
        for cp in copies(h, b, par):
            cp.wait()
        lhs = jnp.concatenate([onehot2(h * half + j, *window(h * half + j, 0)) for j in range(half)], axis=0)
        part = _dot_tn(lhs, ybuf[par, h].reshape(half * 2 * win, ybuf.shape[-1]))
        acc = part if acc is None else acc + part
    acc_ref[...] = acc

    for e in range(n_exp):
        n_win = (offb_ref[e, b + 1] - (offb_ref[e, b] // 8) * 8 + win - 1) // win

        def extra(k, c, e=e):
            first_k, src_k = window(e, k)
            cp = _window_copy(y_hbm, ybuf_x, sem.at[4], e, src_k, win)
            cp.start()
            cp.wait()
            acc_ref[...] += _dot_tn(onehot2(e, first_k, src_k), ybuf_x[...].reshape(2 * win, ybuf_x.shape[-1]))
            return c

        lax.fori_loop(1, n_win, extra, 0)

    z = DEEPNORM_ALPHA * x_ref[...] + gate_ref[row:row + 1, :] * acc_ref[...]
    o_ref[...] = _layer_norm(z) * g_ref[...] + b_ref[...]


def moe_combine_postnorm(y, pos, off, x, mod, row, k_gate, g, b):
    n, d = x.shape
    e, _, cap, _ = y.shape
    tb = min(n, 256)
    nb = n // tb
    win = min(cap, COMBINE_WINDOW)
    offb = jnp.concatenate([off[:, ::tb // LANES], jnp.full((e, 1), cap, jnp.int32)], axis=1)
    grid_spec = pltpu.PrefetchScalarGridSpec(
        num_scalar_prefetch=1,
        grid=(nb,),
        in_specs=[pl.BlockSpec(memory_space=pl.ANY),
                  pl.BlockSpec((e, tb), lambda i, o: (0, i)),
                  pl.BlockSpec((tb, d), lambda i, o: (i, 0)),
                  pl.BlockSpec((8, d), lambda i, o: (0, k_gate)),
                  pl.BlockSpec((1, d), lambda i, o: (0, 0)),
                  pl.BlockSpec((1, d), lambda i, o: (0, 0))],
        out_specs=pl.BlockSpec((tb, d), lambda i, o: (i, 0)),
        scratch_shapes=[pltpu.VMEM((2, 2, e // 2, 2, win, d), BF16), pltpu.VMEM((2, win, d), BF16),
                        pltpu.SemaphoreType.DMA((5,)), pltpu.VMEM((tb, d), F32)],
    )
    est = 8 * tb * d * 4 + (2 * e + 1) * 2 * win * d * 2 + e * win * tb * 2
    return pl.pallas_call(
        functools.partial(_combine_kernel, row=row, cap=cap, win=win),
        out_shape=jax.ShapeDtypeStruct((n, d), F32),
        grid_spec=grid_spec,
        compiler_params=_params(("arbitrary",), est),
        name="moe_combine",
    )(offb, y, pos, x, mod, g.reshape(1, d), b.reshape(1, d))


def moe_route(x, mod, row, w_router_t):
    m = x.shape[0]
    cap = max(1, EC_CAPACITY_FACTOR * m // N_EXPERTS)
    h, aff_t = moe_router(x, mod, row, 3, 4, w_router_t)
    pos, lor, cnt, off = moe_select(aff_t, cap)
    xg, wt = moe_gather(h, aff_t, lor, cnt, cap)
    return xg, wt, (pos, off)


def _rope_angles(n_tokens, rot_dim):
    rows = n_tokens // GRID_W
    row = jnp.repeat(jnp.arange(rows, dtype=F32), GRID_W)
    col = jnp.tile(jnp.arange(GRID_W, dtype=F32), rows)
    n_freq = rot_dim // 4
    inv = ROPE_THETA ** (-jnp.arange(n_freq, dtype=F32) / n_freq)
    return jnp.concatenate([row[:, None] * inv, col[:, None] * inv], axis=-1)


def _gqa_rope_tables(n_tokens):
    ang = _rope_angles(n_tokens, HEAD_DIM)
    c, s = jnp.cos(ang), jnp.sin(ang)
    return jnp.concatenate([c, c], axis=-1), jnp.concatenate([-s, s], axis=-1)


def _mla_rope_tables(n_tokens):
    ang = _rope_angles(n_tokens, MLA_ROPE)
    c, s = jnp.cos(ang), jnp.sin(ang)
    z = jnp.zeros_like(c)
    cos = jnp.concatenate([c, c, z, z], axis=-1)
    sa = jnp.concatenate([-s, z, z, z], axis=-1)
    sb = jnp.concatenate([z, s, z, z], axis=-1)
    return cos, sa, sb


def kernel(x, c, ctx, c_ctx, ada_w, ada_b, ln_g, ln_b, ev_w_in, ev_w_out, hgrn_lb, hgrn_norm_g, gqa_q_norm_g, gqa_k_norm_g, mla_w_down, mla_q_norm_g, mla_kv_norm_g, mla_w_uq, mla_w_ukv, mla_w_o, moe_router, moe_w_gate, moe_w_up, moe_w_down):
    d = D_MODEL
    xl = x[0]
    xc = ctx[0]
    n_lat, n_ctx = xl.shape[0], xc.shape[0]
    cc = jnp.zeros((8, d), F32).at[0].set(c[0]).at[1].set(c_ctx)
    lb_all = jnp.cumsum(jax.nn.softmax(hgrn_lb.astype(F32), axis=1), axis=1)
    gqa_tabs = _gqa_rope_tables(n_lat)
    gqa_tabs_ctx = [jnp.ones((n_ctx, HEAD_DIM), F32), jnp.zeros((n_ctx, HEAD_DIM), F32)]
    mla_tabs = _mla_rope_tables(n_lat)
    mla_tabs_ctx = [jnp.ones((n_ctx, HEAD_DIM), F32), jnp.zeros((n_ctx, HEAD_DIM), F32), jnp.zeros((n_ctx, HEAD_DIM), F32)]
    LAT, CTX = 0, 1

    for l in range(DEPTH):
        last = l == DEPTH - 1
        i = l // 2
        mod = adaln(cc, ada_w[l], ada_b[l])
        if l % 2 == 0:
            w_in = ev_w_in[i].astype(BF16)
            w_out = ev_w_out[i].astype(BF16)
            lb = lb_all[:, l].reshape(2, 1, A_WIDTH)
            scale = HEAD_DIM ** -0.5 * LOG2E
            proj_c = lnmod_matmul(xc, mod, CTX, 0, 1, w_in, 512)
            proj_l = lnmod_matmul(xl, mod, LAT, 0, 1, w_in, 512)
            s0 = jnp.zeros((2, A_HEADS, HEAD_DIM, HEAD_DIM), F32)
            o_c, s_c = hgrn_scan(proj_c, lb, s0)
            o_l, _ = hgrn_scan(proj_l, lb, s_c)
            a_l = hgrn_out(o_l, proj_l, hgrn_norm_g[i])
            qcol, kcol, vcol = 5 * A_WIDTH, 5 * A_WIDTH + B_WIDTH, 5 * A_WIDTH + B_WIDTH + B_KV_WIDTH
            q_l = norm_rope(proj_l, qcol, B_Q_HEADS, gqa_q_norm_g[i], *gqa_tabs, scale)
            k_l = norm_rope(proj_l, kcol, B_KV_HEADS, gqa_k_norm_g[i], *gqa_tabs, 1.0)
            k_c = norm_rope(proj_c, kcol, B_KV_HEADS, gqa_k_norm_g[i], *gqa_tabs_ctx, 1.0)
            v_l = proj_l[:, vcol:].astype(BF16)
            v_c = proj_c[:, vcol:].astype(BF16)
            att = dict(n_heads=B_Q_HEADS, n_kv_heads=B_KV_HEADS, dq=HEAD_DIM, dv=HEAD_DIM)
            b_l = flash_attention(q_l, k_l, v_l, k_c, v_c, **att)
            w_parts = [w_out[:A_WIDTH], w_out[A_WIDTH:]]
            xl_new = proj_postnorm([a_l, b_l], w_parts, xl, mod, LAT, 2, ln_g[l, 0], ln_b[l, 0])
            if not last:
                a_c = hgrn_out(o_c, proj_c, hgrn_norm_g[i])
                q_c = norm_rope(proj_c, qcol, B_Q_HEADS, gqa_q_norm_g[i], *gqa_tabs_ctx, scale)
                b_c = flash_attention(q_c, k_c, v_c, **att)
                xc = proj_postnorm([a_c, b_c], w_parts, xc, mod, CTX, 2, ln_g[l, 0], ln_b[l, 0])
            xl = xl_new
        else:
            hd = HEAD_DIM
            pad = (-mla_w_down.shape[2]) % hd
            w_down = jnp.pad(mla_w_down[i], ((0, 0), (0, pad))).astype(BF16)
            w_uq = mla_w_uq[i].reshape(MLA_Q_LORA, MLA_HEADS, hd + MLA_ROPE)
            w_uq = jnp.pad(w_uq, ((0, 0), (0, 0), (0, MLA_QK - hd - MLA_ROPE))).reshape(MLA_Q_LORA, MLA_HEADS * MLA_QK).astype(BF16)
            w_ukv = mla_w_ukv[i].reshape(MLA_KV_LORA, MLA_HEADS, 2 * hd)
            w_uk = w_ukv[:, :, :hd].reshape(MLA_KV_LORA, MLA_HEADS * hd).astype(BF16)
            w_uv = w_ukv[:, :, hd:].reshape(MLA_KV_LORA, MLA_HEADS * hd).astype(BF16)
            w_o = mla_w_o[i].astype(BF16)
            scale = (hd + MLA_ROPE) ** -0.5 * LOG2E
            dn_c = lnmod_matmul(xc, mod, CTX, 0, 1, w_down, w_down.shape[1])
            dn_l = lnmod_matmul(xl, mod, LAT, 0, 1, w_down, w_down.shape[1])
            q_l = mla_q(dn_l, mla_q_norm_g[i], w_uq, *mla_tabs, scale)
            k_l = mla_k(dn_l, mla_kv_norm_g[i], w_uk, *mla_tabs)
            k_c = mla_k(dn_c, mla_kv_norm_g[i], w_uk, *mla_tabs_ctx)
            v_l = rms_matmul(dn_l, 1, mla_kv_norm_g[i], w_uv, 1024)
            v_c = rms_matmul(dn_c, 1, mla_kv_norm_g[i], w_uv, 1024)
            att = dict(n_heads=MLA_HEADS, n_kv_heads=MLA_HEADS, dq=MLA_QK, dv=hd)
            o_l = flash_attention(q_l, k_l, v_l, k_c, v_c, **att)
            xl_new = proj_postnorm([o_l], [w_o], xl, mod, LAT, 2, ln_g[l, 0], ln_b[l, 0])
            if not last:
                q_c = mla_q(dn_c, mla_q_norm_g[i], w_uq, *mla_tabs_ctx, scale)
                o_c = flash_attention(q_c, k_c, v_c, **att)
                xc = proj_postnorm([o_c], [w_o], xc, mod, CTX, 2, ln_g[l, 0], ln_b[l, 0])
            xl = xl_new

        w_router_t = moe_router[l].T
        segs = [(xl, LAT)] if last else [(xl, LAT), (xc, CTX)]
        routes = [moe_route(xs, mod, row, w_router_t) for xs, row in segs]
        ys = expert_ffn([r[0] for r in routes], [r[1] for r in routes], moe_w_gate, moe_w_up, moe_w_down, l)
        outs = [moe_combine_postnorm(y, *r[2], xs, mod, row, 5, ln_g[l, 1], ln_b[l, 1])
                for y, r, (xs, row) in zip(ys, routes, segs)]
        xl = outs[0]
        if not last:
            xc = outs[1]
    return xl[None]
```

```python
import functools
import math

import numpy as np
import jax
import jax.numpy as jnp
from jax import lax
from jax.experimental import pallas as pl
from jax.experimental.pallas import tpu as pltpu

F32 = jnp.float32
BF16 = jnp.bfloat16

D_MODEL = 2048
DEPTH = 2
GRID_W = 64
HEAD_DIM = 128
A_HEADS = D_MODEL // 256
A_WIDTH = A_HEADS * HEAD_DIM
B_Q_HEADS = D_MODEL // 256
B_KV_HEADS = 2
B_WIDTH = B_Q_HEADS * HEAD_DIM
B_KV_WIDTH = B_KV_HEADS * HEAD_DIM
MLA_HEADS = D_MODEL // 128
MLA_Q_LORA = 512
MLA_KV_LORA = 512
MLA_ROPE = 64
MLA_QK = 2 * HEAD_DIM
N_EXPERTS = 16
EXPERT_FF = D_MODEL // 2
EC_CAPACITY_FACTOR = 2
ROPE_THETA = 10000.0
NORM_EPS = 1e-6
DEEPNORM_ALPHA = (2.0 * DEPTH) ** 0.25

HGRN_CHUNK = 128
HGRN_SMALL_LEVELS = 3
COMBINE_WINDOW = 64
LANES = 128
LOG2E = math.log2(math.e)
V7X_VMEM_BYTES = 64 * 1024 * 1024
VMEM_CAP_BYTES = V7X_VMEM_BYTES - 8 * 1024 * 1024


def _params(semantics, vmem_estimate_bytes):
    limit = int(min(max(2 * vmem_estimate_bytes, 32 * 1024 * 1024), VMEM_CAP_BYTES))
    return pltpu.CompilerParams(dimension_semantics=semantics, vmem_limit_bytes=limit)


def _layer_norm(x):
    mu = jnp.mean(x, axis=-1, keepdims=True)
    xc = x - mu
    var = jnp.mean(xc * xc, axis=-1, keepdims=True)
    return xc * lax.rsqrt(var + NORM_EPS)


def _rms(x):
    return x * lax.rsqrt(jnp.mean(x * x, axis=-1, keepdims=True) + NORM_EPS)


def _dot(a, b):
    return jnp.dot(a, b, preferred_element_type=F32)


def _dot_nt(a, b):
    return lax.dot_general(a, b, (((1,), (1,)), ((), ())), preferred_element_type=F32)


def _dot_tn(a, b):
    return lax.dot_general(a, b, (((0,), (0,)), ((), ())), preferred_element_type=F32)


def _split3(x):
    x1 = x.astype(BF16)
    r1 = x - x1.astype(F32)
    x2 = r1.astype(BF16)
    x3 = (r1 - x2.astype(F32)).astype(BF16)
    return x1, x2, x3


def _adaln_kernel(c_ref, w_ref, b_ref, o_ref):
    c = c_ref[...]
    s = c * jax.nn.sigmoid(c)
    w = w_ref[...]
    s1, s2, s3 = _split3(s)
    w1, w2, w3 = _split3(w)
    acc = _dot(s1, w3) + _dot(s3, w1) + _dot(s2, w2)
    acc = acc + _dot(s1, w2) + _dot(s2, w1)
    acc = acc + _dot(s1, w1)
    o_ref[...] = acc + b_ref[...]


def adaln(cc, w, b):
    d, n = w.shape
    tn = 1536 if n % 1536 == 0 else n
    est = 2 * d * tn * 4 * 2
    return pl.pallas_call(
        _adaln_kernel,
        out_shape=jax.ShapeDtypeStruct((8, n), F32),
        grid=(n // tn,),
        in_specs=[pl.BlockSpec((8, d), lambda j: (0, 0)),
                  pl.BlockSpec((d, tn), lambda j: (0, j)),
                  pl.BlockSpec((1, tn), lambda j: (0, j))],
        out_specs=pl.BlockSpec((8, tn), lambda j: (0, j)),
        compiler_params=_params(("parallel",), est),
        name="adaln",
    )(cc, w, b.reshape(1, n))


def _lnmod_mm_kernel(x_ref, sh_ref, sc_ref, w_ref, o_ref, h_ref, *, row):
    @pl.when(pl.program_id(1) == 0)
    def _():
        hn = _layer_norm(x_ref[...])
        h = hn * (1.0 + sc_ref[row:row + 1, :]) + sh_ref[row:row + 1, :]
        h_ref[...] = h.astype(BF16)

    o_ref[...] = _dot(h_ref[...], w_ref[...]).astype(o_ref.dtype)


def lnmod_matmul(x, mod, row, k_shift, k_scale, w, tn):
    m, d = x.shape
    n = w.shape[1]
    tm = min(m, 1024)
    est = 2 * tm * d * 4 + tm * d * 2 + 2 * d * tn * 2 + 2 * tm * tn * 4
    return pl.pallas_call(
        functools.partial(_lnmod_mm_kernel, row=row),
        out_shape=jax.ShapeDtypeStruct((m, n), F32),
        grid=(m // tm, n // tn),
        in_specs=[pl.BlockSpec((tm, d), lambda i, j: (i, 0)),
                  pl.BlockSpec((8, d), lambda i, j: (0, k_shift)),
                  pl.BlockSpec((8, d), lambda i, j: (0, k_scale)),
                  pl.BlockSpec((d, tn), lambda i, j: (0, j))],
        out_specs=pl.BlockSpec((tm, tn), lambda i, j: (i, j)),
        scratch_shapes=[pltpu.VMEM((tm, d), BF16)],
        compiler_params=_params(("parallel", "arbitrary"), est),
        name="lnmod_matmul",
    )(x, mod, mod, w)


def _hgrn_tables(c):
    n_lvl = int(math.log2(c))
    r = np.arange(c)
    u = np.arange(c)[None, :]
    blocks, masks = [], []
    for l in range(n_lvl):
        half = 1 << l
        base = (r // (2 * half)) * (2 * half)
        anchor = (base + half - 1)[:, None]
        upper = (r >= base + half)[:, None]
        rr = r[:, None]
        if l < HGRN_SMALL_LEVELS:
            blocks.append(np.where(upper, (u > anchor) & (u <= rr), (u > rr) & (u <= anchor)))
        same = (r[:, None] // (2 * half)) == (r[None, :] // (2 * half))
        masks.append(same & upper & ~(upper.T))
    blocks.append(u <= r[:, None])
    blocks.append(np.ones((16, c), bool))
    masks.append(np.eye(c, dtype=bool))
    fwd_s = np.concatenate(blocks, axis=0).astype(np.float32)
    fwd_m = np.stack(masks).astype(np.float32)
    bwd_s = np.concatenate([b[::-1, ::-1] for b in blocks], axis=0).astype(np.float32)
    bwd_m = fwd_m[:, ::-1, ::-1]
    return (jnp.asarray(np.stack([fwd_s, bwd_s]), BF16), jnp.asarray(np.stack([fwd_m, bwd_m]), F32))


def _hgrn_kernel(q_ref, v_ref, f_ref, lb_ref, sums_ref, mask_ref, s0_ref, o_ref, sfin_ref, st_ref):
    c = q_ref.shape[0]
    hd = HEAD_DIM
    n_lvl = mask_ref.shape[0] - 1
    n_small = HGRN_SMALL_LEVELS
    forward = pl.program_id(0) == 0
    j = pl.program_id(1)

    @pl.when(j == 0)
    def _():
        st_ref[...] = s0_ref[...]

    def wide_level(cum, l):
        half = 1 << l
        parts = []
        for base in range(0, c, 2 * half):
            a = base + half - 1
            mid = jnp.where(forward, cum[a:a + 1, :], cum[a + 1:a + 2, :])
            parts.append(jnp.broadcast_to(mid, (2 * half, hd)))
        anchor = parts[0] if len(parts) == 1 else jnp.concatenate(parts, axis=0)
        return -jnp.abs(cum - anchor)

    for h in range(q_ref.shape[1] // hd):
        cols = slice(h * hd, (h + 1) * hd)
        q = q_ref[:, cols]
        vb = v_ref[:, cols].astype(BF16)
        lb = lb_ref[:, cols]
        f = lb + (1.0 - lb) * jax.nn.sigmoid(f_ref[:, cols])
        g = jnp.log(f)
        k = 1.0 - f
        g1 = g.astype(BF16)
        g2 = (g - g1.astype(F32)).astype(BF16)
        e2 = _dot(sums_ref[...], jnp.concatenate([g1, g2], axis=1))
        e = e2[:, hd:] + e2[:, :hd]
        cum = e[n_small * c:(n_small + 1) * c]
        tot = e[(n_small + 1) * c:(n_small + 1) * c + 1]
        rem = tot - cum

        scores = _dot_nt(q.astype(BF16), k.astype(BF16)) * mask_ref[n_lvl]
        for l in range(n_lvl):
            z = jnp.exp(e[l * c:(l + 1) * c] if l < n_small else wide_level(cum, l))
            scores = scores + _dot_nt((q * z).astype(BF16), (k * z).astype(BF16)) * mask_ref[l]

        st = st_ref[h]
        o = _dot(scores.astype(BF16), vb) + _dot_nt((q * jnp.exp(cum)).astype(BF16), st.astype(BF16))
        o_ref[:, cols] = o
        st_new = st * jnp.exp(tot) + _dot_tn(vb, (k * jnp.exp(rem)).astype(BF16))
        st_ref[h] = st_new

    @pl.when(j == pl.num_programs(1) - 1)
    def _():
        sfin_ref[...] = st_ref[...]


def hgrn_scan(proj, lb, s0):
    seq = proj.shape[0]
    c = HGRN_CHUNK
    nc = seq // c
    sums, masks = _hgrn_tables(c)
    hd, w = HEAD_DIM, A_WIDTH

    def blk(d, j):
        return jnp.where(d == 0, j, nc - 1 - j)

    est = (2 * (4 * c * w * 4 + sums.shape[1] * c * 2 + masks.shape[1] * c * c * 4 + 2 * A_HEADS * hd * hd * 4)
           + A_HEADS * hd * hd * 4)
    return pl.pallas_call(
        _hgrn_kernel,
        out_shape=(jax.ShapeDtypeStruct((2, seq, w), F32),
                   jax.ShapeDtypeStruct((2, A_HEADS, hd, hd), F32)),
        grid=(2, nc),
        in_specs=[pl.BlockSpec((c, w), lambda d, j: (blk(d, j), 0)),
                  pl.BlockSpec((c, w), lambda d, j: (blk(d, j), 3)),
                  pl.BlockSpec((c, w), lambda d, j: (blk(d, j), 1 + d)),
                  pl.BlockSpec((None, 1, w), lambda d, j: (d, 0, 0)),
                  pl.BlockSpec((None, sums.shape[1], c), lambda d, j: (d, 0, 0)),
                  pl.BlockSpec((None, masks.shape[1], c, c), lambda d, j: (d, 0, 0, 0)),
                  pl.BlockSpec((None, A_HEADS, hd, hd), lambda d, j: (d, 0, 0, 0))],
        out_specs=(pl.BlockSpec((None, c, w), lambda d, j: (d, blk(d, j), 0)),
                   pl.BlockSpec((None, A_HEADS, hd, hd), lambda d, j: (d, 0, 0, 0))),
        scratch_shapes=[pltpu.VMEM((A_HEADS, hd, hd), F32)],
        compiler_params=_params(("parallel", "arbitrary"), est),
        name="hgrn_scan",
    )(proj, proj, proj, lb, sums, masks, s0)


def _hgrn_out_kernel(o_ref, gate_ref, g_ref, a_ref):
    hd = HEAD_DIM
    for h in range(a_ref.shape[1] // hd):
        cols = slice(h * hd, (h + 1) * hd)
        o = o_ref[0, :, cols] + o_ref[1, :, cols]
        gate = gate_ref[:, cols]
        a_ref[:, cols] = (_rms(o) * g_ref[...] * (gate * jax.nn.sigmoid(gate))).astype(a_ref.dtype)


def hgrn_out(o, proj, norm_g):
    seq = o.shape[1]
    tm = min(seq, 256)
    hd, w = HEAD_DIM, A_WIDTH
    return pl.pallas_call(
        _hgrn_out_kernel,
        out_shape=jax.ShapeDtypeStruct((seq, w), BF16),
        grid=(seq // tm,),
        in_specs=[pl.BlockSpec((2, tm, w), lambda i: (0, i, 0)),
                  pl.BlockSpec((tm, w), lambda i: (i, 4)),
                  pl.BlockSpec((1, hd), lambda i: (0, 0))],
        out_specs=pl.BlockSpec((tm, w), lambda i: (i, 0)),
        compiler_params=_params(("parallel",), 8 * tm * w * 4),
        name="hgrn_out",
    )(o, proj, norm_g.reshape(1, hd))


def _norm_rope_kernel(x_ref, g_ref, cos_ref, sin_ref, o_ref, *, scale):
    hd = HEAD_DIM
    for h in range(o_ref.shape[1] // hd):
        cols = slice(h * hd, (h + 1) * hd)
        y = _rms(x_ref[:, cols]) * g_ref[...]
        y = y * cos_ref[...] + pltpu.roll(y, hd // 2, 1) * sin_ref[...]
        o_ref[:, cols] = (y * scale).astype(o_ref.dtype)


def norm_rope(proj, col0, n_heads, g, cos, sin, scale):
    seq = proj.shape[0]
    tm = min(seq, 256)
    hd = HEAD_DIM
    w = n_heads * hd
    return pl.pallas_call(
        functools.partial(_norm_rope_kernel, scale=scale),
        out_shape=jax.ShapeDtypeStruct((seq, w), BF16),
        grid=(seq // tm,),
        in_specs=[pl.BlockSpec((tm, w), lambda i: (i, col0 // w)),
                  pl.BlockSpec((1, hd), lambda i: (0, 0)),
                  pl.BlockSpec((tm, hd), lambda i: (i, 0)),
                  pl.BlockSpec((tm, hd), lambda i: (i, 0))],
        out_specs=pl.BlockSpec((tm, w), lambda i: (i, 0)),
        compiler_params=_params(("parallel",), 6 * tm * w * 4 + 4 * tm * hd * 4),
        name="norm_rope",
    )(proj, g.reshape(1, hd), cos, sin)


def _flash_update(q, k, v, m_ref, l_ref, acc_ref):
    s = _dot_nt(q, k)
    m_prev = m_ref[...]
    m_new = jnp.maximum(m_prev, jnp.max(s, axis=-1, keepdims=True))
    alpha = jnp.exp2(m_prev - m_new)
    ps = [jnp.exp2(s[:, c * LANES:(c + 1) * LANES] - m_new) for c in range(s.shape[1] // LANES)]
    psum = ps[0]
    for pc in ps[1:]:
        psum = psum + pc
    p = jnp.concatenate([pc.astype(BF16) for pc in ps], axis=1)
    l_ref[...] = alpha * l_ref[...] + psum
    acc_ref[...] = alpha * acc_ref[...] + _dot(p, v)
    m_ref[...] = m_new


def _flash_kernel(*refs, has_ctx):
    if has_ctx:
        q_ref, k_ref, v_ref, kc_ref, vc_ref, o_ref, m_ref, l_ref, acc_ref = refs
    else:
        q_ref, k_ref, v_ref, o_ref, m_ref, l_ref, acc_ref = refs
    j = pl.program_id(2)

    @pl.when(j == 0)
    def _():
        m_ref[...] = jnp.full(m_ref.shape, -jnp.inf, F32)
        l_ref[...] = jnp.zeros(l_ref.shape, F32)
        acc_ref[...] = jnp.zeros(acc_ref.shape, F32)
        if has_ctx:
            _flash_update(q_ref[...], kc_ref[...], vc_ref[...], m_ref, l_ref, acc_ref)

    _flash_update(q_ref[...], k_ref[...], v_ref[...], m_ref, l_ref, acc_ref)

    @pl.when(j == pl.num_programs(2) - 1)
    def _():
        l = jnp.sum(l_ref[...], axis=-1, keepdims=True)
        o_ref[...] = (acc_ref[...] / l).astype(o_ref.dtype)


def flash_attention(q, k, v, k_ctx=None, v_ctx=None, *, n_heads, n_kv_heads, dq, dv):
    n, m = q.shape[0], k.shape[0]
    grp = n_heads // n_kv_heads
    tq = min(n, 2048)
    tk = min(m, 2048)
    has_ctx = k_ctx is not None
    in_specs = [pl.BlockSpec((tq, dq), lambda h, i, j: (i, h)),
                pl.BlockSpec((tk, dq), lambda h, i, j: (j, h // grp)),
                pl.BlockSpec((tk, dv), lambda h, i, j: (j, h // grp))]
    args = [q, k, v]
    if has_ctx:
        mc = k_ctx.shape[0]
        in_specs += [pl.BlockSpec((mc, dq), lambda h, i, j: (0, h // grp)),
                     pl.BlockSpec((mc, dv), lambda h, i, j: (0, h // grp))]
        args += [k_ctx, v_ctx]
    est = 2 * (tq * dq + tk * dq + tk * dv + tq * dv) * 2 + tq * (dv + 256) * 4 + 6 * tq * tk * 4
    return pl.pallas_call(
        functools.partial(_flash_kernel, has_ctx=has_ctx),
        out_shape=jax.ShapeDtypeStruct((n, n_heads * dv), BF16),
        grid=(n_heads, n // tq, m // tk),
        in_specs=in_specs,
        out_specs=pl.BlockSpec((tq, dv), lambda h, i, j: (i, h)),
        scratch_shapes=[pltpu.VMEM((tq, LANES), F32), pltpu.VMEM((tq, LANES), F32), pltpu.VMEM((tq, dv), F32)],
        compiler_params=_params(("parallel", "parallel", "arbitrary"), est),
        name="flash_attention",
    )(*args)


def _proj_postnorm_kernel(*refs, n_in, row):
    a_refs = refs[:n_in]
    w_refs = refs[n_in:2 * n_in]
    x_ref, gate_ref, g_ref, b_ref, o_ref = refs[2 * n_in:]
    y = _dot(a_refs[0][...], w_refs[0][...])
    for a_ref, w_ref in zip(a_refs[1:], w_refs[1:]):
        y = y + _dot(a_ref[...], w_ref[...])
    z = DEEPNORM_ALPHA * x_ref[...] + gate_ref[row:row + 1, :] * y
    o_ref[...] = _layer_norm(z) * g_ref[...] + b_ref[...]


def proj_postnorm(acts, ws, x, mod, row, k_gate, g, b):
    m, d = x.shape
    tm = min(m, 512)
    n_in = len(acts)
    once = pl.Buffered(1)
    in_specs = [pl.BlockSpec((tm, a.shape[1]), lambda i: (i, 0)) for a in acts]
    in_specs += [pl.BlockSpec(w.shape, lambda i: (0, 0), pipeline_mode=once) for w in ws]
    in_specs += [pl.BlockSpec((tm, d), lambda i: (i, 0)),
                 pl.BlockSpec((8, d), lambda i: (0, k_gate)),
                 pl.BlockSpec((1, d), lambda i: (0, 0)),
                 pl.BlockSpec((1, d), lambda i: (0, 0))]
    est = sum(w.size * 2 for w in ws) + sum(2 * tm * a.shape[1] * 2 for a in acts) + 6 * tm * d * 4
    return pl.pallas_call(
        functools.partial(_proj_postnorm_kernel, n_in=n_in, row=row),
        out_shape=jax.ShapeDtypeStruct((m, d), F32),
        grid=(m // tm,),
        in_specs=in_specs,
        out_specs=pl.BlockSpec((tm, d), lambda i: (i, 0)),
        compiler_params=_params(("parallel",), est),
        name="proj_postnorm",
    )(*acts, *ws, x, mod, g.reshape(1, d), b.reshape(1, d))


def _add_postnorm_kernel(y_ref, x_ref, gate_ref, g_ref, b_ref, o_ref, *, row):
    z = DEEPNORM_ALPHA * x_ref[...] + gate_ref[row:row + 1, :] * y_ref[...]
    o_ref[...] = _layer_norm(z) * g_ref[...] + b_ref[...]


def add_postnorm(y, x, mod, row, k_gate, g, b):
    m, d = x.shape
    tm = min(m, 512)
    return pl.pallas_call(
        functools.partial(_add_postnorm_kernel, row=row),
        out_shape=jax.ShapeDtypeStruct((m, d), F32),
        grid=(m // tm,),
        in_specs=[pl.BlockSpec((tm, d), lambda i: (i, 0)),
                  pl.BlockSpec((tm, d), lambda i: (i, 0)),
                  pl.BlockSpec((8, d), lambda i: (0, k_gate)),
                  pl.BlockSpec((1, d), lambda i: (0, 0)),
                  pl.BlockSpec((1, d), lambda i: (0, 0))],
        out_specs=pl.BlockSpec((tm, d), lambda i: (i, 0)),
        compiler_params=_params(("parallel",), 8 * tm * d * 4),
        name="add_postnorm",
    )(y, x, mod, g.reshape(1, d), b.reshape(1, d))


def _rms_mm_kernel(x_ref, g_ref, w_ref, o_ref, a_ref):
    @pl.when(pl.program_id(1) == 0)
    def _():
        a_ref[...] = (_rms(x_ref[...]) * g_ref[...]).astype(BF16)

    o_ref[...] = _dot(a_ref[...], w_ref[...]).astype(o_ref.dtype)


def _mla_q_kernel(x_ref, g_ref, w_ref, cos_ref, sa_ref, sb_ref, o_ref, a_ref, *, scale):
    @pl.when(pl.program_id(1) == 0)
    def _():
        a_ref[...] = (_rms(x_ref[...]) * g_ref[...]).astype(BF16)

    y = _dot(a_ref[...], w_ref[...])
    hd = HEAD_DIM
    for h in range(y.shape[1] // MLA_QK):
        c0 = h * MLA_QK
        o_ref[:, c0:c0 + hd] = (y[:, c0:c0 + hd] * scale).astype(o_ref.dtype)
        r = y[:, c0 + hd:c0 + 2 * hd]
        r = r * cos_ref[...] + pltpu.roll(r, hd - MLA_ROPE // 2, 1) * sa_ref[...] + pltpu.roll(r, MLA_ROPE // 2, 1) * sb_ref[...]
        o_ref[:, c0 + hd:c0 + 2 * hd] = (r * scale).astype(o_ref.dtype)


def _mla_k_kernel(x_ref, g_ref, w_ref, kr_ref, cos_ref, sa_ref, sb_ref, o_ref, a_ref, r_ref):
    hd = HEAD_DIM

    @pl.when(pl.program_id(1) == 0)
    def _():
        a_ref[...] = (_rms(x_ref[...]) * g_ref[...]).astype(BF16)
        r = kr_ref[...]
        r = r * cos_ref[...] + pltpu.roll(r, hd - MLA_ROPE // 2, 1) * sa_ref[...] + pltpu.roll(r, MLA_ROPE // 2, 1) * sb_ref[...]
        r_ref[...] = r.astype(BF16)

    y = _dot(a_ref[...], w_ref[...])
    for h in range(y.shape[1] // hd):
        o_ref[:, h * MLA_QK:h * MLA_QK + hd] = y[:, h * hd:(h + 1) * hd].astype(o_ref.dtype)
        o_ref[:, h * MLA_QK + hd:(h + 1) * MLA_QK] = r_ref[...]


def _mla_specs(m, tm, lora, col_blk):
    return [pl.BlockSpec((tm, lora), lambda i, j: (i, col_blk)),
            pl.BlockSpec((1, lora), lambda i, j: (0, 0))]


def rms_matmul(dn, col_blk, g, w, tn):
    m = dn.shape[0]
    lora, n = w.shape
    tm = min(m, 1024)
    est = 2 * tm * lora * 4 + tm * lora * 2 + 2 * lora * tn * 2 + 2 * tm * tn * 2 + tm * tn * 4
    return pl.pallas_call(
        _rms_mm_kernel,
        out_shape=jax.ShapeDtypeStruct((m, n), BF16),
        grid=(m // tm, n // tn),
        in_specs=_mla_specs(m, tm, lora, col_blk) + [pl.BlockSpec((lora, tn), lambda i, j: (0, j))],
        out_specs=pl.BlockSpec((tm, tn), lambda i, j: (i, j)),
        scratch_shapes=[pltpu.VMEM((tm, lora), BF16)],
        compiler_params=_params(("parallel", "arbitrary"), est),
        name="rms_matmul",
    )(dn, g.reshape(1, lora), w)


def mla_q(dn, g, w, cos, sa, sb, scale):
    m = dn.shape[0]
    lora, n = w.shape
    tm = min(m, 1024)
    tn = 4 * MLA_QK
    hd = HEAD_DIM
    est = 2 * tm * lora * 4 + tm * lora * 2 + 2 * lora * tn * 2 + 2 * tm * tn * 2 + 2 * tm * tn * 4 + 6 * tm * hd * 4
    rope_spec = pl.BlockSpec((tm, hd), lambda i, j: (i, 0))
    return pl.pallas_call(
        functools.partial(_mla_q_kernel, scale=scale),
        out_shape=jax.ShapeDtypeStruct((m, n), BF16),
        grid=(m // tm, n // tn),
        in_specs=_mla_specs(m, tm, lora, 0) + [pl.BlockSpec((lora, tn), lambda i, j: (0, j)),
                                               rope_spec, rope_spec, rope_spec],
        out_specs=pl.BlockSpec((tm, tn), lambda i, j: (i, j)),
        scratch_shapes=[pltpu.VMEM((tm, lora), BF16)],
        compiler_params=_params(("parallel", "arbitrary"), est),
        name="mla_q",
    )(dn, g.reshape(1, lora), w, cos, sa, sb)


def mla_k(dn, g, w, cos, sa, sb):
    m = dn.shape[0]
    lora, n = w.shape
    tm = min(m, 1024)
    hd = HEAD_DIM
    tn = 4 * hd
    kr_blk = (MLA_Q_LORA + MLA_KV_LORA) // hd
    est = 2 * tm * lora * 4 + tm * lora * 2 + 2 * lora * tn * 2 + 4 * tm * tn * 2 + tm * tn * 4 + 8 * tm * hd * 4
    rope_spec = pl.BlockSpec((tm, hd), lambda i, j: (i, 0))
    return pl.pallas_call(
        _mla_k_kernel,
        out_shape=jax.ShapeDtypeStruct((m, 2 * n), BF16),
        grid=(m // tm, n // tn),
        in_specs=_mla_specs(m, tm, lora, 1) + [pl.BlockSpec((lora, tn), lambda i, j: (0, j)),
                                               pl.BlockSpec((tm, hd), lambda i, j: (i, kr_blk)),
                                               rope_spec, rope_spec, rope_spec],
        out_specs=pl.BlockSpec((tm, 2 * tn), lambda i, j: (i, j)),
        scratch_shapes=[pltpu.VMEM((tm, lora), BF16), pltpu.VMEM((tm, hd), BF16)],
        compiler_params=_params(("parallel", "arbitrary"), est),
        name="mla_k",
    )(dn, g.reshape(1, lora), w, dn, cos, sa, sb)


def _router_kernel(x_ref, sh_ref, sc_ref, wr_ref, h_ref, aff_ref, *, row):
    hn = _layer_norm(x_ref[...])
    h = hn * (1.0 + sc_ref[row:row + 1, :]) + sh_ref[row:row + 1, :]
    hb = h.astype(BF16)
    half = h.shape[1] // 2
    bits = pltpu.bitcast(hb.astype(F32), jnp.uint32)
    h_ref[...] = (bits[:, half:] & jnp.uint32(0xFFFF0000)) | (bits[:, :half] >> 16)
    w = wr_ref[...]
    w1 = w.astype(BF16)
    w2 = (w - w1.astype(F32)).astype(BF16)
    h2 = (h - hb.astype(F32)).astype(BF16)
    logits = _dot_nt(w1, hb) + (_dot_nt(w2, hb) + _dot_nt(w1, h2))
    mx = jnp.max(logits, axis=0, keepdims=True)
    p = jnp.exp(logits - mx)
    aff_ref[...] = p / jnp.sum(p, axis=0, keepdims=True)


def moe_router(x, mod, row, k_shift, k_scale, w_router_t):
    m, d = x.shape
    e = w_router_t.shape[0]
    tm = min(m, 512)
    return pl.pallas_call(
        functools.partial(_router_kernel, row=row),
        out_shape=(jax.ShapeDtypeStruct((m, d // 2), jnp.uint32), jax.ShapeDtypeStruct((e, m), F32)),
        grid=(m // tm,),
        in_specs=[pl.BlockSpec((tm, d), lambda i: (i, 0)),
                  pl.BlockSpec((8, d), lambda i: (0, k_shift)),
                  pl.BlockSpec((8, d), lambda i: (0, k_scale)),
                  pl.BlockSpec((e, d), lambda i: (0, 0))],
        out_specs=(pl.BlockSpec((tm, d // 2), lambda i: (i, 0)), pl.BlockSpec((e, tm), lambda i: (0, i))),
        compiler_params=_params(("parallel",), 8 * tm * d * 4),
        name="moe_router",
    )(x, mod, mod, w_router_t)


def _ffn_up_kernel(*refs, n_seg):
    x_refs, (wg_ref, wu_ref) = refs[:n_seg], refs[n_seg:n_seg + 2]
    o_refs, xs_refs = refs[n_seg + 2:2 * n_seg + 2], refs[2 * n_seg + 2:]

    @pl.when(pl.program_id(1) == 0)
    def _():
        for x_ref, xs_ref in zip(x_refs, xs_refs):
            word = x_ref[...]
            first = pltpu.bitcast(word << 16, F32)
            second = pltpu.bitcast(word & jnp.uint32(0xFFFF0000), F32)
            xs_ref[...] = jnp.concatenate([first, second], axis=1).astype(BF16)

    wg = wg_ref[...].astype(BF16)
    wu = wu_ref[...].astype(BF16)
    for xs_ref, o_ref in zip(xs_refs, o_refs):
        x = xs_ref[...]
        g = _dot(x, wg)
        u = _dot(x, wu)
        o_ref[...] = (g * jax.nn.sigmoid(g) * u).astype(o_ref.dtype)


def _ffn_down_kernel(*refs, n_seg):
    h_refs, wd_ref, wt_refs, o_refs = refs[:n_seg], refs[n_seg], refs[n_seg + 1:2 * n_seg + 1], refs[2 * n_seg + 1:]
    wd = wd_ref[...].astype(BF16)
    for h_ref, wt_ref, o_ref in zip(h_refs, wt_refs, o_refs):
        y = _dot(h_ref[...], wd) * wt_ref[...]
        hi = y.astype(BF16)
        o_ref[0] = hi
        o_ref[1] = (y - hi.astype(F32)).astype(BF16)


def expert_ffn(xgs, wts, w_gate, w_up, w_down, layer):
    n_seg = len(xgs)
    e = xgs[0].shape[0]
    d = 2 * xgs[0].shape[2]
    f = w_gate.shape[3]
    rs = [x.shape[1] for x in xgs]
    r = sum(rs)
    tf = min(f, 256)
    est = 2 * (r * d * 2 + 2 * d * tf * 4 + r * tf * 2) + r * d * 2 + 2 * d * tf * 2 + 3 * r * tf * 4
    hids = pl.pallas_call(
        functools.partial(_ffn_up_kernel, n_seg=n_seg),
        out_shape=[jax.ShapeDtypeStruct((e, ri, f), BF16) for ri in rs],
        grid=(e, f // tf),
        in_specs=[pl.BlockSpec((None, ri, d // 2), lambda i, j: (i, 0, 0)) for ri in rs]
        + [pl.BlockSpec((None, None, d, tf), lambda i, j: (layer, i, 0, j))] * 2,
        out_specs=[pl.BlockSpec((None, ri, tf), lambda i, j: (i, 0, j)) for ri in rs],
        scratch_shapes=[pltpu.VMEM((ri, d), BF16) for ri in rs],
        compiler_params=_params(("parallel", "arbitrary"), est),
        name="ffn_up",
    )(*xgs, w_gate, w_up)
    tn = min(d, 512)
    est = 2 * (r * f * 2 + f * tn * 4 + r * tn * 4 + r * LANES * 4) + f * tn * 2 + 2 * r * tn * 4
    return pl.pallas_call(
        functools.partial(_ffn_down_kernel, n_seg=n_seg),
        out_shape=[jax.ShapeDtypeStruct((e, 2, ri, d), BF16) for ri in rs],
        grid=(e, d // tn),
        in_specs=[pl.BlockSpec((None, ri, f), lambda i, j: (i, 0, 0)) for ri in rs]
        + [pl.BlockSpec((None, None, f, tn), lambda i, j: (layer, i, 0, j))]
        + [pl.BlockSpec((None, ri, 1), lambda i, j: (i, 0, 0)) for ri in rs],
        out_specs=[pl.BlockSpec((None, 2, ri, tn), lambda i, j: (i, 0, 0, j)) for ri in rs],
        compiler_params=_params(("parallel", "arbitrary"), est),
        name="ffn_down",
    )(*hids, w_down, *wts)


def _select_kernel(aff_ref, pos_ref, lor_ref, cnt_ref, off_ref, ps_ref, *, cap):
    e, g, ln = aff_ref.shape
    bits = pltpu.bitcast(aff_ref[...], jnp.int32)

    def count(mask):
        per_lane = jnp.sum(jnp.where(mask, 1.0, 0.0), axis=1)
        return jnp.sum(per_lane, axis=1, keepdims=True)[:, :, None]

    def search(i, t):
        cand = t | jnp.left_shift(jnp.int32(1), 30 - i)
        return jnp.where(count(bits >= cand) >= cap, cand, t)

    thr = lax.fori_loop(0, 31, search, jnp.zeros((e, 1, 1), jnp.int32))
    gt = bits > thr
    eq = bits == thr
    need = cap - count(gt)

    r0 = lax.broadcasted_iota(jnp.int32, (ln, ln), 0)
    r1 = lax.broadcasted_iota(jnp.int32, (ln, ln), 1)
    upper = jnp.where(r0 <= r1, 1.0, 0.0).astype(BF16)
    ones = jnp.ones((ln, ln), BF16)
    g0 = lax.broadcasted_iota(jnp.int32, (g, g), 0)
    g1 = lax.broadcasted_iota(jnp.int32, (g, g), 1)
    earlier = jnp.where(g1 < g0, 1.0, 0.0).astype(BF16)

    def prefix(mask):
        x = jnp.where(mask, 1.0, 0.0).astype(BF16).reshape(e * g, ln)
        incl = _dot(x, upper).reshape(e, g, ln)
        tot = _dot(x, ones).reshape(e, g, ln)
        off = jnp.stack([_dot(earlier, tot[i].astype(BF16)) for i in range(e)])
        return incl, tot, off

    incl_eq, _, off_eq = prefix(eq)
    sel = gt | (eq & (off_eq + incl_eq - 1.0 < need))
    incl, tot, off = prefix(sel)
    pos_ref[...] = jnp.where(sel, off + incl - 1.0, -1.0).astype(jnp.int32)
    cnt_ref[...] = tot.astype(jnp.int32)
    off_ref[...] = off.astype(jnp.int32)
    ps_ref[...] = incl

    rank = lax.broadcasted_iota(jnp.int32, (1, ln, 1), 1).astype(F32)
    ones8 = jnp.ones((8, ln), BF16)

    def lane_of_rank(i, c):
        below = jnp.where(ps_ref[i][:, None, :] <= rank, 1.0, 0.0).astype(BF16).reshape(g * ln, ln)
        lor_ref[i] = _dot_nt(ones8, below)[0:1]
        return c

    lax.fori_loop(0, e, lane_of_rank, 0)


def moe_select(aff_t, cap):
    e, n = aff_t.shape
    g = n // LANES
    g_pad = -(-g // 16) * 16
    a = aff_t.reshape(e, g, LANES)
    if g_pad != g:
        a = jnp.concatenate([a, jnp.full((e, g_pad - g, LANES), -1.0, F32)], axis=1)
    shp = jax.ShapeDtypeStruct((e, g_pad, LANES), jnp.int32)
    full = pl.BlockSpec((e, g_pad, LANES), lambda i: (0, 0, 0))
    pos, lor, cnt, off = pl.pallas_call(
        functools.partial(_select_kernel, cap=cap),
        out_shape=(shp, jax.ShapeDtypeStruct((e, 1, g_pad * LANES), F32), shp, shp),
        grid=(1,),
        in_specs=[full],
        out_specs=(full, pl.BlockSpec((e, 1, g_pad * LANES), lambda i: (0, 0, 0)), full, full),
        scratch_shapes=[pltpu.VMEM((e, g_pad, LANES), F32)],
        compiler_params=_params(("arbitrary",), 24 * e * g_pad * LANES * 4 + 4 * g_pad * LANES * LANES * 4),
        name="moe_select",
    )(a)
    return (pos.reshape(e, g_pad * LANES)[:, :n], lor.astype(jnp.int32), cnt[:, :, 0].reshape(e, 1, g_pad),
            off[:, :g, 0])


def _row_copy(h_hbm, x_ref, sem, token, row):
    return pltpu.make_async_copy(h_hbm.at[pl.ds(token, 1), :], x_ref.at[0, pl.ds(row, 1), :], sem)


def _gather_kernel(lor_ref, cnt_ref, aff_ref, h_hbm, x_ref, wt_ref, sem):
    cap = x_ref.shape[1]

    def group(g, slot):
        def row(r, slot):
            token = g * LANES + lor_ref[0, g * LANES + r]
            wt_ref[0, slot] = aff_ref[0, token]
            _row_copy(h_hbm, x_ref, sem, token, slot).start()
            return slot + 1

        return lax.fori_loop(0, cnt_ref[0, g], row, slot)

    lax.fori_loop(0, cnt_ref.shape[1], group, 0)
    pltpu.make_async_copy(h_hbm.at[pl.ds(0, cap), :], x_ref.at[0], sem).wait()


def moe_gather(hp, aff_t, lor, cnt, cap):
    n, w = hp.shape
    e = lor.shape[0]

    def smem(width):
        return pl.BlockSpec((None, 1, width), lambda i: (i, 0, 0), memory_space=pltpu.SMEM)

    x, wt = pl.pallas_call(
        _gather_kernel,
        out_shape=(jax.ShapeDtypeStruct((e, cap, w), jnp.uint32), jax.ShapeDtypeStruct((e, 1, cap), F32)),
        grid=(e,),
        in_specs=[smem(lor.shape[2]), smem(cnt.shape[2]), smem(n), pl.BlockSpec(memory_space=pl.ANY)],
        out_specs=(pl.BlockSpec((1, cap, w), lambda i: (i, 0, 0)), smem(cap)),
        scratch_shapes=[pltpu.SemaphoreType.DMA(())],
        compiler_params=_params(("arbitrary",), 2 * cap * w * 4),
        name="moe_gather",
    )(lor, cnt, aff_t.reshape(e, 1, n), hp)
    return x, wt.reshape(e, cap, 1)


def _window_copy(y_hbm, dst, sem, e, src, win):
    return pltpu.make_async_copy(y_hbm.at[e, :, pl.ds(src, win)], dst, sem)


def _combine_kernel(offb_ref, y_hbm, pos_ref, x_ref, gate_ref, g_ref, b_ref, o_ref, ybuf, ybuf_x, sem, acc_ref,
                    *, row, cap, win):
    b = pl.program_id(0)
    n_exp, tb = pos_ref.shape
    half = n_exp // 2
    par = b % 2

    def window(e, k, blk=b):
        first = (offb_ref[e, blk] // 8) * 8 + k * win
        return first, pl.multiple_of(jnp.minimum(first, cap - win), 8)

    def onehot2(e, first, src):
        slots = src + lax.broadcasted_iota(jnp.int32, (win, 1), 0)
        hit = jnp.logical_and(pos_ref[e:e + 1, :] == slots, slots >= first)
        oh = jnp.where(hit, 1.0, 0.0).astype(BF16)
        return jnp.concatenate([oh, oh], axis=0)

    def copies(h, blk, p):
        return [_window_copy(y_hbm, ybuf.at[p, h, j], sem.at[2 * p + h], h * half + j,
                             window(h * half + j, 0, blk)[1], win) for j in range(half)]

    def start_block(blk, p):
        for h in range(2):
            for cp in copies(h, blk, p):
                cp.start()

    @pl.when(b == 0)
    def _():
        start_block(b, par)

    @pl.when(b + 1 < pl.num_programs(0))
    def _():
        start_block(b + 1, 1 - par)

    acc = None
    for h in range(2):
        for cp in copies(h, b, par):
            cp.wait()
        lhs = jnp.concatenate([onehot2(h * half + j, *window(h * half + j, 0)) for j in range(half)], axis=0)
        part = _dot_tn(lhs, ybuf[par, h].reshape(half * 2 * win, ybuf.shape[-1]))
        acc = part if acc is None else acc + part
    acc_ref[...] = acc

    for e in range(n_exp):
        n_win = (offb_ref[e, b + 1] - (offb_ref[e, b] // 8) * 8 + win - 1) // win

        def extra(k, c, e=e):
            first_k, src_k = window(e, k)
            cp = _window_copy(y_hbm, ybuf_x, sem.at[4], e, src_k, win)
            cp.start()
            cp.wait()
            acc_ref[...] += _dot_tn(onehot2(e, first_k, src_k), ybuf_x[...].reshape(2 * win, ybuf_x.shape[-1]))
            return c

        lax.fori_loop(1, n_win, extra, 0)

    z = DEEPNORM_ALPHA * x_ref[...] + gate_ref[row:row + 1, :] * acc_ref[...]
    o_ref[...] = _layer_norm(z) * g_ref[...] + b_ref[...]


def moe_combine_postnorm(y, pos, off, x, mod, row, k_gate, g, b):
    n, d = x.shape
    e, _, cap, _ = y.shape
    tb = min(n, 256)
    nb = n // tb
    win = min(cap, COMBINE_WINDOW)
    offb = jnp.concatenate([off[:, ::tb // LANES], jnp.full((e, 1), cap, jnp.int32)], axis=1)
    grid_spec = pltpu.PrefetchScalarGridSpec(
        num_scalar_prefetch=1,
        grid=(nb,),
        in_specs=[pl.BlockSpec(memory_space=pl.ANY),
                  pl.BlockSpec((e, tb), lambda i, o: (0, i)),
                  pl.BlockSpec((tb, d), lambda i, o: (i, 0)),
                  pl.BlockSpec((8, d), lambda i, o: (0, k_gate)),
                  pl.BlockSpec((1, d), lambda i, o: (0, 0)),
                  pl.BlockSpec((1, d), lambda i, o: (0, 0))],
        out_specs=pl.BlockSpec((tb, d), lambda i, o: (i, 0)),
        scratch_shapes=[pltpu.VMEM((2, 2, e // 2, 2, win, d), BF16), pltpu.VMEM((2, win, d), BF16),
                        pltpu.SemaphoreType.DMA((5,)), pltpu.VMEM((tb, d), F32)],
    )
    est = 8 * tb * d * 4 + (2 * e + 1) * 2 * win * d * 2 + e * win * tb * 2
    return pl.pallas_call(
        functools.partial(_combine_kernel, row=row, cap=cap, win=win),
        out_shape=jax.ShapeDtypeStruct((n, d), F32),
        grid_spec=grid_spec,
        compiler_params=_params(("arbitrary",), est),
        name="moe_combine",
    )(offb, y, pos, x, mod, g.reshape(1, d), b.reshape(1, d))


def moe_route(x, mod, row, w_router_t):
    m = x.shape[0]
    cap = max(1, EC_CAPACITY_FACTOR * m // N_EXPERTS)
    h, aff_t = moe_router(x, mod, row, 3, 4, w_router_t)
    pos, lor, cnt, off = moe_select(aff_t, cap)
    xg, wt = moe_gather(h, aff_t, lor, cnt, cap)
    return xg, wt, (pos, off)


def _rope_angles(n_tokens, rot_dim):
    rows = n_tokens // GRID_W
    row = jnp.repeat(jnp.arange(rows, dtype=F32), GRID_W)
    col = jnp.tile(jnp.arange(GRID_W, dtype=F32), rows)
    n_freq = rot_dim // 4
    inv = ROPE_THETA ** (-jnp.arange(n_freq, dtype=F32) / n_freq)
    return jnp.concatenate([row[:, None] * inv, col[:, None] * inv], axis=-1)


def _gqa_rope_tables(n_tokens):
    ang = _rope_angles(n_tokens, HEAD_DIM)
    c, s = jnp.cos(ang), jnp.sin(ang)
    return jnp.concatenate([c, c], axis=-1), jnp.concatenate([-s, s], axis=-1)


def _mla_rope_tables(n_tokens):
    ang = _rope_angles(n_tokens, MLA_ROPE)
    c, s = jnp.cos(ang), jnp.sin(ang)
    z = jnp.zeros_like(c)
    cos = jnp.concatenate([c, c, z, z], axis=-1)
    sa = jnp.concatenate([-s, z, z, z], axis=-1)
    sb = jnp.concatenate([z, s, z, z], axis=-1)
    return cos, sa, sb


def kernel(x, c, ctx, c_ctx, ada_w, ada_b, ln_g, ln_b, ev_w_in, ev_w_out, hgrn_lb, hgrn_norm_g, gqa_q_norm_g, gqa_k_norm_g, mla_w_down, mla_q_norm_g, mla_kv_norm_g, mla_w_uq, mla_w_ukv, mla_w_o, moe_router, moe_w_gate, moe_w_up, moe_w_down):
    d = D_MODEL
    xl = x[0]
    xc = ctx[0]
    n_lat, n_ctx = xl.shape[0], xc.shape[0]
    cc = jnp.zeros((8, d), F32).at[0].set(c[0]).at[1].set(c_ctx)
    lb_all = jnp.cumsum(jax.nn.softmax(hgrn_lb.astype(F32), axis=1), axis=1)
    gqa_tabs = _gqa_rope_tables(n_lat)
    gqa_tabs_ctx = [jnp.ones((n_ctx, HEAD_DIM), F32), jnp.zeros((n_ctx, HEAD_DIM), F32)]
    mla_tabs = _mla_rope_tables(n_lat)
    mla_tabs_ctx = [jnp.ones((n_ctx, HEAD_DIM), F32), jnp.zeros((n_ctx, HEAD_DIM), F32), jnp.zeros((n_ctx, HEAD_DIM), F32)]
    LAT, CTX = 0, 1

    for l in range(DEPTH):
        last = l == DEPTH - 1
        i = l // 2
        mod = adaln(cc, ada_w[l], ada_b[l])
        if l % 2 == 0:
            w_in = ev_w_in[i].astype(BF16)
            w_out = ev_w_out[i].astype(BF16)
            lb = lb_all[:, l].reshape(2, 1, A_WIDTH)
            scale = HEAD_DIM ** -0.5 * LOG2E
            proj_c = lnmod_matmul(xc, mod, CTX, 0, 1, w_in, 512)
            proj_l = lnmod_matmul(xl, mod, LAT, 0, 1, w_in, 512)
            s0 = jnp.zeros((2, A_HEADS, HEAD_DIM, HEAD_DIM), F32)
            o_c, s_c = hgrn_scan(proj_c, lb, s0)
            o_l, _ = hgrn_scan(proj_l, lb, s_c)
            a_l = hgrn_out(o_l, proj_l, hgrn_norm_g[i])
            qcol, kcol, vcol = 5 * A_WIDTH, 5 * A_WIDTH + B_WIDTH, 5 * A_WIDTH + B_WIDTH + B_KV_WIDTH
            q_l = norm_rope(proj_l, qcol, B_Q_HEADS, gqa_q_norm_g[i], *gqa_tabs, scale)
            k_l = norm_rope(proj_l, kcol, B_KV_HEADS, gqa_k_norm_g[i], *gqa_tabs, 1.0)
            k_c = norm_rope(proj_c, kcol, B_KV_HEADS, gqa_k_norm_g[i], *gqa_tabs_ctx, 1.0)
            v_l = proj_l[:, vcol:].astype(BF16)
            v_c = proj_c[:, vcol:].astype(BF16)
            att = dict(n_heads=B_Q_HEADS, n_kv_heads=B_KV_HEADS, dq=HEAD_DIM, dv=HEAD_DIM)
            b_l = flash_attention(q_l, k_l, v_l, k_c, v_c, **att)
            w_parts = [w_out[:A_WIDTH], w_out[A_WIDTH:]]
            xl_new = proj_postnorm([a_l, b_l], w_parts, xl, mod, LAT, 2, ln_g[l, 0], ln_b[l, 0])
            if not last:
                a_c = hgrn_out(o_c, proj_c, hgrn_norm_g[i])
                q_c = norm_rope(proj_c, qcol, B_Q_HEADS, gqa_q_norm_g[i], *gqa_tabs_ctx, scale)
                b_c = flash_attention(q_c, k_c, v_c, **att)
                xc = proj_postnorm([a_c, b_c], w_parts, xc, mod, CTX, 2, ln_g[l, 0], ln_b[l, 0])
            xl = xl_new
        else:
            hd = HEAD_DIM
            pad = (-mla_w_down.shape[2]) % hd
            w_down = jnp.pad(mla_w_down[i], ((0, 0), (0, pad))).astype(BF16)
            w_uq = mla_w_uq[i].reshape(MLA_Q_LORA, MLA_HEADS, hd + MLA_ROPE)
            w_uq = jnp.pad(w_uq, ((0, 0), (0, 0), (0, MLA_QK - hd - MLA_ROPE))).reshape(MLA_Q_LORA, MLA_HEADS * MLA_QK).astype(BF16)
            w_ukv = mla_w_ukv[i].reshape(MLA_KV_LORA, MLA_HEADS, 2 * hd)
            w_uk = w_ukv[:, :, :hd].reshape(MLA_KV_LORA, MLA_HEADS * hd).astype(BF16)
            w_uv = w_ukv[:, :, hd:].reshape(MLA_KV_LORA, MLA_HEADS * hd).astype(BF16)
            w_o = mla_w_o[i].astype(BF16)
            scale = (hd + MLA_ROPE) ** -0.5 * LOG2E
            dn_c = lnmod_matmul(xc, mod, CTX, 0, 1, w_down, w_down.shape[1])
            dn_l = lnmod_matmul(xl, mod, LAT, 0, 1, w_down, w_down.shape[1])
            q_l = mla_q(dn_l, mla_q_norm_g[i], w_uq, *mla_tabs, scale)
            k_l = mla_k(dn_l, mla_kv_norm_g[i], w_uk, *mla_tabs)
            k_c = mla_k(dn_c, mla_kv_norm_g[i], w_uk, *mla_tabs_ctx)
            v_l = rms_matmul(dn_l, 1, mla_kv_norm_g[i], w_uv, 1024)
            v_c = rms_matmul(dn_c, 1, mla_kv_norm_g[i], w_uv, 1024)
            att = dict(n_heads=MLA_HEADS, n_kv_heads=MLA_HEADS, dq=MLA_QK, dv=hd)
            o_l = flash_attention(q_l, k_l, v_l, k_c, v_c, **att)
            xl_new = proj_postnorm([o_l], [w_o], xl, mod, LAT, 2, ln_g[l, 0], ln_b[l, 0])
            if not last:
                q_c = mla_q(dn_c, mla_q_norm_g[i], w_uq, *mla_tabs_ctx, scale)
                o_c = flash_attention(q_c, k_c, v_c, **att)
                xc = proj_postnorm([o_c], [w_o], xc, mod, CTX, 2, ln_g[l, 0], ln_b[l, 0])
            xl = xl_new

        w_router_t = moe_router[l].T
        segs = [(xl, LAT)] if last else [(xl, LAT), (xc, CTX)]
        routes = [moe_route(xs, mod, row, w_router_t) for xs, row in segs]
        ys = expert_ffn([r[0] for r in routes], [r[1] for r in routes], moe_w_gate, moe_w_up, moe_w_down, l)
        outs = [moe_combine_postnorm(y, *r[2], xs, mod, row, 5, ln_g[l, 1], ln_b[l, 1])
                for y, r, (xs, row) in zip(ys, routes, segs)]
        xl = outs[0]
        if not last:
            xc = outs[1]
    return xl[None]
```

```python
import functools
import math

import numpy as np
import jax
import jax.numpy as jnp
from jax import lax
from jax.experimental import pallas as pl
from jax.experimental.pallas import tpu as pltpu

F32 = jnp.float32
BF16 = jnp.bfloat16

D_MODEL = 2048
DEPTH = 2
GRID_W = 64
HEAD_DIM = 128
A_HEADS = D_MODEL // 256
A_WIDTH = A_HEADS * HEAD_DIM
B_Q_HEADS = D_MODEL // 256
B_KV_HEADS = 2
B_WIDTH = B_Q_HEADS * HEAD_DIM
B_KV_WIDTH = B_KV_HEADS * HEAD_DIM
MLA_HEADS = D_MODEL // 128
MLA_Q_LORA = 512
MLA_KV_LORA = 512
MLA_ROPE = 64
MLA_QK = 2 * HEAD_DIM
N_EXPERTS = 16
EXPERT_FF = D_MODEL // 2
EC_CAPACITY_FACTOR = 2
ROPE_THETA = 10000.0
NORM_EPS = 1e-6
DEEPNORM_ALPHA = (2.0 * DEPTH) ** 0.25

HGRN_CHUNK = 128
HGRN_SMALL_LEVELS = 3
GATHER_UNROLL = 8
COMBINE_WINDOW = 64
LANES = 128
LOG2E = math.log2(math.e)
V7X_VMEM_BYTES = 64 * 1024 * 1024
VMEM_CAP_BYTES = V7X_VMEM_BYTES - 8 * 1024 * 1024


def _params(semantics, vmem_estimate_bytes):
    limit = int(min(max(2 * vmem_estimate_bytes, 32 * 1024 * 1024), VMEM_CAP_BYTES))
    return pltpu.CompilerParams(dimension_semantics=semantics, vmem_limit_bytes=limit)


def _layer_norm(x):
    mu = jnp.mean(x, axis=-1, keepdims=True)
    xc = x - mu
    var = jnp.mean(xc * xc, axis=-1, keepdims=True)
    return xc * lax.rsqrt(var + NORM_EPS)


def _rms(x):
    return x * lax.rsqrt(jnp.mean(x * x, axis=-1, keepdims=True) + NORM_EPS)


def _dot(a, b):
    return jnp.dot(a, b, preferred_element_type=F32)


def _dot_nt(a, b):
    return lax.dot_general(a, b, (((1,), (1,)), ((), ())), preferred_element_type=F32)


def _dot_tn(a, b):
    return lax.dot_general(a, b, (((0,), (0,)), ((), ())), preferred_element_type=F32)


def _split3(x):
    x1 = x.astype(BF16)
    r1 = x - x1.astype(F32)
    x2 = r1.astype(BF16)
    x3 = (r1 - x2.astype(F32)).astype(BF16)
    return x1, x2, x3


def _adaln_kernel(c_ref, w_ref, b_ref, o_ref):
    c = c_ref[...]
    s = c * jax.nn.sigmoid(c)
    w = w_ref[...]
    s1, s2, s3 = _split3(s)
    w1, w2, w3 = _split3(w)
    acc = _dot(s1, w3) + _dot(s3, w1) + _dot(s2, w2)
    acc = acc + _dot(s1, w2) + _dot(s2, w1)
    acc = acc + _dot(s1, w1)
    o_ref[...] = acc + b_ref[...]


def adaln(cc, w, b):
    d, n = w.shape
    tn = 1536 if n % 1536 == 0 else n
    est = 2 * d * tn * 4 * 2
    return pl.pallas_call(
        _adaln_kernel,
        out_shape=jax.ShapeDtypeStruct((8, n), F32),
        grid=(n // tn,),
        in_specs=[pl.BlockSpec((8, d), lambda j: (0, 0)),
                  pl.BlockSpec((d, tn), lambda j: (0, j)),
                  pl.BlockSpec((1, tn), lambda j: (0, j))],
        out_specs=pl.BlockSpec((8, tn), lambda j: (0, j)),
        compiler_params=_params(("parallel",), est),
        name="adaln",
    )(cc, w, b.reshape(1, n))


def _lnmod_mm_kernel(x_ref, sh_ref, sc_ref, w_ref, o_ref, h_ref, *, row):
    @pl.when(pl.program_id(1) == 0)
    def _():
        hn = _layer_norm(x_ref[...])
        h = hn * (1.0 + sc_ref[row:row + 1, :]) + sh_ref[row:row + 1, :]
        h_ref[...] = h.astype(BF16)

    o_ref[...] = _dot(h_ref[...], w_ref[...]).astype(o_ref.dtype)


def lnmod_matmul(x, mod, row, k_shift, k_scale, w, tn):
    m, d = x.shape
    n = w.shape[1]
    tm = min(m, 1024)
    est = 2 * tm * d * 4 + tm * d * 2 + 2 * d * tn * 2 + 2 * tm * tn * 4
    return pl.pallas_call(
        functools.partial(_lnmod_mm_kernel, row=row),
        out_shape=jax.ShapeDtypeStruct((m, n), F32),
        grid=(m // tm, n // tn),
        in_specs=[pl.BlockSpec((tm, d), lambda i, j: (i, 0)),
                  pl.BlockSpec((8, d), lambda i, j: (0, k_shift)),
                  pl.BlockSpec((8, d), lambda i, j: (0, k_scale)),
                  pl.BlockSpec((d, tn), lambda i, j: (0, j))],
        out_specs=pl.BlockSpec((tm, tn), lambda i, j: (i, j)),
        scratch_shapes=[pltpu.VMEM((tm, d), BF16)],
        compiler_params=_params(("parallel", "arbitrary"), est),
        name="lnmod_matmul",
    )(x, mod, mod, w)


def _hgrn_tables(c):
    n_lvl = int(math.log2(c))
    r = np.arange(c)
    u = np.arange(c)[None, :]
    blocks, masks = [], []
    for l in range(n_lvl):
        half = 1 << l
        base = (r // (2 * half)) * (2 * half)
        anchor = (base + half - 1)[:, None]
        upper = (r >= base + half)[:, None]
        rr = r[:, None]
        if l < HGRN_SMALL_LEVELS:
            blocks.append(np.where(upper, (u > anchor) & (u <= rr), (u > rr) & (u <= anchor)))
        same = (r[:, None] // (2 * half)) == (r[None, :] // (2 * half))
        masks.append(same & upper & ~(upper.T))
    blocks.append(u <= r[:, None])
    blocks.append(np.ones((16, c), bool))
    masks.append(np.eye(c, dtype=bool))
    fwd_s = np.concatenate(blocks, axis=0).astype(np.float32)
    fwd_m = np.stack(masks).astype(np.float32)
    bwd_s = np.concatenate([b[::-1, ::-1] for b in blocks], axis=0).astype(np.float32)
    bwd_m = fwd_m[:, ::-1, ::-1]
    return (jnp.asarray(np.stack([fwd_s, bwd_s]), BF16), jnp.asarray(np.stack([fwd_m, bwd_m]), F32))


def _hgrn_kernel(q_ref, v_ref, f_ref, lb_ref, sums_ref, mask_ref, s0_ref, o_ref, sfin_ref, st_ref):
    c = q_ref.shape[0]
    hd = HEAD_DIM
    n_lvl = mask_ref.shape[0] - 1
    n_small = HGRN_SMALL_LEVELS
    forward = pl.program_id(0) == 0
    j = pl.program_id(1)

    @pl.when(j == 0)
    def _():
        st_ref[...] = s0_ref[...]

    def wide_level(cum, l):
        half = 1 << l
        parts = []
        for base in range(0, c, 2 * half):
            a = base + half - 1
            mid = jnp.where(forward, cum[a:a + 1, :], cum[a + 1:a + 2, :])
            parts.append(jnp.broadcast_to(mid, (2 * half, hd)))
        anchor = parts[0] if len(parts) == 1 else jnp.concatenate(parts, axis=0)
        return -jnp.abs(cum - anchor)

    for h in range(q_ref.shape[1] // hd):
        cols = slice(h * hd, (h + 1) * hd)
        q = q_ref[:, cols]
        vb = v_ref[:, cols].astype(BF16)
        lb = lb_ref[:, cols]
        f = lb + (1.0 - lb) * jax.nn.sigmoid(f_ref[:, cols])
        g = jnp.log(f)
        k = 1.0 - f
        g1 = g.astype(BF16)
        g2 = (g - g1.astype(F32)).astype(BF16)
        e2 = _dot(sums_ref[...], jnp.concatenate([g1, g2], axis=1))
        e = e2[:, hd:] + e2[:, :hd]
        cum = e[n_small * c:(n_small + 1) * c]
        tot = e[(n_small + 1) * c:(n_small + 1) * c + 1]
        rem = tot - cum

        scores = _dot_nt(q.astype(BF16), k.astype(BF16)) * mask_ref[n_lvl]
        for l in range(n_lvl):
            z = jnp.exp(e[l * c:(l + 1) * c] if l < n_small else wide_level(cum, l))
            scores = scores + _dot_nt((q * z).astype(BF16), (k * z).astype(BF16)) * mask_ref[l]

        st = st_ref[h]
        o = _dot(scores.astype(BF16), vb) + _dot_nt((q * jnp.exp(cum)).astype(BF16), st.astype(BF16))
        o_ref[:, cols] = o
        st_new = st * jnp.exp(tot) + _dot_tn(vb, (k * jnp.exp(rem)).astype(BF16))
        st_ref[h] = st_new

    @pl.when(j == pl.num_programs(1) - 1)
    def _():
        sfin_ref[...] = st_ref[...]


def hgrn_scan(proj, lb, s0):
    seq = proj.shape[0]
    c = HGRN_CHUNK
    nc = seq // c
    sums, masks = _hgrn_tables(c)
    hd, w = HEAD_DIM, A_WIDTH

    def blk(d, j):
        return jnp.where(d == 0, j, nc - 1 - j)

    est = (2 * (4 * c * w * 4 + sums.shape[1] * c * 2 + masks.shape[1] * c * c * 4 + 2 * A_HEADS * hd * hd * 4)
           + A_HEADS * hd * hd * 4)
    return pl.pallas_call(
        _hgrn_kernel,
        out_shape=(jax.ShapeDtypeStruct((2, seq, w), F32),
                   jax.ShapeDtypeStruct((2, A_HEADS, hd, hd), F32)),
        grid=(2, nc),
        in_specs=[pl.BlockSpec((c, w), lambda d, j: (blk(d, j), 0)),
                  pl.BlockSpec((c, w), lambda d, j: (blk(d, j), 3)),
                  pl.BlockSpec((c, w), lambda d, j: (blk(d, j), 1 + d)),
                  pl.BlockSpec((None, 1, w), lambda d, j: (d, 0, 0)),
                  pl.BlockSpec((None, sums.shape[1], c), lambda d, j: (d, 0, 0)),
                  pl.BlockSpec((None, masks.shape[1], c, c), lambda d, j: (d, 0, 0, 0)),
                  pl.BlockSpec((None, A_HEADS, hd, hd), lambda d, j: (d, 0, 0, 0))],
        out_specs=(pl.BlockSpec((None, c, w), lambda d, j: (d, blk(d, j), 0)),
                   pl.BlockSpec((None, A_HEADS, hd, hd), lambda d, j: (d, 0, 0, 0))),
        scratch_shapes=[pltpu.VMEM((A_HEADS, hd, hd), F32)],
        compiler_params=_params(("parallel", "arbitrary"), est),
        name="hgrn_scan",
    )(proj, proj, proj, lb, sums, masks, s0)


def _hgrn_out_kernel(o_ref, gate_ref, g_ref, a_ref):
    hd = HEAD_DIM
    for h in range(a_ref.shape[1] // hd):
        cols = slice(h * hd, (h + 1) * hd)
        o = o_ref[0, :, cols] + o_ref[1, :, cols]
        gate = gate_ref[:, cols]
        a_ref[:, cols] = (_rms(o) * g_ref[...] * (gate * jax.nn.sigmoid(gate))).astype(a_ref.dtype)


def hgrn_out(o, proj, norm_g):
    seq = o.shape[1]
    tm = min(seq, 256)
    hd, w = HEAD_DIM, A_WIDTH
    return pl.pallas_call(
        _hgrn_out_kernel,
        out_shape=jax.ShapeDtypeStruct((seq, w), BF16),
        grid=(seq // tm,),
        in_specs=[pl.BlockSpec((2, tm, w), lambda i: (0, i, 0)),
                  pl.BlockSpec((tm, w), lambda i: (i, 4)),
                  pl.BlockSpec((1, hd), lambda i: (0, 0))],
        out_specs=pl.BlockSpec((tm, w), lambda i: (i, 0)),
        compiler_params=_params(("parallel",), 8 * tm * w * 4),
        name="hgrn_out",
    )(o, proj, norm_g.reshape(1, hd))


def _norm_rope_kernel(x_ref, g_ref, cos_ref, sin_ref, o_ref, *, scale):
    hd = HEAD_DIM
    for h in range(o_ref.shape[1] // hd):
        cols = slice(h * hd, (h + 1) * hd)
        y = _rms(x_ref[:, cols]) * g_ref[...]
        y = y * cos_ref[...] + pltpu.roll(y, hd // 2, 1) * sin_ref[...]
        o_ref[:, cols] = (y * scale).astype(o_ref.dtype)


def norm_rope(proj, col0, n_heads, g, cos, sin, scale):
    seq = proj.shape[0]
    tm = min(seq, 256)
    hd = HEAD_DIM
    w = n_heads * hd
    return pl.pallas_call(
        functools.partial(_norm_rope_kernel, scale=scale),
        out_shape=jax.ShapeDtypeStruct((seq, w), BF16),
        grid=(seq // tm,),
        in_specs=[pl.BlockSpec((tm, w), lambda i: (i, col0 // w)),
                  pl.BlockSpec((1, hd), lambda i: (0, 0)),
                  pl.BlockSpec((tm, hd), lambda i: (i, 0)),
                  pl.BlockSpec((tm, hd), lambda i: (i, 0))],
        out_specs=pl.BlockSpec((tm, w), lambda i: (i, 0)),
        compiler_params=_params(("parallel",), 6 * tm * w * 4 + 4 * tm * hd * 4),
        name="norm_rope",
    )(proj, g.reshape(1, hd), cos, sin)


def _flash_update(q, k, v, m_ref, l_ref, acc_ref):
    s = _dot_nt(q, k)
    m_prev = m_ref[...]
    m_new = jnp.maximum(m_prev, jnp.max(s, axis=-1, keepdims=True))
    alpha = jnp.exp2(m_prev - m_new)
    ps = [jnp.exp2(s[:, c * LANES:(c + 1) * LANES] - m_new) for c in range(s.shape[1] // LANES)]
    psum = ps[0]
    for pc in ps[1:]:
        psum = psum + pc
    p = jnp.concatenate([pc.astype(BF16) for pc in ps], axis=1)
    l_ref[...] = alpha * l_ref[...] + psum
    acc_ref[...] = alpha * acc_ref[...] + _dot(p, v)
    m_ref[...] = m_new


def _flash_kernel(*refs, has_ctx):
    if has_ctx:
        q_ref, k_ref, v_ref, kc_ref, vc_ref, o_ref, m_ref, l_ref, acc_ref = refs
    else:
        q_ref, k_ref, v_ref, o_ref, m_ref, l_ref, acc_ref = refs
    j = pl.program_id(2)

    @pl.when(j == 0)
    def _():
        m_ref[...] = jnp.full(m_ref.shape, -jnp.inf, F32)
        l_ref[...] = jnp.zeros(l_ref.shape, F32)
        acc_ref[...] = jnp.zeros(acc_ref.shape, F32)
        if has_ctx:
            _flash_update(q_ref[...], kc_ref[...], vc_ref[...], m_ref, l_ref, acc_ref)

    _flash_update(q_ref[...], k_ref[...], v_ref[...], m_ref, l_ref, acc_ref)

    @pl.when(j == pl.num_programs(2) - 1)
    def _():
        l = jnp.sum(l_ref[...], axis=-1, keepdims=True)
        o_ref[...] = (acc_ref[...] / l).astype(o_ref.dtype)


def flash_attention(q, k, v, k_ctx=None, v_ctx=None, *, n_heads, n_kv_heads, dq, dv):
    n, m = q.shape[0], k.shape[0]
    grp = n_heads // n_kv_heads
    tq = min(n, 2048)
    tk = min(m, 2048)
    has_ctx = k_ctx is not None
    in_specs = [pl.BlockSpec((tq, dq), lambda h, i, j: (i, h)),
                pl.BlockSpec((tk, dq), lambda h, i, j: (j, h // grp)),
                pl.BlockSpec((tk, dv), lambda h, i, j: (j, h // grp))]
    args = [q, k, v]
    if has_ctx:
        mc = k_ctx.shape[0]
        in_specs += [pl.BlockSpec((mc, dq), lambda h, i, j: (0, h // grp)),
                     pl.BlockSpec((mc, dv), lambda h, i, j: (0, h // grp))]
        args += [k_ctx, v_ctx]
    est = 2 * (tq * dq + tk * dq + tk * dv + tq * dv) * 2 + tq * (dv + 256) * 4 + 6 * tq * tk * 4
    return pl.pallas_call(
        functools.partial(_flash_kernel, has_ctx=has_ctx),
        out_shape=jax.ShapeDtypeStruct((n, n_heads * dv), BF16),
        grid=(n_heads, n // tq, m // tk),
        in_specs=in_specs,
        out_specs=pl.BlockSpec((tq, dv), lambda h, i, j: (i, h)),
        scratch_shapes=[pltpu.VMEM((tq, LANES), F32), pltpu.VMEM((tq, LANES), F32), pltpu.VMEM((tq, dv), F32)],
        compiler_params=_params(("parallel", "parallel", "arbitrary"), est),
        name="flash_attention",
    )(*args)


def _proj_postnorm_kernel(*refs, n_in, row):
    a_refs = refs[:n_in]
    w_refs = refs[n_in:2 * n_in]
    x_ref, gate_ref, g_ref, b_ref, o_ref = refs[2 * n_in:]
    y = _dot(a_refs[0][...], w_refs[0][...])
    for a_ref, w_ref in zip(a_refs[1:], w_refs[1:]):
        y = y + _dot(a_ref[...], w_ref[...])
    z = DEEPNORM_ALPHA * x_ref[...] + gate_ref[row:row + 1, :] * y
    o_ref[...] = _layer_norm(z) * g_ref[...] + b_ref[...]


def proj_postnorm(acts, ws, x, mod, row, k_gate, g, b):
    m, d = x.shape
    tm = min(m, 512)
    n_in = len(acts)
    once = pl.Buffered(1)
    in_specs = [pl.BlockSpec((tm, a.shape[1]), lambda i: (i, 0)) for a in acts]
    in_specs += [pl.BlockSpec(w.shape, lambda i: (0, 0), pipeline_mode=once) for w in ws]
    in_specs += [pl.BlockSpec((tm, d), lambda i: (i, 0)),
                 pl.BlockSpec((8, d), lambda i: (0, k_gate)),
                 pl.BlockSpec((1, d), lambda i: (0, 0)),
                 pl.BlockSpec((1, d), lambda i: (0, 0))]
    est = sum(w.size * 2 for w in ws) + sum(2 * tm * a.shape[1] * 2 for a in acts) + 6 * tm * d * 4
    return pl.pallas_call(
        functools.partial(_proj_postnorm_kernel, n_in=n_in, row=row),
        out_shape=jax.ShapeDtypeStruct((m, d), F32),
        grid=(m // tm,),
        in_specs=in_specs,
        out_specs=pl.BlockSpec((tm, d), lambda i: (i, 0)),
        compiler_params=_params(("parallel",), est),
        name="proj_postnorm",
    )(*acts, *ws, x, mod, g.reshape(1, d), b.reshape(1, d))


def _add_postnorm_kernel(y_ref, x_ref, gate_ref, g_ref, b_ref, o_ref, *, row):
    z = DEEPNORM_ALPHA * x_ref[...] + gate_ref[row:row + 1, :] * y_ref[...]
    o_ref[...] = _layer_norm(z) * g_ref[...] + b_ref[...]


def add_postnorm(y, x, mod, row, k_gate, g, b):
    m, d = x.shape
    tm = min(m, 512)
    return pl.pallas_call(
        functools.partial(_add_postnorm_kernel, row=row),
        out_shape=jax.ShapeDtypeStruct((m, d), F32),
        grid=(m // tm,),
        in_specs=[pl.BlockSpec((tm, d), lambda i: (i, 0)),
                  pl.BlockSpec((tm, d), lambda i: (i, 0)),
                  pl.BlockSpec((8, d), lambda i: (0, k_gate)),
                  pl.BlockSpec((1, d), lambda i: (0, 0)),
                  pl.BlockSpec((1, d), lambda i: (0, 0))],
        out_specs=pl.BlockSpec((tm, d), lambda i: (i, 0)),
        compiler_params=_params(("parallel",), 8 * tm * d * 4),
        name="add_postnorm",
    )(y, x, mod, g.reshape(1, d), b.reshape(1, d))


def _rms_mm_kernel(x_ref, g_ref, w_ref, o_ref, a_ref):
    @pl.when(pl.program_id(1) == 0)
    def _():
        a_ref[...] = (_rms(x_ref[...]) * g_ref[...]).astype(BF16)

    o_ref[...] = _dot(a_ref[...], w_ref[...]).astype(o_ref.dtype)


def _mla_q_kernel(x_ref, g_ref, w_ref, cos_ref, sa_ref, sb_ref, o_ref, a_ref, *, scale):
    @pl.when(pl.program_id(1) == 0)
    def _():
        a_ref[...] = (_rms(x_ref[...]) * g_ref[...]).astype(BF16)

    y = _dot(a_ref[...], w_ref[...])
    hd = HEAD_DIM
    for h in range(y.shape[1] // MLA_QK):
        c0 = h * MLA_QK
        o_ref[:, c0:c0 + hd] = (y[:, c0:c0 + hd] * scale).astype(o_ref.dtype)
        r = y[:, c0 + hd:c0 + 2 * hd]
        r = r * cos_ref[...] + pltpu.roll(r, hd - MLA_ROPE // 2, 1) * sa_ref[...] + pltpu.roll(r, MLA_ROPE // 2, 1) * sb_ref[...]
        o_ref[:, c0 + hd:c0 + 2 * hd] = (r * scale).astype(o_ref.dtype)


def _mla_k_kernel(x_ref, g_ref, w_ref, kr_ref, cos_ref, sa_ref, sb_ref, o_ref, a_ref, r_ref):
    hd = HEAD_DIM

    @pl.when(pl.program_id(1) == 0)
    def _():
        a_ref[...] = (_rms(x_ref[...]) * g_ref[...]).astype(BF16)
        r = kr_ref[...]
        r = r * cos_ref[...] + pltpu.roll(r, hd - MLA_ROPE // 2, 1) * sa_ref[...] + pltpu.roll(r, MLA_ROPE // 2, 1) * sb_ref[...]
        r_ref[...] = r.astype(BF16)

    y = _dot(a_ref[...], w_ref[...])
    for h in range(y.shape[1] // hd):
        o_ref[:, h * MLA_QK:h * MLA_QK + hd] = y[:, h * hd:(h + 1) * hd].astype(o_ref.dtype)
        o_ref[:, h * MLA_QK + hd:(h + 1) * MLA_QK] = r_ref[...]


def _mla_specs(m, tm, lora, col_blk):
    return [pl.BlockSpec((tm, lora), lambda i, j: (i, col_blk)),
            pl.BlockSpec((1, lora), lambda i, j: (0, 0))]


def rms_matmul(dn, col_blk, g, w, tn):
    m = dn.shape[0]
    lora, n = w.shape
    tm = min(m, 1024)
    est = 2 * tm * lora * 4 + tm * lora * 2 + 2 * lora * tn * 2 + 2 * tm * tn * 2 + tm * tn * 4
    return pl.pallas_call(
        _rms_mm_kernel,
        out_shape=jax.ShapeDtypeStruct((m, n), BF16),
        grid=(m // tm, n // tn),
        in_specs=_mla_specs(m, tm, lora, col_blk) + [pl.BlockSpec((lora, tn), lambda i, j: (0, j))],
        out_specs=pl.BlockSpec((tm, tn), lambda i, j: (i, j)),
        scratch_shapes=[pltpu.VMEM((tm, lora), BF16)],
        compiler_params=_params(("parallel", "arbitrary"), est),
        name="rms_matmul",
    )(dn, g.reshape(1, lora), w)


def mla_q(dn, g, w, cos, sa, sb, scale):
    m = dn.shape[0]
    lora, n = w.shape
    tm = min(m, 1024)
    tn = 4 * MLA_QK
    hd = HEAD_DIM
    est = 2 * tm * lora * 4 + tm * lora * 2 + 2 * lora * tn * 2 + 2 * tm * tn * 2 + 2 * tm * tn * 4 + 6 * tm * hd * 4
    rope_spec = pl.BlockSpec((tm, hd), lambda i, j: (i, 0))
    return pl.pallas_call(
        functools.partial(_mla_q_kernel, scale=scale),
        out_shape=jax.ShapeDtypeStruct((m, n), BF16),
        grid=(m // tm, n // tn),
        in_specs=_mla_specs(m, tm, lora, 0) + [pl.BlockSpec((lora, tn), lambda i, j: (0, j)),
                                               rope_spec, rope_spec, rope_spec],
        out_specs=pl.BlockSpec((tm, tn), lambda i, j: (i, j)),
        scratch_shapes=[pltpu.VMEM((tm, lora), BF16)],
        compiler_params=_params(("parallel", "arbitrary"), est),
        name="mla_q",
    )(dn, g.reshape(1, lora), w, cos, sa, sb)


def mla_k(dn, g, w, cos, sa, sb):
    m = dn.shape[0]
    lora, n = w.shape
    tm = min(m, 1024)
    hd = HEAD_DIM
    tn = 4 * hd
    kr_blk = (MLA_Q_LORA + MLA_KV_LORA) // hd
    est = 2 * tm * lora * 4 + tm * lora * 2 + 2 * lora * tn * 2 + 4 * tm * tn * 2 + tm * tn * 4 + 8 * tm * hd * 4
    rope_spec = pl.BlockSpec((tm, hd), lambda i, j: (i, 0))
    return pl.pallas_call(
        _mla_k_kernel,
        out_shape=jax.ShapeDtypeStruct((m, 2 * n), BF16),
        grid=(m // tm, n // tn),
        in_specs=_mla_specs(m, tm, lora, 1) + [pl.BlockSpec((lora, tn), lambda i, j: (0, j)),
                                               pl.BlockSpec((tm, hd), lambda i, j: (i, kr_blk)),
                                               rope_spec, rope_spec, rope_spec],
        out_specs=pl.BlockSpec((tm, 2 * tn), lambda i, j: (i, j)),
        scratch_shapes=[pltpu.VMEM((tm, lora), BF16), pltpu.VMEM((tm, hd), BF16)],
        compiler_params=_params(("parallel", "arbitrary"), est),
        name="mla_k",
    )(dn, g.reshape(1, lora), w, dn, cos, sa, sb)


def _router_kernel(x_ref, sh_ref, sc_ref, wr_ref, h_ref, aff_ref, *, row):
    hn = _layer_norm(x_ref[...])
    h = hn * (1.0 + sc_ref[row:row + 1, :]) + sh_ref[row:row + 1, :]
    hb = h.astype(BF16)
    half = h.shape[1] // 2
    bits = pltpu.bitcast(hb.astype(F32), jnp.uint32)
    h_ref[...] = (bits[:, half:] & jnp.uint32(0xFFFF0000)) | (bits[:, :half] >> 16)
    w = wr_ref[...]
    w1 = w.astype(BF16)
    w2 = (w - w1.astype(F32)).astype(BF16)
    h2 = (h - hb.astype(F32)).astype(BF16)
    logits = _dot_nt(w1, hb) + (_dot_nt(w2, hb) + _dot_nt(w1, h2))
    mx = jnp.max(logits, axis=0, keepdims=True)
    p = jnp.exp(logits - mx)
    aff_ref[...] = p / jnp.sum(p, axis=0, keepdims=True)


def moe_router(x, mod, row, k_shift, k_scale, w_router_t):
    m, d = x.shape
    e = w_router_t.shape[0]
    tm = min(m, 512)
    return pl.pallas_call(
        functools.partial(_router_kernel, row=row),
        out_shape=(jax.ShapeDtypeStruct((m, d // 2), jnp.uint32), jax.ShapeDtypeStruct((e, m), F32)),
        grid=(m // tm,),
        in_specs=[pl.BlockSpec((tm, d), lambda i: (i, 0)),
                  pl.BlockSpec((8, d), lambda i: (0, k_shift)),
                  pl.BlockSpec((8, d), lambda i: (0, k_scale)),
                  pl.BlockSpec((e, d), lambda i: (0, 0))],
        out_specs=(pl.BlockSpec((tm, d // 2), lambda i: (i, 0)), pl.BlockSpec((e, tm), lambda i: (0, i))),
        compiler_params=_params(("parallel",), 8 * tm * d * 4),
        name="moe_router",
    )(x, mod, mod, w_router_t)


def _ffn_up_kernel(*refs, n_seg):
    x_refs, (wg_ref, wu_ref) = refs[:n_seg], refs[n_seg:n_seg + 2]
    o_refs, xs_refs = refs[n_seg + 2:2 * n_seg + 2], refs[2 * n_seg + 2:]

    @pl.when(pl.program_id(1) == 0)
    def _():
        for x_ref, xs_ref in zip(x_refs, xs_refs):
            word = x_ref[...]
            first = pltpu.bitcast(word << 16, F32)
            second = pltpu.bitcast(word & jnp.uint32(0xFFFF0000), F32)
            xs_ref[...] = jnp.concatenate([first, second], axis=1).astype(BF16)

    wg = wg_ref[...].astype(BF16)
    wu = wu_ref[...].astype(BF16)
    for xs_ref, o_ref in zip(xs_refs, o_refs):
        x = xs_ref[...]
        g = _dot(x, wg)
        u = _dot(x, wu)
        o_ref[...] = (g * jax.nn.sigmoid(g) * u).astype(o_ref.dtype)


def _ffn_down_kernel(*refs, n_seg):
    h_refs, wd_ref, wt_refs, o_refs = refs[:n_seg], refs[n_seg], refs[n_seg + 1:2 * n_seg + 1], refs[2 * n_seg + 1:]
    wd = wd_ref[...].astype(BF16)
    for h_ref, wt_ref, o_ref in zip(h_refs, wt_refs, o_refs):
        y = _dot(h_ref[...], wd) * wt_ref[...]
        hi = y.astype(BF16)
        o_ref[0] = hi
        o_ref[1] = (y - hi.astype(F32)).astype(BF16)


def expert_ffn(xgs, wts, w_gate, w_up, w_down, layer):
    n_seg = len(xgs)
    e = xgs[0].shape[0]
    d = 2 * xgs[0].shape[2]
    f = w_gate.shape[3]
    rs = [x.shape[1] for x in xgs]
    r = sum(rs)
    tf = min(f, 256)
    est = 2 * (r * d * 2 + 2 * d * tf * 4 + r * tf * 2) + r * d * 2 + 2 * d * tf * 2 + 3 * r * tf * 4
    hids = pl.pallas_call(
        functools.partial(_ffn_up_kernel, n_seg=n_seg),
        out_shape=[jax.ShapeDtypeStruct((e, ri, f), BF16) for ri in rs],
        grid=(e, f // tf),
        in_specs=[pl.BlockSpec((None, ri, d // 2), lambda i, j: (i, 0, 0)) for ri in rs]
        + [pl.BlockSpec((None, None, d, tf), lambda i, j: (layer, i, 0, j))] * 2,
        out_specs=[pl.BlockSpec((None, ri, tf), lambda i, j: (i, 0, j)) for ri in rs],
        scratch_shapes=[pltpu.VMEM((ri, d), BF16) for ri in rs],
        compiler_params=_params(("parallel", "arbitrary"), est),
        name="ffn_up",
    )(*xgs, w_gate, w_up)
    tn = min(d, 512)
    est = 2 * (r * f * 2 + f * tn * 4 + r * tn * 4 + r * LANES * 4) + f * tn * 2 + 2 * r * tn * 4
    return pl.pallas_call(
        functools.partial(_ffn_down_kernel, n_seg=n_seg),
        out_shape=[jax.ShapeDtypeStruct((e, 2, ri, d), BF16) for ri in rs],
        grid=(e, d // tn),
        in_specs=[pl.BlockSpec((None, ri, f), lambda i, j: (i, 0, 0)) for ri in rs]
        + [pl.BlockSpec((None, None, f, tn), lambda i, j: (layer, i, 0, j))]
        + [pl.BlockSpec((None, ri, 1), lambda i, j: (i, 0, 0)) for ri in rs],
        out_specs=[pl.BlockSpec((None, 2, ri, tn), lambda i, j: (i, 0, 0, j)) for ri in rs],
        compiler_params=_params(("parallel", "arbitrary"), est),
        name="ffn_down",
    )(*hids, w_down, *wts)


def _select_kernel(aff_ref, pos_ref, idx_ref, wt_ref, off_ref, incl_ref, tot_ref, offs_ref, *, cap):
    e, g, ln = aff_ref.shape
    bits = pltpu.bitcast(aff_ref[...], jnp.int32)

    def count(mask):
        per_lane = jnp.sum(jnp.where(mask, 1.0, 0.0), axis=1)
        return jnp.sum(per_lane, axis=1, keepdims=True)[:, :, None]

    def search(i, t):
        cand = t | jnp.left_shift(jnp.int32(1), 30 - i)
        return jnp.where(count(bits >= cand) >= cap, cand, t)

    thr = lax.fori_loop(0, 31, search, jnp.zeros((e, 1, 1), jnp.int32))
    gt = bits > thr
    eq = bits == thr
    need = cap - count(gt)

    r0 = lax.broadcasted_iota(jnp.int32, (ln, ln), 0)
    r1 = lax.broadcasted_iota(jnp.int32, (ln, ln), 1)
    upper = jnp.where(r0 <= r1, 1.0, 0.0).astype(BF16)
    ones = jnp.ones((ln, ln), BF16)
    g0 = lax.broadcasted_iota(jnp.int32, (g, g), 0)
    g1 = lax.broadcasted_iota(jnp.int32, (g, g), 1)
    earlier = jnp.where(g1 < g0, 1.0, 0.0).astype(BF16)

    def prefix(mask):
        x = jnp.where(mask, 1.0, 0.0).astype(BF16).reshape(e * g, ln)
        incl = _dot(x, upper).reshape(e, g, ln)
        tot = _dot(x, ones).reshape(e, g, ln)
        off = jnp.stack([_dot(earlier, tot[i].astype(BF16)) for i in range(e)])
        return incl, tot, off

    incl_eq, _, off_eq = prefix(eq)
    sel = gt | (eq & (off_eq + incl_eq - 1.0 < need))
    incl, tot, off = prefix(sel)
    pos_ref[...] = jnp.where(sel, off + incl - 1.0, -1.0).astype(jnp.int32)
    off_ref[...] = off.astype(jnp.int32)
    incl_ref[...] = incl
    tot_ref[...] = tot
    offs_ref[...] = off

    slot = lax.broadcasted_iota(jnp.int32, (cap, ln), 0).astype(F32)
    lane = lax.broadcasted_iota(jnp.int32, (cap, ln), 1).astype(F32)

    def tokens_of_slots(i, c):
        ends = jnp.transpose(offs_ref[i] + tot_ref[i])[0:1, :]
        before = jnp.where(ends <= slot, 1.0, 0.0).astype(BF16)
        grp = _dot(before, ones)
        rank = slot - _dot(before, tot_ref[i].astype(BF16))
        pick = jnp.where(lane == grp, 1.0, 0.0).astype(BF16)
        incl_g = _dot(pick, incl_ref[i].astype(BF16))
        lane_p = _dot(jnp.where(incl_g <= rank, 1.0, 0.0).astype(BF16), ones)
        idx_ref[i] = (grp * ln + lane_p)[:, 0:1].astype(jnp.int32)
        a1, a2, a3 = _split3(aff_ref[i])
        aff_g = (_dot(pick, a3) + _dot(pick, a2)) + _dot(pick, a1)
        wt_ref[i] = jnp.sum(jnp.where(lane == lane_p, aff_g, 0.0), axis=1, keepdims=True)
        return c

    lax.fori_loop(0, e, tokens_of_slots, 0)


def moe_select(aff_t, cap):
    e, n = aff_t.shape
    g = n // LANES
    assert g <= LANES, "token groups are mapped onto the 128 lanes"
    a = aff_t.reshape(e, g, LANES)
    if g != LANES:
        a = jnp.concatenate([a, jnp.full((e, LANES - g, LANES), -1.0, F32)], axis=1)
    shp = jax.ShapeDtypeStruct((e, LANES, LANES), jnp.int32)
    full = pl.BlockSpec((e, LANES, LANES), lambda i: (0, 0, 0))
    per_slot = pl.BlockSpec((e, cap, 1), lambda i: (0, 0, 0))
    pos, idx, wt, off = pl.pallas_call(
        functools.partial(_select_kernel, cap=cap),
        out_shape=(shp, jax.ShapeDtypeStruct((e, cap, 1), jnp.int32), jax.ShapeDtypeStruct((e, cap, 1), F32), shp),
        grid=(1,),
        in_specs=[full],
        out_specs=(full, per_slot, per_slot, full),
        scratch_shapes=[pltpu.VMEM((e, LANES, LANES), F32)] * 3,
        compiler_params=_params(("arbitrary",), 28 * e * LANES * LANES * 4 + 16 * cap * LANES * 4),
        name="moe_select",
    )(a)
    return pos.reshape(e, LANES * LANES)[:, :n], idx.reshape(e, 1, cap), wt, off[:, :g, 0]


def _row_copy(h_hbm, x_ref, sem, token, row):
    return pltpu.make_async_copy(h_hbm.at[pl.ds(token, 1), :], x_ref.at[0, pl.ds(row, 1), :], sem)


def _gather_kernel(idx_ref, h_hbm, x_ref, sem):
    cap = x_ref.shape[1]

    def rows(i, c):
        for u in range(GATHER_UNROLL):
            s = i * GATHER_UNROLL + u
            _row_copy(h_hbm, x_ref, sem, idx_ref[0, s], s).start()
        return c

    lax.fori_loop(0, cap // GATHER_UNROLL, rows, 0)
    pltpu.make_async_copy(h_hbm.at[pl.ds(0, cap), :], x_ref.at[0], sem).wait()


def moe_gather(hp, idx):
    w = hp.shape[1]
    e, _, cap = idx.shape
    return pl.pallas_call(
        _gather_kernel,
        out_shape=jax.ShapeDtypeStruct((e, cap, w), jnp.uint32),
        grid=(e,),
        in_specs=[pl.BlockSpec((None, 1, cap), lambda i: (i, 0, 0), memory_space=pltpu.SMEM),
                  pl.BlockSpec(memory_space=pl.ANY)],
        out_specs=pl.BlockSpec((1, cap, w), lambda i: (i, 0, 0)),
        scratch_shapes=[pltpu.SemaphoreType.DMA(())],
        compiler_params=_params(("arbitrary",), 2 * cap * w * 4),
        name="moe_gather",
    )(idx, hp)


def _window_copy(y_hbm, dst, sem, e, src, win):
    return pltpu.make_async_copy(y_hbm.at[e, :, pl.ds(src, win)], dst, sem)


def _combine_kernel(offb_ref, y_hbm, pos_ref, x_ref, gate_ref, g_ref, b_ref, o_ref, ybuf, ybuf_x, sem, acc_ref,
                    *, row, cap, win):
    b = pl.program_id(0)
    n_exp, tb = pos_ref.shape
    half = n_exp // 2
    par = b % 2

    def window(e, k, blk=b):
        first = (offb_ref[e, blk] // 8) * 8 + k * win
        return first, pl.multiple_of(jnp.minimum(first, cap - win), 8)

    def onehot2(e, first, src):
        slots = src + lax.broadcasted_iota(jnp.int32, (win, 1), 0)
        hit = jnp.logical_and(pos_ref[e:e + 1, :] == slots, slots >= first)
        oh = jnp.where(hit, 1.0, 0.0).astype(BF16)
        return jnp.concatenate([oh, oh], axis=0)

    def copies(h, blk, p):
        return [_window_copy(y_hbm, ybuf.at[p, h, j], sem.at[2 * p + h], h * half + j,
                             window(h * half + j, 0, blk)[1], win) for j in range(half)]

    def start_block(blk, p):
        for h in range(2):
            for cp in copies(h, blk, p):
                cp.start()

    @pl.when(b == 0)
    def _():
        start_block(b, par)

    @pl.when(b + 1 < pl.num_programs(0))
    def _():
        start_block(b + 1, 1 - par)

    acc = None
    for h in range(2):
        for cp in copies(h, b, par):
            cp.wait()
        lhs = jnp.concatenate([onehot2(h * half + j, *window(h * half + j, 0)) for j in range(half)], axis=0)
        part = _dot_tn(lhs, ybuf[par, h].reshape(half * 2 * win, ybuf.shape[-1]))
        acc = part if acc is None else acc + part
    acc_ref[...] = acc

    for e in range(n_exp):
        n_win = (offb_ref[e, b + 1] - (offb_ref[e, b] // 8) * 8 + win - 1) // win

        def extra(k, c, e=e):
            first_k, src_k = window(e, k)
            cp = _window_copy(y_hbm, ybuf_x, sem.at[4], e, src_k, win)
            cp.start()
            cp.wait()
            acc_ref[...] += _dot_tn(onehot2(e, first_k, src_k), ybuf_x[...].reshape(2 * win, ybuf_x.shape[-1]))
            return c

        lax.fori_loop(1, n_win, extra, 0)

    z = DEEPNORM_ALPHA * x_ref[...] + gate_ref[row:row + 1, :] * acc_ref[...]
    o_ref[...] = _layer_norm(z) * g_ref[...] + b_ref[...]


def moe_combine_postnorm(y, pos, off, x, mod, row, k_gate, g, b):
    n, d = x.shape
    e, _, cap, _ = y.shape
    tb = min(n, 256)
    nb = n // tb
    win = min(cap, COMBINE_WINDOW)
    offb = jnp.concatenate([off[:, ::tb // LANES], jnp.full((e, 1), cap, jnp.int32)], axis=1)
    grid_spec = pltpu.PrefetchScalarGridSpec(
        num_scalar_prefetch=1,
        grid=(nb,),
        in_specs=[pl.BlockSpec(memory_space=pl.ANY),
                  pl.BlockSpec((e, tb), lambda i, o: (0, i)),
                  pl.BlockSpec((tb, d), lambda i, o: (i, 0)),
                  pl.BlockSpec((8, d), lambda i, o: (0, k_gate)),
                  pl.BlockSpec((1, d), lambda i, o: (0, 0)),
                  pl.BlockSpec((1, d), lambda i, o: (0, 0))],
        out_specs=pl.BlockSpec((tb, d), lambda i, o: (i, 0)),
        scratch_shapes=[pltpu.VMEM((2, 2, e // 2, 2, win, d), BF16), pltpu.VMEM((2, win, d), BF16),
                        pltpu.SemaphoreType.DMA((5,)), pltpu.VMEM((tb, d), F32)],
    )
    est = 8 * tb * d * 4 + (2 * e + 1) * 2 * win * d * 2 + e * win * tb * 2
    return pl.pallas_call(
        functools.partial(_combine_kernel, row=row, cap=cap, win=win),
        out_shape=jax.ShapeDtypeStruct((n, d), F32),
        grid_spec=grid_spec,
        compiler_params=_params(("arbitrary",), est),
        name="moe_combine",
    )(offb, y, pos, x, mod, g.reshape(1, d), b.reshape(1, d))


def moe_route(x, mod, row, w_router_t):
    m = x.shape[0]
    cap = max(1, EC_CAPACITY_FACTOR * m // N_EXPERTS)
    h, aff_t = moe_router(x, mod, row, 3, 4, w_router_t)
    pos, idx, wt, off = moe_select(aff_t, cap)
    return moe_gather(h, idx), wt, (pos, off)


def _rope_angles(n_tokens, rot_dim):
    rows = n_tokens // GRID_W
    row = jnp.repeat(jnp.arange(rows, dtype=F32), GRID_W)
    col = jnp.tile(jnp.arange(GRID_W, dtype=F32), rows)
    n_freq = rot_dim // 4
    inv = ROPE_THETA ** (-jnp.arange(n_freq, dtype=F32) / n_freq)
    return jnp.concatenate([row[:, None] * inv, col[:, None] * inv], axis=-1)


def _gqa_rope_tables(n_tokens):
    ang = _rope_angles(n_tokens, HEAD_DIM)
    c, s = jnp.cos(ang), jnp.sin(ang)
    return jnp.concatenate([c, c], axis=-1), jnp.concatenate([-s, s], axis=-1)


def _mla_rope_tables(n_tokens):
    ang = _rope_angles(n_tokens, MLA_ROPE)
    c, s = jnp.cos(ang), jnp.sin(ang)
    z = jnp.zeros_like(c)
    cos = jnp.concatenate([c, c, z, z], axis=-1)
    sa = jnp.concatenate([-s, z, z, z], axis=-1)
    sb = jnp.concatenate([z, s, z, z], axis=-1)
    return cos, sa, sb


def kernel(x, c, ctx, c_ctx, ada_w, ada_b, ln_g, ln_b, ev_w_in, ev_w_out, hgrn_lb, hgrn_norm_g, gqa_q_norm_g, gqa_k_norm_g, mla_w_down, mla_q_norm_g, mla_kv_norm_g, mla_w_uq, mla_w_ukv, mla_w_o, moe_router, moe_w_gate, moe_w_up, moe_w_down):
    d = D_MODEL
    xl = x[0]
    xc = ctx[0]
    n_lat, n_ctx = xl.shape[0], xc.shape[0]
    cc = jnp.zeros((8, d), F32).at[0].set(c[0]).at[1].set(c_ctx)
    lb_all = jnp.cumsum(jax.nn.softmax(hgrn_lb.astype(F32), axis=1), axis=1)
    gqa_tabs = _gqa_rope_tables(n_lat)
    gqa_tabs_ctx = [jnp.ones((n_ctx, HEAD_DIM), F32), jnp.zeros((n_ctx, HEAD_DIM), F32)]
    mla_tabs = _mla_rope_tables(n_lat)
    mla_tabs_ctx = [jnp.ones((n_ctx, HEAD_DIM), F32), jnp.zeros((n_ctx, HEAD_DIM), F32), jnp.zeros((n_ctx, HEAD_DIM), F32)]
    LAT, CTX = 0, 1

    for l in range(DEPTH):
        last = l == DEPTH - 1
        i = l // 2
        mod = adaln(cc, ada_w[l], ada_b[l])
        if l % 2 == 0:
            w_in = ev_w_in[i].astype(BF16)
            w_out = ev_w_out[i].astype(BF16)
            lb = lb_all[:, l].reshape(2, 1, A_WIDTH)
            scale = HEAD_DIM ** -0.5 * LOG2E
            proj_c = lnmod_matmul(xc, mod, CTX, 0, 1, w_in, 512)
            proj_l = lnmod_matmul(xl, mod, LAT, 0, 1, w_in, 512)
            s0 = jnp.zeros((2, A_HEADS, HEAD_DIM, HEAD_DIM), F32)
            o_c, s_c = hgrn_scan(proj_c, lb, s0)
            o_l, _ = hgrn_scan(proj_l, lb, s_c)
            a_l = hgrn_out(o_l, proj_l, hgrn_norm_g[i])
            qcol, kcol, vcol = 5 * A_WIDTH, 5 * A_WIDTH + B_WIDTH, 5 * A_WIDTH + B_WIDTH + B_KV_WIDTH
            q_l = norm_rope(proj_l, qcol, B_Q_HEADS, gqa_q_norm_g[i], *gqa_tabs, scale)
            k_l = norm_rope(proj_l, kcol, B_KV_HEADS, gqa_k_norm_g[i], *gqa_tabs, 1.0)
            k_c = norm_rope(proj_c, kcol, B_KV_HEADS, gqa_k_norm_g[i], *gqa_tabs_ctx, 1.0)
            v_l = proj_l[:, vcol:].astype(BF16)
            v_c = proj_c[:, vcol:].astype(BF16)
            att = dict(n_heads=B_Q_HEADS, n_kv_heads=B_KV_HEADS, dq=HEAD_DIM, dv=HEAD_DIM)
            b_l = flash_attention(q_l, k_l, v_l, k_c, v_c, **att)
            w_parts = [w_out[:A_WIDTH], w_out[A_WIDTH:]]
            xl_new = proj_postnorm([a_l, b_l], w_parts, xl, mod, LAT, 2, ln_g[l, 0], ln_b[l, 0])
            if not last:
                a_c = hgrn_out(o_c, proj_c, hgrn_norm_g[i])
                q_c = norm_rope(proj_c, qcol, B_Q_HEADS, gqa_q_norm_g[i], *gqa_tabs_ctx, scale)
                b_c = flash_attention(q_c, k_c, v_c, **att)
                xc = proj_postnorm([a_c, b_c], w_parts, xc, mod, CTX, 2, ln_g[l, 0], ln_b[l, 0])
            xl = xl_new
        else:
            hd = HEAD_DIM
            pad = (-mla_w_down.shape[2]) % hd
            w_down = jnp.pad(mla_w_down[i], ((0, 0), (0, pad))).astype(BF16)
            w_uq = mla_w_uq[i].reshape(MLA_Q_LORA, MLA_HEADS, hd + MLA_ROPE)
            w_uq = jnp.pad(w_uq, ((0, 0), (0, 0), (0, MLA_QK - hd - MLA_ROPE))).reshape(MLA_Q_LORA, MLA_HEADS * MLA_QK).astype(BF16)
            w_ukv = mla_w_ukv[i].reshape(MLA_KV_LORA, MLA_HEADS, 2 * hd)
            w_uk = w_ukv[:, :, :hd].reshape(MLA_KV_LORA, MLA_HEADS * hd).astype(BF16)
            w_uv = w_ukv[:, :, hd:].reshape(MLA_KV_LORA, MLA_HEADS * hd).astype(BF16)
            w_o = mla_w_o[i].astype(BF16)
            scale = (hd + MLA_ROPE) ** -0.5 * LOG2E
            dn_c = lnmod_matmul(xc, mod, CTX, 0, 1, w_down, w_down.shape[1])
            dn_l = lnmod_matmul(xl, mod, LAT, 0, 1, w_down, w_down.shape[1])
            q_l = mla_q(dn_l, mla_q_norm_g[i], w_uq, *mla_tabs, scale)
            k_l = mla_k(dn_l, mla_kv_norm_g[i], w_uk, *mla_tabs)
            k_c = mla_k(dn_c, mla_kv_norm_g[i], w_uk, *mla_tabs_ctx)
            v_l = rms_matmul(dn_l, 1, mla_kv_norm_g[i], w_uv, 1024)
            v_c = rms_matmul(dn_c, 1, mla_kv_norm_g[i], w_uv, 1024)
            att = dict(n_heads=MLA_HEADS, n_kv_heads=MLA_HEADS, dq=MLA_QK, dv=hd)
            o_l = flash_attention(q_l, k_l, v_l, k_c, v_c, **att)
            xl_new = proj_postnorm([o_l], [w_o], xl, mod, LAT, 2, ln_g[l, 0], ln_b[l, 0])
            if not last:
                q_c = mla_q(dn_c, mla_q_norm_g[i], w_uq, *mla_tabs_ctx, scale)
                o_c = flash_attention(q_c, k_c, v_c, **att)
                xc = proj_postnorm([o_c], [w_o], xc, mod, CTX, 2, ln_g[l, 0], ln_b[l, 0])
            xl = xl_new

        w_router_t = moe_router[l].T
        segs = [(xl, LAT)] if last else [(xl, LAT), (xc, CTX)]
        routes = [moe_route(xs, mod, row, w_router_t) for xs, row in segs]
        ys = expert_ffn([r[0] for r in routes], [r[1] for r in routes], moe_w_gate, moe_w_up, moe_w_down, l)
        outs = [moe_combine_postnorm(y, *r[2], xs, mod, row, 5, ln_g[l, 1], ln_b[l, 1])
                for y, r, (xs, row) in zip(ys, routes, segs)]
        xl = outs[0]
        if not last:
            xc = outs[1]
    return xl[None]
```

```python
import functools
import math

import numpy as np
import jax
import jax.numpy as jnp
from jax import lax
from jax.experimental import pallas as pl
from jax.experimental.pallas import tpu as pltpu

F32 = jnp.float32
BF16 = jnp.bfloat16

D_MODEL = 2048
DEPTH = 2
GRID_W = 64
HEAD_DIM = 128
A_HEADS = D_MODEL // 256
A_WIDTH = A_HEADS * HEAD_DIM
B_Q_HEADS = D_MODEL // 256
B_KV_HEADS = 2
B_WIDTH = B_Q_HEADS * HEAD_DIM
B_KV_WIDTH = B_KV_HEADS * HEAD_DIM
MLA_HEADS = D_MODEL // 128
MLA_Q_LORA = 512
MLA_KV_LORA = 512
MLA_ROPE = 64
MLA_QK = 2 * HEAD_DIM
N_EXPERTS = 16
EXPERT_FF = D_MODEL // 2
EC_CAPACITY_FACTOR = 2
ROPE_THETA = 10000.0
NORM_EPS = 1e-6
DEEPNORM_ALPHA = (2.0 * DEPTH) ** 0.25

HGRN_CHUNK = 128
HGRN_SMALL_LEVELS = 3
GATHER_UNROLL = 8
COMBINE_WINDOW = 64
LANES = 128
LOG2E = math.log2(math.e)
V7X_VMEM_BYTES = 64 * 1024 * 1024
VMEM_CAP_BYTES = V7X_VMEM_BYTES - 8 * 1024 * 1024


def _params(semantics, vmem_estimate_bytes):
    limit = int(min(max(2 * vmem_estimate_bytes, 32 * 1024 * 1024), VMEM_CAP_BYTES))
    return pltpu.CompilerParams(dimension_semantics=semantics, vmem_limit_bytes=limit)


def _layer_norm(x):
    mu = jnp.mean(x, axis=-1, keepdims=True)
    xc = x - mu
    var = jnp.mean(xc * xc, axis=-1, keepdims=True)
    return xc * lax.rsqrt(var + NORM_EPS)


def _rms(x):
    return x * lax.rsqrt(jnp.mean(x * x, axis=-1, keepdims=True) + NORM_EPS)


def _dot(a, b):
    return jnp.dot(a, b, preferred_element_type=F32)


def _dot_nt(a, b):
    return lax.dot_general(a, b, (((1,), (1,)), ((), ())), preferred_element_type=F32)


def _dot_tn(a, b):
    return lax.dot_general(a, b, (((0,), (0,)), ((), ())), preferred_element_type=F32)


def _split3(x):
    x1 = x.astype(BF16)
    r1 = x - x1.astype(F32)
    x2 = r1.astype(BF16)
    x3 = (r1 - x2.astype(F32)).astype(BF16)
    return x1, x2, x3


def _adaln_kernel(c_ref, w_ref, b_ref, o_ref):
    c = c_ref[...]
    s = c * jax.nn.sigmoid(c)
    w = w_ref[...]
    s1, s2, s3 = _split3(s)
    w1 = w.astype(BF16)
    w2 = (w - w1.astype(F32)).astype(BF16)
    acc = (_dot(s3, w1) + _dot(s2, w2)) + (_dot(s1, w2) + _dot(s2, w1))
    o_ref[...] = (acc + _dot(s1, w1)) + b_ref[...]


def adaln(cc, w, b):
    d, n = w.shape
    tn = 1536 if n % 1536 == 0 else n
    est = 2 * d * tn * 4 * 2
    return pl.pallas_call(
        _adaln_kernel,
        out_shape=jax.ShapeDtypeStruct((8, n), F32),
        grid=(n // tn,),
        in_specs=[pl.BlockSpec((8, d), lambda j: (0, 0)),
                  pl.BlockSpec((d, tn), lambda j: (0, j)),
                  pl.BlockSpec((1, tn), lambda j: (0, j))],
        out_specs=pl.BlockSpec((8, tn), lambda j: (0, j)),
        compiler_params=_params(("parallel",), est),
        name="adaln",
    )(cc, w, b.reshape(1, n))


def _lnmod_mm_kernel(x_ref, sh_ref, sc_ref, w_ref, o_ref, h_ref, *, row):
    @pl.when(pl.program_id(1) == 0)
    def _():
        hn = _layer_norm(x_ref[...])
        h = hn * (1.0 + sc_ref[row:row + 1, :]) + sh_ref[row:row + 1, :]
        h_ref[...] = h.astype(BF16)

    o_ref[...] = _dot(h_ref[...], w_ref[...]).astype(o_ref.dtype)


def lnmod_matmul(x, mod, row, k_shift, k_scale, w, tn):
    m, d = x.shape
    n = w.shape[1]
    tm = min(m, 1024)
    est = 2 * tm * d * 4 + tm * d * 2 + 2 * d * tn * 2 + 2 * tm * tn * 4
    return pl.pallas_call(
        functools.partial(_lnmod_mm_kernel, row=row),
        out_shape=jax.ShapeDtypeStruct((m, n), F32),
        grid=(m // tm, n // tn),
        in_specs=[pl.BlockSpec((tm, d), lambda i, j: (i, 0)),
                  pl.BlockSpec((8, d), lambda i, j: (0, k_shift)),
                  pl.BlockSpec((8, d), lambda i, j: (0, k_scale)),
                  pl.BlockSpec((d, tn), lambda i, j: (0, j))],
        out_specs=pl.BlockSpec((tm, tn), lambda i, j: (i, j)),
        scratch_shapes=[pltpu.VMEM((tm, d), BF16)],
        compiler_params=_params(("parallel", "arbitrary"), est),
        name="lnmod_matmul",
    )(x, mod, mod, w)


def _hgrn_tables(c):
    n_lvl = int(math.log2(c))
    r = np.arange(c)
    u = np.arange(c)[None, :]
    blocks, masks = [], []
    for l in range(n_lvl):
        half = 1 << l
        base = (r // (2 * half)) * (2 * half)
        anchor = (base + half - 1)[:, None]
        upper = (r >= base + half)[:, None]
        rr = r[:, None]
        if l < HGRN_SMALL_LEVELS:
            blocks.append(np.where(upper, (u > anchor) & (u <= rr), (u > rr) & (u <= anchor)))
        same = (r[:, None] // (2 * half)) == (r[None, :] // (2 * half))
        masks.append(same & upper & ~(upper.T))
    blocks.append(u <= r[:, None])
    blocks.append(np.ones((16, c), bool))
    masks.append(np.eye(c, dtype=bool))
    fwd_s = np.concatenate(blocks, axis=0).astype(np.float32)
    fwd_m = np.stack(masks).astype(np.float32)
    bwd_s = np.concatenate([b[::-1, ::-1] for b in blocks], axis=0).astype(np.float32)
    bwd_m = fwd_m[:, ::-1, ::-1]
    return (jnp.asarray(np.stack([fwd_s, bwd_s]), BF16), jnp.asarray(np.stack([fwd_m, bwd_m]), F32))


def _hgrn_kernel(q_ref, v_ref, f_ref, lb_ref, sums_ref, mask_ref, s0_ref, o_ref, sfin_ref, st_ref):
    c = q_ref.shape[0]
    hd = HEAD_DIM
    n_lvl = mask_ref.shape[0] - 1
    n_small = HGRN_SMALL_LEVELS
    forward = pl.program_id(0) == 0
    j = pl.program_id(1)

    @pl.when(j == 0)
    def _():
        st_ref[...] = s0_ref[...]

    def wide_level(cum, l):
        half = 1 << l
        parts = []
        for base in range(0, c, 2 * half):
            a = base + half - 1
            mid = jnp.where(forward, cum[a:a + 1, :], cum[a + 1:a + 2, :])
            parts.append(jnp.broadcast_to(mid, (2 * half, hd)))
        anchor = parts[0] if len(parts) == 1 else jnp.concatenate(parts, axis=0)
        return -jnp.abs(cum - anchor)

    for h in range(q_ref.shape[1] // hd):
        cols = slice(h * hd, (h + 1) * hd)
        q = q_ref[:, cols]
        vb = v_ref[:, cols].astype(BF16)
        lb = lb_ref[:, cols]
        f = lb + (1.0 - lb) * jax.nn.sigmoid(f_ref[:, cols])
        g = jnp.log(f)
        k = 1.0 - f
        g1 = g.astype(BF16)
        g2 = (g - g1.astype(F32)).astype(BF16)
        e2 = _dot(sums_ref[...], jnp.concatenate([g1, g2], axis=1))
        e = e2[:, hd:] + e2[:, :hd]
        cum = e[n_small * c:(n_small + 1) * c]
        tot = e[(n_small + 1) * c:(n_small + 1) * c + 1]
        rem = tot - cum

        scores = _dot_nt(q.astype(BF16), k.astype(BF16)) * mask_ref[n_lvl]
        for l in range(n_lvl):
            z = jnp.exp(e[l * c:(l + 1) * c] if l < n_small else wide_level(cum, l))
            scores = scores + _dot_nt((q * z).astype(BF16), (k * z).astype(BF16)) * mask_ref[l]

        st = st_ref[h]
        o = _dot(scores.astype(BF16), vb) + _dot_nt((q * jnp.exp(cum)).astype(BF16), st.astype(BF16))
        o_ref[:, cols] = o
        st_new = st * jnp.exp(tot) + _dot_tn(vb, (k * jnp.exp(rem)).astype(BF16))
        st_ref[h] = st_new

    @pl.when(j == pl.num_programs(1) - 1)
    def _():
        sfin_ref[...] = st_ref[...]


def hgrn_scan(proj, lb, s0):
    seq = proj.shape[0]
    c = HGRN_CHUNK
    nc = seq // c
    sums, masks = _hgrn_tables(c)
    hd, w = HEAD_DIM, A_WIDTH

    def blk(d, j):
        return jnp.where(d == 0, j, nc - 1 - j)

    est = (2 * (4 * c * w * 4 + sums.shape[1] * c * 2 + masks.shape[1] * c * c * 4 + 2 * A_HEADS * hd * hd * 4)
           + A_HEADS * hd * hd * 4)
    return pl.pallas_call(
        _hgrn_kernel,
        out_shape=(jax.ShapeDtypeStruct((2, seq, w), F32),
                   jax.ShapeDtypeStruct((2, A_HEADS, hd, hd), F32)),
        grid=(2, nc),
        in_specs=[pl.BlockSpec((c, w), lambda d, j: (blk(d, j), 0)),
                  pl.BlockSpec((c, w), lambda d, j: (blk(d, j), 3)),
                  pl.BlockSpec((c, w), lambda d, j: (blk(d, j), 1 + d)),
                  pl.BlockSpec((None, 1, w), lambda d, j: (d, 0, 0)),
                  pl.BlockSpec((None, sums.shape[1], c), lambda d, j: (d, 0, 0)),
                  pl.BlockSpec((None, masks.shape[1], c, c), lambda d, j: (d, 0, 0, 0)),
                  pl.BlockSpec((None, A_HEADS, hd, hd), lambda d, j: (d, 0, 0, 0))],
        out_specs=(pl.BlockSpec((None, c, w), lambda d, j: (d, blk(d, j), 0)),
                   pl.BlockSpec((None, A_HEADS, hd, hd), lambda d, j: (d, 0, 0, 0))),
        scratch_shapes=[pltpu.VMEM((A_HEADS, hd, hd), F32)],
        compiler_params=_params(("parallel", "arbitrary"), est),
        name="hgrn_scan",
    )(proj, proj, proj, lb, sums, masks, s0)


def _hgrn_out_kernel(o_ref, gate_ref, g_ref, a_ref):
    hd = HEAD_DIM
    for h in range(a_ref.shape[1] // hd):
        cols = slice(h * hd, (h + 1) * hd)
        o = o_ref[0, :, cols] + o_ref[1, :, cols]
        gate = gate_ref[:, cols]
        a_ref[:, cols] = (_rms(o) * g_ref[...] * (gate * jax.nn.sigmoid(gate))).astype(a_ref.dtype)


def hgrn_out(o, proj, norm_g):
    seq = o.shape[1]
    tm = min(seq, 256)
    hd, w = HEAD_DIM, A_WIDTH
    return pl.pallas_call(
        _hgrn_out_kernel,
        out_shape=jax.ShapeDtypeStruct((seq, w), BF16),
        grid=(seq // tm,),
        in_specs=[pl.BlockSpec((2, tm, w), lambda i: (0, i, 0)),
                  pl.BlockSpec((tm, w), lambda i: (i, 4)),
                  pl.BlockSpec((1, hd), lambda i: (0, 0))],
        out_specs=pl.BlockSpec((tm, w), lambda i: (i, 0)),
        compiler_params=_params(("parallel",), 8 * tm * w * 4),
        name="hgrn_out",
    )(o, proj, norm_g.reshape(1, hd))


def _norm_rope_kernel(x_ref, g_ref, cos_ref, sin_ref, o_ref, *, scale):
    hd = HEAD_DIM
    for h in range(o_ref.shape[1] // hd):
        cols = slice(h * hd, (h + 1) * hd)
        y = _rms(x_ref[:, cols]) * g_ref[...]
        y = y * cos_ref[...] + pltpu.roll(y, hd // 2, 1) * sin_ref[...]
        o_ref[:, cols] = (y * scale).astype(o_ref.dtype)


def norm_rope(proj, col0, n_heads, g, cos, sin, scale):
    seq = proj.shape[0]
    tm = min(seq, 256)
    hd = HEAD_DIM
    w = n_heads * hd
    return pl.pallas_call(
        functools.partial(_norm_rope_kernel, scale=scale),
        out_shape=jax.ShapeDtypeStruct((seq, w), BF16),
        grid=(seq // tm,),
        in_specs=[pl.BlockSpec((tm, w), lambda i: (i, col0 // w)),
                  pl.BlockSpec((1, hd), lambda i: (0, 0)),
                  pl.BlockSpec((tm, hd), lambda i: (i, 0)),
                  pl.BlockSpec((tm, hd), lambda i: (i, 0))],
        out_specs=pl.BlockSpec((tm, w), lambda i: (i, 0)),
        compiler_params=_params(("parallel",), 6 * tm * w * 4 + 4 * tm * hd * 4),
        name="norm_rope",
    )(proj, g.reshape(1, hd), cos, sin)


def _cast_kernel(x_ref, o_ref):
    o_ref[...] = x_ref[...].astype(o_ref.dtype)


def cast_columns(proj, col0, width):
    seq = proj.shape[0]
    tm = min(seq, 1024)
    return pl.pallas_call(
        _cast_kernel,
        out_shape=jax.ShapeDtypeStruct((seq, width), BF16),
        grid=(seq // tm,),
        in_specs=[pl.BlockSpec((tm, width), lambda i: (i, col0 // width))],
        out_specs=pl.BlockSpec((tm, width), lambda i: (i, 0)),
        compiler_params=_params(("parallel",), 4 * tm * width * 4),
        name="cast_columns",
    )(proj)


def _flash_update(q, k, v, m_ref, l_ref, acc_ref):
    s = _dot_nt(q, k)
    m_prev = m_ref[...]
    m_new = jnp.maximum(m_prev, jnp.max(s, axis=-1, keepdims=True))
    alpha = jnp.exp2(m_prev - m_new)
    ps = [jnp.exp2(s[:, c * LANES:(c + 1) * LANES] - m_new) for c in range(s.shape[1] // LANES)]
    psum = ps[0]
    for pc in ps[1:]:
        psum = psum + pc
    p = jnp.concatenate([pc.astype(BF16) for pc in ps], axis=1)
    l_ref[...] = alpha * l_ref[...] + psum
    acc_ref[...] = alpha * acc_ref[...] + _dot(p, v)
    m_ref[...] = m_new


def _flash_kernel(*refs, has_ctx):
    if has_ctx:
        q_ref, k_ref, v_ref, kc_ref, vc_ref, o_ref, m_ref, l_ref, acc_ref = refs
    else:
        q_ref, k_ref, v_ref, o_ref, m_ref, l_ref, acc_ref = refs
    j = pl.program_id(2)

    @pl.when(j == 0)
    def _():
        m_ref[...] = jnp.full(m_ref.shape, -jnp.inf, F32)
        l_ref[...] = jnp.zeros(l_ref.shape, F32)
        acc_ref[...] = jnp.zeros(acc_ref.shape, F32)
        if has_ctx:
            _flash_update(q_ref[...], kc_ref[...], vc_ref[...], m_ref, l_ref, acc_ref)

    _flash_update(q_ref[...], k_ref[...], v_ref[...], m_ref, l_ref, acc_ref)

    @pl.when(j == pl.num_programs(2) - 1)
    def _():
        l = jnp.sum(l_ref[...], axis=-1, keepdims=True)
        o_ref[...] = (acc_ref[...] / l).astype(o_ref.dtype)


def flash_attention(q, k, v, k_ctx=None, v_ctx=None, *, n_heads, n_kv_heads, dq, dv):
    n, m = q.shape[0], k.shape[0]
    grp = n_heads // n_kv_heads
    tq = min(n, 2048)
    tk = min(m, 2048)
    has_ctx = k_ctx is not None
    in_specs = [pl.BlockSpec((tq, dq), lambda h, i, j: (i, h)),
                pl.BlockSpec((tk, dq), lambda h, i, j: (j, h // grp)),
                pl.BlockSpec((tk, dv), lambda h, i, j: (j, h // grp))]
    args = [q, k, v]
    if has_ctx:
        mc = k_ctx.shape[0]
        in_specs += [pl.BlockSpec((mc, dq), lambda h, i, j: (0, h // grp)),
                     pl.BlockSpec((mc, dv), lambda h, i, j: (0, h // grp))]
        args += [k_ctx, v_ctx]
    est = 2 * (tq * dq + tk * dq + tk * dv + tq * dv) * 2 + tq * (dv + 256) * 4 + 6 * tq * tk * 4
    return pl.pallas_call(
        functools.partial(_flash_kernel, has_ctx=has_ctx),
        out_shape=jax.ShapeDtypeStruct((n, n_heads * dv), BF16),
        grid=(n_heads, n // tq, m // tk),
        in_specs=in_specs,
        out_specs=pl.BlockSpec((tq, dv), lambda h, i, j: (i, h)),
        scratch_shapes=[pltpu.VMEM((tq, LANES), F32), pltpu.VMEM((tq, LANES), F32), pltpu.VMEM((tq, dv), F32)],
        compiler_params=_params(("parallel", "parallel", "arbitrary"), est),
        name="flash_attention",
    )(*args)


def _proj_postnorm_kernel(*refs, n_in, row):
    a_refs = refs[:n_in]
    w_refs = refs[n_in:2 * n_in]
    x_ref, gate_ref, g_ref, b_ref, o_ref = refs[2 * n_in:]
    y = _dot(a_refs[0][...], w_refs[0][...])
    for a_ref, w_ref in zip(a_refs[1:], w_refs[1:]):
        y = y + _dot(a_ref[...], w_ref[...])
    z = DEEPNORM_ALPHA * x_ref[...] + gate_ref[row:row + 1, :] * y
    o_ref[...] = _layer_norm(z) * g_ref[...] + b_ref[...]


def proj_postnorm(acts, ws, x, mod, row, k_gate, g, b):
    m, d = x.shape
    tm = min(m, 512)
    n_in = len(acts)
    once = pl.Buffered(1)
    in_specs = [pl.BlockSpec((tm, a.shape[1]), lambda i: (i, 0)) for a in acts]
    in_specs += [pl.BlockSpec(w.shape, lambda i: (0, 0), pipeline_mode=once) for w in ws]
    in_specs += [pl.BlockSpec((tm, d), lambda i: (i, 0)),
                 pl.BlockSpec((8, d), lambda i: (0, k_gate)),
                 pl.BlockSpec((1, d), lambda i: (0, 0)),
                 pl.BlockSpec((1, d), lambda i: (0, 0))]
    est = sum(w.size * 2 for w in ws) + sum(2 * tm * a.shape[1] * 2 for a in acts) + 6 * tm * d * 4
    return pl.pallas_call(
        functools.partial(_proj_postnorm_kernel, n_in=n_in, row=row),
        out_shape=jax.ShapeDtypeStruct((m, d), F32),
        grid=(m // tm,),
        in_specs=in_specs,
        out_specs=pl.BlockSpec((tm, d), lambda i: (i, 0)),
        compiler_params=_params(("parallel",), est),
        name="proj_postnorm",
    )(*acts, *ws, x, mod, g.reshape(1, d), b.reshape(1, d))


def _add_postnorm_kernel(y_ref, x_ref, gate_ref, g_ref, b_ref, o_ref, *, row):
    z = DEEPNORM_ALPHA * x_ref[...] + gate_ref[row:row + 1, :] * y_ref[...]
    o_ref[...] = _layer_norm(z) * g_ref[...] + b_ref[...]


def add_postnorm(y, x, mod, row, k_gate, g, b):
    m, d = x.shape
    tm = min(m, 512)
    return pl.pallas_call(
        functools.partial(_add_postnorm_kernel, row=row),
        out_shape=jax.ShapeDtypeStruct((m, d), F32),
        grid=(m // tm,),
        in_specs=[pl.BlockSpec((tm, d), lambda i: (i, 0)),
                  pl.BlockSpec((tm, d), lambda i: (i, 0)),
                  pl.BlockSpec((8, d), lambda i: (0, k_gate)),
                  pl.BlockSpec((1, d), lambda i: (0, 0)),
                  pl.BlockSpec((1, d), lambda i: (0, 0))],
        out_specs=pl.BlockSpec((tm, d), lambda i: (i, 0)),
        compiler_params=_params(("parallel",), 8 * tm * d * 4),
        name="add_postnorm",
    )(y, x, mod, g.reshape(1, d), b.reshape(1, d))


def _rms_mm_kernel(x_ref, g_ref, w_ref, o_ref, a_ref):
    @pl.when(pl.program_id(1) == 0)
    def _():
        a_ref[...] = (_rms(x_ref[...]) * g_ref[...]).astype(BF16)

    o_ref[...] = _dot(a_ref[...], w_ref[...]).astype(o_ref.dtype)


def _mla_q_kernel(x_ref, g_ref, w_ref, cos_ref, sa_ref, sb_ref, o_ref, a_ref, *, scale):
    @pl.when(pl.program_id(1) == 0)
    def _():
        a_ref[...] = (_rms(x_ref[...]) * g_ref[...]).astype(BF16)

    y = _dot(a_ref[...], w_ref[...])
    hd = HEAD_DIM
    for h in range(y.shape[1] // MLA_QK):
        c0 = h * MLA_QK
        o_ref[:, c0:c0 + hd] = (y[:, c0:c0 + hd] * scale).astype(o_ref.dtype)
        r = y[:, c0 + hd:c0 + 2 * hd]
        r = r * cos_ref[...] + pltpu.roll(r, hd - MLA_ROPE // 2, 1) * sa_ref[...] + pltpu.roll(r, MLA_ROPE // 2, 1) * sb_ref[...]
        o_ref[:, c0 + hd:c0 + 2 * hd] = (r * scale).astype(o_ref.dtype)


def _mla_k_kernel(x_ref, g_ref, w_ref, kr_ref, cos_ref, sa_ref, sb_ref, o_ref, a_ref, r_ref):
    hd = HEAD_DIM

    @pl.when(pl.program_id(1) == 0)
    def _():
        a_ref[...] = (_rms(x_ref[...]) * g_ref[...]).astype(BF16)
        r = kr_ref[...]
        r = r * cos_ref[...] + pltpu.roll(r, hd - MLA_ROPE // 2, 1) * sa_ref[...] + pltpu.roll(r, MLA_ROPE // 2, 1) * sb_ref[...]
        r_ref[...] = r.astype(BF16)

    y = _dot(a_ref[...], w_ref[...])
    for h in range(y.shape[1] // hd):
        o_ref[:, h * MLA_QK:h * MLA_QK + hd] = y[:, h * hd:(h + 1) * hd].astype(o_ref.dtype)
        o_ref[:, h * MLA_QK + hd:(h + 1) * MLA_QK] = r_ref[...]


def _mla_specs(m, tm, lora, col_blk):
    return [pl.BlockSpec((tm, lora), lambda i, j: (i, col_blk)),
            pl.BlockSpec((1, lora), lambda i, j: (0, 0))]


def rms_matmul(dn, col_blk, g, w, tn):
    m = dn.shape[0]
    lora, n = w.shape
    tm = min(m, 1024)
    est = 2 * tm * lora * 4 + tm * lora * 2 + 2 * lora * tn * 2 + 2 * tm * tn * 2 + tm * tn * 4
    return pl.pallas_call(
        _rms_mm_kernel,
        out_shape=jax.ShapeDtypeStruct((m, n), BF16),
        grid=(m // tm, n // tn),
        in_specs=_mla_specs(m, tm, lora, col_blk) + [pl.BlockSpec((lora, tn), lambda i, j: (0, j))],
        out_specs=pl.BlockSpec((tm, tn), lambda i, j: (i, j)),
        scratch_shapes=[pltpu.VMEM((tm, lora), BF16)],
        compiler_params=_params(("parallel", "arbitrary"), est),
        name="rms_matmul",
    )(dn, g.reshape(1, lora), w)


def mla_q(dn, g, w, cos, sa, sb, scale):
    m = dn.shape[0]
    lora, n = w.shape
    tm = min(m, 1024)
    tn = 4 * MLA_QK
    hd = HEAD_DIM
    est = 2 * tm * lora * 4 + tm * lora * 2 + 2 * lora * tn * 2 + 2 * tm * tn * 2 + 2 * tm * tn * 4 + 6 * tm * hd * 4
    rope_spec = pl.BlockSpec((tm, hd), lambda i, j: (i, 0))
    return pl.pallas_call(
        functools.partial(_mla_q_kernel, scale=scale),
        out_shape=jax.ShapeDtypeStruct((m, n), BF16),
        grid=(m // tm, n // tn),
        in_specs=_mla_specs(m, tm, lora, 0) + [pl.BlockSpec((lora, tn), lambda i, j: (0, j)),
                                               rope_spec, rope_spec, rope_spec],
        out_specs=pl.BlockSpec((tm, tn), lambda i, j: (i, j)),
        scratch_shapes=[pltpu.VMEM((tm, lora), BF16)],
        compiler_params=_params(("parallel", "arbitrary"), est),
        name="mla_q",
    )(dn, g.reshape(1, lora), w, cos, sa, sb)


def mla_k(dn, g, w, cos, sa, sb):
    m = dn.shape[0]
    lora, n = w.shape
    tm = min(m, 1024)
    hd = HEAD_DIM
    tn = 4 * hd
    kr_blk = (MLA_Q_LORA + MLA_KV_LORA) // hd
    est = 2 * tm * lora * 4 + tm * lora * 2 + 2 * lora * tn * 2 + 4 * tm * tn * 2 + tm * tn * 4 + 8 * tm * hd * 4
    rope_spec = pl.BlockSpec((tm, hd), lambda i, j: (i, 0))
    return pl.pallas_call(
        _mla_k_kernel,
        out_shape=jax.ShapeDtypeStruct((m, 2 * n), BF16),
        grid=(m // tm, n // tn),
        in_specs=_mla_specs(m, tm, lora, 1) + [pl.BlockSpec((lora, tn), lambda i, j: (0, j)),
                                               pl.BlockSpec((tm, hd), lambda i, j: (i, kr_blk)),
                                               rope_spec, rope_spec, rope_spec],
        out_specs=pl.BlockSpec((tm, 2 * tn), lambda i, j: (i, j)),
        scratch_shapes=[pltpu.VMEM((tm, lora), BF16), pltpu.VMEM((tm, hd), BF16)],
        compiler_params=_params(("parallel", "arbitrary"), est),
        name="mla_k",
    )(dn, g.reshape(1, lora), w, dn, cos, sa, sb)


def _router_kernel(x_ref, sh_ref, sc_ref, wr_ref, h_ref, aff_ref, *, row):
    hn = _layer_norm(x_ref[...])
    h = hn * (1.0 + sc_ref[row:row + 1, :]) + sh_ref[row:row + 1, :]
    hb = h.astype(BF16)
    half = h.shape[1] // 2
    bits = pltpu.bitcast(hb.astype(F32), jnp.uint32)
    h_ref[...] = (bits[:, half:] & jnp.uint32(0xFFFF0000)) | (bits[:, :half] >> 16)
    w = wr_ref[...]
    w1 = w.astype(BF16)
    w2 = (w - w1.astype(F32)).astype(BF16)
    h2 = (h - hb.astype(F32)).astype(BF16)
    logits = _dot_nt(w1, hb) + (_dot_nt(w2, hb) + _dot_nt(w1, h2))
    mx = jnp.max(logits, axis=0, keepdims=True)
    p = jnp.exp(logits - mx)
    aff_ref[...] = p / jnp.sum(p, axis=0, keepdims=True)


def moe_router(x, mod, row, k_shift, k_scale, w_router_t):
    m, d = x.shape
    e = w_router_t.shape[0]
    tm = min(m, 512)
    return pl.pallas_call(
        functools.partial(_router_kernel, row=row),
        out_shape=(jax.ShapeDtypeStruct((m, d // 2), jnp.uint32), jax.ShapeDtypeStruct((e, m), F32)),
        grid=(m // tm,),
        in_specs=[pl.BlockSpec((tm, d), lambda i: (i, 0)),
                  pl.BlockSpec((8, d), lambda i: (0, k_shift)),
                  pl.BlockSpec((8, d), lambda i: (0, k_scale)),
                  pl.BlockSpec((e, d), lambda i: (0, 0))],
        out_specs=(pl.BlockSpec((tm, d // 2), lambda i: (i, 0)), pl.BlockSpec((e, tm), lambda i: (0, i))),
        compiler_params=_params(("parallel",), 8 * tm * d * 4),
        name="moe_router",
    )(x, mod, mod, w_router_t)


def _ffn_up_kernel(*refs, n_seg):
    x_refs, (wg_ref, wu_ref) = refs[:n_seg], refs[n_seg:n_seg + 2]
    o_refs, xs_refs = refs[n_seg + 2:2 * n_seg + 2], refs[2 * n_seg + 2:]

    @pl.when(pl.program_id(1) == 0)
    def _():
        for x_ref, xs_ref in zip(x_refs, xs_refs):
            word = x_ref[...]
            first = pltpu.bitcast(word << 16, F32)
            second = pltpu.bitcast(word & jnp.uint32(0xFFFF0000), F32)
            xs_ref[...] = jnp.concatenate([first, second], axis=1).astype(BF16)

    wg = wg_ref[...].astype(BF16)
    wu = wu_ref[...].astype(BF16)
    for xs_ref, o_ref in zip(xs_refs, o_refs):
        x = xs_ref[...]
        g = _dot(x, wg)
        u = _dot(x, wu)
        o_ref[...] = (g * jax.nn.sigmoid(g) * u).astype(o_ref.dtype)


def _ffn_down_kernel(*refs, n_seg):
    h_refs, wd_ref, wt_refs, o_refs = refs[:n_seg], refs[n_seg], refs[n_seg + 1:2 * n_seg + 1], refs[2 * n_seg + 1:]
    wd = wd_ref[...].astype(BF16)
    for h_ref, wt_ref, o_ref in zip(h_refs, wt_refs, o_refs):
        y = _dot(h_ref[...], wd) * wt_ref[...]
        hi = y.astype(BF16)
        o_ref[0] = hi
        o_ref[1] = (y - hi.astype(F32)).astype(BF16)


def expert_ffn(xgs, wts, w_gate, w_up, w_down, layer):
    n_seg = len(xgs)
    e = xgs[0].shape[0]
    d = 2 * xgs[0].shape[2]
    f = w_gate.shape[3]
    rs = [x.shape[1] for x in xgs]
    r = sum(rs)
    tf = min(f, 256)
    est = 2 * (r * d * 2 + 2 * d * tf * 4 + r * tf * 2) + r * d * 2 + 2 * d * tf * 2 + 3 * r * tf * 4
    hids = pl.pallas_call(
        functools.partial(_ffn_up_kernel, n_seg=n_seg),
        out_shape=[jax.ShapeDtypeStruct((e, ri, f), BF16) for ri in rs],
        grid=(e, f // tf),
        in_specs=[pl.BlockSpec((None, ri, d // 2), lambda i, j: (i, 0, 0)) for ri in rs]
        + [pl.BlockSpec((None, None, d, tf), lambda i, j: (layer, i, 0, j))] * 2,
        out_specs=[pl.BlockSpec((None, ri, tf), lambda i, j: (i, 0, j)) for ri in rs],
        scratch_shapes=[pltpu.VMEM((ri, d), BF16) for ri in rs],
        compiler_params=_params(("parallel", "arbitrary"), est),
        name="ffn_up",
    )(*xgs, w_gate, w_up)
    tn = min(d, 512)
    est = 2 * (r * f * 2 + f * tn * 4 + r * tn * 4 + r * LANES * 4) + f * tn * 2 + 2 * r * tn * 4
    return pl.pallas_call(
        functools.partial(_ffn_down_kernel, n_seg=n_seg),
        out_shape=[jax.ShapeDtypeStruct((e, 2, ri, d), BF16) for ri in rs],
        grid=(e, d // tn),
        in_specs=[pl.BlockSpec((None, ri, f), lambda i, j: (i, 0, 0)) for ri in rs]
        + [pl.BlockSpec((None, None, f, tn), lambda i, j: (layer, i, 0, j))]
        + [pl.BlockSpec((None, ri, 1), lambda i, j: (i, 0, 0)) for ri in rs],
        out_specs=[pl.BlockSpec((None, 2, ri, tn), lambda i, j: (i, 0, 0, j)) for ri in rs],
        compiler_params=_params(("parallel", "arbitrary"), est),
        name="ffn_down",
    )(*hids, w_down, *wts)


def _select_kernel(aff_ref, pos_ref, idx_ref, wt_ref, off_ref, incl_ref, tot_ref, offs_ref, *, cap):
    e, g, ln = aff_ref.shape
    bits = pltpu.bitcast(aff_ref[...], jnp.int32)

    def count(mask):
        per_lane = jnp.sum(jnp.where(mask, 1.0, 0.0), axis=1)
        return jnp.sum(per_lane, axis=1, keepdims=True)[:, :, None]

    def search(i, t):
        cand = t | jnp.left_shift(jnp.int32(1), 30 - i)
        return jnp.where(count(bits >= cand) >= cap, cand, t)

    thr = lax.fori_loop(0, 31, search, jnp.zeros((e, 1, 1), jnp.int32))
    gt = bits > thr
    eq = bits == thr
    need = cap - count(gt)

    r0 = lax.broadcasted_iota(jnp.int32, (ln, ln), 0)
    r1 = lax.broadcasted_iota(jnp.int32, (ln, ln), 1)
    upper = jnp.where(r0 <= r1, 1.0, 0.0).astype(BF16)
    ones = jnp.ones((ln, ln), BF16)
    g0 = lax.broadcasted_iota(jnp.int32, (g, g), 0)
    g1 = lax.broadcasted_iota(jnp.int32, (g, g), 1)
    earlier = jnp.where(g1 < g0, 1.0, 0.0).astype(BF16)

    def prefix(mask):
        x = jnp.where(mask, 1.0, 0.0).astype(BF16).reshape(e * g, ln)
        incl = _dot(x, upper).reshape(e, g, ln)
        tot = _dot(x, ones).reshape(e, g, ln)
        off = jnp.stack([_dot(earlier, tot[i].astype(BF16)) for i in range(e)])
        return incl, tot, off

    incl_eq, _, off_eq = prefix(eq)
    sel = gt | (eq & (off_eq + incl_eq - 1.0 < need))
    incl, tot, off = prefix(sel)
    pos_ref[...] = jnp.where(sel, off + incl - 1.0, -1.0).astype(jnp.int32)
    off_ref[...] = off.astype(jnp.int32)
    incl_ref[...] = incl
    tot_ref[...] = tot
    offs_ref[...] = off

    slot = lax.broadcasted_iota(jnp.int32, (cap, ln), 0).astype(F32)
    lane = lax.broadcasted_iota(jnp.int32, (cap, ln), 1).astype(F32)

    def tokens_of_slots(i, c):
        ends = jnp.transpose(offs_ref[i] + tot_ref[i])[0:1, :]
        before = jnp.where(ends <= slot, 1.0, 0.0).astype(BF16)
        grp = _dot(before, ones)
        rank = slot - _dot(before, tot_ref[i].astype(BF16))
        pick = jnp.where(lane == grp, 1.0, 0.0).astype(BF16)
        incl_g = _dot(pick, incl_ref[i].astype(BF16))
        lane_p = _dot(jnp.where(incl_g <= rank, 1.0, 0.0).astype(BF16), ones)
        idx_ref[i] = (grp * ln + lane_p)[:, 0:1].astype(jnp.int32)
        a1, a2, a3 = _split3(aff_ref[i])
        aff_g = (_dot(pick, a3) + _dot(pick, a2)) + _dot(pick, a1)
        wt_ref[i] = jnp.sum(jnp.where(lane == lane_p, aff_g, 0.0), axis=1, keepdims=True)
        return c

    lax.fori_loop(0, e, tokens_of_slots, 0)


def moe_select(aff_t, cap):
    e, n = aff_t.shape
    g = n // LANES
    assert g <= LANES, "token groups are mapped onto the 128 lanes"
    a = aff_t.reshape(e, g, LANES)
    if g != LANES:
        a = jnp.concatenate([a, jnp.full((e, LANES - g, LANES), -1.0, F32)], axis=1)
    shp = jax.ShapeDtypeStruct((e, LANES, LANES), jnp.int32)
    full = pl.BlockSpec((e, LANES, LANES), lambda i: (0, 0, 0))
    per_slot = pl.BlockSpec((e, cap, 1), lambda i: (0, 0, 0))
    pos, idx, wt, off = pl.pallas_call(
        functools.partial(_select_kernel, cap=cap),
        out_shape=(shp, jax.ShapeDtypeStruct((e, cap, 1), jnp.int32), jax.ShapeDtypeStruct((e, cap, 1), F32), shp),
        grid=(1,),
        in_specs=[full],
        out_specs=(full, per_slot, per_slot, full),
        scratch_shapes=[pltpu.VMEM((e, LANES, LANES), F32)] * 3,
        compiler_params=_params(("arbitrary",), 28 * e * LANES * LANES * 4 + 16 * cap * LANES * 4),
        name="moe_select",
    )(a)
    return pos.reshape(e, LANES * LANES)[:, :n], idx.reshape(e, 1, cap), wt, off[:, :g, 0]


def _row_copy(h_hbm, x_ref, sem, token, row):
    return pltpu.make_async_copy(h_hbm.at[pl.ds(token, 1), :], x_ref.at[0, pl.ds(row, 1), :], sem)


def _gather_kernel(idx_ref, h_hbm, x_ref, sem):
    cap = x_ref.shape[1]

    def rows(i, c):
        for u in range(GATHER_UNROLL):
            s = i * GATHER_UNROLL + u
            _row_copy(h_hbm, x_ref, sem, idx_ref[0, s], s).start()
        return c

    lax.fori_loop(0, cap // GATHER_UNROLL, rows, 0)
    pltpu.make_async_copy(h_hbm.at[pl.ds(0, cap), :], x_ref.at[0], sem).wait()


def moe_gather(hp, idx):
    w = hp.shape[1]
    e, _, cap = idx.shape
    return pl.pallas_call(
        _gather_kernel,
        out_shape=jax.ShapeDtypeStruct((e, cap, w), jnp.uint32),
        grid=(e,),
        in_specs=[pl.BlockSpec((None, 1, cap), lambda i: (i, 0, 0), memory_space=pltpu.SMEM),
                  pl.BlockSpec(memory_space=pl.ANY)],
        out_specs=pl.BlockSpec((1, cap, w), lambda i: (i, 0, 0)),
        scratch_shapes=[pltpu.SemaphoreType.DMA(())],
        compiler_params=_params(("arbitrary",), 2 * cap * w * 4),
        name="moe_gather",
    )(idx, hp)


def _window_copy(y_hbm, dst, sem, e, src, win):
    return pltpu.make_async_copy(y_hbm.at[e, :, pl.ds(src, win)], dst, sem)


def _combine_kernel(offb_ref, y_hbm, pos_ref, x_ref, gate_ref, g_ref, b_ref, o_ref, ybuf, ybuf_x, sem, acc_ref,
                    *, row, cap, win):
    b = pl.program_id(0)
    n_exp, tb = pos_ref.shape
    half = n_exp // 2
    par = b % 2

    def window(e, k, blk=b):
        first = (offb_ref[e, blk] // 8) * 8 + k * win
        return first, pl.multiple_of(jnp.minimum(first, cap - win), 8)

    def onehot2(e, first, src):
        slots = src + lax.broadcasted_iota(jnp.int32, (win, 1), 0)
        hit = jnp.logical_and(pos_ref[e:e + 1, :] == slots, slots >= first)
        oh = jnp.where(hit, 1.0, 0.0).astype(BF16)
        return jnp.concatenate([oh, oh], axis=0)

    def copies(h, blk, p):
        return [_window_copy(y_hbm, ybuf.at[p, h, j], sem.at[2 * p + h], h * half + j,
                             window(h * half + j, 0, blk)[1], win) for j in range(half)]

    def start_block(blk, p):
        for h in range(2):
            for cp in copies(h, blk, p):
                cp.start()

    @pl.when(b == 0)
    def _():
        start_block(b, par)

    @pl.when(b + 1 < pl.num_programs(0))
    def _():
        start_block(b + 1, 1 - par)

    acc = None
    for h in range(2):
        for cp in copies(h, b, par):
            cp.wait()
        lhs = jnp.concatenate([onehot2(h * half + j, *window(h * half + j, 0)) for j in range(half)], axis=0)
        part = _dot_tn(lhs, ybuf[par, h].reshape(half * 2 * win, ybuf.shape[-1]))
        acc = part if acc is None else acc + part
    acc_ref[...] = acc

    for e in range(n_exp):
        n_win = (offb_ref[e, b + 1] - (offb_ref[e, b] // 8) * 8 + win - 1) // win

        def extra(k, c, e=e):
            first_k, src_k = window(e, k)
            cp = _window_copy(y_hbm, ybuf_x, sem.at[4], e, src_k, win)
            cp.start()
            cp.wait()
            acc_ref[...] += _dot_tn(onehot2(e, first_k, src_k), ybuf_x[...].reshape(2 * win, ybuf_x.shape[-1]))
            return c

        lax.fori_loop(1, n_win, extra, 0)

    z = DEEPNORM_ALPHA * x_ref[...] + gate_ref[row:row + 1, :] * acc_ref[...]
    o_ref[...] = _layer_norm(z) * g_ref[...] + b_ref[...]


def moe_combine_postnorm(y, pos, off, x, mod, row, k_gate, g, b):
    n, d = x.shape
    e, _, cap, _ = y.shape
    tb = min(n, 256)
    nb = n // tb
    win = min(cap, COMBINE_WINDOW)
    offb = jnp.concatenate([off[:, ::tb // LANES], jnp.full((e, 1), cap, jnp.int32)], axis=1)
    grid_spec = pltpu.PrefetchScalarGridSpec(
        num_scalar_prefetch=1,
        grid=(nb,),
        in_specs=[pl.BlockSpec(memory_space=pl.ANY),
                  pl.BlockSpec((e, tb), lambda i, o: (0, i)),
                  pl.BlockSpec((tb, d), lambda i, o: (i, 0)),
                  pl.BlockSpec((8, d), lambda i, o: (0, k_gate)),
                  pl.BlockSpec((1, d), lambda i, o: (0, 0)),
                  pl.BlockSpec((1, d), lambda i, o: (0, 0))],
        out_specs=pl.BlockSpec((tb, d), lambda i, o: (i, 0)),
        scratch_shapes=[pltpu.VMEM((2, 2, e // 2, 2, win, d), BF16), pltpu.VMEM((2, win, d), BF16),
                        pltpu.SemaphoreType.DMA((5,)), pltpu.VMEM((tb, d), F32)],
    )
    est = 8 * tb * d * 4 + (2 * e + 1) * 2 * win * d * 2 + e * win * tb * 2
    return pl.pallas_call(
        functools.partial(_combine_kernel, row=row, cap=cap, win=win),
        out_shape=jax.ShapeDtypeStruct((n, d), F32),
        grid_spec=grid_spec,
        compiler_params=_params(("arbitrary",), est),
        name="moe_combine",
    )(offb, y, pos, x, mod, g.reshape(1, d), b.reshape(1, d))


def moe_route(x, mod, row, w_router_t):
    m = x.shape[0]
    cap = max(1, EC_CAPACITY_FACTOR * m // N_EXPERTS)
    h, aff_t = moe_router(x, mod, row, 3, 4, w_router_t)
    pos, idx, wt, off = moe_select(aff_t, cap)
    return moe_gather(h, idx), wt, (pos, off)


def _rope_angles(n_tokens, rot_dim):
    rows = n_tokens // GRID_W
    row = jnp.repeat(jnp.arange(rows, dtype=F32), GRID_W)
    col = jnp.tile(jnp.arange(GRID_W, dtype=F32), rows)
    n_freq = rot_dim // 4
    inv = ROPE_THETA ** (-jnp.arange(n_freq, dtype=F32) / n_freq)
    return jnp.concatenate([row[:, None] * inv, col[:, None] * inv], axis=-1)


def _gqa_rope_tables(n_tokens):
    ang = _rope_angles(n_tokens, HEAD_DIM)
    c, s = jnp.cos(ang), jnp.sin(ang)
    return jnp.concatenate([c, c], axis=-1), jnp.concatenate([-s, s], axis=-1)


def _mla_rope_tables(n_tokens):
    ang = _rope_angles(n_tokens, MLA_ROPE)
    c, s = jnp.cos(ang), jnp.sin(ang)
    z = jnp.zeros_like(c)
    cos = jnp.concatenate([c, c, z, z], axis=-1)
    sa = jnp.concatenate([-s, z, z, z], axis=-1)
    sb = jnp.concatenate([z, s, z, z], axis=-1)
    return cos, sa, sb


def kernel(x, c, ctx, c_ctx, ada_w, ada_b, ln_g, ln_b, ev_w_in, ev_w_out, hgrn_lb, hgrn_norm_g, gqa_q_norm_g, gqa_k_norm_g, mla_w_down, mla_q_norm_g, mla_kv_norm_g, mla_w_uq, mla_w_ukv, mla_w_o, moe_router, moe_w_gate, moe_w_up, moe_w_down):
    d = D_MODEL
    xl = x[0]
    xc = ctx[0]
    n_lat, n_ctx = xl.shape[0], xc.shape[0]
    cc = jnp.zeros((8, d), F32).at[0].set(c[0]).at[1].set(c_ctx)
    lb_all = jnp.cumsum(jax.nn.softmax(hgrn_lb.astype(F32), axis=1), axis=1)
    gqa_tabs = _gqa_rope_tables(n_lat)
    gqa_tabs_ctx = [jnp.ones((n_ctx, HEAD_DIM), F32), jnp.zeros((n_ctx, HEAD_DIM), F32)]
    mla_tabs = _mla_rope_tables(n_lat)
    mla_tabs_ctx = [jnp.ones((n_ctx, HEAD_DIM), F32), jnp.zeros((n_ctx, HEAD_DIM), F32), jnp.zeros((n_ctx, HEAD_DIM), F32)]
    LAT, CTX = 0, 1

    for l in range(DEPTH):
        last = l == DEPTH - 1
        i = l // 2
        mod = adaln(cc, ada_w[l], ada_b[l])
        if l % 2 == 0:
            w_in = ev_w_in[i].astype(BF16)
            w_out = ev_w_out[i].astype(BF16)
            lb = lb_all[:, l].reshape(2, 1, A_WIDTH)
            scale = HEAD_DIM ** -0.5 * LOG2E
            proj_c = lnmod_matmul(xc, mod, CTX, 0, 1, w_in, 512)
            proj_l = lnmod_matmul(xl, mod, LAT, 0, 1, w_in, 512)
            s0 = jnp.zeros((2, A_HEADS, HEAD_DIM, HEAD_DIM), F32)
            o_c, s_c = hgrn_scan(proj_c, lb, s0)
            o_l, _ = hgrn_scan(proj_l, lb, s_c)
            a_l = hgrn_out(o_l, proj_l, hgrn_norm_g[i])
            qcol, kcol, vcol = 5 * A_WIDTH, 5 * A_WIDTH + B_WIDTH, 5 * A_WIDTH + B_WIDTH + B_KV_WIDTH
            q_l = norm_rope(proj_l, qcol, B_Q_HEADS, gqa_q_norm_g[i], *gqa_tabs, scale)
            k_l = norm_rope(proj_l, kcol, B_KV_HEADS, gqa_k_norm_g[i], *gqa_tabs, 1.0)
            k_c = norm_rope(proj_c, kcol, B_KV_HEADS, gqa_k_norm_g[i], *gqa_tabs_ctx, 1.0)
            v_l = cast_columns(proj_l, vcol, B_KV_WIDTH)
            v_c = cast_columns(proj_c, vcol, B_KV_WIDTH)
            att = dict(n_heads=B_Q_HEADS, n_kv_heads=B_KV_HEADS, dq=HEAD_DIM, dv=HEAD_DIM)
            b_l = flash_attention(q_l, k_l, v_l, k_c, v_c, **att)
            w_parts = [w_out[:A_WIDTH], w_out[A_WIDTH:]]
            xl_new = proj_postnorm([a_l, b_l], w_parts, xl, mod, LAT, 2, ln_g[l, 0], ln_b[l, 0])
            if not last:
                a_c = hgrn_out(o_c, proj_c, hgrn_norm_g[i])
                q_c = norm_rope(proj_c, qcol, B_Q_HEADS, gqa_q_norm_g[i], *gqa_tabs_ctx, scale)
                b_c = flash_attention(q_c, k_c, v_c, **att)
                xc = proj_postnorm([a_c, b_c], w_parts, xc, mod, CTX, 2, ln_g[l, 0], ln_b[l, 0])
            xl = xl_new
        else:
            hd = HEAD_DIM
            pad = (-mla_w_down.shape[2]) % hd
            w_down = jnp.pad(mla_w_down[i], ((0, 0), (0, pad))).astype(BF16)
            w_uq = mla_w_uq[i].reshape(MLA_Q_LORA, MLA_HEADS, hd + MLA_ROPE)
            w_uq = jnp.pad(w_uq, ((0, 0), (0, 0), (0, MLA_QK - hd - MLA_ROPE))).reshape(MLA_Q_LORA, MLA_HEADS * MLA_QK).astype(BF16)
            w_ukv = mla_w_ukv[i].reshape(MLA_KV_LORA, MLA_HEADS, 2 * hd)
            w_uk = w_ukv[:, :, :hd].reshape(MLA_KV_LORA, MLA_HEADS * hd).astype(BF16)
            w_uv = w_ukv[:, :, hd:].reshape(MLA_KV_LORA, MLA_HEADS * hd).astype(BF16)
            w_o = mla_w_o[i].astype(BF16)
            scale = (hd + MLA_ROPE) ** -0.5 * LOG2E
            dn_c = lnmod_matmul(xc, mod, CTX, 0, 1, w_down, w_down.shape[1])
            dn_l = lnmod_matmul(xl, mod, LAT, 0, 1, w_down, w_down.shape[1])
            q_l = mla_q(dn_l, mla_q_norm_g[i], w_uq, *mla_tabs, scale)
            k_l = mla_k(dn_l, mla_kv_norm_g[i], w_uk, *mla_tabs)
            k_c = mla_k(dn_c, mla_kv_norm_g[i], w_uk, *mla_tabs_ctx)
            v_l = rms_matmul(dn_l, 1, mla_kv_norm_g[i], w_uv, 1024)
            v_c = rms_matmul(dn_c, 1, mla_kv_norm_g[i], w_uv, 1024)
            att = dict(n_heads=MLA_HEADS, n_kv_heads=MLA_HEADS, dq=MLA_QK, dv=hd)
            o_l = flash_attention(q_l, k_l, v_l, k_c, v_c, **att)
            xl_new = proj_postnorm([o_l], [w_o], xl, mod, LAT, 2, ln_g[l, 0], ln_b[l, 0])
            if not last:
                q_c = mla_q(dn_c, mla_q_norm_g[i], w_uq, *mla_tabs_ctx, scale)
                o_c = flash_attention(q_c, k_c, v_c, **att)
                xc = proj_postnorm([o_c], [w_o], xc, mod, CTX, 2, ln_g[l, 0], ln_b[l, 0])
            xl = xl_new

        w_router_t = moe_router[l].T
        segs = [(xl, LAT)] if last else [(xl, LAT), (xc, CTX)]
        routes = [moe_route(xs, mod, row, w_router_t) for xs, row in segs]
        ys = expert_ffn([r[0] for r in routes], [r[1] for r in routes], moe_w_gate, moe_w_up, moe_w_down, l)
        outs = [moe_combine_postnorm(y, *r[2], xs, mod, row, 5, ln_g[l, 1], ln_b[l, 1])
                for y, r, (xs, row) in zip(ys, routes, segs)]
        xl = outs[0]
        if not last:
            xc = outs[1]
    return xl[None]
```

```python
import functools
import math

import numpy as np
import jax
import jax.numpy as jnp
from jax import lax
from jax.experimental import pallas as pl
from jax.experimental.pallas import tpu as pltpu

F32 = jnp.float32
BF16 = jnp.bfloat16

D_MODEL = 2048
DEPTH = 2
GRID_W = 64
HEAD_DIM = 128
A_HEADS = D_MODEL // 256
A_WIDTH = A_HEADS * HEAD_DIM
B_Q_HEADS = D_MODEL // 256
B_KV_HEADS = 2
B_WIDTH = B_Q_HEADS * HEAD_DIM
B_KV_WIDTH = B_KV_HEADS * HEAD_DIM
MLA_HEADS = D_MODEL // 128
MLA_Q_LORA = 512
MLA_KV_LORA = 512
MLA_ROPE = 64
MLA_QK = 2 * HEAD_DIM
N_EXPERTS = 16
EXPERT_FF = D_MODEL // 2
EC_CAPACITY_FACTOR = 2
ROPE_THETA = 10000.0
NORM_EPS = 1e-6
DEEPNORM_ALPHA = (2.0 * DEPTH) ** 0.25

HGRN_CHUNK = 128
HGRN_SMALL_LEVELS = 3
GATHER_UNROLL = 8
COMBINE_WINDOW = 64
LANES = 128
LOG2E = math.log2(math.e)
V7X_VMEM_BYTES = 64 * 1024 * 1024
VMEM_CAP_BYTES = V7X_VMEM_BYTES - 8 * 1024 * 1024


def _params(semantics, vmem_estimate_bytes):
    limit = int(min(max(2 * vmem_estimate_bytes, 32 * 1024 * 1024), VMEM_CAP_BYTES))
    return pltpu.CompilerParams(dimension_semantics=semantics, vmem_limit_bytes=limit)


def _layer_norm(x):
    mu = jnp.mean(x, axis=-1, keepdims=True)
    xc = x - mu
    var = jnp.mean(xc * xc, axis=-1, keepdims=True)
    return xc * lax.rsqrt(var + NORM_EPS)


def _rms(x):
    return x * lax.rsqrt(jnp.mean(x * x, axis=-1, keepdims=True) + NORM_EPS)


def _dot(a, b):
    return jnp.dot(a, b, preferred_element_type=F32)


def _dot_nt(a, b):
    return lax.dot_general(a, b, (((1,), (1,)), ((), ())), preferred_element_type=F32)


def _dot_tn(a, b):
    return lax.dot_general(a, b, (((0,), (0,)), ((), ())), preferred_element_type=F32)


def _split3(x):
    x1 = x.astype(BF16)
    r1 = x - x1.astype(F32)
    x2 = r1.astype(BF16)
    x3 = (r1 - x2.astype(F32)).astype(BF16)
    return x1, x2, x3


def _adaln_kernel(c_ref, w_ref, b_ref, o_ref):
    c = c_ref[...]
    s = c * jax.nn.sigmoid(c)
    w = w_ref[...]
    s1, s2, s3 = _split3(s)
    w1 = w.astype(BF16)
    w2 = (w - w1.astype(F32)).astype(BF16)
    acc = (_dot(s3, w1) + _dot(s2, w2)) + (_dot(s1, w2) + _dot(s2, w1))
    o_ref[...] = (acc + _dot(s1, w1)) + b_ref[...]


def adaln(cc, w, b, layer):
    _, d, n = w.shape
    tn = 1536 if n % 1536 == 0 else n
    est = 2 * d * tn * 4 * 2
    return pl.pallas_call(
        _adaln_kernel,
        out_shape=jax.ShapeDtypeStruct((8, n), F32),
        grid=(n // tn,),
        in_specs=[pl.BlockSpec((8, d), lambda j: (0, 0)),
                  pl.BlockSpec((None, d, tn), lambda j: (layer, 0, j)),
                  pl.BlockSpec((None, 1, tn), lambda j: (layer, 0, j))],
        out_specs=pl.BlockSpec((8, tn), lambda j: (0, j)),
        compiler_params=_params(("parallel",), est),
        name="adaln",
    )(cc, w, b.reshape(b.shape[0], 1, n))


def _lnmod_mm_kernel(x_ref, sh_ref, sc_ref, w_ref, o_ref, h_ref, *, row):
    @pl.when(pl.program_id(1) == 0)
    def _():
        hn = _layer_norm(x_ref[...])
        h = hn * (1.0 + sc_ref[row:row + 1, :]) + sh_ref[row:row + 1, :]
        h_ref[...] = h.astype(BF16)

    o_ref[...] = _dot(h_ref[...], w_ref[...]).astype(o_ref.dtype)


def lnmod_matmul(x, mod, row, k_shift, k_scale, w, tn):
    m, d = x.shape
    n = w.shape[1]
    tm = min(m, 1024)
    est = 2 * tm * d * 4 + tm * d * 2 + 2 * d * tn * 2 + 2 * tm * tn * 4
    return pl.pallas_call(
        functools.partial(_lnmod_mm_kernel, row=row),
        out_shape=jax.ShapeDtypeStruct((m, n), F32),
        grid=(m // tm, n // tn),
        in_specs=[pl.BlockSpec((tm, d), lambda i, j: (i, 0)),
                  pl.BlockSpec((8, d), lambda i, j: (0, k_shift)),
                  pl.BlockSpec((8, d), lambda i, j: (0, k_scale)),
                  pl.BlockSpec((d, tn), lambda i, j: (0, j))],
        out_specs=pl.BlockSpec((tm, tn), lambda i, j: (i, j)),
        scratch_shapes=[pltpu.VMEM((tm, d), BF16)],
        compiler_params=_params(("parallel", "arbitrary"), est),
        name="lnmod_matmul",
    )(x, mod, mod, w)


def _hgrn_tables(c):
    n_lvl = int(math.log2(c))
    r = np.arange(c)
    u = np.arange(c)[None, :]
    blocks, masks = [], []
    for l in range(n_lvl):
        half = 1 << l
        base = (r // (2 * half)) * (2 * half)
        anchor = (base + half - 1)[:, None]
        upper = (r >= base + half)[:, None]
        rr = r[:, None]
        if l < HGRN_SMALL_LEVELS:
            blocks.append(np.where(upper, (u > anchor) & (u <= rr), (u > rr) & (u <= anchor)))
        same = (r[:, None] // (2 * half)) == (r[None, :] // (2 * half))
        masks.append(same & upper & ~(upper.T))
    blocks.append(u <= r[:, None])
    blocks.append(np.ones((16, c), bool))
    masks.append(np.eye(c, dtype=bool))
    fwd_s = np.concatenate(blocks, axis=0).astype(np.float32)
    fwd_m = np.stack(masks).astype(np.float32)
    bwd_s = np.concatenate([b[::-1, ::-1] for b in blocks], axis=0).astype(np.float32)
    bwd_m = fwd_m[:, ::-1, ::-1]
    return (jnp.asarray(np.stack([fwd_s, bwd_s]), BF16), jnp.asarray(np.stack([fwd_m, bwd_m]), F32))


def _hgrn_kernel(q_ref, v_ref, f_ref, lb_ref, sums_ref, mask_ref, s0_ref, o_ref, sfin_ref, st_ref):
    c = q_ref.shape[0]
    hd = HEAD_DIM
    n_lvl = mask_ref.shape[0] - 1
    n_small = HGRN_SMALL_LEVELS
    forward = pl.program_id(0) == 0
    j = pl.program_id(1)

    @pl.when(j == 0)
    def _():
        st_ref[...] = s0_ref[...]

    def wide_level(cum, l):
        half = 1 << l
        parts = []
        for base in range(0, c, 2 * half):
            a = base + half - 1
            mid = jnp.where(forward, cum[a:a + 1, :], cum[a + 1:a + 2, :])
            parts.append(jnp.broadcast_to(mid, (2 * half, hd)))
        anchor = parts[0] if len(parts) == 1 else jnp.concatenate(parts, axis=0)
        return -jnp.abs(cum - anchor)

    for h in range(q_ref.shape[1] // hd):
        cols = slice(h * hd, (h + 1) * hd)
        q = q_ref[:, cols]
        vb = v_ref[:, cols].astype(BF16)
        lb = lb_ref[:, cols]
        f = lb + (1.0 - lb) * jax.nn.sigmoid(f_ref[:, cols])
        g = jnp.log(f)
        k = 1.0 - f
        g1 = g.astype(BF16)
        g2 = (g - g1.astype(F32)).astype(BF16)
        e2 = _dot(sums_ref[...], jnp.concatenate([g1, g2], axis=1))
        e = e2[:, hd:] + e2[:, :hd]
        cum = e[n_small * c:(n_small + 1) * c]
        tot = e[(n_small + 1) * c:(n_small + 1) * c + 1]
        rem = tot - cum

        scores = _dot_nt(q.astype(BF16), k.astype(BF16)) * mask_ref[n_lvl]
        for l in range(n_lvl):
            z = jnp.exp(e[l * c:(l + 1) * c] if l < n_small else wide_level(cum, l))
            scores = scores + _dot_nt((q * z).astype(BF16), (k * z).astype(BF16)) * mask_ref[l]

        st = st_ref[h]
        o = _dot(scores.astype(BF16), vb) + _dot_nt((q * jnp.exp(cum)).astype(BF16), st.astype(BF16))
        o_ref[:, cols] = o
        st_new = st * jnp.exp(tot) + _dot_tn(vb, (k * jnp.exp(rem)).astype(BF16))
        st_ref[h] = st_new

    @pl.when(j == pl.num_programs(1) - 1)
    def _():
        sfin_ref[...] = st_ref[...]


def hgrn_scan(proj, lb, s0):
    seq = proj.shape[0]
    c = HGRN_CHUNK
    nc = seq // c
    sums, masks = _hgrn_tables(c)
    hd, w = HEAD_DIM, A_WIDTH

    def blk(d, j):
        return jnp.where(d == 0, j, nc - 1 - j)

    est = (2 * (4 * c * w * 4 + sums.shape[1] * c * 2 + masks.shape[1] * c * c * 4 + 2 * A_HEADS * hd * hd * 4)
           + A_HEADS * hd * hd * 4)
    return pl.pallas_call(
        _hgrn_kernel,
        out_shape=(jax.ShapeDtypeStruct((2, seq, w), F32),
                   jax.ShapeDtypeStruct((2, A_HEADS, hd, hd), F32)),
        grid=(2, nc),
        in_specs=[pl.BlockSpec((c, w), lambda d, j: (blk(d, j), 0)),
                  pl.BlockSpec((c, w), lambda d, j: (blk(d, j), 3)),
                  pl.BlockSpec((c, w), lambda d, j: (blk(d, j), 1 + d)),
                  pl.BlockSpec((None, 1, w), lambda d, j: (d, 0, 0)),
                  pl.BlockSpec((None, sums.shape[1], c), lambda d, j: (d, 0, 0)),
                  pl.BlockSpec((None, masks.shape[1], c, c), lambda d, j: (d, 0, 0, 0)),
                  pl.BlockSpec((None, A_HEADS, hd, hd), lambda d, j: (d, 0, 0, 0))],
        out_specs=(pl.BlockSpec((None, c, w), lambda d, j: (d, blk(d, j), 0)),
                   pl.BlockSpec((None, A_HEADS, hd, hd), lambda d, j: (d, 0, 0, 0))),
        scratch_shapes=[pltpu.VMEM((A_HEADS, hd, hd), F32)],
        compiler_params=_params(("parallel", "arbitrary"), est),
        name="hgrn_scan",
    )(proj, proj, proj, lb, sums, masks, s0)


def _hgrn_out_kernel(o_ref, gate_ref, g_ref, a_ref):
    hd = HEAD_DIM
    for h in range(a_ref.shape[1] // hd):
        cols = slice(h * hd, (h + 1) * hd)
        o = o_ref[0, :, cols] + o_ref[1, :, cols]
        gate = gate_ref[:, cols]
        a_ref[:, cols] = (_rms(o) * g_ref[...] * (gate * jax.nn.sigmoid(gate))).astype(a_ref.dtype)


def hgrn_out(o, proj, norm_g):
    seq = o.shape[1]
    tm = min(seq, 256)
    hd, w = HEAD_DIM, A_WIDTH
    return pl.pallas_call(
        _hgrn_out_kernel,
        out_shape=jax.ShapeDtypeStruct((seq, w), BF16),
        grid=(seq // tm,),
        in_specs=[pl.BlockSpec((2, tm, w), lambda i: (0, i, 0)),
                  pl.BlockSpec((tm, w), lambda i: (i, 4)),
                  pl.BlockSpec((1, hd), lambda i: (0, 0))],
        out_specs=pl.BlockSpec((tm, w), lambda i: (i, 0)),
        compiler_params=_params(("parallel",), 8 * tm * w * 4),
        name="hgrn_out",
    )(o, proj, norm_g.reshape(1, hd))


def _norm_rope_kernel(x_ref, g_ref, cos_ref, sin_ref, o_ref, *, scale):
    hd = HEAD_DIM
    for h in range(o_ref.shape[1] // hd):
        cols = slice(h * hd, (h + 1) * hd)
        y = _rms(x_ref[:, cols]) * g_ref[...]
        y = y * cos_ref[...] + pltpu.roll(y, hd // 2, 1) * sin_ref[...]
        o_ref[:, cols] = (y * scale).astype(o_ref.dtype)


def norm_rope(proj, col0, n_heads, g, cos, sin, scale):
    seq = proj.shape[0]
    tm = min(seq, 256)
    hd = HEAD_DIM
    w = n_heads * hd
    return pl.pallas_call(
        functools.partial(_norm_rope_kernel, scale=scale),
        out_shape=jax.ShapeDtypeStruct((seq, w), BF16),
        grid=(seq // tm,),
        in_specs=[pl.BlockSpec((tm, w), lambda i: (i, col0 // w)),
                  pl.BlockSpec((1, hd), lambda i: (0, 0)),
                  pl.BlockSpec((tm, hd), lambda i: (i, 0)),
                  pl.BlockSpec((tm, hd), lambda i: (i, 0))],
        out_specs=pl.BlockSpec((tm, w), lambda i: (i, 0)),
        compiler_params=_params(("parallel",), 6 * tm * w * 4 + 4 * tm * hd * 4),
        name="norm_rope",
    )(proj, g.reshape(1, hd), cos, sin)


def _cast_kernel(x_ref, o_ref):
    o_ref[...] = x_ref[...].astype(o_ref.dtype)


def cast_columns(proj, col0, width):
    seq = proj.shape[0]
    tm = min(seq, 1024)
    return pl.pallas_call(
        _cast_kernel,
        out_shape=jax.ShapeDtypeStruct((seq, width), BF16),
        grid=(seq // tm,),
        in_specs=[pl.BlockSpec((tm, width), lambda i: (i, col0 // width))],
        out_specs=pl.BlockSpec((tm, width), lambda i: (i, 0)),
        compiler_params=_params(("parallel",), 4 * tm * width * 4),
        name="cast_columns",
    )(proj)


def _flash_update(q, k, v, m_ref, l_ref, acc_ref):
    s = _dot_nt(q, k)
    m_prev = m_ref[...]
    m_new = jnp.maximum(m_prev, jnp.max(s, axis=-1, keepdims=True))
    alpha = jnp.exp2(m_prev - m_new)
    ps = [jnp.exp2(s[:, c * LANES:(c + 1) * LANES] - m_new) for c in range(s.shape[1] // LANES)]
    psum = ps[0]
    for pc in ps[1:]:
        psum = psum + pc
    p = jnp.concatenate([pc.astype(BF16) for pc in ps], axis=1)
    l_ref[...] = alpha * l_ref[...] + psum
    acc_ref[...] = alpha * acc_ref[...] + _dot(p, v)
    m_ref[...] = m_new


def _flash_kernel(*refs, has_ctx):
    if has_ctx:
        q_ref, k_ref, v_ref, kc_ref, vc_ref, o_ref, m_ref, l_ref, acc_ref = refs
    else:
        q_ref, k_ref, v_ref, o_ref, m_ref, l_ref, acc_ref = refs
    j = pl.program_id(2)

    @pl.when(j == 0)
    def _():
        m_ref[...] = jnp.full(m_ref.shape, -jnp.inf, F32)
        l_ref[...] = jnp.zeros(l_ref.shape, F32)
        acc_ref[...] = jnp.zeros(acc_ref.shape, F32)
        if has_ctx:
            _flash_update(q_ref[...], kc_ref[...], vc_ref[...], m_ref, l_ref, acc_ref)

    _flash_update(q_ref[...], k_ref[...], v_ref[...], m_ref, l_ref, acc_ref)

    @pl.when(j == pl.num_programs(2) - 1)
    def _():
        l = jnp.sum(l_ref[...], axis=-1, keepdims=True)
        o_ref[...] = (acc_ref[...] / l).astype(o_ref.dtype)


def flash_attention(q, k, v, k_ctx=None, v_ctx=None, *, n_heads, n_kv_heads, dq, dv):
    n, m = q.shape[0], k.shape[0]
    grp = n_heads // n_kv_heads
    tq = min(n, 2048)
    tk = min(m, 2048)
    has_ctx = k_ctx is not None
    in_specs = [pl.BlockSpec((tq, dq), lambda h, i, j: (i, h)),
                pl.BlockSpec((tk, dq), lambda h, i, j: (j, h // grp)),
                pl.BlockSpec((tk, dv), lambda h, i, j: (j, h // grp))]
    args = [q, k, v]
    if has_ctx:
        mc = k_ctx.shape[0]
        in_specs += [pl.BlockSpec((mc, dq), lambda h, i, j: (0, h // grp)),
                     pl.BlockSpec((mc, dv), lambda h, i, j: (0, h // grp))]
        args += [k_ctx, v_ctx]
    est = 2 * (tq * dq + tk * dq + tk * dv + tq * dv) * 2 + tq * (dv + 256) * 4 + 6 * tq * tk * 4
    return pl.pallas_call(
        functools.partial(_flash_kernel, has_ctx=has_ctx),
        out_shape=jax.ShapeDtypeStruct((n, n_heads * dv), BF16),
        grid=(n_heads, n // tq, m // tk),
        in_specs=in_specs,
        out_specs=pl.BlockSpec((tq, dv), lambda h, i, j: (i, h)),
        scratch_shapes=[pltpu.VMEM((tq, LANES), F32), pltpu.VMEM((tq, LANES), F32), pltpu.VMEM((tq, dv), F32)],
        compiler_params=_params(("parallel", "parallel", "arbitrary"), est),
        name="flash_attention",
    )(*args)


def _proj_postnorm_kernel(*refs, n_in, row):
    a_refs = refs[:n_in]
    w_refs = refs[n_in:2 * n_in]
    x_ref, gate_ref, g_ref, b_ref, o_ref = refs[2 * n_in:]
    y = _dot(a_refs[0][...], w_refs[0][...])
    for a_ref, w_ref in zip(a_refs[1:], w_refs[1:]):
        y = y + _dot(a_ref[...], w_ref[...])
    z = DEEPNORM_ALPHA * x_ref[...] + gate_ref[row:row + 1, :] * y
    o_ref[...] = _layer_norm(z) * g_ref[...] + b_ref[...]


def proj_postnorm(acts, ws, x, mod, row, k_gate, g, b):
    m, d = x.shape
    tm = min(m, 512)
    n_in = len(acts)
    once = pl.Buffered(1)
    in_specs = [pl.BlockSpec((tm, a.shape[1]), lambda i: (i, 0)) for a in acts]
    in_specs += [pl.BlockSpec(w.shape, lambda i: (0, 0), pipeline_mode=once) for w in ws]
    in_specs += [pl.BlockSpec((tm, d), lambda i: (i, 0)),
                 pl.BlockSpec((8, d), lambda i: (0, k_gate)),
                 pl.BlockSpec((1, d), lambda i: (0, 0)),
                 pl.BlockSpec((1, d), lambda i: (0, 0))]
    est = sum(w.size * 2 for w in ws) + sum(2 * tm * a.shape[1] * 2 for a in acts) + 6 * tm * d * 4
    return pl.pallas_call(
        functools.partial(_proj_postnorm_kernel, n_in=n_in, row=row),
        out_shape=jax.ShapeDtypeStruct((m, d), F32),
        grid=(m // tm,),
        in_specs=in_specs,
        out_specs=pl.BlockSpec((tm, d), lambda i: (i, 0)),
        compiler_params=_params(("parallel",), est),
        name="proj_postnorm",
    )(*acts, *ws, x, mod, g.reshape(1, d), b.reshape(1, d))


def _add_postnorm_kernel(y_ref, x_ref, gate_ref, g_ref, b_ref, o_ref, *, row):
    z = DEEPNORM_ALPHA * x_ref[...] + gate_ref[row:row + 1, :] * y_ref[...]
    o_ref[...] = _layer_norm(z) * g_ref[...] + b_ref[...]


def add_postnorm(y, x, mod, row, k_gate, g, b):
    m, d = x.shape
    tm = min(m, 512)
    return pl.pallas_call(
        functools.partial(_add_postnorm_kernel, row=row),
        out_shape=jax.ShapeDtypeStruct((m, d), F32),
        grid=(m // tm,),
        in_specs=[pl.BlockSpec((tm, d), lambda i: (i, 0)),
                  pl.BlockSpec((tm, d), lambda i: (i, 0)),
                  pl.BlockSpec((8, d), lambda i: (0, k_gate)),
                  pl.BlockSpec((1, d), lambda i: (0, 0)),
                  pl.BlockSpec((1, d), lambda i: (0, 0))],
        out_specs=pl.BlockSpec((tm, d), lambda i: (i, 0)),
        compiler_params=_params(("parallel",), 8 * tm * d * 4),
        name="add_postnorm",
    )(y, x, mod, g.reshape(1, d), b.reshape(1, d))


def _rms_mm_kernel(x_ref, g_ref, w_ref, o_ref, a_ref):
    @pl.when(pl.program_id(1) == 0)
    def _():
        a_ref[...] = (_rms(x_ref[...]) * g_ref[...]).astype(BF16)

    o_ref[...] = _dot(a_ref[...], w_ref[...]).astype(o_ref.dtype)


def _mla_q_kernel(x_ref, g_ref, w_ref, cos_ref, sa_ref, sb_ref, o_ref, a_ref, *, scale):
    @pl.when(pl.program_id(1) == 0)
    def _():
        a_ref[...] = (_rms(x_ref[...]) * g_ref[...]).astype(BF16)

    y = _dot(a_ref[...], w_ref[...])
    hd = HEAD_DIM
    for h in range(y.shape[1] // MLA_QK):
        c0 = h * MLA_QK
        o_ref[:, c0:c0 + hd] = (y[:, c0:c0 + hd] * scale).astype(o_ref.dtype)
        r = y[:, c0 + hd:c0 + 2 * hd]
        r = r * cos_ref[...] + pltpu.roll(r, hd - MLA_ROPE // 2, 1) * sa_ref[...] + pltpu.roll(r, MLA_ROPE // 2, 1) * sb_ref[...]
        o_ref[:, c0 + hd:c0 + 2 * hd] = (r * scale).astype(o_ref.dtype)


def _mla_k_kernel(x_ref, g_ref, w_ref, kr_ref, cos_ref, sa_ref, sb_ref, o_ref, a_ref, r_ref):
    hd = HEAD_DIM

    @pl.when(pl.program_id(1) == 0)
    def _():
        a_ref[...] = (_rms(x_ref[...]) * g_ref[...]).astype(BF16)
        r = kr_ref[...]
        r = r * cos_ref[...] + pltpu.roll(r, hd - MLA_ROPE // 2, 1) * sa_ref[...] + pltpu.roll(r, MLA_ROPE // 2, 1) * sb_ref[...]
        r_ref[...] = r.astype(BF16)

    y = _dot(a_ref[...], w_ref[...])
    for h in range(y.shape[1] // hd):
        o_ref[:, h * MLA_QK:h * MLA_QK + hd] = y[:, h * hd:(h + 1) * hd].astype(o_ref.dtype)
        o_ref[:, h * MLA_QK + hd:(h + 1) * MLA_QK] = r_ref[...]


def _mla_specs(m, tm, lora, col_blk):
    return [pl.BlockSpec((tm, lora), lambda i, j: (i, col_blk)),
            pl.BlockSpec((1, lora), lambda i, j: (0, 0))]


def rms_matmul(dn, col_blk, g, w, tn):
    m = dn.shape[0]
    lora, n = w.shape
    tm = min(m, 1024)
    est = 2 * tm * lora * 4 + tm * lora * 2 + 2 * lora * tn * 2 + 2 * tm * tn * 2 + tm * tn * 4
    return pl.pallas_call(
        _rms_mm_kernel,
        out_shape=jax.ShapeDtypeStruct((m, n), BF16),
        grid=(m // tm, n // tn),
        in_specs=_mla_specs(m, tm, lora, col_blk) + [pl.BlockSpec((lora, tn), lambda i, j: (0, j))],
        out_specs=pl.BlockSpec((tm, tn), lambda i, j: (i, j)),
        scratch_shapes=[pltpu.VMEM((tm, lora), BF16)],
        compiler_params=_params(("parallel", "arbitrary"), est),
        name="rms_matmul",
    )(dn, g.reshape(1, lora), w)


def mla_q(dn, g, w, cos, sa, sb, scale):
    m = dn.shape[0]
    lora, n = w.shape
    tm = min(m, 1024)
    tn = 4 * MLA_QK
    hd = HEAD_DIM
    est = 2 * tm * lora * 4 + tm * lora * 2 + 2 * lora * tn * 2 + 2 * tm * tn * 2 + 2 * tm * tn * 4 + 6 * tm * hd * 4
    rope_spec = pl.BlockSpec((tm, hd), lambda i, j: (i, 0))
    return pl.pallas_call(
        functools.partial(_mla_q_kernel, scale=scale),
        out_shape=jax.ShapeDtypeStruct((m, n), BF16),
        grid=(m // tm, n // tn),
        in_specs=_mla_specs(m, tm, lora, 0) + [pl.BlockSpec((lora, tn), lambda i, j: (0, j)),
                                               rope_spec, rope_spec, rope_spec],
        out_specs=pl.BlockSpec((tm, tn), lambda i, j: (i, j)),
        scratch_shapes=[pltpu.VMEM((tm, lora), BF16)],
        compiler_params=_params(("parallel", "arbitrary"), est),
        name="mla_q",
    )(dn, g.reshape(1, lora), w, cos, sa, sb)


def mla_k(dn, g, w, cos, sa, sb):
    m = dn.shape[0]
    lora, n = w.shape
    tm = min(m, 1024)
    hd = HEAD_DIM
    tn = 4 * hd
    kr_blk = (MLA_Q_LORA + MLA_KV_LORA) // hd
    est = 2 * tm * lora * 4 + tm * lora * 2 + 2 * lora * tn * 2 + 4 * tm * tn * 2 + tm * tn * 4 + 8 * tm * hd * 4
    rope_spec = pl.BlockSpec((tm, hd), lambda i, j: (i, 0))
    return pl.pallas_call(
        _mla_k_kernel,
        out_shape=jax.ShapeDtypeStruct((m, 2 * n), BF16),
        grid=(m // tm, n // tn),
        in_specs=_mla_specs(m, tm, lora, 1) + [pl.BlockSpec((lora, tn), lambda i, j: (0, j)),
                                               pl.BlockSpec((tm, hd), lambda i, j: (i, kr_blk)),
                                               rope_spec, rope_spec, rope_spec],
        out_specs=pl.BlockSpec((tm, 2 * tn), lambda i, j: (i, j)),
        scratch_shapes=[pltpu.VMEM((tm, lora), BF16), pltpu.VMEM((tm, hd), BF16)],
        compiler_params=_params(("parallel", "arbitrary"), est),
        name="mla_k",
    )(dn, g.reshape(1, lora), w, dn, cos, sa, sb)


def _router_kernel(x_ref, sh_ref, sc_ref, wr_ref, h_ref, aff_ref, *, row):
    hn = _layer_norm(x_ref[...])
    h = hn * (1.0 + sc_ref[row:row + 1, :]) + sh_ref[row:row + 1, :]
    hb = h.astype(BF16)
    half = h.shape[1] // 2
    bits = pltpu.bitcast(hb.astype(F32), jnp.uint32)
    h_ref[...] = (bits[:, half:] & jnp.uint32(0xFFFF0000)) | (bits[:, :half] >> 16)
    w = wr_ref[...]
    w1 = w.astype(BF16)
    w2 = (w - w1.astype(F32)).astype(BF16)
    h2 = (h - hb.astype(F32)).astype(BF16)
    logits = _dot_nt(w1, hb) + (_dot_nt(w2, hb) + _dot_nt(w1, h2))
    mx = jnp.max(logits, axis=0, keepdims=True)
    p = jnp.exp(logits - mx)
    aff_ref[...] = p / jnp.sum(p, axis=0, keepdims=True)


def moe_router(x, mod, row, k_shift, k_scale, w_router_t):
    m, d = x.shape
    e = w_router_t.shape[0]
    tm = min(m, 512)
    return pl.pallas_call(
        functools.partial(_router_kernel, row=row),
        out_shape=(jax.ShapeDtypeStruct((m, d // 2), jnp.uint32), jax.ShapeDtypeStruct((e, m), F32)),
        grid=(m // tm,),
        in_specs=[pl.BlockSpec((tm, d), lambda i: (i, 0)),
                  pl.BlockSpec((8, d), lambda i: (0, k_shift)),
                  pl.BlockSpec((8, d), lambda i: (0, k_scale)),
                  pl.BlockSpec((e, d), lambda i: (0, 0))],
        out_specs=(pl.BlockSpec((tm, d // 2), lambda i: (i, 0)), pl.BlockSpec((e, tm), lambda i: (0, i))),
        compiler_params=_params(("parallel",), 8 * tm * d * 4),
        name="moe_router",
    )(x, mod, mod, w_router_t)


def _ffn_up_kernel(*refs, n_seg):
    x_refs, (wg_ref, wu_ref) = refs[:n_seg], refs[n_seg:n_seg + 2]
    o_refs, xs_refs = refs[n_seg + 2:2 * n_seg + 2], refs[2 * n_seg + 2:]

    @pl.when(pl.program_id(1) == 0)
    def _():
        for x_ref, xs_ref in zip(x_refs, xs_refs):
            word = x_ref[...]
            first = pltpu.bitcast(word << 16, F32)
            second = pltpu.bitcast(word & jnp.uint32(0xFFFF0000), F32)
            xs_ref[...] = jnp.concatenate([first, second], axis=1).astype(BF16)

    wg = wg_ref[...].astype(BF16)
    wu = wu_ref[...].astype(BF16)
    for xs_ref, o_ref in zip(xs_refs, o_refs):
        x = xs_ref[...]
        g = _dot(x, wg)
        u = _dot(x, wu)
        o_ref[...] = (g * jax.nn.sigmoid(g) * u).astype(o_ref.dtype)


def _ffn_down_kernel(*refs, n_seg):
    h_refs, wd_ref, wt_refs, o_refs = refs[:n_seg], refs[n_seg], refs[n_seg + 1:2 * n_seg + 1], refs[2 * n_seg + 1:]
    wd = wd_ref[...].astype(BF16)
    for h_ref, wt_ref, o_ref in zip(h_refs, wt_refs, o_refs):
        y = _dot(h_ref[...], wd) * wt_ref[...]
        hi = y.astype(BF16)
        o_ref[0] = hi
        o_ref[1] = (y - hi.astype(F32)).astype(BF16)


def expert_ffn(xgs, wts, w_gate, w_up, w_down, layer):
    n_seg = len(xgs)
    e = xgs[0].shape[0]
    d = 2 * xgs[0].shape[2]
    f = w_gate.shape[3]
    rs = [x.shape[1] for x in xgs]
    r = sum(rs)
    tf = min(f, 256)
    est = 2 * (r * d * 2 + 2 * d * tf * 4 + r * tf * 2) + r * d * 2 + 2 * d * tf * 2 + 3 * r * tf * 4
    hids = pl.pallas_call(
        functools.partial(_ffn_up_kernel, n_seg=n_seg),
        out_shape=[jax.ShapeDtypeStruct((e, ri, f), BF16) for ri in rs],
        grid=(e, f // tf),
        in_specs=[pl.BlockSpec((None, ri, d // 2), lambda i, j: (i, 0, 0)) for ri in rs]
        + [pl.BlockSpec((None, None, d, tf), lambda i, j: (layer, i, 0, j))] * 2,
        out_specs=[pl.BlockSpec((None, ri, tf), lambda i, j: (i, 0, j)) for ri in rs],
        scratch_shapes=[pltpu.VMEM((ri, d), BF16) for ri in rs],
        compiler_params=_params(("parallel", "arbitrary"), est),
        name="ffn_up",
    )(*xgs, w_gate, w_up)
    tn = min(d, 512)
    est = 2 * (r * f * 2 + f * tn * 4 + r * tn * 4 + r * LANES * 4) + f * tn * 2 + 2 * r * tn * 4
    return pl.pallas_call(
        functools.partial(_ffn_down_kernel, n_seg=n_seg),
        out_shape=[jax.ShapeDtypeStruct((e, 2, ri, d), BF16) for ri in rs],
        grid=(e, d // tn),
        in_specs=[pl.BlockSpec((None, ri, f), lambda i, j: (i, 0, 0)) for ri in rs]
        + [pl.BlockSpec((None, None, f, tn), lambda i, j: (layer, i, 0, j))]
        + [pl.BlockSpec((None, ri, 1), lambda i, j: (i, 0, 0)) for ri in rs],
        out_specs=[pl.BlockSpec((None, 2, ri, tn), lambda i, j: (i, 0, 0, j)) for ri in rs],
        compiler_params=_params(("parallel", "arbitrary"), est),
        name="ffn_down",
    )(*hids, w_down, *wts)


def _select_kernel(aff_ref, pos_ref, idx_ref, wt_ref, off_ref, incl_ref, tot_ref, offs_ref, *, cap):
    e, g, ln = aff_ref.shape
    bits = pltpu.bitcast(aff_ref[...], jnp.int32)

    def count(mask):
        per_lane = jnp.sum(jnp.where(mask, 1.0, 0.0), axis=1)
        return jnp.sum(per_lane, axis=1, keepdims=True)[:, :, None]

    def search(i, t):
        cand = t | jnp.left_shift(jnp.int32(1), 30 - i)
        return jnp.where(count(bits >= cand) >= cap, cand, t)

    thr = lax.fori_loop(0, 31, search, jnp.zeros((e, 1, 1), jnp.int32))
    gt = bits > thr
    eq = bits == thr
    need = cap - count(gt)

    r0 = lax.broadcasted_iota(jnp.int32, (ln, ln), 0)
    r1 = lax.broadcasted_iota(jnp.int32, (ln, ln), 1)
    upper = jnp.where(r0 <= r1, 1.0, 0.0).astype(BF16)
    ones = jnp.ones((ln, ln), BF16)
    g0 = lax.broadcasted_iota(jnp.int32, (g, g), 0)
    g1 = lax.broadcasted_iota(jnp.int32, (g, g), 1)
    earlier = jnp.where(g1 < g0, 1.0, 0.0).astype(BF16)

    def prefix(mask):
        x = jnp.where(mask, 1.0, 0.0).astype(BF16).reshape(e * g, ln)
        incl = _dot(x, upper).reshape(e, g, ln)
        tot = _dot(x, ones).reshape(e, g, ln)
        off = jnp.stack([_dot(earlier, tot[i].astype(BF16)) for i in range(e)])
        return incl, tot, off

    incl_eq, _, off_eq = prefix(eq)
    sel = gt | (eq & (off_eq + incl_eq - 1.0 < need))
    incl, tot, off = prefix(sel)
    pos_ref[...] = jnp.where(sel, off + incl - 1.0, -1.0).astype(jnp.int32)
    off_ref[...] = off.astype(jnp.int32)
    incl_ref[...] = incl
    tot_ref[...] = tot
    offs_ref[...] = off

    slot = lax.broadcasted_iota(jnp.int32, (cap, ln), 0).astype(F32)
    lane = lax.broadcasted_iota(jnp.int32, (cap, ln), 1).astype(F32)

    def tokens_of_slots(i, c):
        ends = jnp.transpose(offs_ref[i] + tot_ref[i])[0:1, :]
        before = jnp.where(ends <= slot, 1.0, 0.0).astype(BF16)
        grp = _dot(before, ones)
        rank = slot - _dot(before, tot_ref[i].astype(BF16))
        pick = jnp.where(lane == grp, 1.0, 0.0).astype(BF16)
        incl_g = _dot(pick, incl_ref[i].astype(BF16))
        lane_p = _dot(jnp.where(incl_g <= rank, 1.0, 0.0).astype(BF16), ones)
        idx_ref[i] = (grp * ln + lane_p)[:, 0:1].astype(jnp.int32)
        a1, a2, a3 = _split3(aff_ref[i])
        aff_g = (_dot(pick, a3) + _dot(pick, a2)) + _dot(pick, a1)
        wt_ref[i] = jnp.sum(jnp.where(lane == lane_p, aff_g, 0.0), axis=1, keepdims=True)
        return c

    lax.fori_loop(0, e, tokens_of_slots, 0)


def moe_select(aff_t, cap):
    e, n = aff_t.shape
    g = n // LANES
    assert g <= LANES, "token groups are mapped onto the 128 lanes"
    a = aff_t.reshape(e, g, LANES)
    if g != LANES:
        a = jnp.concatenate([a, jnp.full((e, LANES - g, LANES), -1.0, F32)], axis=1)
    shp = jax.ShapeDtypeStruct((e, LANES, LANES), jnp.int32)
    full = pl.BlockSpec((e, LANES, LANES), lambda i: (0, 0, 0))
    per_slot = pl.BlockSpec((e, cap, 1), lambda i: (0, 0, 0))
    pos, idx, wt, off = pl.pallas_call(
        functools.partial(_select_kernel, cap=cap),
        out_shape=(shp, jax.ShapeDtypeStruct((e, cap, 1), jnp.int32), jax.ShapeDtypeStruct((e, cap, 1), F32), shp),
        grid=(1,),
        in_specs=[full],
        out_specs=(full, per_slot, per_slot, full),
        scratch_shapes=[pltpu.VMEM((e, LANES, LANES), F32)] * 3,
        compiler_params=_params(("arbitrary",), 28 * e * LANES * LANES * 4 + 16 * cap * LANES * 4),
        name="moe_select",
    )(a)
    return pos.reshape(e, LANES * LANES)[:, :n], idx.reshape(e, 1, cap), wt, off[:, :g, 0]


def _row_copy(h_hbm, x_ref, sem, token, row):
    return pltpu.make_async_copy(h_hbm.at[pl.ds(token, 1), :], x_ref.at[0, pl.ds(row, 1), :], sem)


def _gather_kernel(idx_ref, h_hbm, x_ref, sem):
    cap = x_ref.shape[1]

    def rows(i, c):
        for u in range(GATHER_UNROLL):
            s = i * GATHER_UNROLL + u
            _row_copy(h_hbm, x_ref, sem, idx_ref[0, s], s).start()
        return c

    lax.fori_loop(0, cap // GATHER_UNROLL, rows, 0)
    pltpu.make_async_copy(h_hbm.at[pl.ds(0, cap), :], x_ref.at[0], sem).wait()


def moe_gather(hp, idx):
    w = hp.shape[1]
    e, _, cap = idx.shape
    return pl.pallas_call(
        _gather_kernel,
        out_shape=jax.ShapeDtypeStruct((e, cap, w), jnp.uint32),
        grid=(e,),
        in_specs=[pl.BlockSpec((None, 1, cap), lambda i: (i, 0, 0), memory_space=pltpu.SMEM),
                  pl.BlockSpec(memory_space=pl.ANY)],
        out_specs=pl.BlockSpec((1, cap, w), lambda i: (i, 0, 0)),
        scratch_shapes=[pltpu.SemaphoreType.DMA(())],
        compiler_params=_params(("arbitrary",), 2 * cap * w * 4),
        name="moe_gather",
    )(idx, hp)


def _window_copy(y_hbm, dst, sem, e, src, win):
    return pltpu.make_async_copy(y_hbm.at[e, :, pl.ds(src, win)], dst, sem)


def _combine_kernel(offb_ref, y_hbm, pos_ref, x_ref, gate_ref, g_ref, b_ref, o_ref, ybuf, ybuf_x, sem, acc_ref,
                    *, row, cap, win):
    b = pl.program_id(0)
    n_exp, tb = pos_ref.shape
    half = n_exp // 2
    par = b % 2

    def window(e, k, blk=b):
        first = (offb_ref[e, blk] // 8) * 8 + k * win
        return first, pl.multiple_of(jnp.minimum(first, cap - win), 8)

    def onehot2(e, first, src):
        slots = src + lax.broadcasted_iota(jnp.int32, (win, 1), 0)
        hit = jnp.logical_and(pos_ref[e:e + 1, :] == slots, slots >= first)
        oh = jnp.where(hit, 1.0, 0.0).astype(BF16)
        return jnp.concatenate([oh, oh], axis=0)

    def copies(h, blk, p):
        return [_window_copy(y_hbm, ybuf.at[p, h, j], sem.at[2 * p + h], h * half + j,
                             window(h * half + j, 0, blk)[1], win) for j in range(half)]

    def start_block(blk, p):
        for h in range(2):
            for cp in copies(h, blk, p):
                cp.start()

    @pl.when(b == 0)
    def _():
        start_block(b, par)

    @pl.when(b + 1 < pl.num_programs(0))
    def _():
        start_block(b + 1, 1 - par)

    acc = None
    for h in range(2):
        for cp in copies(h, b, par):
            cp.wait()
        lhs = jnp.concatenate([onehot2(h * half + j, *window(h * half + j, 0)) for j in range(half)], axis=0)
        part = _dot_tn(lhs, ybuf[par, h].reshape(half * 2 * win, ybuf.shape[-1]))
        acc = part if acc is None else acc + part
    acc_ref[...] = acc

    for e in range(n_exp):
        n_win = (offb_ref[e, b + 1] - (offb_ref[e, b] // 8) * 8 + win - 1) // win

        def extra(k, c, e=e):
            first_k, src_k = window(e, k)
            cp = _window_copy(y_hbm, ybuf_x, sem.at[4], e, src_k, win)
            cp.start()
            cp.wait()
            acc_ref[...] += _dot_tn(onehot2(e, first_k, src_k), ybuf_x[...].reshape(2 * win, ybuf_x.shape[-1]))
            return c

        lax.fori_loop(1, n_win, extra, 0)

    z = DEEPNORM_ALPHA * x_ref[...] + gate_ref[row:row + 1, :] * acc_ref[...]
    o_ref[...] = _layer_norm(z) * g_ref[...] + b_ref[...]


def moe_combine_postnorm(y, pos, off, x, mod, row, k_gate, g, b):
    n, d = x.shape
    e, _, cap, _ = y.shape
    tb = min(n, 256)
    nb = n // tb
    win = min(cap, COMBINE_WINDOW)
    offb = jnp.concatenate([off[:, ::tb // LANES], jnp.full((e, 1), cap, jnp.int32)], axis=1)
    grid_spec = pltpu.PrefetchScalarGridSpec(
        num_scalar_prefetch=1,
        grid=(nb,),
        in_specs=[pl.BlockSpec(memory_space=pl.ANY),
                  pl.BlockSpec((e, tb), lambda i, o: (0, i)),
                  pl.BlockSpec((tb, d), lambda i, o: (i, 0)),
                  pl.BlockSpec((8, d), lambda i, o: (0, k_gate)),
                  pl.BlockSpec((1, d), lambda i, o: (0, 0)),
                  pl.BlockSpec((1, d), lambda i, o: (0, 0))],
        out_specs=pl.BlockSpec((tb, d), lambda i, o: (i, 0)),
        scratch_shapes=[pltpu.VMEM((2, 2, e // 2, 2, win, d), BF16), pltpu.VMEM((2, win, d), BF16),
                        pltpu.SemaphoreType.DMA((5,)), pltpu.VMEM((tb, d), F32)],
    )
    est = 8 * tb * d * 4 + (2 * e + 1) * 2 * win * d * 2 + e * win * tb * 2
    return pl.pallas_call(
        functools.partial(_combine_kernel, row=row, cap=cap, win=win),
        out_shape=jax.ShapeDtypeStruct((n, d), F32),
        grid_spec=grid_spec,
        compiler_params=_params(("arbitrary",), est),
        name="moe_combine",
    )(offb, y, pos, x, mod, g.reshape(1, d), b.reshape(1, d))


def moe_route(x, mod, row, w_router_t):
    m = x.shape[0]
    cap = max(1, EC_CAPACITY_FACTOR * m // N_EXPERTS)
    h, aff_t = moe_router(x, mod, row, 3, 4, w_router_t)
    pos, idx, wt, off = moe_select(aff_t, cap)
    return moe_gather(h, idx), wt, (pos, off)


def _rope_angles(n_tokens, rot_dim):
    rows = n_tokens // GRID_W
    row = jnp.repeat(jnp.arange(rows, dtype=F32), GRID_W)
    col = jnp.tile(jnp.arange(GRID_W, dtype=F32), rows)
    n_freq = rot_dim // 4
    inv = ROPE_THETA ** (-jnp.arange(n_freq, dtype=F32) / n_freq)
    return jnp.concatenate([row[:, None] * inv, col[:, None] * inv], axis=-1)


def _gqa_rope_tables(n_tokens):
    ang = _rope_angles(n_tokens, HEAD_DIM)
    c, s = jnp.cos(ang), jnp.sin(ang)
    return jnp.concatenate([c, c], axis=-1), jnp.concatenate([-s, s], axis=-1)


def _mla_rope_tables(n_tokens):
    ang = _rope_angles(n_tokens, MLA_ROPE)
    c, s = jnp.cos(ang), jnp.sin(ang)
    z = jnp.zeros_like(c)
    cos = jnp.concatenate([c, c, z, z], axis=-1)
    sa = jnp.concatenate([-s, z, z, z], axis=-1)
    sb = jnp.concatenate([z, s, z, z], axis=-1)
    return cos, sa, sb


def kernel(x, c, ctx, c_ctx, ada_w, ada_b, ln_g, ln_b, ev_w_in, ev_w_out, hgrn_lb, hgrn_norm_g, gqa_q_norm_g, gqa_k_norm_g, mla_w_down, mla_q_norm_g, mla_kv_norm_g, mla_w_uq, mla_w_ukv, mla_w_o, moe_router, moe_w_gate, moe_w_up, moe_w_down):
    d = D_MODEL
    xl = x[0]
    xc = ctx[0]
    n_lat, n_ctx = xl.shape[0], xc.shape[0]
    cc = jnp.zeros((8, d), F32).at[0].set(c[0]).at[1].set(c_ctx)
    lb_all = jnp.cumsum(jax.nn.softmax(hgrn_lb.astype(F32), axis=1), axis=1)
    gqa_tabs = _gqa_rope_tables(n_lat)
    gqa_tabs_ctx = [jnp.ones((n_ctx, HEAD_DIM), F32), jnp.zeros((n_ctx, HEAD_DIM), F32)]
    mla_tabs = _mla_rope_tables(n_lat)
    mla_tabs_ctx = [jnp.ones((n_ctx, HEAD_DIM), F32), jnp.zeros((n_ctx, HEAD_DIM), F32), jnp.zeros((n_ctx, HEAD_DIM), F32)]
    LAT, CTX = 0, 1

    for l in range(DEPTH):
        last = l == DEPTH - 1
        i = l // 2
        mod = adaln(cc, ada_w, ada_b, l)
        if l % 2 == 0:
            w_in = ev_w_in[i].astype(BF16)
            w_out = ev_w_out[i].astype(BF16)
            lb = lb_all[:, l].reshape(2, 1, A_WIDTH)
            scale = HEAD_DIM ** -0.5 * LOG2E
            proj_c = lnmod_matmul(xc, mod, CTX, 0, 1, w_in, 512)
            proj_l = lnmod_matmul(xl, mod, LAT, 0, 1, w_in, 512)
            s0 = jnp.zeros((2, A_HEADS, HEAD_DIM, HEAD_DIM), F32)
            o_c, s_c = hgrn_scan(proj_c, lb, s0)
            o_l, _ = hgrn_scan(proj_l, lb, s_c)
            a_l = hgrn_out(o_l, proj_l, hgrn_norm_g[i])
            qcol, kcol, vcol = 5 * A_WIDTH, 5 * A_WIDTH + B_WIDTH, 5 * A_WIDTH + B_WIDTH + B_KV_WIDTH
            q_l = norm_rope(proj_l, qcol, B_Q_HEADS, gqa_q_norm_g[i], *gqa_tabs, scale)
            k_l = norm_rope(proj_l, kcol, B_KV_HEADS, gqa_k_norm_g[i], *gqa_tabs, 1.0)
            k_c = norm_rope(proj_c, kcol, B_KV_HEADS, gqa_k_norm_g[i], *gqa_tabs_ctx, 1.0)
            v_l = cast_columns(proj_l, vcol, B_KV_WIDTH)
            v_c = cast_columns(proj_c, vcol, B_KV_WIDTH)
            att = dict(n_heads=B_Q_HEADS, n_kv_heads=B_KV_HEADS, dq=HEAD_DIM, dv=HEAD_DIM)
            b_l = flash_attention(q_l, k_l, v_l, k_c, v_c, **att)
            w_parts = [w_out[:A_WIDTH], w_out[A_WIDTH:]]
            xl_new = proj_postnorm([a_l, b_l], w_parts, xl, mod, LAT, 2, ln_g[l, 0], ln_b[l, 0])
            if not last:
                a_c = hgrn_out(o_c, proj_c, hgrn_norm_g[i])
                q_c = norm_rope(proj_c, qcol, B_Q_HEADS, gqa_q_norm_g[i], *gqa_tabs_ctx, scale)
                b_c = flash_attention(q_c, k_c, v_c, **att)
                xc = proj_postnorm([a_c, b_c], w_parts, xc, mod, CTX, 2, ln_g[l, 0], ln_b[l, 0])
            xl = xl_new
        else:
            hd = HEAD_DIM
            pad = (-mla_w_down.shape[2]) % hd
            w_down = jnp.pad(mla_w_down[i], ((0, 0), (0, pad))).astype(BF16)
            w_uq = mla_w_uq[i].reshape(MLA_Q_LORA, MLA_HEADS, hd + MLA_ROPE)
            w_uq = jnp.pad(w_uq, ((0, 0), (0, 0), (0, MLA_QK - hd - MLA_ROPE))).reshape(MLA_Q_LORA, MLA_HEADS * MLA_QK).astype(BF16)
            w_ukv = mla_w_ukv[i].reshape(MLA_KV_LORA, MLA_HEADS, 2 * hd)
            w_uk = w_ukv[:, :, :hd].reshape(MLA_KV_LORA, MLA_HEADS * hd).astype(BF16)
            w_uv = w_ukv[:, :, hd:].reshape(MLA_KV_LORA, MLA_HEADS * hd).astype(BF16)
            w_o = mla_w_o[i].astype(BF16)
            scale = (hd + MLA_ROPE) ** -0.5 * LOG2E
            dn_c = lnmod_matmul(xc, mod, CTX, 0, 1, w_down, w_down.shape[1])
            dn_l = lnmod_matmul(xl, mod, LAT, 0, 1, w_down, w_down.shape[1])
            q_l = mla_q(dn_l, mla_q_norm_g[i], w_uq, *mla_tabs, scale)
            k_l = mla_k(dn_l, mla_kv_norm_g[i], w_uk, *mla_tabs)
            k_c = mla_k(dn_c, mla_kv_norm_g[i], w_uk, *mla_tabs_ctx)
            v_l = rms_matmul(dn_l, 1, mla_kv_norm_g[i], w_uv, 1024)
            v_c = rms_matmul(dn_c, 1, mla_kv_norm_g[i], w_uv, 1024)
            att = dict(n_heads=MLA_HEADS, n_kv_heads=MLA_HEADS, dq=MLA_QK, dv=hd)
            o_l = flash_attention(q_l, k_l, v_l, k_c, v_c, **att)
            xl_new = proj_postnorm([o_l], [w_o], xl, mod, LAT, 2, ln_g[l, 0], ln_b[l, 0])
            if not last:
                q_c = mla_q(dn_c, mla_q_norm_g[i], w_uq, *mla_tabs_ctx, scale)
                o_c = flash_attention(q_c, k_c, v_c, **att)
                xc = proj_postnorm([o_c], [w_o], xc, mod, CTX, 2, ln_g[l, 0], ln_b[l, 0])
            xl = xl_new

        w_router_t = moe_router[l].T
        segs = [(xl, LAT)] if last else [(xl, LAT), (xc, CTX)]
        routes = [moe_route(xs, mod, row, w_router_t) for xs, row in segs]
        ys = expert_ffn([r[0] for r in routes], [r[1] for r in routes], moe_w_gate, moe_w_up, moe_w_down, l)
        outs = [moe_combine_postnorm(y, *r[2], xs, mod, row, 5, ln_g[l, 1], ln_b[l, 1])
                for y, r, (xs, row) in zip(ys, routes, segs)]
        xl = outs[0]
        if not last:
            xc = outs[1]
    return xl[None]
```

```python
import functools
import math

import numpy as np
import jax
import jax.numpy as jnp
from jax import lax
from jax.experimental import pallas as pl
from jax.experimental.pallas import tpu as pltpu

F32 = jnp.float32
BF16 = jnp.bfloat16

D_MODEL = 2048
DEPTH = 2
GRID_W = 64
HEAD_DIM = 128
A_HEADS = D_MODEL // 256
A_WIDTH = A_HEADS * HEAD_DIM
B_Q_HEADS = D_MODEL // 256
B_KV_HEADS = 2
B_WIDTH = B_Q_HEADS * HEAD_DIM
B_KV_WIDTH = B_KV_HEADS * HEAD_DIM
MLA_HEADS = D_MODEL // 128
MLA_Q_LORA = 512
MLA_KV_LORA = 512
MLA_ROPE = 64
MLA_QK = 2 * HEAD_DIM
N_EXPERTS = 16
EXPERT_FF = D_MODEL // 2
EC_CAPACITY_FACTOR = 2
ROPE_THETA = 10000.0
NORM_EPS = 1e-6
DEEPNORM_ALPHA = (2.0 * DEPTH) ** 0.25

HGRN_CHUNK = 128
HGRN_CHUNKS_PER_STEP = 4
HGRN_SMALL_LEVELS = 3
GATHER_UNROLL = 8
COMBINE_WINDOW = 64
LANES = 128
LOG2E = math.log2(math.e)
V7X_VMEM_BYTES = 64 * 1024 * 1024
VMEM_CAP_BYTES = V7X_VMEM_BYTES - 8 * 1024 * 1024


def _params(semantics, vmem_estimate_bytes):
    limit = int(min(max(2 * vmem_estimate_bytes, 32 * 1024 * 1024), VMEM_CAP_BYTES))
    return pltpu.CompilerParams(dimension_semantics=semantics, vmem_limit_bytes=limit)


def _layer_norm(x):
    mu = jnp.mean(x, axis=-1, keepdims=True)
    xc = x - mu
    var = jnp.mean(xc * xc, axis=-1, keepdims=True)
    return xc * lax.rsqrt(var + NORM_EPS)


def _rms(x):
    return x * lax.rsqrt(jnp.mean(x * x, axis=-1, keepdims=True) + NORM_EPS)


def _dot(a, b):
    return jnp.dot(a, b, preferred_element_type=F32)


def _dot_nt(a, b):
    return lax.dot_general(a, b, (((1,), (1,)), ((), ())), preferred_element_type=F32)


def _dot_tn(a, b):
    return lax.dot_general(a, b, (((0,), (0,)), ((), ())), preferred_element_type=F32)


def _split3(x):
    x1 = x.astype(BF16)
    r1 = x - x1.astype(F32)
    x2 = r1.astype(BF16)
    x3 = (r1 - x2.astype(F32)).astype(BF16)
    return x1, x2, x3


def _adaln_kernel(c_ref, w_ref, b_ref, o_ref):
    c = c_ref[...]
    s = c * jax.nn.sigmoid(c)
    w = w_ref[...]
    s1, s2, s3 = _split3(s)
    w1 = w.astype(BF16)
    w2 = (w - w1.astype(F32)).astype(BF16)
    acc = (_dot(s3, w1) + _dot(s2, w2)) + (_dot(s1, w2) + _dot(s2, w1))
    o_ref[...] = (acc + _dot(s1, w1)) + b_ref[...]


def adaln(cc, w, b, layer):
    _, d, n = w.shape
    tn = 1536 if n % 1536 == 0 else n
    est = 2 * d * tn * 4 * 2
    return pl.pallas_call(
        _adaln_kernel,
        out_shape=jax.ShapeDtypeStruct((8, n), F32),
        grid=(n // tn,),
        in_specs=[pl.BlockSpec((8, d), lambda j: (0, 0)),
                  pl.BlockSpec((None, d, tn), lambda j: (layer, 0, j)),
                  pl.BlockSpec((None, 1, tn), lambda j: (layer, 0, j))],
        out_specs=pl.BlockSpec((8, tn), lambda j: (0, j)),
        compiler_params=_params(("parallel",), est),
        name="adaln",
    )(cc, w, b.reshape(b.shape[0], 1, n))


def _lnmod_mm_kernel(x_ref, sh_ref, sc_ref, w_ref, o_ref, h_ref, *, row):
    @pl.when(pl.program_id(1) == 0)
    def _():
        hn = _layer_norm(x_ref[...])
        h = hn * (1.0 + sc_ref[row:row + 1, :]) + sh_ref[row:row + 1, :]
        h_ref[...] = h.astype(BF16)

    o_ref[...] = _dot(h_ref[...], w_ref[...]).astype(o_ref.dtype)


def lnmod_matmul(x, mod, row, k_shift, k_scale, w, tn):
    m, d = x.shape
    n = w.shape[1]
    tm = min(m, 1024)
    est = 2 * tm * d * 4 + tm * d * 2 + 2 * d * tn * 2 + 2 * tm * tn * 4
    return pl.pallas_call(
        functools.partial(_lnmod_mm_kernel, row=row),
        out_shape=jax.ShapeDtypeStruct((m, n), F32),
        grid=(m // tm, n // tn),
        in_specs=[pl.BlockSpec((tm, d), lambda i, j: (i, 0)),
                  pl.BlockSpec((8, d), lambda i, j: (0, k_shift)),
                  pl.BlockSpec((8, d), lambda i, j: (0, k_scale)),
                  pl.BlockSpec((d, tn), lambda i, j: (0, j))],
        out_specs=pl.BlockSpec((tm, tn), lambda i, j: (i, j)),
        scratch_shapes=[pltpu.VMEM((tm, d), BF16)],
        compiler_params=_params(("parallel", "arbitrary"), est),
        name="lnmod_matmul",
    )(x, mod, mod, w)


def _hgrn_tables(c):
    n_lvl = int(math.log2(c))
    r = np.arange(c)
    u = np.arange(c)[None, :]
    blocks, masks = [], []
    for l in range(n_lvl):
        half = 1 << l
        base = (r // (2 * half)) * (2 * half)
        anchor = (base + half - 1)[:, None]
        upper = (r >= base + half)[:, None]
        rr = r[:, None]
        if l < HGRN_SMALL_LEVELS:
            blocks.append(np.where(upper, (u > anchor) & (u <= rr), (u > rr) & (u <= anchor)))
        same = (r[:, None] // (2 * half)) == (r[None, :] // (2 * half))
        masks.append(same & upper & ~(upper.T))
    blocks.append(u <= r[:, None])
    blocks.append(np.ones((16, c), bool))
    masks.append(np.eye(c, dtype=bool))
    fwd_s = np.concatenate(blocks, axis=0).astype(np.float32)
    fwd_m = np.stack(masks).astype(np.float32)
    bwd_s = np.concatenate([b[::-1, ::-1] for b in blocks], axis=0).astype(np.float32)
    bwd_m = fwd_m[:, ::-1, ::-1]
    return (jnp.asarray(np.stack([fwd_s, bwd_s]), BF16), jnp.asarray(np.stack([fwd_m, bwd_m]), F32))


def _hgrn_kernel(q_ref, v_ref, f_ref, lb_ref, sums_ref, mask_ref, s0_ref, o_ref, sfin_ref, st_ref):
    c = HGRN_CHUNK
    n_sub = q_ref.shape[0] // c
    hd = HEAD_DIM
    n_lvl = mask_ref.shape[0] - 1
    n_small = HGRN_SMALL_LEVELS
    forward = pl.program_id(0) == 0
    j = pl.program_id(1)

    @pl.when(j == 0)
    def _():
        st_ref[...] = s0_ref[...]

    def wide_level(cum, l):
        half = 1 << l
        parts = []
        for base in range(0, c, 2 * half):
            a = base + half - 1
            mid = jnp.where(forward, cum[a:a + 1, :], cum[a + 1:a + 2, :])
            parts.append(jnp.broadcast_to(mid, (2 * half, hd)))
        anchor = parts[0] if len(parts) == 1 else jnp.concatenate(parts, axis=0)
        return -jnp.abs(cum - anchor)

    for s, h in [(s, h) for s in range(n_sub) for h in range(q_ref.shape[1] // hd)]:
        cols = slice(h * hd, (h + 1) * hd)
        rows = pl.ds(pl.multiple_of(jnp.where(forward, s * c, (n_sub - 1 - s) * c), c), c)
        q = q_ref[rows, cols]
        vb = v_ref[rows, cols].astype(BF16)
        lb = lb_ref[:, cols]
        f = lb + (1.0 - lb) * jax.nn.sigmoid(f_ref[rows, cols])
        g = jnp.log(f)
        k = 1.0 - f
        g1 = g.astype(BF16)
        g2 = (g - g1.astype(F32)).astype(BF16)
        e2 = _dot(sums_ref[...], jnp.concatenate([g1, g2], axis=1))
        e = e2[:, hd:] + e2[:, :hd]
        cum = e[n_small * c:(n_small + 1) * c]
        tot = e[(n_small + 1) * c:(n_small + 1) * c + 1]
        rem = tot - cum

        scores = _dot_nt(q.astype(BF16), k.astype(BF16)) * mask_ref[n_lvl]
        for l in range(n_lvl):
            z = jnp.exp(e[l * c:(l + 1) * c] if l < n_small else wide_level(cum, l))
            scores = scores + _dot_nt((q * z).astype(BF16), (k * z).astype(BF16)) * mask_ref[l]

        st = st_ref[h]
        o = _dot(scores.astype(BF16), vb) + _dot_nt((q * jnp.exp(cum)).astype(BF16), st.astype(BF16))
        o_ref[rows, cols] = o
        st_new = st * jnp.exp(tot) + _dot_tn(vb, (k * jnp.exp(rem)).astype(BF16))
        st_ref[h] = st_new

    @pl.when(j == pl.num_programs(1) - 1)
    def _():
        sfin_ref[...] = st_ref[...]


def hgrn_scan(proj, lb, s0):
    seq = proj.shape[0]
    c = HGRN_CHUNK
    cb = min(seq, HGRN_CHUNKS_PER_STEP * c)
    nc = seq // cb
    sums, masks = _hgrn_tables(c)
    hd, w = HEAD_DIM, A_WIDTH

    def blk(d, j):
        return jnp.where(d == 0, j, nc - 1 - j)

    est = (2 * (4 * cb * w * 4 + sums.shape[1] * c * 2 + masks.shape[1] * c * c * 4 + 2 * A_HEADS * hd * hd * 4)
           + A_HEADS * hd * hd * 4)
    return pl.pallas_call(
        _hgrn_kernel,
        out_shape=(jax.ShapeDtypeStruct((2, seq, w), F32),
                   jax.ShapeDtypeStruct((2, A_HEADS, hd, hd), F32)),
        grid=(2, nc),
        in_specs=[pl.BlockSpec((cb, w), lambda d, j: (blk(d, j), 0)),
                  pl.BlockSpec((cb, w), lambda d, j: (blk(d, j), 3)),
                  pl.BlockSpec((cb, w), lambda d, j: (blk(d, j), 1 + d)),
                  pl.BlockSpec((None, 1, w), lambda d, j: (d, 0, 0)),
                  pl.BlockSpec((None, sums.shape[1], c), lambda d, j: (d, 0, 0)),
                  pl.BlockSpec((None, masks.shape[1], c, c), lambda d, j: (d, 0, 0, 0)),
                  pl.BlockSpec((None, A_HEADS, hd, hd), lambda d, j: (d, 0, 0, 0))],
        out_specs=(pl.BlockSpec((None, cb, w), lambda d, j: (d, blk(d, j), 0)),
                   pl.BlockSpec((None, A_HEADS, hd, hd), lambda d, j: (d, 0, 0, 0))),
        scratch_shapes=[pltpu.VMEM((A_HEADS, hd, hd), F32)],
        compiler_params=_params(("parallel", "arbitrary"), est),
        name="hgrn_scan",
    )(proj, proj, proj, lb, sums, masks, s0)


def _hgrn_out_kernel(o_ref, gate_ref, g_ref, a_ref):
    hd = HEAD_DIM
    for h in range(a_ref.shape[1] // hd):
        cols = slice(h * hd, (h + 1) * hd)
        o = o_ref[0, :, cols] + o_ref[1, :, cols]
        gate = gate_ref[:, cols]
        a_ref[:, cols] = (_rms(o) * g_ref[...] * (gate * jax.nn.sigmoid(gate))).astype(a_ref.dtype)


def hgrn_out(o, proj, norm_g):
    seq = o.shape[1]
    tm = min(seq, 256)
    hd, w = HEAD_DIM, A_WIDTH
    return pl.pallas_call(
        _hgrn_out_kernel,
        out_shape=jax.ShapeDtypeStruct((seq, w), BF16),
        grid=(seq // tm,),
        in_specs=[pl.BlockSpec((2, tm, w), lambda i: (0, i, 0)),
                  pl.BlockSpec((tm, w), lambda i: (i, 4)),
                  pl.BlockSpec((1, hd), lambda i: (0, 0))],
        out_specs=pl.BlockSpec((tm, w), lambda i: (i, 0)),
        compiler_params=_params(("parallel",), 8 * tm * w * 4),
        name="hgrn_out",
    )(o, proj, norm_g.reshape(1, hd))


def _norm_rope_kernel(x_ref, g_ref, cos_ref, sin_ref, o_ref, *, scale):
    hd = HEAD_DIM
    for h in range(o_ref.shape[1] // hd):
        cols = slice(h * hd, (h + 1) * hd)
        y = _rms(x_ref[:, cols]) * g_ref[...]
        y = y * cos_ref[...] + pltpu.roll(y, hd // 2, 1) * sin_ref[...]
        o_ref[:, cols] = (y * scale).astype(o_ref.dtype)


def norm_rope(proj, col0, n_heads, g, cos, sin, scale):
    seq = proj.shape[0]
    tm = min(seq, 256)
    hd = HEAD_DIM
    w = n_heads * hd
    return pl.pallas_call(
        functools.partial(_norm_rope_kernel, scale=scale),
        out_shape=jax.ShapeDtypeStruct((seq, w), BF16),
        grid=(seq // tm,),
        in_specs=[pl.BlockSpec((tm, w), lambda i: (i, col0 // w)),
                  pl.BlockSpec((1, hd), lambda i: (0, 0)),
                  pl.BlockSpec((tm, hd), lambda i: (i, 0)),
                  pl.BlockSpec((tm, hd), lambda i: (i, 0))],
        out_specs=pl.BlockSpec((tm, w), lambda i: (i, 0)),
        compiler_params=_params(("parallel",), 6 * tm * w * 4 + 4 * tm * hd * 4),
        name="norm_rope",
    )(proj, g.reshape(1, hd), cos, sin)


def _cast_kernel(x_ref, o_ref):
    o_ref[...] = x_ref[...].astype(o_ref.dtype)


def cast_columns(proj, col0, width):
    seq = proj.shape[0]
    tm = min(seq, 1024)
    return pl.pallas_call(
        _cast_kernel,
        out_shape=jax.ShapeDtypeStruct((seq, width), BF16),
        grid=(seq // tm,),
        in_specs=[pl.BlockSpec((tm, width), lambda i: (i, col0 // width))],
        out_specs=pl.BlockSpec((tm, width), lambda i: (i, 0)),
        compiler_params=_params(("parallel",), 4 * tm * width * 4),
        name="cast_columns",
    )(proj)


def _flash_update(q, k, v, m_ref, l_ref, acc_ref):
    s = _dot_nt(q, k)
    m_prev = m_ref[...]
    m_new = jnp.maximum(m_prev, jnp.max(s, axis=-1, keepdims=True))
    alpha = jnp.exp2(m_prev - m_new)
    ps = [jnp.exp2(s[:, c * LANES:(c + 1) * LANES] - m_new) for c in range(s.shape[1] // LANES)]
    psum = ps[0]
    for pc in ps[1:]:
        psum = psum + pc
    p = jnp.concatenate([pc.astype(BF16) for pc in ps], axis=1)
    l_ref[...] = alpha * l_ref[...] + psum
    acc_ref[...] = alpha * acc_ref[...] + _dot(p, v)
    m_ref[...] = m_new


def _flash_kernel(*refs, has_ctx):
    if has_ctx:
        q_ref, k_ref, v_ref, kc_ref, vc_ref, o_ref, m_ref, l_ref, acc_ref = refs
    else:
        q_ref, k_ref, v_ref, o_ref, m_ref, l_ref, acc_ref = refs
    j = pl.program_id(2)

    @pl.when(j == 0)
    def _():
        m_ref[...] = jnp.full(m_ref.shape, -jnp.inf, F32)
        l_ref[...] = jnp.zeros(l_ref.shape, F32)
        acc_ref[...] = jnp.zeros(acc_ref.shape, F32)
        if has_ctx:
            _flash_update(q_ref[...], kc_ref[...], vc_ref[...], m_ref, l_ref, acc_ref)

    _flash_update(q_ref[...], k_ref[...], v_ref[...], m_ref, l_ref, acc_ref)

    @pl.when(j == pl.num_programs(2) - 1)
    def _():
        l = jnp.sum(l_ref[...], axis=-1, keepdims=True)
        o_ref[...] = (acc_ref[...] / l).astype(o_ref.dtype)


def flash_attention(q, k, v, k_ctx=None, v_ctx=None, *, n_heads, n_kv_heads, dq, dv):
    n, m = q.shape[0], k.shape[0]
    grp = n_heads // n_kv_heads
    tq = min(n, 2048)
    tk = min(m, 2048)
    has_ctx = k_ctx is not None
    in_specs = [pl.BlockSpec((tq, dq), lambda h, i, j: (i, h)),
                pl.BlockSpec((tk, dq), lambda h, i, j: (j, h // grp)),
                pl.BlockSpec((tk, dv), lambda h, i, j: (j, h // grp))]
    args = [q, k, v]
    if has_ctx:
        mc = k_ctx.shape[0]
        in_specs += [pl.BlockSpec((mc, dq), lambda h, i, j: (0, h // grp)),
                     pl.BlockSpec((mc, dv), lambda h, i, j: (0, h // grp))]
        args += [k_ctx, v_ctx]
    est = 2 * (tq * dq + tk * dq + tk * dv + tq * dv) * 2 + tq * (dv + 256) * 4 + 6 * tq * tk * 4
    return pl.pallas_call(
        functools.partial(_flash_kernel, has_ctx=has_ctx),
        out_shape=jax.ShapeDtypeStruct((n, n_heads * dv), BF16),
        grid=(n_heads, n // tq, m // tk),
        in_specs=in_specs,
        out_specs=pl.BlockSpec((tq, dv), lambda h, i, j: (i, h)),
        scratch_shapes=[pltpu.VMEM((tq, LANES), F32), pltpu.VMEM((tq, LANES), F32), pltpu.VMEM((tq, dv), F32)],
        compiler_params=_params(("parallel", "parallel", "arbitrary"), est),
        name="flash_attention",
    )(*args)


def _proj_postnorm_kernel(*refs, n_in, row):
    a_refs = refs[:n_in]
    w_refs = refs[n_in:2 * n_in]
    x_ref, gate_ref, g_ref, b_ref, o_ref = refs[2 * n_in:]
    y = _dot(a_refs[0][...], w_refs[0][...])
    for a_ref, w_ref in zip(a_refs[1:], w_refs[1:]):
        y = y + _dot(a_ref[...], w_ref[...])
    z = DEEPNORM_ALPHA * x_ref[...] + gate_ref[row:row + 1, :] * y
    o_ref[...] = _layer_norm(z) * g_ref[...] + b_ref[...]


def proj_postnorm(acts, ws, x, mod, row, k_gate, g, b):
    m, d = x.shape
    tm = min(m, 512)
    n_in = len(acts)
    once = pl.Buffered(1)
    in_specs = [pl.BlockSpec((tm, a.shape[1]), lambda i: (i, 0)) for a in acts]
    in_specs += [pl.BlockSpec(w.shape, lambda i: (0, 0), pipeline_mode=once) for w in ws]
    in_specs += [pl.BlockSpec((tm, d), lambda i: (i, 0)),
                 pl.BlockSpec((8, d), lambda i: (0, k_gate)),
                 pl.BlockSpec((1, d), lambda i: (0, 0)),
                 pl.BlockSpec((1, d), lambda i: (0, 0))]
    est = sum(w.size * 2 for w in ws) + sum(2 * tm * a.shape[1] * 2 for a in acts) + 6 * tm * d * 4
    return pl.pallas_call(
        functools.partial(_proj_postnorm_kernel, n_in=n_in, row=row),
        out_shape=jax.ShapeDtypeStruct((m, d), F32),
        grid=(m // tm,),
        in_specs=in_specs,
        out_specs=pl.BlockSpec((tm, d), lambda i: (i, 0)),
        compiler_params=_params(("parallel",), est),
        name="proj_postnorm",
    )(*acts, *ws, x, mod, g.reshape(1, d), b.reshape(1, d))


def _add_postnorm_kernel(y_ref, x_ref, gate_ref, g_ref, b_ref, o_ref, *, row):
    z = DEEPNORM_ALPHA * x_ref[...] + gate_ref[row:row + 1, :] * y_ref[...]
    o_ref[...] = _layer_norm(z) * g_ref[...] + b_ref[...]


def add_postnorm(y, x, mod, row, k_gate, g, b):
    m, d = x.shape
    tm = min(m, 512)
    return pl.pallas_call(
        functools.partial(_add_postnorm_kernel, row=row),
        out_shape=jax.ShapeDtypeStruct((m, d), F32),
        grid=(m // tm,),
        in_specs=[pl.BlockSpec((tm, d), lambda i: (i, 0)),
                  pl.BlockSpec((tm, d), lambda i: (i, 0)),
                  pl.BlockSpec((8, d), lambda i: (0, k_gate)),
                  pl.BlockSpec((1, d), lambda i: (0, 0)),
                  pl.BlockSpec((1, d), lambda i: (0, 0))],
        out_specs=pl.BlockSpec((tm, d), lambda i: (i, 0)),
        compiler_params=_params(("parallel",), 8 * tm * d * 4),
        name="add_postnorm",
    )(y, x, mod, g.reshape(1, d), b.reshape(1, d))


def _rms_mm_kernel(x_ref, g_ref, w_ref, o_ref, a_ref):
    @pl.when(pl.program_id(1) == 0)
    def _():
        a_ref[...] = (_rms(x_ref[...]) * g_ref[...]).astype(BF16)

    o_ref[...] = _dot(a_ref[...], w_ref[...]).astype(o_ref.dtype)


def _mla_q_kernel(x_ref, g_ref, w_ref, cos_ref, sa_ref, sb_ref, o_ref, a_ref, *, scale):
    @pl.when(pl.program_id(1) == 0)
    def _():
        a_ref[...] = (_rms(x_ref[...]) * g_ref[...]).astype(BF16)

    y = _dot(a_ref[...], w_ref[...])
    hd = HEAD_DIM
    for h in range(y.shape[1] // MLA_QK):
        c0 = h * MLA_QK
        o_ref[:, c0:c0 + hd] = (y[:, c0:c0 + hd] * scale).astype(o_ref.dtype)
        r = y[:, c0 + hd:c0 + 2 * hd]
        r = r * cos_ref[...] + pltpu.roll(r, hd - MLA_ROPE // 2, 1) * sa_ref[...] + pltpu.roll(r, MLA_ROPE // 2, 1) * sb_ref[...]
        o_ref[:, c0 + hd:c0 + 2 * hd] = (r * scale).astype(o_ref.dtype)


def _mla_k_kernel(x_ref, g_ref, w_ref, kr_ref, cos_ref, sa_ref, sb_ref, o_ref, a_ref, r_ref):
    hd = HEAD_DIM

    @pl.when(pl.program_id(1) == 0)
    def _():
        a_ref[...] = (_rms(x_ref[...]) * g_ref[...]).astype(BF16)
        r = kr_ref[...]
        r = r * cos_ref[...] + pltpu.roll(r, hd - MLA_ROPE // 2, 1) * sa_ref[...] + pltpu.roll(r, MLA_ROPE // 2, 1) * sb_ref[...]
        r_ref[...] = r.astype(BF16)

    y = _dot(a_ref[...], w_ref[...])
    for h in range(y.shape[1] // hd):
        o_ref[:, h * MLA_QK:h * MLA_QK + hd] = y[:, h * hd:(h + 1) * hd].astype(o_ref.dtype)
        o_ref[:, h * MLA_QK + hd:(h + 1) * MLA_QK] = r_ref[...]


def _mla_specs(m, tm, lora, col_blk):
    return [pl.BlockSpec((tm, lora), lambda i, j: (i, col_blk)),
            pl.BlockSpec((1, lora), lambda i, j: (0, 0))]


def rms_matmul(dn, col_blk, g, w, tn):
    m = dn.shape[0]
    lora, n = w.shape
    tm = min(m, 1024)
    est = 2 * tm * lora * 4 + tm * lora * 2 + 2 * lora * tn * 2 + 2 * tm * tn * 2 + tm * tn * 4
    return pl.pallas_call(
        _rms_mm_kernel,
        out_shape=jax.ShapeDtypeStruct((m, n), BF16),
        grid=(m // tm, n // tn),
        in_specs=_mla_specs(m, tm, lora, col_blk) + [pl.BlockSpec((lora, tn), lambda i, j: (0, j))],
        out_specs=pl.BlockSpec((tm, tn), lambda i, j: (i, j)),
        scratch_shapes=[pltpu.VMEM((tm, lora), BF16)],
        compiler_params=_params(("parallel", "arbitrary"), est),
        name="rms_matmul",
    )(dn, g.reshape(1, lora), w)


def mla_q(dn, g, w, cos, sa, sb, scale):
    m = dn.shape[0]
    lora, n = w.shape
    tm = min(m, 1024)
    tn = 4 * MLA_QK
    hd = HEAD_DIM
    est = 2 * tm * lora * 4 + tm * lora * 2 + 2 * lora * tn * 2 + 2 * tm * tn * 2 + 2 * tm * tn * 4 + 6 * tm * hd * 4
    rope_spec = pl.BlockSpec((tm, hd), lambda i, j: (i, 0))
    return pl.pallas_call(
        functools.partial(_mla_q_kernel, scale=scale),
        out_shape=jax.ShapeDtypeStruct((m, n), BF16),
        grid=(m // tm, n // tn),
        in_specs=_mla_specs(m, tm, lora, 0) + [pl.BlockSpec((lora, tn), lambda i, j: (0, j)),
                                               rope_spec, rope_spec, rope_spec],
        out_specs=pl.BlockSpec((tm, tn), lambda i, j: (i, j)),
        scratch_shapes=[pltpu.VMEM((tm, lora), BF16)],
        compiler_params=_params(("parallel", "arbitrary"), est),
        name="mla_q",
    )(dn, g.reshape(1, lora), w, cos, sa, sb)


def mla_k(dn, g, w, cos, sa, sb):
    m = dn.shape[0]
    lora, n = w.shape
    tm = min(m, 1024)
    hd = HEAD_DIM
    tn = 4 * hd
    kr_blk = (MLA_Q_LORA + MLA_KV_LORA) // hd
    est = 2 * tm * lora * 4 + tm * lora * 2 + 2 * lora * tn * 2 + 4 * tm * tn * 2 + tm * tn * 4 + 8 * tm * hd * 4
    rope_spec = pl.BlockSpec((tm, hd), lambda i, j: (i, 0))
    return pl.pallas_call(
        _mla_k_kernel,
        out_shape=jax.ShapeDtypeStruct((m, 2 * n), BF16),
        grid=(m // tm, n // tn),
        in_specs=_mla_specs(m, tm, lora, 1) + [pl.BlockSpec((lora, tn), lambda i, j: (0, j)),
                                               pl.BlockSpec((tm, hd), lambda i, j: (i, kr_blk)),
                                               rope_spec, rope_spec, rope_spec],
        out_specs=pl.BlockSpec((tm, 2 * tn), lambda i, j: (i, j)),
        scratch_shapes=[pltpu.VMEM((tm, lora), BF16), pltpu.VMEM((tm, hd), BF16)],
        compiler_params=_params(("parallel", "arbitrary"), est),
        name="mla_k",
    )(dn, g.reshape(1, lora), w, dn, cos, sa, sb)


def _router_kernel(x_ref, sh_ref, sc_ref, wr_ref, h_ref, aff_ref, *, row):
    hn = _layer_norm(x_ref[...])
    h = hn * (1.0 + sc_ref[row:row + 1, :]) + sh_ref[row:row + 1, :]
    hb = h.astype(BF16)
    half = h.shape[1] // 2
    bits = pltpu.bitcast(hb.astype(F32), jnp.uint32)
    h_ref[...] = (bits[:, half:] & jnp.uint32(0xFFFF0000)) | (bits[:, :half] >> 16)
    w = wr_ref[...]
    w1 = w.astype(BF16)
    w2 = (w - w1.astype(F32)).astype(BF16)
    h2 = (h - hb.astype(F32)).astype(BF16)
    logits = _dot_nt(w1, hb) + (_dot_nt(w2, hb) + _dot_nt(w1, h2))
    mx = jnp.max(logits, axis=0, keepdims=True)
    p = jnp.exp(logits - mx)
    aff_ref[...] = p / jnp.sum(p, axis=0, keepdims=True)


def moe_router(x, mod, row, k_shift, k_scale, w_router_t):
    m, d = x.shape
    e = w_router_t.shape[0]
    tm = min(m, 512)
    return pl.pallas_call(
        functools.partial(_router_kernel, row=row),
        out_shape=(jax.ShapeDtypeStruct((m, d // 2), jnp.uint32), jax.ShapeDtypeStruct((e, m), F32)),
        grid=(m // tm,),
        in_specs=[pl.BlockSpec((tm, d), lambda i: (i, 0)),
                  pl.BlockSpec((8, d), lambda i: (0, k_shift)),
                  pl.BlockSpec((8, d), lambda i: (0, k_scale)),
                  pl.BlockSpec((e, d), lambda i: (0, 0))],
        out_specs=(pl.BlockSpec((tm, d // 2), lambda i: (i, 0)), pl.BlockSpec((e, tm), lambda i: (0, i))),
        compiler_params=_params(("parallel",), 8 * tm * d * 4),
        name="moe_router",
    )(x, mod, mod, w_router_t)


def _ffn_up_kernel(*refs, n_seg):
    x_refs, (wg_ref, wu_ref) = refs[:n_seg], refs[n_seg:n_seg + 2]
    o_refs, xs_refs = refs[n_seg + 2:2 * n_seg + 2], refs[2 * n_seg + 2:]

    @pl.when(pl.program_id(1) == 0)
    def _():
        for x_ref, xs_ref in zip(x_refs, xs_refs):
            word = x_ref[...]
            first = pltpu.bitcast(word << 16, F32)
            second = pltpu.bitcast(word & jnp.uint32(0xFFFF0000), F32)
            xs_ref[...] = jnp.concatenate([first, second], axis=1).astype(BF16)

    wg = wg_ref[...].astype(BF16)
    wu = wu_ref[...].astype(BF16)
    for xs_ref, o_ref in zip(xs_refs, o_refs):
        x = xs_ref[...]
        g = _dot(x, wg)
        u = _dot(x, wu)
        o_ref[...] = (g * jax.nn.sigmoid(g) * u).astype(o_ref.dtype)


def _ffn_down_kernel(*refs, n_seg):
    h_refs, wd_ref, wt_refs, o_refs = refs[:n_seg], refs[n_seg], refs[n_seg + 1:2 * n_seg + 1], refs[2 * n_seg + 1:]
    wd = wd_ref[...].astype(BF16)
    for h_ref, wt_ref, o_ref in zip(h_refs, wt_refs, o_refs):
        y = _dot(h_ref[...], wd) * wt_ref[...]
        hi = y.astype(BF16)
        o_ref[0] = hi
        o_ref[1] = (y - hi.astype(F32)).astype(BF16)


def expert_ffn(xgs, wts, w_gate, w_up, w_down, layer):
    n_seg = len(xgs)
    e = xgs[0].shape[0]
    d = 2 * xgs[0].shape[2]
    f = w_gate.shape[3]
    rs = [x.shape[1] for x in xgs]
    r = sum(rs)
    tf = min(f, 256)
    est = 2 * (r * d * 2 + 2 * d * tf * 4 + r * tf * 2) + r * d * 2 + 2 * d * tf * 2 + 3 * r * tf * 4
    hids = pl.pallas_call(
        functools.partial(_ffn_up_kernel, n_seg=n_seg),
        out_shape=[jax.ShapeDtypeStruct((e, ri, f), BF16) for ri in rs],
        grid=(e, f // tf),
        in_specs=[pl.BlockSpec((None, ri, d // 2), lambda i, j: (i, 0, 0)) for ri in rs]
        + [pl.BlockSpec((None, None, d, tf), lambda i, j: (layer, i, 0, j))] * 2,
        out_specs=[pl.BlockSpec((None, ri, tf), lambda i, j: (i, 0, j)) for ri in rs],
        scratch_shapes=[pltpu.VMEM((ri, d), BF16) for ri in rs],
        compiler_params=_params(("parallel", "arbitrary"), est),
        name="ffn_up",
    )(*xgs, w_gate, w_up)
    tn = min(d, 512)
    est = 2 * (r * f * 2 + f * tn * 4 + r * tn * 4 + r * LANES * 4) + f * tn * 2 + 2 * r * tn * 4
    return pl.pallas_call(
        functools.partial(_ffn_down_kernel, n_seg=n_seg),
        out_shape=[jax.ShapeDtypeStruct((e, 2, ri, d), BF16) for ri in rs],
        grid=(e, d // tn),
        in_specs=[pl.BlockSpec((None, ri, f), lambda i, j: (i, 0, 0)) for ri in rs]
        + [pl.BlockSpec((None, None, f, tn), lambda i, j: (layer, i, 0, j))]
        + [pl.BlockSpec((None, ri, 1), lambda i, j: (i, 0, 0)) for ri in rs],
        out_specs=[pl.BlockSpec((None, 2, ri, tn), lambda i, j: (i, 0, 0, j)) for ri in rs],
        compiler_params=_params(("parallel", "arbitrary"), est),
        name="ffn_down",
    )(*hids, w_down, *wts)


def _select_kernel(aff_ref, pos_ref, idx_ref, wt_ref, off_ref, incl_ref, tot_ref, offs_ref, *, cap):
    e, g, ln = aff_ref.shape
    bits = pltpu.bitcast(aff_ref[...], jnp.int32)

    def count(mask):
        per_lane = jnp.sum(jnp.where(mask, 1.0, 0.0), axis=1)
        return jnp.sum(per_lane, axis=1, keepdims=True)[:, :, None]

    def search(i, t):
        cand = t | jnp.left_shift(jnp.int32(1), 30 - i)
        return jnp.where(count(bits >= cand) >= cap, cand, t)

    thr = lax.fori_loop(0, 31, search, jnp.zeros((e, 1, 1), jnp.int32))
    gt = bits > thr
    eq = bits == thr
    need = cap - count(gt)

    r0 = lax.broadcasted_iota(jnp.int32, (ln, ln), 0)
    r1 = lax.broadcasted_iota(jnp.int32, (ln, ln), 1)
    upper = jnp.where(r0 <= r1, 1.0, 0.0).astype(BF16)
    ones = jnp.ones((ln, ln), BF16)
    g0 = lax.broadcasted_iota(jnp.int32, (g, g), 0)
    g1 = lax.broadcasted_iota(jnp.int32, (g, g), 1)
    earlier = jnp.where(g1 < g0, 1.0, 0.0).astype(BF16)

    def prefix(mask):
        x = jnp.where(mask, 1.0, 0.0).astype(BF16).reshape(e * g, ln)
        incl = _dot(x, upper).reshape(e, g, ln)
        tot = _dot(x, ones).reshape(e, g, ln)
        off = jnp.stack([_dot(earlier, tot[i].astype(BF16)) for i in range(e)])
        return incl, tot, off

    incl_eq, _, off_eq = prefix(eq)
    sel = gt | (eq & (off_eq + incl_eq - 1.0 < need))
    incl, tot, off = prefix(sel)
    pos_ref[...] = jnp.where(sel, off + incl - 1.0, -1.0).astype(jnp.int32)
    off_ref[...] = off.astype(jnp.int32)
    incl_ref[...] = incl
    tot_ref[...] = tot
    offs_ref[...] = off

    slot = lax.broadcasted_iota(jnp.int32, (cap, ln), 0).astype(F32)
    lane = lax.broadcasted_iota(jnp.int32, (cap, ln), 1).astype(F32)

    def tokens_of_slots(i, c):
        ends = jnp.transpose(offs_ref[i] + tot_ref[i])[0:1, :]
        before = jnp.where(ends <= slot, 1.0, 0.0).astype(BF16)
        grp = _dot(before, ones)
        rank = slot - _dot(before, tot_ref[i].astype(BF16))
        pick = jnp.where(lane == grp, 1.0, 0.0).astype(BF16)
        incl_g = _dot(pick, incl_ref[i].astype(BF16))
        lane_p = _dot(jnp.where(incl_g <= rank, 1.0, 0.0).astype(BF16), ones)
        idx_ref[i] = (grp * ln + lane_p)[:, 0:1].astype(jnp.int32)
        a1, a2, a3 = _split3(aff_ref[i])
        aff_g = (_dot(pick, a3) + _dot(pick, a2)) + _dot(pick, a1)
        wt_ref[i] = jnp.sum(jnp.where(lane == lane_p, aff_g, 0.0), axis=1, keepdims=True)
        return c

    lax.fori_loop(0, e, tokens_of_slots, 0)


def moe_select(aff_t, cap):
    e, n = aff_t.shape
    g = n // LANES
    assert g <= LANES, "token groups are mapped onto the 128 lanes"
    a = aff_t.reshape(e, g, LANES)
    if g != LANES:
        a = jnp.concatenate([a, jnp.full((e, LANES - g, LANES), -1.0, F32)], axis=1)
    shp = jax.ShapeDtypeStruct((e, LANES, LANES), jnp.int32)
    full = pl.BlockSpec((e, LANES, LANES), lambda i: (0, 0, 0))
    per_slot = pl.BlockSpec((e, cap, 1), lambda i: (0, 0, 0))
    pos, idx, wt, off = pl.pallas_call(
        functools.partial(_select_kernel, cap=cap),
        out_shape=(shp, jax.ShapeDtypeStruct((e, cap, 1), jnp.int32), jax.ShapeDtypeStruct((e, cap, 1), F32), shp),
        grid=(1,),
        in_specs=[full],
        out_specs=(full, per_slot, per_slot, full),
        scratch_shapes=[pltpu.VMEM((e, LANES, LANES), F32)] * 3,
        compiler_params=_params(("arbitrary",), 28 * e * LANES * LANES * 4 + 16 * cap * LANES * 4),
        name="moe_select",
    )(a)
    return pos.reshape(e, LANES * LANES)[:, :n], idx.reshape(e, 1, cap), wt, off[:, :g, 0]


def _row_copy(h_hbm, x_ref, sem, token, row):
    return pltpu.make_async_copy(h_hbm.at[pl.ds(token, 1), :], x_ref.at[0, pl.ds(row, 1), :], sem)


def _gather_kernel(idx_ref, h_hbm, x_ref, sem):
    cap = x_ref.shape[1]

    def rows(i, c):
        for u in range(GATHER_UNROLL):
            s = i * GATHER_UNROLL + u
            _row_copy(h_hbm, x_ref, sem, idx_ref[0, s], s).start()
        return c

    lax.fori_loop(0, cap // GATHER_UNROLL, rows, 0)
    pltpu.make_async_copy(h_hbm.at[pl.ds(0, cap), :], x_ref.at[0], sem).wait()


def moe_gather(hp, idx):
    w = hp.shape[1]
    e, _, cap = idx.shape
    return pl.pallas_call(
        _gather_kernel,
        out_shape=jax.ShapeDtypeStruct((e, cap, w), jnp.uint32),
        grid=(e,),
        in_specs=[pl.BlockSpec((None, 1, cap), lambda i: (i, 0, 0), memory_space=pltpu.SMEM),
                  pl.BlockSpec(memory_space=pl.ANY)],
        out_specs=pl.BlockSpec((1, cap, w), lambda i: (i, 0, 0)),
        scratch_shapes=[pltpu.SemaphoreType.DMA(())],
        compiler_params=_params(("arbitrary",), 2 * cap * w * 4),
        name="moe_gather",
    )(idx, hp)


def _window_copy(y_hbm, dst, sem, e, src, win):
    return pltpu.make_async_copy(y_hbm.at[e, :, pl.ds(src, win)], dst, sem)


def _combine_kernel(offb_ref, y_hbm, pos_ref, x_ref, gate_ref, g_ref, b_ref, o_ref, ybuf, ybuf_x, sem, acc_ref,
                    *, row, cap, win):
    b = pl.program_id(0)
    n_exp, tb = pos_ref.shape
    half = n_exp // 2
    par = b % 2

    def window(e, k, blk=b):
        first = (offb_ref[e, blk] // 8) * 8 + k * win
        return first, pl.multiple_of(jnp.minimum(first, cap - win), 8)

    def onehot2(e, first, src):
        slots = src + lax.broadcasted_iota(jnp.int32, (win, 1), 0)
        hit = jnp.logical_and(pos_ref[e:e + 1, :] == slots, slots >= first)
        oh = jnp.where(hit, 1.0, 0.0).astype(BF16)
        return jnp.concatenate([oh, oh], axis=0)

    def copies(h, blk, p):
        return [_window_copy(y_hbm, ybuf.at[p, h, j], sem.at[2 * p + h], h * half + j,
                             window(h * half + j, 0, blk)[1], win) for j in range(half)]

    def start_block(blk, p):
        for h in range(2):
            for cp in copies(h, blk, p):
                cp.start()

    @pl.when(b == 0)
    def _():
        start_block(b, par)

    @pl.when(b + 1 < pl.num_programs(0))
    def _():
        start_block(b + 1, 1 - par)

    acc = None
    for h in range(2):
        for cp in copies(h, b, par):
            cp.wait()
        lhs = jnp.concatenate([onehot2(h * half + j, *window(h * half + j, 0)) for j in range(half)], axis=0)
        part = _dot_tn(lhs, ybuf[par, h].reshape(half * 2 * win, ybuf.shape[-1]))
        acc = part if acc is None else acc + part
    acc_ref[...] = acc

    for e in range(n_exp):
        n_win = (offb_ref[e, b + 1] - (offb_ref[e, b] // 8) * 8 + win - 1) // win

        def extra(k, c, e=e):
            first_k, src_k = window(e, k)
            cp = _window_copy(y_hbm, ybuf_x, sem.at[4], e, src_k, win)
            cp.start()
            cp.wait()
            acc_ref[...] += _dot_tn(onehot2(e, first_k, src_k), ybuf_x[...].reshape(2 * win, ybuf_x.shape[-1]))
            return c

        lax.fori_loop(1, n_win, extra, 0)

    z = DEEPNORM_ALPHA * x_ref[...] + gate_ref[row:row + 1, :] * acc_ref[...]
    o_ref[...] = _layer_norm(z) * g_ref[...] + b_ref[...]


def moe_combine_postnorm(y, pos, off, x, mod, row, k_gate, g, b):
    n, d = x.shape
    e, _, cap, _ = y.shape
    tb = min(n, 256)
    nb = n // tb
    win = min(cap, COMBINE_WINDOW)
    offb = jnp.concatenate([off[:, ::tb // LANES], jnp.full((e, 1), cap, jnp.int32)], axis=1)
    grid_spec = pltpu.PrefetchScalarGridSpec(
        num_scalar_prefetch=1,
        grid=(nb,),
        in_specs=[pl.BlockSpec(memory_space=pl.ANY),
                  pl.BlockSpec((e, tb), lambda i, o: (0, i)),
                  pl.BlockSpec((tb, d), lambda i, o: (i, 0)),
                  pl.BlockSpec((8, d), lambda i, o: (0, k_gate)),
                  pl.BlockSpec((1, d), lambda i, o: (0, 0)),
                  pl.BlockSpec((1, d), lambda i, o: (0, 0))],
        out_specs=pl.BlockSpec((tb, d), lambda i, o: (i, 0)),
        scratch_shapes=[pltpu.VMEM((2, 2, e // 2, 2, win, d), BF16), pltpu.VMEM((2, win, d), BF16),
                        pltpu.SemaphoreType.DMA((5,)), pltpu.VMEM((tb, d), F32)],
    )
    est = 8 * tb * d * 4 + (2 * e + 1) * 2 * win * d * 2 + e * win * tb * 2
    return pl.pallas_call(
        functools.partial(_combine_kernel, row=row, cap=cap, win=win),
        out_shape=jax.ShapeDtypeStruct((n, d), F32),
        grid_spec=grid_spec,
        compiler_params=_params(("arbitrary",), est),
        name="moe_combine",
    )(offb, y, pos, x, mod, g.reshape(1, d), b.reshape(1, d))


def moe_route(x, mod, row, w_router_t):
    m = x.shape[0]
    cap = max(1, EC_CAPACITY_FACTOR * m // N_EXPERTS)
    h, aff_t = moe_router(x, mod, row, 3, 4, w_router_t)
    pos, idx, wt, off = moe_select(aff_t, cap)
    return moe_gather(h, idx), wt, (pos, off)


def _rope_angles(n_tokens, rot_dim):
    rows = n_tokens // GRID_W
    row = jnp.repeat(jnp.arange(rows, dtype=F32), GRID_W)
    col = jnp.tile(jnp.arange(GRID_W, dtype=F32), rows)
    n_freq = rot_dim // 4
    inv = ROPE_THETA ** (-jnp.arange(n_freq, dtype=F32) / n_freq)
    return jnp.concatenate([row[:, None] * inv, col[:, None] * inv], axis=-1)


def _gqa_rope_tables(n_tokens):
    ang = _rope_angles(n_tokens, HEAD_DIM)
    c, s = jnp.cos(ang), jnp.sin(ang)
    return jnp.concatenate([c, c], axis=-1), jnp.concatenate([-s, s], axis=-1)


def _mla_rope_tables(n_tokens):
    ang = _rope_angles(n_tokens, MLA_ROPE)
    c, s = jnp.cos(ang), jnp.sin(ang)
    z = jnp.zeros_like(c)
    cos = jnp.concatenate([c, c, z, z], axis=-1)
    sa = jnp.concatenate([-s, z, z, z], axis=-1)
    sb = jnp.concatenate([z, s, z, z], axis=-1)
    return cos, sa, sb


def kernel(x, c, ctx, c_ctx, ada_w, ada_b, ln_g, ln_b, ev_w_in, ev_w_out, hgrn_lb, hgrn_norm_g, gqa_q_norm_g, gqa_k_norm_g, mla_w_down, mla_q_norm_g, mla_kv_norm_g, mla_w_uq, mla_w_ukv, mla_w_o, moe_router, moe_w_gate, moe_w_up, moe_w_down):
    d = D_MODEL
    xl = x[0]
    xc = ctx[0]
    n_lat, n_ctx = xl.shape[0], xc.shape[0]
    cc = jnp.zeros((8, d), F32).at[0].set(c[0]).at[1].set(c_ctx)
    lb_all = jnp.cumsum(jax.nn.softmax(hgrn_lb.astype(F32), axis=1), axis=1)
    gqa_tabs = _gqa_rope_tables(n_lat)
    gqa_tabs_ctx = [jnp.ones((n_ctx, HEAD_DIM), F32), jnp.zeros((n_ctx, HEAD_DIM), F32)]
    mla_tabs = _mla_rope_tables(n_lat)
    mla_tabs_ctx = [jnp.ones((n_ctx, HEAD_DIM), F32), jnp.zeros((n_ctx, HEAD_DIM), F32), jnp.zeros((n_ctx, HEAD_DIM), F32)]
    LAT, CTX = 0, 1

    for l in range(DEPTH):
        last = l == DEPTH - 1
        i = l // 2
        mod = adaln(cc, ada_w, ada_b, l)
        if l % 2 == 0:
            w_in = ev_w_in[i].astype(BF16)
            w_out = ev_w_out[i].astype(BF16)
            lb = lb_all[:, l].reshape(2, 1, A_WIDTH)
            scale = HEAD_DIM ** -0.5 * LOG2E
            proj_c = lnmod_matmul(xc, mod, CTX, 0, 1, w_in, 512)
            proj_l = lnmod_matmul(xl, mod, LAT, 0, 1, w_in, 512)
            s0 = jnp.zeros((2, A_HEADS, HEAD_DIM, HEAD_DIM), F32)
            o_c, s_c = hgrn_scan(proj_c, lb, s0)
            o_l, _ = hgrn_scan(proj_l, lb, s_c)
            a_l = hgrn_out(o_l, proj_l, hgrn_norm_g[i])
            qcol, kcol, vcol = 5 * A_WIDTH, 5 * A_WIDTH + B_WIDTH, 5 * A_WIDTH + B_WIDTH + B_KV_WIDTH
            q_l = norm_rope(proj_l, qcol, B_Q_HEADS, gqa_q_norm_g[i], *gqa_tabs, scale)
            k_l = norm_rope(proj_l, kcol, B_KV_HEADS, gqa_k_norm_g[i], *gqa_tabs, 1.0)
            k_c = norm_rope(proj_c, kcol, B_KV_HEADS, gqa_k_norm_g[i], *gqa_tabs_ctx, 1.0)
            v_l = cast_columns(proj_l, vcol, B_KV_WIDTH)
            v_c = cast_columns(proj_c, vcol, B_KV_WIDTH)
            att = dict(n_heads=B_Q_HEADS, n_kv_heads=B_KV_HEADS, dq=HEAD_DIM, dv=HEAD_DIM)
            b_l = flash_attention(q_l, k_l, v_l, k_c, v_c, **att)
            w_parts = [w_out[:A_WIDTH], w_out[A_WIDTH:]]
            xl_new = proj_postnorm([a_l, b_l], w_parts, xl, mod, LAT, 2, ln_g[l, 0], ln_b[l, 0])
            if not last:
                a_c = hgrn_out(o_c, proj_c, hgrn_norm_g[i])
                q_c = norm_rope(proj_c, qcol, B_Q_HEADS, gqa_q_norm_g[i], *gqa_tabs_ctx, scale)
                b_c = flash_attention(q_c, k_c, v_c, **att)
                xc = proj_postnorm([a_c, b_c], w_parts, xc, mod, CTX, 2, ln_g[l, 0], ln_b[l, 0])
            xl = xl_new
        else:
            hd = HEAD_DIM
            pad = (-mla_w_down.shape[2]) % hd
            w_down = jnp.pad(mla_w_down[i], ((0, 0), (0, pad))).astype(BF16)
            w_uq = mla_w_uq[i].reshape(MLA_Q_LORA, MLA_HEADS, hd + MLA_ROPE)
            w_uq = jnp.pad(w_uq, ((0, 0), (0, 0), (0, MLA_QK - hd - MLA_ROPE))).reshape(MLA_Q_LORA, MLA_HEADS * MLA_QK).astype(BF16)
            w_ukv = mla_w_ukv[i].reshape(MLA_KV_LORA, MLA_HEADS, 2 * hd)
            w_uk = w_ukv[:, :, :hd].reshape(MLA_KV_LORA, MLA_HEADS * hd).astype(BF16)
            w_uv = w_ukv[:, :, hd:].reshape(MLA_KV_LORA, MLA_HEADS * hd).astype(BF16)
            w_o = mla_w_o[i].astype(BF16)
            scale = (hd + MLA_ROPE) ** -0.5 * LOG2E
            dn_c = lnmod_matmul(xc, mod, CTX, 0, 1, w_down, w_down.shape[1])
            dn_l = lnmod_matmul(xl, mod, LAT, 0, 1, w_down, w_down.shape[1])
            q_l = mla_q(dn_l, mla_q_norm_g[i], w_uq, *mla_tabs, scale)
            k_l = mla_k(dn_l, mla_kv_norm_g[i], w_uk, *mla_tabs)
            k_c = mla_k(dn_c, mla_kv_norm_g[i], w_uk, *mla_tabs_ctx)
            v_l = rms_matmul(dn_l, 1, mla_kv_norm_g[i], w_uv, 1024)
            v_c = rms_matmul(dn_c, 1, mla_kv_norm_g[i], w_uv, 1024)
            att = dict(n_heads=MLA_HEADS, n_kv_heads=MLA_HEADS, dq=MLA_QK, dv=hd)
            o_l = flash_attention(q_l, k_l, v_l, k_c, v_c, **att)
            xl_new = proj_postnorm([o_l], [w_o], xl, mod, LAT, 2, ln_g[l, 0], ln_b[l, 0])
            if not last:
                q_c = mla_q(dn_c, mla_q_norm_g[i], w_uq, *mla_tabs_ctx, scale)
                o_c = flash_attention(q_c, k_c, v_c, **att)
                xc = proj_postnorm([o_c], [w_o], xc, mod, CTX, 2, ln_g[l, 0], ln_b[l, 0])
            xl = xl_new

        w_router_t = moe_router[l].T
        segs = [(xl, LAT)] if last else [(xl, LAT), (xc, CTX)]
        routes = [moe_route(xs, mod, row, w_router_t) for xs, row in segs]
        ys = expert_ffn([r[0] for r in routes], [r[1] for r in routes], moe_w_gate, moe_w_up, moe_w_down, l)
        outs = [moe_combine_postnorm(y, *r[2], xs, mod, row, 5, ln_g[l, 1], ln_b[l, 1])
                for y, r, (xs, row) in zip(ys, routes, segs)]
        xl = outs[0]
        if not last:
            xc = outs[1]
    return xl[None]
```

```python
import functools
import math

import numpy as np
import jax
import jax.numpy as jnp
from jax import lax
from jax.experimental import pallas as pl
from jax.experimental.pallas import tpu as pltpu

F32 = jnp.float32
BF16 = jnp.bfloat16

D_MODEL = 2048
DEPTH = 2
GRID_W = 64
HEAD_DIM = 128
A_HEADS = D_MODEL // 256
A_WIDTH = A_HEADS * HEAD_DIM
B_Q_HEADS = D_MODEL // 256
B_KV_HEADS = 2
B_WIDTH = B_Q_HEADS * HEAD_DIM
B_KV_WIDTH = B_KV_HEADS * HEAD_DIM
MLA_HEADS = D_MODEL // 128
MLA_Q_LORA = 512
MLA_KV_LORA = 512
MLA_ROPE = 64
MLA_QK = 2 * HEAD_DIM
N_EXPERTS = 16
EXPERT_FF = D_MODEL // 2
EC_CAPACITY_FACTOR = 2
ROPE_THETA = 10000.0
NORM_EPS = 1e-6
DEEPNORM_ALPHA = (2.0 * DEPTH) ** 0.25

HGRN_CHUNK = 128
HGRN_CHUNKS_PER_STEP = 4
HGRN_SMALL_LEVELS = 3
GATHER_UNROLL = 8
COMBINE_WINDOW = 64
LANES = 128
LOG2E = math.log2(math.e)
V7X_VMEM_BYTES = 64 * 1024 * 1024
VMEM_CAP_BYTES = V7X_VMEM_BYTES - 8 * 1024 * 1024


def _params(semantics, vmem_estimate_bytes):
    limit = int(min(max(2 * vmem_estimate_bytes, 32 * 1024 * 1024), VMEM_CAP_BYTES))
    return pltpu.CompilerParams(dimension_semantics=semantics, vmem_limit_bytes=limit)


def _layer_norm(x):
    mu = jnp.mean(x, axis=-1, keepdims=True)
    xc = x - mu
    var = jnp.mean(xc * xc, axis=-1, keepdims=True)
    return xc * lax.rsqrt(var + NORM_EPS)


def _rms(x):
    return x * lax.rsqrt(jnp.mean(x * x, axis=-1, keepdims=True) + NORM_EPS)


def _dot(a, b):
    return jnp.dot(a, b, preferred_element_type=F32)


def _dot_nt(a, b):
    return lax.dot_general(a, b, (((1,), (1,)), ((), ())), preferred_element_type=F32)


def _dot_tn(a, b):
    return lax.dot_general(a, b, (((0,), (0,)), ((), ())), preferred_element_type=F32)


def _split3(x):
    x1 = x.astype(BF16)
    r1 = x - x1.astype(F32)
    x2 = r1.astype(BF16)
    x3 = (r1 - x2.astype(F32)).astype(BF16)
    return x1, x2, x3


def _adaln_kernel(c_ref, w_ref, b_ref, o_ref):
    c = c_ref[...]
    s = c * jax.nn.sigmoid(c)
    w = w_ref[...]
    s1, s2, s3 = _split3(s)
    w1 = w.astype(BF16)
    w2 = (w - w1.astype(F32)).astype(BF16)
    acc = (_dot(s3, w1) + _dot(s2, w2)) + (_dot(s1, w2) + _dot(s2, w1))
    o_ref[...] = (acc + _dot(s1, w1)) + b_ref[...]


def adaln(cc, w, b, layer):
    _, d, n = w.shape
    tn = 1536 if n % 1536 == 0 else n
    est = 2 * d * tn * 4 * 2
    return pl.pallas_call(
        _adaln_kernel,
        out_shape=jax.ShapeDtypeStruct((8, n), F32),
        grid=(n // tn,),
        in_specs=[pl.BlockSpec((8, d), lambda j: (0, 0)),
                  pl.BlockSpec((None, d, tn), lambda j: (layer, 0, j)),
                  pl.BlockSpec((None, 1, tn), lambda j: (layer, 0, j))],
        out_specs=pl.BlockSpec((8, tn), lambda j: (0, j)),
        compiler_params=_params(("parallel",), est),
        name="adaln",
    )(cc, w, b.reshape(b.shape[0], 1, n))


def _lnmod_mm_kernel(x_ref, sh_ref, sc_ref, w_ref, o_ref, h_ref, *, row):
    @pl.when(pl.program_id(1) == 0)
    def _():
        hn = _layer_norm(x_ref[...])
        h = hn * (1.0 + sc_ref[row:row + 1, :]) + sh_ref[row:row + 1, :]
        h_ref[...] = h.astype(BF16)

    o_ref[...] = _dot(h_ref[...], w_ref[...]).astype(o_ref.dtype)


def lnmod_matmul(x, mod, row, k_shift, k_scale, w, tn):
    m, d = x.shape
    n = w.shape[1]
    tm = min(m, 1024)
    est = 2 * tm * d * 4 + tm * d * 2 + 2 * d * tn * 2 + 2 * tm * tn * 4
    return pl.pallas_call(
        functools.partial(_lnmod_mm_kernel, row=row),
        out_shape=jax.ShapeDtypeStruct((m, n), F32),
        grid=(m // tm, n // tn),
        in_specs=[pl.BlockSpec((tm, d), lambda i, j: (i, 0)),
                  pl.BlockSpec((8, d), lambda i, j: (0, k_shift)),
                  pl.BlockSpec((8, d), lambda i, j: (0, k_scale)),
                  pl.BlockSpec((d, tn), lambda i, j: (0, j))],
        out_specs=pl.BlockSpec((tm, tn), lambda i, j: (i, j)),
        scratch_shapes=[pltpu.VMEM((tm, d), BF16)],
        compiler_params=_params(("parallel", "arbitrary"), est),
        name="lnmod_matmul",
    )(x, mod, mod, w)


def _hgrn_tables(c):
    n_lvl = int(math.log2(c))
    r = np.arange(c)
    u = np.arange(c)[None, :]
    blocks, masks = [], []
    for l in range(n_lvl):
        half = 1 << l
        base = (r // (2 * half)) * (2 * half)
        anchor = (base + half - 1)[:, None]
        upper = (r >= base + half)[:, None]
        rr = r[:, None]
        if l < HGRN_SMALL_LEVELS:
            blocks.append(np.where(upper, (u > anchor) & (u <= rr), (u > rr) & (u <= anchor)))
        same = (r[:, None] // (2 * half)) == (r[None, :] // (2 * half))
        masks.append(same & upper & ~(upper.T))
    blocks.append(u <= r[:, None])
    blocks.append(np.ones((16, c), bool))
    masks.append(np.eye(c, dtype=bool))
    fwd_s = np.concatenate(blocks, axis=0).astype(np.float32)
    fwd_m = np.stack(masks).astype(np.float32)
    bwd_s = np.concatenate([b[::-1, ::-1] for b in blocks], axis=0).astype(np.float32)
    bwd_m = fwd_m[:, ::-1, ::-1]
    return (jnp.asarray(np.stack([fwd_s, bwd_s]), BF16), jnp.asarray(np.stack([fwd_m, bwd_m]), F32))


def _hgrn_kernel(q_ref, v_ref, f_ref, lb_ref, sums_ref, mask_ref, s0_ref, o_ref, sfin_ref, st_ref):
    c = HGRN_CHUNK
    n_sub = q_ref.shape[0] // c
    hd = HEAD_DIM
    n_lvl = mask_ref.shape[0] - 1
    n_small = HGRN_SMALL_LEVELS
    forward = pl.program_id(0) == 0
    j = pl.program_id(1)

    @pl.when(j == 0)
    def _():
        st_ref[...] = s0_ref[...]

    def wide_level(cum, l):
        half = 1 << l
        parts = []
        for base in range(0, c, 2 * half):
            a = base + half - 1
            mid = jnp.where(forward, cum[a:a + 1, :], cum[a + 1:a + 2, :])
            parts.append(jnp.broadcast_to(mid, (2 * half, hd)))
        anchor = parts[0] if len(parts) == 1 else jnp.concatenate(parts, axis=0)
        return -jnp.abs(cum - anchor)

    for s, h in [(s, h) for s in range(n_sub) for h in range(q_ref.shape[1] // hd)]:
        cols = slice(h * hd, (h + 1) * hd)
        rows = pl.ds(pl.multiple_of(jnp.where(forward, s * c, (n_sub - 1 - s) * c), c), c)
        q = q_ref[rows, cols]
        vb = v_ref[rows, cols].astype(BF16)
        lb = lb_ref[:, cols]
        f = lb + (1.0 - lb) * jax.nn.sigmoid(f_ref[rows, cols])
        g = jnp.log(f)
        k = 1.0 - f
        g1 = g.astype(BF16)
        g2 = (g - g1.astype(F32)).astype(BF16)
        e2 = _dot(sums_ref[...], jnp.concatenate([g1, g2], axis=1))
        e = e2[:, hd:] + e2[:, :hd]
        cum = e[n_small * c:(n_small + 1) * c]
        tot = e[(n_small + 1) * c:(n_small + 1) * c + 1]
        rem = tot - cum

        scores = _dot_nt(q.astype(BF16), k.astype(BF16)) * mask_ref[n_lvl]
        for l in range(n_lvl):
            z = jnp.exp(e[l * c:(l + 1) * c] if l < n_small else wide_level(cum, l))
            scores = scores + _dot_nt((q * z).astype(BF16), (k * z).astype(BF16)) * mask_ref[l]

        st = st_ref[h]
        o = _dot(scores.astype(BF16), vb) + _dot_nt((q * jnp.exp(cum)).astype(BF16), st.astype(BF16))
        o_ref[rows, cols] = o
        st_new = st * jnp.exp(tot) + _dot_tn(vb, (k * jnp.exp(rem)).astype(BF16))
        st_ref[h] = st_new

    @pl.when(j == pl.num_programs(1) - 1)
    def _():
        sfin_ref[...] = st_ref[...]


def hgrn_scan(proj, lb, s0):
    seq = proj.shape[0]
    c = HGRN_CHUNK
    cb = min(seq, HGRN_CHUNKS_PER_STEP * c)
    nc = seq // cb
    sums, masks = _hgrn_tables(c)
    hd, w = HEAD_DIM, A_WIDTH

    def blk(d, j):
        return jnp.where(d == 0, j, nc - 1 - j)

    est = (2 * (4 * cb * w * 4 + sums.shape[1] * c * 2 + masks.shape[1] * c * c * 4 + 2 * A_HEADS * hd * hd * 4)
           + A_HEADS * hd * hd * 4)
    return pl.pallas_call(
        _hgrn_kernel,
        out_shape=(jax.ShapeDtypeStruct((2, seq, w), F32),
                   jax.ShapeDtypeStruct((2, A_HEADS, hd, hd), F32)),
        grid=(2, nc),
        in_specs=[pl.BlockSpec((cb, w), lambda d, j: (blk(d, j), 0)),
                  pl.BlockSpec((cb, w), lambda d, j: (blk(d, j), 3)),
                  pl.BlockSpec((cb, w), lambda d, j: (blk(d, j), 1 + d)),
                  pl.BlockSpec((None, 1, w), lambda d, j: (d, 0, 0)),
                  pl.BlockSpec((None, sums.shape[1], c), lambda d, j: (d, 0, 0)),
                  pl.BlockSpec((None, masks.shape[1], c, c), lambda d, j: (d, 0, 0, 0)),
                  pl.BlockSpec((None, A_HEADS, hd, hd), lambda d, j: (d, 0, 0, 0))],
        out_specs=(pl.BlockSpec((None, cb, w), lambda d, j: (d, blk(d, j), 0)),
                   pl.BlockSpec((None, A_HEADS, hd, hd), lambda d, j: (d, 0, 0, 0))),
        scratch_shapes=[pltpu.VMEM((A_HEADS, hd, hd), F32)],
        compiler_params=_params(("parallel", "arbitrary"), est),
        name="hgrn_scan",
    )(proj, proj, proj, lb, sums, masks, s0)


def _hgrn_out_kernel(o_ref, gate_ref, g_ref, a_ref):
    hd = HEAD_DIM
    for h in range(a_ref.shape[1] // hd):
        cols = slice(h * hd, (h + 1) * hd)
        o = o_ref[0, :, cols] + o_ref[1, :, cols]
        gate = gate_ref[:, cols]
        a_ref[:, cols] = (_rms(o) * g_ref[...] * (gate * jax.nn.sigmoid(gate))).astype(a_ref.dtype)


def hgrn_out(o, proj, norm_g):
    seq = o.shape[1]
    tm = min(seq, 256)
    hd, w = HEAD_DIM, A_WIDTH
    return pl.pallas_call(
        _hgrn_out_kernel,
        out_shape=jax.ShapeDtypeStruct((seq, w), BF16),
        grid=(seq // tm,),
        in_specs=[pl.BlockSpec((2, tm, w), lambda i: (0, i, 0)),
                  pl.BlockSpec((tm, w), lambda i: (i, 4)),
                  pl.BlockSpec((1, hd), lambda i: (0, 0))],
        out_specs=pl.BlockSpec((tm, w), lambda i: (i, 0)),
        compiler_params=_params(("parallel",), 8 * tm * w * 4),
        name="hgrn_out",
    )(o, proj, norm_g.reshape(1, hd))


def _norm_rope_kernel(x_ref, g_ref, cos_ref, sin_ref, o_ref, *, scale):
    hd = HEAD_DIM
    for h in range(o_ref.shape[1] // hd):
        cols = slice(h * hd, (h + 1) * hd)
        y = _rms(x_ref[:, cols]) * g_ref[...]
        y = y * cos_ref[...] + pltpu.roll(y, hd // 2, 1) * sin_ref[...]
        o_ref[:, cols] = (y * scale).astype(o_ref.dtype)


def norm_rope(proj, col0, n_heads, g, cos, sin, scale):
    seq = proj.shape[0]
    tm = min(seq, 256)
    hd = HEAD_DIM
    w = n_heads * hd
    return pl.pallas_call(
        functools.partial(_norm_rope_kernel, scale=scale),
        out_shape=jax.ShapeDtypeStruct((seq, w), BF16),
        grid=(seq // tm,),
        in_specs=[pl.BlockSpec((tm, w), lambda i: (i, col0 // w)),
                  pl.BlockSpec((1, hd), lambda i: (0, 0)),
                  pl.BlockSpec((tm, hd), lambda i: (i, 0)),
                  pl.BlockSpec((tm, hd), lambda i: (i, 0))],
        out_specs=pl.BlockSpec((tm, w), lambda i: (i, 0)),
        compiler_params=_params(("parallel",), 6 * tm * w * 4 + 4 * tm * hd * 4),
        name="norm_rope",
    )(proj, g.reshape(1, hd), cos, sin)


def _cast_kernel(x_ref, o_ref):
    o_ref[...] = x_ref[...].astype(o_ref.dtype)


def cast_columns(proj, col0, width):
    seq = proj.shape[0]
    tm = min(seq, 1024)
    return pl.pallas_call(
        _cast_kernel,
        out_shape=jax.ShapeDtypeStruct((seq, width), BF16),
        grid=(seq // tm,),
        in_specs=[pl.BlockSpec((tm, width), lambda i: (i, col0 // width))],
        out_specs=pl.BlockSpec((tm, width), lambda i: (i, 0)),
        compiler_params=_params(("parallel",), 4 * tm * width * 4),
        name="cast_columns",
    )(proj)


def _flash_update(q, k, v, m_ref, l_ref, acc_ref):
    s = _dot_nt(q, k)
    m_prev = m_ref[...]
    m_new = jnp.maximum(m_prev, jnp.max(s, axis=-1, keepdims=True))
    alpha = jnp.exp2(m_prev - m_new)
    ps = [jnp.exp2(s[:, c * LANES:(c + 1) * LANES] - m_new) for c in range(s.shape[1] // LANES)]
    psum = ps[0]
    for pc in ps[1:]:
        psum = psum + pc
    p = jnp.concatenate([pc.astype(BF16) for pc in ps], axis=1)
    l_ref[...] = alpha * l_ref[...] + psum
    acc_ref[...] = alpha * acc_ref[...] + _dot(p, v)
    m_ref[...] = m_new


def _flash_kernel(*refs, has_ctx):
    if has_ctx:
        q_ref, k_ref, v_ref, kc_ref, vc_ref, o_ref, m_ref, l_ref, acc_ref = refs
    else:
        q_ref, k_ref, v_ref, o_ref, m_ref, l_ref, acc_ref = refs
    j = pl.program_id(2)

    @pl.when(j == 0)
    def _():
        m_ref[...] = jnp.full(m_ref.shape, -jnp.inf, F32)
        l_ref[...] = jnp.zeros(l_ref.shape, F32)
        acc_ref[...] = jnp.zeros(acc_ref.shape, F32)
        if has_ctx:
            _flash_update(q_ref[...], kc_ref[...], vc_ref[...], m_ref, l_ref, acc_ref)

    _flash_update(q_ref[...], k_ref[...], v_ref[...], m_ref, l_ref, acc_ref)

    @pl.when(j == pl.num_programs(2) - 1)
    def _():
        l = jnp.sum(l_ref[...], axis=-1, keepdims=True)
        o_ref[...] = (acc_ref[...] / l).astype(o_ref.dtype)


def flash_attention(q, k, v, k_ctx=None, v_ctx=None, *, n_heads, n_kv_heads, dq, dv):
    n, m = q.shape[0], k.shape[0]
    grp = n_heads // n_kv_heads
    tq = min(n, 2048)
    tk = min(m, 2048)
    has_ctx = k_ctx is not None
    in_specs = [pl.BlockSpec((tq, dq), lambda h, i, j: (i, h)),
                pl.BlockSpec((tk, dq), lambda h, i, j: (j, h // grp)),
                pl.BlockSpec((tk, dv), lambda h, i, j: (j, h // grp))]
    args = [q, k, v]
    if has_ctx:
        mc = k_ctx.shape[0]
        in_specs += [pl.BlockSpec((mc, dq), lambda h, i, j: (0, h // grp)),
                     pl.BlockSpec((mc, dv), lambda h, i, j: (0, h // grp))]
        args += [k_ctx, v_ctx]
    est = 2 * (tq * dq + tk * dq + tk * dv + tq * dv) * 2 + tq * (dv + 256) * 4 + 6 * tq * tk * 4
    return pl.pallas_call(
        functools.partial(_flash_kernel, has_ctx=has_ctx),
        out_shape=jax.ShapeDtypeStruct((n, n_heads * dv), BF16),
        grid=(n_heads, n // tq, m // tk),
        in_specs=in_specs,
        out_specs=pl.BlockSpec((tq, dv), lambda h, i, j: (i, h)),
        scratch_shapes=[pltpu.VMEM((tq, LANES), F32), pltpu.VMEM((tq, LANES), F32), pltpu.VMEM((tq, dv), F32)],
        compiler_params=_params(("parallel", "parallel", "arbitrary"), est),
        name="flash_attention",
    )(*args)


def _proj_postnorm_kernel(*refs, n_in, row):
    a_refs = refs[:n_in]
    w_refs = refs[n_in:2 * n_in]
    x_ref, gate_ref, g_ref, b_ref, o_ref = refs[2 * n_in:]
    y = _dot(a_refs[0][...], w_refs[0][...])
    for a_ref, w_ref in zip(a_refs[1:], w_refs[1:]):
        y = y + _dot(a_ref[...], w_ref[...])
    z = DEEPNORM_ALPHA * x_ref[...] + gate_ref[row:row + 1, :] * y
    o_ref[...] = _layer_norm(z) * g_ref[...] + b_ref[...]


def proj_postnorm(acts, ws, x, mod, row, k_gate, g, b):
    m, d = x.shape
    tm = min(m, 512)
    n_in = len(acts)
    once = pl.Buffered(1)
    in_specs = [pl.BlockSpec((tm, a.shape[1]), lambda i: (i, 0)) for a in acts]
    in_specs += [pl.BlockSpec(w.shape, lambda i: (0, 0), pipeline_mode=once) for w in ws]
    in_specs += [pl.BlockSpec((tm, d), lambda i: (i, 0)),
                 pl.BlockSpec((8, d), lambda i: (0, k_gate)),
                 pl.BlockSpec((1, d), lambda i: (0, 0)),
                 pl.BlockSpec((1, d), lambda i: (0, 0))]
    est = sum(w.size * 2 for w in ws) + sum(2 * tm * a.shape[1] * 2 for a in acts) + 6 * tm * d * 4
    return pl.pallas_call(
        functools.partial(_proj_postnorm_kernel, n_in=n_in, row=row),
        out_shape=jax.ShapeDtypeStruct((m, d), F32),
        grid=(m // tm,),
        in_specs=in_specs,
        out_specs=pl.BlockSpec((tm, d), lambda i: (i, 0)),
        compiler_params=_params(("parallel",), est),
        name="proj_postnorm",
    )(*acts, *ws, x, mod, g.reshape(1, d), b.reshape(1, d))


def _add_postnorm_kernel(y_ref, x_ref, gate_ref, g_ref, b_ref, o_ref, *, row):
    z = DEEPNORM_ALPHA * x_ref[...] + gate_ref[row:row + 1, :] * y_ref[...]
    o_ref[...] = _layer_norm(z) * g_ref[...] + b_ref[...]


def add_postnorm(y, x, mod, row, k_gate, g, b):
    m, d = x.shape
    tm = min(m, 512)
    return pl.pallas_call(
        functools.partial(_add_postnorm_kernel, row=row),
        out_shape=jax.ShapeDtypeStruct((m, d), F32),
        grid=(m // tm,),
        in_specs=[pl.BlockSpec((tm, d), lambda i: (i, 0)),
                  pl.BlockSpec((tm, d), lambda i: (i, 0)),
                  pl.BlockSpec((8, d), lambda i: (0, k_gate)),
                  pl.BlockSpec((1, d), lambda i: (0, 0)),
                  pl.BlockSpec((1, d), lambda i: (0, 0))],
        out_specs=pl.BlockSpec((tm, d), lambda i: (i, 0)),
        compiler_params=_params(("parallel",), 8 * tm * d * 4),
        name="add_postnorm",
    )(y, x, mod, g.reshape(1, d), b.reshape(1, d))


def _rms_mm_kernel(x_ref, g_ref, w_ref, o_ref, a_ref):
    @pl.when(pl.program_id(1) == 0)
    def _():
        a_ref[...] = (_rms(x_ref[...]) * g_ref[...]).astype(BF16)

    o_ref[...] = _dot(a_ref[...], w_ref[...]).astype(o_ref.dtype)


def _mla_q_kernel(x_ref, g_ref, w_ref, cos_ref, sa_ref, sb_ref, o_ref, a_ref, *, scale):
    @pl.when(pl.program_id(1) == 0)
    def _():
        a_ref[...] = (_rms(x_ref[...]) * g_ref[...]).astype(BF16)

    y = _dot(a_ref[...], w_ref[...])
    hd = HEAD_DIM
    for h in range(y.shape[1] // MLA_QK):
        c0 = h * MLA_QK
        o_ref[:, c0:c0 + hd] = (y[:, c0:c0 + hd] * scale).astype(o_ref.dtype)
        r = y[:, c0 + hd:c0 + 2 * hd]
        r = r * cos_ref[...] + pltpu.roll(r, hd - MLA_ROPE // 2, 1) * sa_ref[...] + pltpu.roll(r, MLA_ROPE // 2, 1) * sb_ref[...]
        o_ref[:, c0 + hd:c0 + 2 * hd] = (r * scale).astype(o_ref.dtype)


def _mla_k_kernel(x_ref, g_ref, w_ref, kr_ref, cos_ref, sa_ref, sb_ref, o_ref, a_ref, r_ref):
    hd = HEAD_DIM

    @pl.when(pl.program_id(1) == 0)
    def _():
        a_ref[...] = (_rms(x_ref[...]) * g_ref[...]).astype(BF16)
        r = kr_ref[...]
        r = r * cos_ref[...] + pltpu.roll(r, hd - MLA_ROPE // 2, 1) * sa_ref[...] + pltpu.roll(r, MLA_ROPE // 2, 1) * sb_ref[...]
        r_ref[...] = r.astype(BF16)

    y = _dot(a_ref[...], w_ref[...])
    for h in range(y.shape[1] // hd):
        o_ref[:, h * MLA_QK:h * MLA_QK + hd] = y[:, h * hd:(h + 1) * hd].astype(o_ref.dtype)
        o_ref[:, h * MLA_QK + hd:(h + 1) * MLA_QK] = r_ref[...]


def _mla_specs(m, tm, lora, col_blk):
    return [pl.BlockSpec((tm, lora), lambda i, j: (i, col_blk)),
            pl.BlockSpec((1, lora), lambda i, j: (0, 0))]


def rms_matmul(dn, col_blk, g, w, tn):
    m = dn.shape[0]
    lora, n = w.shape
    tm = min(m, 1024)
    est = 2 * tm * lora * 4 + tm * lora * 2 + 2 * lora * tn * 2 + 2 * tm * tn * 2 + tm * tn * 4
    return pl.pallas_call(
        _rms_mm_kernel,
        out_shape=jax.ShapeDtypeStruct((m, n), BF16),
        grid=(m // tm, n // tn),
        in_specs=_mla_specs(m, tm, lora, col_blk) + [pl.BlockSpec((lora, tn), lambda i, j: (0, j))],
        out_specs=pl.BlockSpec((tm, tn), lambda i, j: (i, j)),
        scratch_shapes=[pltpu.VMEM((tm, lora), BF16)],
        compiler_params=_params(("parallel", "arbitrary"), est),
        name="rms_matmul",
    )(dn, g.reshape(1, lora), w)


def mla_q(dn, g, w, cos, sa, sb, scale):
    m = dn.shape[0]
    lora, n = w.shape
    tm = min(m, 1024)
    tn = 4 * MLA_QK
    hd = HEAD_DIM
    est = 2 * tm * lora * 4 + tm * lora * 2 + 2 * lora * tn * 2 + 2 * tm * tn * 2 + 2 * tm * tn * 4 + 6 * tm * hd * 4
    rope_spec = pl.BlockSpec((tm, hd), lambda i, j: (i, 0))
    return pl.pallas_call(
        functools.partial(_mla_q_kernel, scale=scale),
        out_shape=jax.ShapeDtypeStruct((m, n), BF16),
        grid=(m // tm, n // tn),
        in_specs=_mla_specs(m, tm, lora, 0) + [pl.BlockSpec((lora, tn), lambda i, j: (0, j)),
                                               rope_spec, rope_spec, rope_spec],
        out_specs=pl.BlockSpec((tm, tn), lambda i, j: (i, j)),
        scratch_shapes=[pltpu.VMEM((tm, lora), BF16)],
        compiler_params=_params(("parallel", "arbitrary"), est),
        name="mla_q",
    )(dn, g.reshape(1, lora), w, cos, sa, sb)


def mla_k(dn, g, w, cos, sa, sb):
    m = dn.shape[0]
    lora, n = w.shape
    tm = min(m, 1024)
    hd = HEAD_DIM
    tn = 4 * hd
    kr_blk = (MLA_Q_LORA + MLA_KV_LORA) // hd
    est = 2 * tm * lora * 4 + tm * lora * 2 + 2 * lora * tn * 2 + 4 * tm * tn * 2 + tm * tn * 4 + 8 * tm * hd * 4
    rope_spec = pl.BlockSpec((tm, hd), lambda i, j: (i, 0))
    return pl.pallas_call(
        _mla_k_kernel,
        out_shape=jax.ShapeDtypeStruct((m, 2 * n), BF16),
        grid=(m // tm, n // tn),
        in_specs=_mla_specs(m, tm, lora, 1) + [pl.BlockSpec((lora, tn), lambda i, j: (0, j)),
                                               pl.BlockSpec((tm, hd), lambda i, j: (i, kr_blk)),
                                               rope_spec, rope_spec, rope_spec],
        out_specs=pl.BlockSpec((tm, 2 * tn), lambda i, j: (i, j)),
        scratch_shapes=[pltpu.VMEM((tm, lora), BF16), pltpu.VMEM((tm, hd), BF16)],
        compiler_params=_params(("parallel", "arbitrary"), est),
        name="mla_k",
    )(dn, g.reshape(1, lora), w, dn, cos, sa, sb)


def _router_kernel(x_ref, sh_ref, sc_ref, wr_ref, h_ref, aff_ref, *, row):
    hn = _layer_norm(x_ref[...])
    h = hn * (1.0 + sc_ref[row:row + 1, :]) + sh_ref[row:row + 1, :]
    hb = h.astype(BF16)
    half = h.shape[1] // 2
    bits = pltpu.bitcast(hb.astype(F32), jnp.uint32)
    h_ref[...] = (bits[:, half:] & jnp.uint32(0xFFFF0000)) | (bits[:, :half] >> 16)
    w = wr_ref[...]
    w1 = w.astype(BF16)
    w2 = (w - w1.astype(F32)).astype(BF16)
    h2 = (h - hb.astype(F32)).astype(BF16)
    logits = _dot_nt(w1, hb) + (_dot_nt(w2, hb) + _dot_nt(w1, h2))
    mx = jnp.max(logits, axis=0, keepdims=True)
    p = jnp.exp(logits - mx)
    aff_ref[...] = p / jnp.sum(p, axis=0, keepdims=True)


def moe_router(x, mod, row, k_shift, k_scale, w_router_t):
    m, d = x.shape
    e = w_router_t.shape[0]
    tm = min(m, 512)
    return pl.pallas_call(
        functools.partial(_router_kernel, row=row),
        out_shape=(jax.ShapeDtypeStruct((m, d // 2), jnp.uint32), jax.ShapeDtypeStruct((e, m), F32)),
        grid=(m // tm,),
        in_specs=[pl.BlockSpec((tm, d), lambda i: (i, 0)),
                  pl.BlockSpec((8, d), lambda i: (0, k_shift)),
                  pl.BlockSpec((8, d), lambda i: (0, k_scale)),
                  pl.BlockSpec((e, d), lambda i: (0, 0))],
        out_specs=(pl.BlockSpec((tm, d // 2), lambda i: (i, 0)), pl.BlockSpec((e, tm), lambda i: (0, i))),
        compiler_params=_params(("parallel",), 8 * tm * d * 4),
        name="moe_router",
    )(x, mod, mod, w_router_t)


def _ffn_up_kernel(*refs, n_seg):
    x_refs, (wg_ref, wu_ref) = refs[:n_seg], refs[n_seg:n_seg + 2]
    o_refs, xs_refs = refs[n_seg + 2:2 * n_seg + 2], refs[2 * n_seg + 2:]

    @pl.when(pl.program_id(1) == 0)
    def _():
        for x_ref, xs_ref in zip(x_refs, xs_refs):
            word = x_ref[...]
            first = pltpu.bitcast(word << 16, F32)
            second = pltpu.bitcast(word & jnp.uint32(0xFFFF0000), F32)
            xs_ref[...] = jnp.concatenate([first, second], axis=1).astype(BF16)

    wg = wg_ref[...].astype(BF16)
    wu = wu_ref[...].astype(BF16)
    for xs_ref, o_ref in zip(xs_refs, o_refs):
        x = xs_ref[...]
        g = _dot(x, wg)
        u = _dot(x, wu)
        o_ref[...] = (g * jax.nn.sigmoid(g) * u).astype(o_ref.dtype)


def _ffn_down_kernel(*refs, n_seg):
    h_refs, wd_ref, wt_refs, o_refs = refs[:n_seg], refs[n_seg], refs[n_seg + 1:2 * n_seg + 1], refs[2 * n_seg + 1:]
    wd = wd_ref[...].astype(BF16)
    for h_ref, wt_ref, o_ref in zip(h_refs, wt_refs, o_refs):
        y = _dot(h_ref[...], wd) * wt_ref[...]
        hi = y.astype(BF16)
        o_ref[0] = hi
        o_ref[1] = (y - hi.astype(F32)).astype(BF16)


def expert_ffn(xgs, wts, w_gate, w_up, w_down, layer):
    n_seg = len(xgs)
    e = xgs[0].shape[0]
    d = 2 * xgs[0].shape[2]
    f = w_gate.shape[3]
    rs = [x.shape[1] for x in xgs]
    r = sum(rs)
    tf = min(f, 256)
    est = 2 * (r * d * 2 + 2 * d * tf * 4 + r * tf * 2) + r * d * 2 + 2 * d * tf * 2 + 3 * r * tf * 4
    hids = pl.pallas_call(
        functools.partial(_ffn_up_kernel, n_seg=n_seg),
        out_shape=[jax.ShapeDtypeStruct((e, ri, f), BF16) for ri in rs],
        grid=(e, f // tf),
        in_specs=[pl.BlockSpec((None, ri, d // 2), lambda i, j: (i, 0, 0)) for ri in rs]
        + [pl.BlockSpec((None, None, d, tf), lambda i, j: (layer, i, 0, j))] * 2,
        out_specs=[pl.BlockSpec((None, ri, tf), lambda i, j: (i, 0, j)) for ri in rs],
        scratch_shapes=[pltpu.VMEM((ri, d), BF16) for ri in rs],
        compiler_params=_params(("parallel", "arbitrary"), est),
        name="ffn_up",
    )(*xgs, w_gate, w_up)
    tn = min(d, 512)
    est = 2 * (r * f * 2 + f * tn * 4 + r * tn * 4 + r * LANES * 4) + f * tn * 2 + 2 * r * tn * 4
    return pl.pallas_call(
        functools.partial(_ffn_down_kernel, n_seg=n_seg),
        out_shape=[jax.ShapeDtypeStruct((e, 2, ri, d), BF16) for ri in rs],
        grid=(e, d // tn),
        in_specs=[pl.BlockSpec((None, ri, f), lambda i, j: (i, 0, 0)) for ri in rs]
        + [pl.BlockSpec((None, None, f, tn), lambda i, j: (layer, i, 0, j))]
        + [pl.BlockSpec((None, ri, 1), lambda i, j: (i, 0, 0)) for ri in rs],
        out_specs=[pl.BlockSpec((None, 2, ri, tn), lambda i, j: (i, 0, 0, j)) for ri in rs],
        compiler_params=_params(("parallel", "arbitrary"), est),
        name="ffn_down",
    )(*hids, w_down, *wts)


def _select_kernel(aff_ref, pos_ref, idx_ref, wt_ref, off_ref, incl_ref, tot_ref, offs_ref, *, cap):
    e, g, ln = aff_ref.shape
    bits = pltpu.bitcast(aff_ref[...], jnp.int32)

    def count(mask):
        per_lane = jnp.sum(jnp.where(mask, 1.0, 0.0), axis=1)
        return jnp.sum(per_lane, axis=1, keepdims=True)[:, :, None]

    def search(i, t):
        cand = t | jnp.left_shift(jnp.int32(1), 30 - i)
        return jnp.where(count(bits >= cand) >= cap, cand, t)

    thr = lax.fori_loop(0, 31, search, jnp.zeros((e, 1, 1), jnp.int32))
    gt = bits > thr
    eq = bits == thr
    need = cap - count(gt)

    r0 = lax.broadcasted_iota(jnp.int32, (ln, ln), 0)
    r1 = lax.broadcasted_iota(jnp.int32, (ln, ln), 1)
    upper = jnp.where(r0 <= r1, 1.0, 0.0).astype(BF16)
    ones = jnp.ones((ln, ln), BF16)
    g0 = lax.broadcasted_iota(jnp.int32, (g, g), 0)
    g1 = lax.broadcasted_iota(jnp.int32, (g, g), 1)
    earlier = jnp.where(g1 < g0, 1.0, 0.0).astype(BF16)

    def prefix(mask):
        x = jnp.where(mask, 1.0, 0.0).astype(BF16).reshape(e * g, ln)
        incl = _dot(x, upper).reshape(e, g, ln)
        tot = _dot(x, ones).reshape(e, g, ln)
        off = jnp.stack([_dot(earlier, tot[i].astype(BF16)) for i in range(e)])
        return incl, tot, off

    incl_eq, _, off_eq = prefix(eq)
    sel = gt | (eq & (off_eq + incl_eq - 1.0 < need))
    incl, tot, off = prefix(sel)
    pos_ref[...] = jnp.where(sel, off + incl - 1.0, -1.0).astype(jnp.int32)
    off_ref[...] = off.astype(jnp.int32)
    incl_ref[...] = incl
    tot_ref[...] = tot
    offs_ref[...] = off

    slot = lax.broadcasted_iota(jnp.int32, (cap, ln), 0).astype(F32)
    lane = lax.broadcasted_iota(jnp.int32, (cap, ln), 1).astype(F32)

    def tokens_of_slots(i, c):
        ends = jnp.transpose(offs_ref[i] + tot_ref[i])[0:1, :]
        before = jnp.where(ends <= slot, 1.0, 0.0).astype(BF16)
        grp = _dot(before, ones)
        rank = slot - _dot(before, tot_ref[i].astype(BF16))
        pick = jnp.where(lane == grp, 1.0, 0.0).astype(BF16)
        incl_g = _dot(pick, incl_ref[i].astype(BF16))
        lane_p = _dot(jnp.where(incl_g <= rank, 1.0, 0.0).astype(BF16), ones)
        idx_ref[i] = (grp * ln + lane_p)[:, 0:1].astype(jnp.int32)
        a1, a2, a3 = _split3(aff_ref[i])
        aff_g = (_dot(pick, a3) + _dot(pick, a2)) + _dot(pick, a1)
        wt_ref[i] = jnp.sum(jnp.where(lane == lane_p, aff_g, 0.0), axis=1, keepdims=True)
        return c

    lax.fori_loop(0, e, tokens_of_slots, 0)


def moe_select(aff_t, cap):
    e, n = aff_t.shape
    g = n // LANES
    assert g <= LANES, "token groups are mapped onto the 128 lanes"
    a = aff_t.reshape(e, g, LANES)
    if g != LANES:
        a = jnp.concatenate([a, jnp.full((e, LANES - g, LANES), -1.0, F32)], axis=1)
    shp = jax.ShapeDtypeStruct((e, LANES, LANES), jnp.int32)
    full = pl.BlockSpec((e, LANES, LANES), lambda i: (0, 0, 0))
    per_slot = pl.BlockSpec((e, cap, 1), lambda i: (0, 0, 0))
    pos, idx, wt, off = pl.pallas_call(
        functools.partial(_select_kernel, cap=cap),
        out_shape=(shp, jax.ShapeDtypeStruct((e, cap, 1), jnp.int32), jax.ShapeDtypeStruct((e, cap, 1), F32), shp),
        grid=(1,),
        in_specs=[full],
        out_specs=(full, per_slot, per_slot, full),
        scratch_shapes=[pltpu.VMEM((e, LANES, LANES), F32)] * 3,
        compiler_params=_params(("arbitrary",), 28 * e * LANES * LANES * 4 + 16 * cap * LANES * 4),
        name="moe_select",
    )(a)
    return pos.reshape(e, LANES * LANES)[:, :n], idx.reshape(e, 1, cap), wt, off[:, :g, 0]


def _row_copy(h_hbm, x_ref, sem, token, row):
    return pltpu.make_async_copy(h_hbm.at[pl.ds(token, 1), :], x_ref.at[0, pl.ds(row, 1), :], sem)


def _gather_kernel(idx_ref, h_hbm, x_ref, sem):
    cap = x_ref.shape[1]

    def rows(i, c):
        for u in range(GATHER_UNROLL):
            s = i * GATHER_UNROLL + u
            _row_copy(h_hbm, x_ref, sem, idx_ref[0, s], s).start()
        return c

    lax.fori_loop(0, cap // GATHER_UNROLL, rows, 0)
    pltpu.make_async_copy(h_hbm.at[pl.ds(0, cap), :], x_ref.at[0], sem).wait()


def moe_gather(hp, idx):
    w = hp.shape[1]
    e, _, cap = idx.shape
    return pl.pallas_call(
        _gather_kernel,
        out_shape=jax.ShapeDtypeStruct((e, cap, w), jnp.uint32),
        grid=(e,),
        in_specs=[pl.BlockSpec((None, 1, cap), lambda i: (i, 0, 0), memory_space=pltpu.SMEM),
                  pl.BlockSpec(memory_space=pl.ANY)],
        out_specs=pl.BlockSpec((1, cap, w), lambda i: (i, 0, 0)),
        scratch_shapes=[pltpu.SemaphoreType.DMA(())],
        compiler_params=_params(("arbitrary",), 2 * cap * w * 4),
        name="moe_gather",
    )(idx, hp)


def _window_copy(y_hbm, dst, sem, e, src, win):
    return pltpu.make_async_copy(y_hbm.at[e, :, pl.ds(src, win)], dst, sem)


def _combine_kernel(offb_ref, y_hbm, pos_ref, x_ref, gate_ref, g_ref, b_ref, o_ref, ybuf, ybuf_x, sem, acc_ref,
                    *, row, cap, win):
    b = pl.program_id(0)
    n_exp, tb = pos_ref.shape
    half = n_exp // 2
    par = b % 2

    def window(e, k, blk=b):
        first = (offb_ref[e, blk] // 8) * 8 + k * win
        return first, pl.multiple_of(jnp.minimum(first, cap - win), 8)

    def onehot2(e, first, src):
        slots = src + lax.broadcasted_iota(jnp.int32, (win, 1), 0)
        hit = jnp.logical_and(pos_ref[e:e + 1, :] == slots, slots >= first)
        oh = jnp.where(hit, 1.0, 0.0).astype(BF16)
        return jnp.concatenate([oh, oh], axis=0)

    def copies(h, blk, p):
        return [_window_copy(y_hbm, ybuf.at[p, h, j], sem.at[2 * p + h], h * half + j,
                             window(h * half + j, 0, blk)[1], win) for j in range(half)]

    def start_block(blk, p):
        for h in range(2):
            for cp in copies(h, blk, p):
                cp.start()

    @pl.when(b == 0)
    def _():
        start_block(b, par)

    @pl.when(b + 1 < pl.num_programs(0))
    def _():
        start_block(b + 1, 1 - par)

    acc = None
    for h in range(2):
        for cp in copies(h, b, par):
            cp.wait()
        lhs = jnp.concatenate([onehot2(h * half + j, *window(h * half + j, 0)) for j in range(half)], axis=0)
        part = _dot_tn(lhs, ybuf[par, h].reshape(half * 2 * win, ybuf.shape[-1]))
        acc = part if acc is None else acc + part
    acc_ref[...] = acc

    for e in range(n_exp):
        n_win = (offb_ref[e, b + 1] - (offb_ref[e, b] // 8) * 8 + win - 1) // win

        def extra(k, c, e=e):
            first_k, src_k = window(e, k)
            cp = _window_copy(y_hbm, ybuf_x, sem.at[4], e, src_k, win)
            cp.start()
            cp.wait()
            acc_ref[...] += _dot_tn(onehot2(e, first_k, src_k), ybuf_x[...].reshape(2 * win, ybuf_x.shape[-1]))
            return c

        lax.fori_loop(1, n_win, extra, 0)

    z = DEEPNORM_ALPHA * x_ref[...] + gate_ref[row:row + 1, :] * acc_ref[...]
    o_ref[...] = _layer_norm(z) * g_ref[...] + b_ref[...]


def moe_combine_postnorm(y, pos, off, x, mod, row, k_gate, g, b):
    n, d = x.shape
    e, _, cap, _ = y.shape
    tb = min(n, 256)
    nb = n // tb
    win = min(cap, COMBINE_WINDOW)
    offb = jnp.concatenate([off[:, ::tb // LANES], jnp.full((e, 1), cap, jnp.int32)], axis=1)
    grid_spec = pltpu.PrefetchScalarGridSpec(
        num_scalar_prefetch=1,
        grid=(nb,),
        in_specs=[pl.BlockSpec(memory_space=pl.ANY),
                  pl.BlockSpec((e, tb), lambda i, o: (0, i)),
                  pl.BlockSpec((tb, d), lambda i, o: (i, 0)),
                  pl.BlockSpec((8, d), lambda i, o: (0, k_gate)),
                  pl.BlockSpec((1, d), lambda i, o: (0, 0)),
                  pl.BlockSpec((1, d), lambda i, o: (0, 0))],
        out_specs=pl.BlockSpec((tb, d), lambda i, o: (i, 0)),
        scratch_shapes=[pltpu.VMEM((2, 2, e // 2, 2, win, d), BF16), pltpu.VMEM((2, win, d), BF16),
                        pltpu.SemaphoreType.DMA((5,)), pltpu.VMEM((tb, d), F32)],
    )
    est = 8 * tb * d * 4 + (2 * e + 1) * 2 * win * d * 2 + e * win * tb * 2
    return pl.pallas_call(
        functools.partial(_combine_kernel, row=row, cap=cap, win=win),
        out_shape=jax.ShapeDtypeStruct((n, d), F32),
        grid_spec=grid_spec,
        compiler_params=_params(("arbitrary",), est),
        name="moe_combine",
    )(offb, y, pos, x, mod, g.reshape(1, d), b.reshape(1, d))


def moe_route(x, mod, row, w_router_t):
    m = x.shape[0]
    cap = max(1, EC_CAPACITY_FACTOR * m // N_EXPERTS)
    h, aff_t = moe_router(x, mod, row, 3, 4, w_router_t)
    pos, idx, wt, off = moe_select(aff_t, cap)
    return moe_gather(h, idx), wt, (pos, off)


def _rope_angles(n_tokens, rot_dim):
    rows = n_tokens // GRID_W
    row = np.repeat(np.arange(rows, dtype=np.float32), GRID_W)
    col = np.tile(np.arange(GRID_W, dtype=np.float32), rows)
    n_freq = rot_dim // 4
    inv = (ROPE_THETA ** (-np.arange(n_freq, dtype=np.float32) / n_freq)).astype(np.float32)
    return np.concatenate([row[:, None] * inv, col[:, None] * inv], axis=-1)


@functools.lru_cache(maxsize=None)
def _gqa_rope_tables(n_tokens):
    ang = _rope_angles(n_tokens, HEAD_DIM)
    c, s = np.cos(ang), np.sin(ang)
    return np.concatenate([c, c], axis=-1), np.concatenate([-s, s], axis=-1)


@functools.lru_cache(maxsize=None)
def _mla_rope_tables(n_tokens):
    ang = _rope_angles(n_tokens, MLA_ROPE)
    c, s = np.cos(ang), np.sin(ang)
    z = np.zeros_like(c)
    cos = np.concatenate([c, c, z, z], axis=-1)
    sa = np.concatenate([-s, z, z, z], axis=-1)
    sb = np.concatenate([z, s, z, z], axis=-1)
    return cos, sa, sb


def kernel(x, c, ctx, c_ctx, ada_w, ada_b, ln_g, ln_b, ev_w_in, ev_w_out, hgrn_lb, hgrn_norm_g, gqa_q_norm_g, gqa_k_norm_g, mla_w_down, mla_q_norm_g, mla_kv_norm_g, mla_w_uq, mla_w_ukv, mla_w_o, moe_router, moe_w_gate, moe_w_up, moe_w_down):
    d = D_MODEL
    xl = x[0]
    xc = ctx[0]
    n_lat, n_ctx = xl.shape[0], xc.shape[0]
    cc = jnp.zeros((8, d), F32).at[0].set(c[0]).at[1].set(c_ctx)
    lb_all = jnp.cumsum(jax.nn.softmax(hgrn_lb.astype(F32), axis=1), axis=1)
    gqa_tabs = _gqa_rope_tables(n_lat)
    gqa_tabs_ctx = [jnp.ones((n_ctx, HEAD_DIM), F32), jnp.zeros((n_ctx, HEAD_DIM), F32)]
    mla_tabs = _mla_rope_tables(n_lat)
    mla_tabs_ctx = [jnp.ones((n_ctx, HEAD_DIM), F32), jnp.zeros((n_ctx, HEAD_DIM), F32), jnp.zeros((n_ctx, HEAD_DIM), F32)]
    LAT, CTX = 0, 1

    for l in range(DEPTH):
        last = l == DEPTH - 1
        i = l // 2
        mod = adaln(cc, ada_w, ada_b, l)
        if l % 2 == 0:
            w_in = ev_w_in[i].astype(BF16)
            w_out = ev_w_out[i].astype(BF16)
            lb = lb_all[:, l].reshape(2, 1, A_WIDTH)
            scale = HEAD_DIM ** -0.5 * LOG2E
            proj_c = lnmod_matmul(xc, mod, CTX, 0, 1, w_in, 512)
            proj_l = lnmod_matmul(xl, mod, LAT, 0, 1, w_in, 512)
            s0 = jnp.zeros((2, A_HEADS, HEAD_DIM, HEAD_DIM), F32)
            o_c, s_c = hgrn_scan(proj_c, lb, s0)
            o_l, _ = hgrn_scan(proj_l, lb, s_c)
            a_l = hgrn_out(o_l, proj_l, hgrn_norm_g[i])
            qcol, kcol, vcol = 5 * A_WIDTH, 5 * A_WIDTH + B_WIDTH, 5 * A_WIDTH + B_WIDTH + B_KV_WIDTH
            q_l = norm_rope(proj_l, qcol, B_Q_HEADS, gqa_q_norm_g[i], *gqa_tabs, scale)
            k_l = norm_rope(proj_l, kcol, B_KV_HEADS, gqa_k_norm_g[i], *gqa_tabs, 1.0)
            k_c = norm_rope(proj_c, kcol, B_KV_HEADS, gqa_k_norm_g[i], *gqa_tabs_ctx, 1.0)
            v_l = cast_columns(proj_l, vcol, B_KV_WIDTH)
            v_c = cast_columns(proj_c, vcol, B_KV_WIDTH)
            att = dict(n_heads=B_Q_HEADS, n_kv_heads=B_KV_HEADS, dq=HEAD_DIM, dv=HEAD_DIM)
            b_l = flash_attention(q_l, k_l, v_l, k_c, v_c, **att)
            w_parts = [w_out[:A_WIDTH], w_out[A_WIDTH:]]
            xl_new = proj_postnorm([a_l, b_l], w_parts, xl, mod, LAT, 2, ln_g[l, 0], ln_b[l, 0])
            if not last:
                a_c = hgrn_out(o_c, proj_c, hgrn_norm_g[i])
                q_c = norm_rope(proj_c, qcol, B_Q_HEADS, gqa_q_norm_g[i], *gqa_tabs_ctx, scale)
                b_c = flash_attention(q_c, k_c, v_c, **att)
                xc = proj_postnorm([a_c, b_c], w_parts, xc, mod, CTX, 2, ln_g[l, 0], ln_b[l, 0])
            xl = xl_new
        else:
            hd = HEAD_DIM
            pad = (-mla_w_down.shape[2]) % hd
            w_down = jnp.pad(mla_w_down[i], ((0, 0), (0, pad))).astype(BF16)
            w_uq = mla_w_uq[i].reshape(MLA_Q_LORA, MLA_HEADS, hd + MLA_ROPE)
            w_uq = jnp.pad(w_uq, ((0, 0), (0, 0), (0, MLA_QK - hd - MLA_ROPE))).reshape(MLA_Q_LORA, MLA_HEADS * MLA_QK).astype(BF16)
            w_ukv = mla_w_ukv[i].reshape(MLA_KV_LORA, MLA_HEADS, 2 * hd)
            w_uk = w_ukv[:, :, :hd].reshape(MLA_KV_LORA, MLA_HEADS * hd).astype(BF16)
            w_uv = w_ukv[:, :, hd:].reshape(MLA_KV_LORA, MLA_HEADS * hd).astype(BF16)
            w_o = mla_w_o[i].astype(BF16)
            scale = (hd + MLA_ROPE) ** -0.5 * LOG2E
            dn_c = lnmod_matmul(xc, mod, CTX, 0, 1, w_down, w_down.shape[1])
            dn_l = lnmod_matmul(xl, mod, LAT, 0, 1, w_down, w_down.shape[1])
            q_l = mla_q(dn_l, mla_q_norm_g[i], w_uq, *mla_tabs, scale)
            k_l = mla_k(dn_l, mla_kv_norm_g[i], w_uk, *mla_tabs)
            k_c = mla_k(dn_c, mla_kv_norm_g[i], w_uk, *mla_tabs_ctx)
            v_l = rms_matmul(dn_l, 1, mla_kv_norm_g[i], w_uv, 1024)
            v_c = rms_matmul(dn_c, 1, mla_kv_norm_g[i], w_uv, 1024)
            att = dict(n_heads=MLA_HEADS, n_kv_heads=MLA_HEADS, dq=MLA_QK, dv=hd)
            o_l = flash_attention(q_l, k_l, v_l, k_c, v_c, **att)
            xl_new = proj_postnorm([o_l], [w_o], xl, mod, LAT, 2, ln_g[l, 0], ln_b[l, 0])
            if not last:
                q_c = mla_q(dn_c, mla_q_norm_g[i], w_uq, *mla_tabs_ctx, scale)
                o_c = flash_attention(q_c, k_c, v_c, **att)
                xc = proj_postnorm([o_c], [w_o], xc, mod, CTX, 2, ln_g[l, 0], ln_b[l, 0])
            xl = xl_new

        w_router_t = moe_router[l].T
        segs = [(xl, LAT)] if last else [(xl, LAT), (xc, CTX)]
        routes = [moe_route(xs, mod, row, w_router_t) for xs, row in segs]
        ys = expert_ffn([r[0] for r in routes], [r[1] for r in routes], moe_w_gate, moe_w_up, moe_w_down, l)
        outs = [moe_combine_postnorm(y, *r[2], xs, mod, row, 5, ln_g[l, 1], ln_b[l, 1])
                for y, r, (xs, row) in zip(ys, routes, segs)]
        xl = outs[0]
        if not last:
            xc = outs[1]
    return xl[None]
```
